```python
import math
import jax, jax.numpy as jnp
from jax import lax
import numpy as np

D_MODEL = 1024
BATCH = 4
SEQ = 4096
DEPTH = 2
DEC_BATCH = 32
DEC_SEQ = 64
PAST_LEN = 2048

CHUNK = 64
N_META = 16
N_BRANCH = 3
RW_HEADS = 12
RW_HEAD_DIM = 64
RW_WIDTH = RW_HEADS * RW_HEAD_DIM
RW_LORA_W = 64
RW_LORA_A = 64
RW_SHIFT = 3 * RW_WIDTH + RW_LORA_W + RW_LORA_A
S5_GROUPS = 32
S5_GROUP_CH = 16
S5_WIDTH = S5_GROUPS * S5_GROUP_CH
S5_STATE = 64
ML_HEADS = 4
ML_HEAD_DIM = 192
ML_WIDTH = ML_HEADS * ML_HEAD_DIM
ML_CONV = 4
IN_SIZES = (RW_SHIFT, RW_WIDTH, S5_WIDTH, S5_WIDTH, 2 * ML_WIDTH, ML_WIDTH, 2 * ML_HEADS, ML_WIDTH, ML_WIDTH, N_BRANCH * D_MODEL)
IN_COLS = sum(IN_SIZES)
DN_ALPHA = (2 * DEPTH) ** 0.25
DN_BETA = (8 * DEPTH) ** -0.25
LN_EPS = 1e-5
RW_GN_EPS = 64e-5
F32 = jnp.float32

kernel_name = 'hybrid_rwkv7_s5_mlstm_stream_step'


def split_cols(x, sizes):
    return jnp.split(x, np.cumsum(sizes)[:-1].tolist(), axis=-1)


def layer_norm(x, g, b):
    xf = x.astype(F32)
    mu = jnp.mean(xf, -1, keepdims=True)
    var = jnp.mean(jnp.square(xf - mu), -1, keepdims=True)
    return (xf - mu) * lax.rsqrt(var + LN_EPS) * g + b


def head_norm(y, g, eps):
    mu = jnp.mean(y, -1, keepdims=True)
    var = jnp.mean(jnp.square(y - mu), -1, keepdims=True)
    return (y - mu) * lax.rsqrt(var + eps) * g.reshape(y.shape[-2:])


def token_shift(u, prev, mu):
    u_prev = jnp.concatenate([prev[:, None].astype(u.dtype), u[:, :-1]], axis=1)
    return u + (u_prev - u) * mu, u[:, -1]


def rwkv7_branch(cols, gate, shift_prev, wkv0, p):
    bsz, L, _ = cols.shape
    xs, shift_new = token_shift(cols, shift_prev, p['rw_mu'])
    r, k, v, xw, xa = split_cols(xs, (RW_WIDTH, RW_WIDTH, RW_WIDTH, RW_LORA_W, RW_LORA_A))
    w_log = -jax.nn.softplus(-(p['rw_w0'] + jnp.tanh(xw) @ p['rw_w2'])) - 0.5
    decay = jnp.exp(-jnp.exp(w_log))
    a = jax.nn.sigmoid(p['rw_a0'] + xa @ p['rw_a2'])
    hd = lambda t: t.reshape(bsz, L, RW_HEADS, RW_HEAD_DIM)
    kk = hd(k * p['rw_kk'])
    kk = kk * lax.rsqrt(jnp.sum(kk * kk, -1, keepdims=True) + 1e-12)
    k = k * (1.0 + (a - 1.0) * p['rw_ka'])
    r, k, v, a, decay = hd(r), hd(k), hd(v), hd(a), hd(decay)

    def step(S, inp):
        r_t, k_t, v_t, kk_t, a_t, w_t = inp
        sa = jnp.einsum('bhvk,bhk->bhv', S, -kk_t)
        S = S * w_t[:, :, None, :] + sa[..., None] * (kk_t * a_t)[:, :, None, :] + v_t[..., None] * k_t[:, :, None, :]
        return S, jnp.einsum('bhvk,bhk->bhv', S, r_t)

    seq = tuple(jnp.moveaxis(t, 1, 0) for t in (r, k, v, kk, a, decay))
    wkv_new, y = lax.scan(step, wkv0.astype(F32), seq)
    y = jnp.moveaxis(y, 0, 1)
    y = head_norm(y, p['rw_ln_g'], RW_GN_EPS) + p['rw_ln_b'].reshape(RW_HEADS, RW_HEAD_DIM)
    y = y + jnp.sum(r * k * p['rw_rk'], -1, keepdims=True) * v
    y = y.reshape(bsz, L, RW_WIDTH) * jax.nn.silu(gate)
    return y, shift_new, wkv_new


def _lin_combine(e1, e2):
    a1, b1 = e1
    a2, b2 = e2
    return a1 * a2, a2 * b1 + b2


def s5_branch(u, gate, h0_re, h0_im, p):
    bsz, L, _ = u.shape
    lam = lax.complex(p['s5_a_re'].astype(F32), p['s5_a_im'].astype(F32))
    dt = jnp.exp(p['s5_log_dt'].astype(F32))[:, None]
    lam_bar = jnp.exp(lam * dt)
    b_mat = lax.complex(p['s5_b_re'].astype(F32), p['s5_b_im'].astype(F32))
    b_bar = ((lam_bar - 1.0) / lam)[..., None] * b_mat
    c_mat = lax.complex(p['s5_c_re'].astype(F32), p['s5_c_im'].astype(F32))
    ug = u.reshape(bsz, L, S5_GROUPS, S5_GROUP_CH).astype(jnp.complex64)
    bu = jnp.einsum('gph,blgh->blgp', b_bar, ug)
    lam_seq = jnp.broadcast_to(lam_bar, bu.shape)
    lam_pow, hs = lax.associative_scan(_lin_combine, (lam_seq, bu), axis=1)
    h0 = lax.complex(h0_re.astype(F32), h0_im.astype(F32))
    hs = hs + lam_pow * h0[:, None]
    y = jnp.einsum('ghp,blgp->blgh', c_mat, hs).real.reshape(bsz, L, S5_WIDTH) + p['s5_d'] * u
    y = jax.nn.gelu(y)
    y = y * jax.nn.sigmoid(y @ p['s5_w_glu'] + p['s5_b_glu'])
    y = y * jax.nn.silu(gate)
    h_last = hs[:, -1]
    return y, jnp.real(h_last), jnp.imag(h_last)


def mlstm_chunkwise(q, k, v, log_i, log_f, state, blk):
    bsz, L, H, Dh = q.shape
    nb = L // blk
    blocks = lambda t: jnp.moveaxis(t.reshape((bsz, nb, blk) + t.shape[2:]), 1, 0)
    causal = jnp.tril(jnp.ones((blk, blk), dtype=bool))

    def step(carry, inp):
        C, n, m = carry
        qc, kc, vc, lic, lfc = inp
        b = jnp.cumsum(lfc, axis=1)
        g = b + m[:, None, :]
        d = b[:, :, None, :] - b[:, None, :, :] + lic[:, None, :, :]
        d = jnp.where(causal[None, :, :, None], d, -jnp.inf)
        m_row = jnp.maximum(g, jnp.max(d, axis=2))
        s = jnp.einsum('bthd,bshd->btsh', qc, kc) * jnp.exp(d - m_row[:, :, None, :])
        w_inter = jnp.exp(g - m_row)
        num = jnp.einsum('btsh,bshv->bthv', s, vc) + w_inter[..., None] * jnp.einsum('bthk,bhkv->bthv', qc, C)
        den = jnp.sum(s, axis=2) + w_inter * jnp.einsum('bthk,bhk->bth', qc, n)
        h = num / jnp.maximum(jnp.abs(den), jnp.exp(-m_row))[..., None]
        b_end = b[:, -1]
        e_log = b_end[:, None, :] - b + lic
        m_new = jnp.maximum(b_end + m, jnp.max(e_log, axis=1))
        we = jnp.exp(e_log - m_new[:, None, :])
        keep = jnp.exp(b_end + m - m_new)
        C_new = keep[..., None, None] * C + jnp.einsum('bsh,bshk,bshv->bhkv', we, kc, vc)
        n_new = keep[..., None] * n + jnp.einsum('bsh,bshk->bhk', we, kc)
        return (C_new, n_new, m_new), h

    state, h = lax.scan(step, state, tuple(blocks(t) for t in (q, k, v, log_i, log_f)))
    h = jnp.moveaxis(h, 0, 1).reshape(bsz, L, H, Dh)
    return h, state


def mlstm_branch(qk_in, v, if_pre, o_pre, z, conv_prev, C0, n0, m0, p, segments):
    bsz, L, _ = qk_in.shape
    xp = jnp.concatenate([conv_prev.astype(F32), qk_in], axis=1)
    conv = p['ml_conv_b'] + xp[:, 0:L] * p['ml_conv_w'][0]
    for j in range(1, ML_CONV):
        conv = conv + xp[:, j:j + L] * p['ml_conv_w'][j]
    conv_new = xp[:, L:]
    q, k = jnp.split(jax.nn.silu(conv), 2, axis=-1)
    hd = lambda t: t.reshape(bsz, L, ML_HEADS, ML_HEAD_DIM)
    q, k, vh = hd(q), hd(k) / math.sqrt(ML_HEAD_DIM), hd(v)
    log_i, f_pre = jnp.split(if_pre + p['ml_b_if'], 2, axis=-1)
    log_f = jax.nn.log_sigmoid(f_pre)
    state = (C0.astype(F32), n0.astype(F32), m0.astype(F32))
    outs = []
    start = 0
    for seg_len, blk in segments:
        sl = slice(start, start + seg_len)
        h_seg, state = mlstm_chunkwise(q[:, sl], k[:, sl], vh[:, sl], log_i[:, sl], log_f[:, sl], state, blk)
        outs.append(h_seg)
        start += seg_len
    h = jnp.concatenate(outs, axis=1)
    h = head_norm(h, p['ml_ln_g'], LN_EPS).reshape(bsz, L, ML_WIDTH)
    y = jax.nn.sigmoid(o_pre) * h * jax.nn.silu(z)
    return y, conv_new, state[0], state[1], state[2]


def trunk_layer(x, st, p, segments):
    rw_shift0, rw_wkv0, s5_re0, s5_im0, ml_conv0, ml_c0, ml_n0, ml_m0 = st
    proj = (x @ p['w_in']).astype(F32)
    rw_cols, rw_gate, s5_u, s5_gate, ml_qk, ml_v, ml_if, ml_o, ml_z, merge = split_cols(proj, IN_SIZES)
    y_rw, rw_shift1, rw_wkv1 = rwkv7_branch(rw_cols, rw_gate, rw_shift0, rw_wkv0, p)
    y_s5, s5_re1, s5_im1 = s5_branch(s5_u, s5_gate, s5_re0, s5_im0, p)
    y_ml, ml_conv1, ml_c1, ml_n1, ml_m1 = mlstm_branch(ml_qk, ml_v, ml_if, ml_o, ml_z, ml_conv0, ml_c0, ml_n0, ml_m0, p, segments)
    gates = jax.nn.sigmoid(merge + p['b_merge']).reshape(x.shape[:2] + (N_BRANCH, D_MODEL))
    merged = (gates[..., 0, :] * (y_rw @ p['w_br_rw'])
              + gates[..., 1, :] * (y_s5 @ p['w_br_s5'])
              + gates[..., 2, :] * (y_ml @ p['w_br_ml']))
    out = merged @ p['w_out']
    x_new = layer_norm(DN_ALPHA * x.astype(F32) + out, p['ln_g'], p['ln_b']).astype(x.dtype)
    return x_new, (rw_shift1, rw_wkv1, s5_re1, s5_im1, ml_conv1, ml_c1, ml_n1, ml_m1)


def setup_inputs(seed: int = 0) -> dict:
    key = jax.random.key(seed)
    ks = iter(jax.random.split(key, 64))
    nrm = lambda shape, scale: scale * jax.random.normal(next(ks), shape, F32)
    uni = lambda shape, lo, hi: jax.random.uniform(next(ks), shape, F32, lo, hi)
    L = DEPTH
    return {
        'x_prompt': nrm((BATCH, SEQ, D_MODEL), 1.0),
        'x_sample': nrm((DEC_BATCH, DEC_SEQ, D_MODEL), 1.0),
        'state_rwkv_shift': nrm((L, DEC_BATCH, RW_SHIFT), 1.0),
        'state_rwkv_wkv': nrm((L, DEC_BATCH, RW_HEADS, RW_HEAD_DIM, RW_HEAD_DIM), 0.1),
        'state_s5_re': nrm((L, DEC_BATCH, S5_GROUPS, S5_STATE), 0.5),
        'state_s5_im': nrm((L, DEC_BATCH, S5_GROUPS, S5_STATE), 0.5),
        'state_mlstm_conv': nrm((L, DEC_BATCH, ML_CONV - 1, 2 * ML_WIDTH), 1.0),
        'state_mlstm_c': nrm((L, DEC_BATCH, ML_HEADS, ML_HEAD_DIM, ML_HEAD_DIM), 0.1),
        'state_mlstm_n': nrm((L, DEC_BATCH, ML_HEADS, ML_HEAD_DIM), 0.1),
        'state_mlstm_m': nrm((L, DEC_BATCH, ML_HEADS), 0.5),
        'meta': nrm((N_META, D_MODEL), 1.0),
        'in_ln_g': 1.0 + nrm((D_MODEL,), 0.02),
        'in_ln_b': nrm((D_MODEL,), 0.02),
        'w_in': nrm((L, D_MODEL, IN_COLS), D_MODEL ** -0.5),
        'rw_mu': uni((L, RW_SHIFT), 0.0, 1.0),
        'rw_w0': uni((L, RW_WIDTH), -3.0, 1.0),
        'rw_w2': nrm((L, RW_LORA_W, RW_WIDTH), 0.1 * RW_LORA_W ** -0.5),
        'rw_a0': nrm((L, RW_WIDTH), 0.1),
        'rw_a2': nrm((L, RW_LORA_A, RW_WIDTH), RW_LORA_A ** -0.5),
        'rw_kk': 0.85 + nrm((L, RW_WIDTH), 0.02),
        'rw_ka': 1.0 + nrm((L, RW_WIDTH), 0.02),
        'rw_rk': nrm((L, RW_HEADS, RW_HEAD_DIM), 0.1),
        'rw_ln_g': 1.0 + nrm((L, RW_WIDTH), 0.02),
        'rw_ln_b': nrm((L, RW_WIDTH), 0.02),
        's5_a_re': -0.5 + nrm((L, S5_GROUPS, S5_STATE), 0.01),
        's5_a_im': jnp.pi * jnp.arange(S5_STATE, dtype=F32) + nrm((L, S5_GROUPS, S5_STATE), 0.01),
        's5_b_re': nrm((L, S5_GROUPS, S5_STATE, S5_GROUP_CH), (2 * S5_GROUP_CH) ** -0.5),
        's5_b_im': nrm((L, S5_GROUPS, S5_STATE, S5_GROUP_CH), (2 * S5_GROUP_CH) ** -0.5),
        's5_c_re': nrm((L, S5_GROUPS, S5_GROUP_CH, S5_STATE), (2 * S5_STATE) ** -0.5),
        's5_c_im': nrm((L, S5_GROUPS, S5_GROUP_CH, S5_STATE), (2 * S5_STATE) ** -0.5),
        's5_d': nrm((L, S5_WIDTH), 1.0),
        's5_log_dt': uni((L, S5_GROUPS), math.log(0.001), math.log(0.1)),
        's5_w_glu': nrm((L, S5_WIDTH, S5_WIDTH), S5_WIDTH ** -0.5),
        's5_b_glu': nrm((L, S5_WIDTH), 0.02),
        'ml_conv_w': nrm((L, ML_CONV, 2 * ML_WIDTH), 0.5),
        'ml_conv_b': nrm((L, 2 * ML_WIDTH), 0.02),
        'ml_b_if': jnp.concatenate([nrm((L, ML_HEADS), 0.1), uni((L, ML_HEADS), 3.0, 6.0)], axis=-1),
        'ml_ln_g': 1.0 + nrm((L, ML_WIDTH), 0.02),
        'b_merge': nrm((L, N_BRANCH * D_MODEL), 0.02),
        'w_br_rw': nrm((L, RW_WIDTH, D_MODEL), DN_BETA * RW_WIDTH ** -0.5),
        'w_br_s5': nrm((L, S5_WIDTH, D_MODEL), DN_BETA * S5_WIDTH ** -0.5),
        'w_br_ml': nrm((L, ML_WIDTH, D_MODEL), DN_BETA * ML_WIDTH ** -0.5),
        'w_out': nrm((L, D_MODEL, D_MODEL), DN_BETA * D_MODEL ** -0.5),
        'ln_g': 1.0 + nrm((L, D_MODEL), 0.02),
        'ln_b': nrm((L, D_MODEL), 0.02),
    }


def reference(x_prompt, x_sample, state_rwkv_shift, state_rwkv_wkv, state_s5_re, state_s5_im,
              state_mlstm_conv, state_mlstm_c, state_mlstm_n, state_mlstm_m,
              meta, in_ln_g, in_ln_b, w_in, rw_mu, rw_w0, rw_w2, rw_a0, rw_a2, rw_kk, rw_ka, rw_rk,
              rw_ln_g, rw_ln_b, s5_a_re, s5_a_im, s5_b_re, s5_b_im, s5_c_re, s5_c_im, s5_d, s5_log_dt,
              s5_w_glu, s5_b_glu, ml_conv_w, ml_conv_b, ml_b_if, ml_ln_g, b_merge, w_br_rw, w_br_s5,
              w_br_ml, w_out, ln_g, ln_b):
    def layer_params(l):
        return dict(w_in=w_in[l], rw_mu=rw_mu[l], rw_w0=rw_w0[l], rw_w2=rw_w2[l], rw_a0=rw_a0[l],
                    rw_a2=rw_a2[l], rw_kk=rw_kk[l], rw_ka=rw_ka[l], rw_rk=rw_rk[l], rw_ln_g=rw_ln_g[l],
                    rw_ln_b=rw_ln_b[l], s5_a_re=s5_a_re[l], s5_a_im=s5_a_im[l], s5_b_re=s5_b_re[l],
                    s5_b_im=s5_b_im[l], s5_c_re=s5_c_re[l], s5_c_im=s5_c_im[l], s5_d=s5_d[l],
                    s5_log_dt=s5_log_dt[l], s5_w_glu=s5_w_glu[l], s5_b_glu=s5_b_glu[l],
                    ml_conv_w=ml_conv_w[l], ml_conv_b=ml_conv_b[l], ml_b_if=ml_b_if[l], ml_ln_g=ml_ln_g[l],
                    b_merge=b_merge[l], w_br_rw=w_br_rw[l], w_br_s5=w_br_s5[l], w_br_ml=w_br_ml[l],
                    w_out=w_out[l], ln_g=ln_g[l], ln_b=ln_b[l])

    bp, sp = x_prompt.shape[0], x_prompt.shape[1]
    xp = jnp.concatenate([jnp.broadcast_to(meta[None], (bp, N_META, D_MODEL)).astype(x_prompt.dtype), x_prompt], axis=1)
    xp = layer_norm(xp, in_ln_g, in_ln_b).astype(x_prompt.dtype)
    seg_prompt = ((N_META, N_META), (sp, CHUNK))
    z0 = (jnp.zeros((bp, RW_SHIFT), F32), jnp.zeros((bp, RW_HEADS, RW_HEAD_DIM, RW_HEAD_DIM), F32),
          jnp.zeros((bp, S5_GROUPS, S5_STATE), F32), jnp.zeros((bp, S5_GROUPS, S5_STATE), F32),
          jnp.zeros((bp, ML_CONV - 1, 2 * ML_WIDTH), F32), jnp.zeros((bp, ML_HEADS, ML_HEAD_DIM, ML_HEAD_DIM), F32),
          jnp.zeros((bp, ML_HEADS, ML_HEAD_DIM), F32), jnp.zeros((bp, ML_HEADS), F32))
    p_states = []
    for l in range(DEPTH):
        xp, st = trunk_layer(xp, z0, layer_params(l), seg_prompt)
        p_states.append(st)
    y_prompt = xp[:, N_META:]

    ds = x_sample.shape[1]
    xs = layer_norm(x_sample, in_ln_g, in_ln_b).astype(x_sample.dtype)
    seg_sample = ((ds, ds),)
    s_states = []
    for l in range(DEPTH):
        st0 = (state_rwkv_shift[l], state_rwkv_wkv[l], state_s5_re[l], state_s5_im[l],
               state_mlstm_conv[l], state_mlstm_c[l], state_mlstm_n[l], state_mlstm_m[l])
        xs, st = trunk_layer(xs, st0, layer_params(l), seg_sample)
        s_states.append(st)
    y_sample = xs

    (p_rw_shift, p_rw_wkv, p_s5_re, p_s5_im, p_ml_conv, p_ml_c, p_ml_n, p_ml_m) = [jnp.stack(s, 0) for s in zip(*p_states)]
    (s_rw_shift, s_rw_wkv, s_s5_re, s_s5_im, s_ml_conv, s_ml_c, s_ml_n, s_ml_m) = [jnp.stack(s, 0) for s in zip(*s_states)]
    return (y_prompt, y_sample,
            p_rw_shift, p_rw_wkv, p_s5_re, p_s5_im, p_ml_conv, p_ml_c, p_ml_n, p_ml_m,
            s_rw_shift, s_rw_wkv, s_s5_re, s_s5_im, s_ml_conv, s_ml_c, s_ml_n, s_ml_m)
```

```python
import functools
import math

import jax
import jax.numpy as jnp
from jax import lax
from jax.experimental import pallas as pl
from jax.experimental.pallas import tpu as pltpu

F32 = jnp.float32
BF16 = jnp.bfloat16

D_MODEL = 1024
DEPTH = 2
N_META = 16
RW_HEADS = 12
RW_HEAD_DIM = 64
RW_WIDTH = RW_HEADS * RW_HEAD_DIM
RW_PAIRS = RW_HEADS // 2
RW_LORA = 64
RW_RKV = 3 * RW_WIDTH
RW_SHIFT = RW_RKV + 2 * RW_LORA
S5_GROUPS = 32
S5_GROUP_CH = 16
S5_WIDTH = S5_GROUPS * S5_GROUP_CH
S5_STATE = 64
S5_NBLK = 4
S5_BLK_STATE = 512
ML_HEADS = 4
ML_HEAD_DIM = 192
ML_HEAD_PAD = 256
ML_WIDTH = ML_HEADS * ML_HEAD_DIM
ML_WIDTH_PAD = ML_HEADS * ML_HEAD_PAD
ML_CONV = 4
DN_ALPHA = (2 * DEPTH) ** 0.25
LN_EPS = 1e-5
RW_GN_EPS = 64e-5

LANE = 128
SUBLANES = 8
BLK = 64
PAD = BLK - N_META
SCAN_LEVELS = 6
SCAN_HALO = 32

OFF_MG, W_MG = 0, 3 * D_MODEL
OFF_S5, W_S5 = 3072, 2 * S5_WIDTH
OFF_QK, W_QK = 4096, 2 * ML_WIDTH_PAD
OFF_OZ, W_OZ = 6144, 2 * ML_WIDTH_PAD
OFF_MV, W_MV = 8192, ML_WIDTH_PAD
OFF_RKV, W_RKV = 9216, RW_RKV
OFF_RWG, W_RWG = 11520, RW_WIDTH
OFF_WA, W_WA = 12288, LANE
OFF_MI, W_MI = 12416, LANE
OFF_MF, W_MF = 12544, LANE
P_COLS = 12672
P_TN = 1152

VMEM_LIMIT = 56 * 1024 * 1024


def _cparams(sem):
    return pltpu.CompilerParams(dimension_semantics=sem, vmem_limit_bytes=VMEM_LIMIT)


def _bdot(a, b):
    return jnp.dot(a.astype(BF16), b.astype(BF16), preferred_element_type=F32)


def _hdot(a, b):
    return jnp.dot(a, b, precision=lax.Precision.HIGHEST, preferred_element_type=F32)


def _sigmoid(x):
    return 1.0 / (1.0 + jnp.exp(-x))


def _silu(x):
    return x * _sigmoid(x)


def _softplus(x):
    return jnp.maximum(x, 0.0) + jnp.log1p(jnp.exp(-jnp.abs(x)))


def _pad_row_mask(tile_idx, tm, seq_rows, pad):
    pos0 = lax.rem(tile_idx * tm, seq_rows)
    row = lax.broadcasted_iota(jnp.int32, (tm, 1), 0) + pos0
    return row < pad


def _layer_norm_rows(x, g, b):
    mu = jnp.mean(x, axis=-1, keepdims=True)
    d = x - mu
    var = jnp.mean(d * d, axis=-1, keepdims=True)
    return d * lax.rsqrt(var + LN_EPS) * g + b


def _ln_in_kernel(x_ref, g_ref, b_ref, o_ref, *, tm, seq_rows, pad):
    y = _layer_norm_rows(x_ref[...], g_ref[...], b_ref[...])
    if pad:
        y = jnp.where(_pad_row_mask(pl.program_id(0), tm, seq_rows, pad), 0.0, y)
    o_ref[...] = y


def _ln_in(x, g, b, *, tm, seq_rows, pad):
    n = x.shape[0]
    return pl.pallas_call(
        functools.partial(_ln_in_kernel, tm=tm, seq_rows=seq_rows, pad=pad),
        out_shape=jax.ShapeDtypeStruct((n, D_MODEL), F32),
        grid=(n // tm,),
        in_specs=[pl.BlockSpec((tm, D_MODEL), lambda i: (i, 0)),
                  pl.BlockSpec((1, D_MODEL), lambda i: (0, 0)),
                  pl.BlockSpec((1, D_MODEL), lambda i: (0, 0))],
        out_specs=pl.BlockSpec((tm, D_MODEL), lambda i: (i, 0)),
        compiler_params=_cparams(("parallel",)),
        name="ln_in",
    )(x, g, b)


def _proj_kernel(x_ref, w_ref, o_ref):
    o_ref[...] = jnp.dot(x_ref[...].astype(BF16), w_ref[...], preferred_element_type=F32)


def _proj(x, w_all, *, tm):
    n = x.shape[0]
    return pl.pallas_call(
        _proj_kernel,
        out_shape=jax.ShapeDtypeStruct((n, P_COLS), F32),
        grid=(n // tm, P_COLS // P_TN),
        in_specs=[pl.BlockSpec((tm, D_MODEL), lambda i, j: (i, 0)),
                  pl.BlockSpec((D_MODEL, P_TN), lambda i, j: (0, j))],
        out_specs=pl.BlockSpec((tm, P_TN), lambda i, j: (i, j)),
        compiler_params=_cparams(("parallel", "arbitrary")),
        name="proj",
    )(x, w_all)


def _head_sum(x, ones_blk):
    parts = [_hdot(x[:, i * LANE:(i + 1) * LANE], ones_blk) for i in range(RW_WIDTH // LANE)]
    return jnp.concatenate(parts, axis=-1)


def _shift_rows(u, carry):
    rolled = pltpu.roll(u, 1, 0)
    row = lax.broadcasted_iota(jnp.int32, u.shape, 0)
    return jnp.where(row == 0, carry, rolled)


def _rw_prep_kernel(rkv_ref, wa_ref, sh0_rkv_ref, sh0_wa_ref, mu_rkv_ref, mu_wa_ref,
                    w0_ref, w2_ref, a0_ref, a2_ref, kk_ref, ka_ref, rk_ref, ones_ref,
                    r_out, k_out, v_out, nkk_out, kka_out, w_out, bon_out, sh_rkv_out, sh_wa_out,
                    c_rkv, c_wa):
    j = pl.program_id(1)

    @pl.when(j == 0)
    def _():
        c_rkv[...] = sh0_rkv_ref[...]
        c_wa[...] = sh0_wa_ref[...]

    u = rkv_ref[...]
    uw = wa_ref[...]
    xs = u + (_shift_rows(u, c_rkv[...]) - u) * mu_rkv_ref[...]
    xwa = uw + (_shift_rows(uw, c_wa[...]) - uw) * mu_wa_ref[...]
    c_rkv[...] = u[BLK - 1:BLK, :]
    c_wa[...] = uw[BLK - 1:BLK, :]
    sh_rkv_out[...] = u[BLK - 1:BLK, :]
    sh_wa_out[...] = uw[BLK - 1:BLK, :]

    r = xs[:, 0:RW_WIDTH]
    k = xs[:, RW_WIDTH:2 * RW_WIDTH]
    v = xs[:, 2 * RW_WIDTH:3 * RW_WIDTH]
    w_log = -_softplus(-(w0_ref[...] + _bdot(jnp.tanh(xwa), w2_ref[...]))) - 0.5
    decay = jnp.exp(-jnp.exp(w_log))
    a = _sigmoid(a0_ref[...] + _bdot(xwa, a2_ref[...]))
    ones_blk = ones_ref[...]
    kk = k * kk_ref[...]
    kk = kk * lax.rsqrt(_head_sum(kk * kk, ones_blk) + 1e-12)
    k = k * (1.0 + (a - 1.0) * ka_ref[...])
    r_out[...] = r
    k_out[...] = k
    v_out[...] = v
    nkk_out[...] = -kk
    kka_out[...] = kk * a
    w_out[...] = decay
    bon_out[...] = _head_sum(r * k * rk_ref[...], ones_blk) * v


def _rw_prep(p3, sh0_rkv, sh0_wa, lp):
    bsz, seq_rows, _ = p3.shape
    nblk = seq_rows // BLK
    row_spec = pl.BlockSpec((None, BLK, RW_WIDTH), lambda b, j: (b, j, 0))
    par = lambda w: pl.BlockSpec((1, w), lambda b, j: (0, 0))
    mat = pl.BlockSpec((LANE, RW_WIDTH), lambda b, j: (0, 0))
    st = lambda w: pl.BlockSpec((None, 1, w), lambda b, j: (b, 0, 0))
    row_shape = jax.ShapeDtypeStruct((bsz, seq_rows, RW_WIDTH), F32)
    return pl.pallas_call(
        _rw_prep_kernel,
        out_shape=(row_shape,) * 7 + (jax.ShapeDtypeStruct((bsz, 1, RW_RKV), F32),
                                      jax.ShapeDtypeStruct((bsz, 1, LANE), F32)),
        grid=(bsz, nblk),
        in_specs=[pl.BlockSpec((None, BLK, W_RKV), lambda b, j: (b, j, OFF_RKV // W_RKV)),
                  pl.BlockSpec((None, BLK, W_WA), lambda b, j: (b, j, OFF_WA // W_WA)),
                  st(RW_RKV), st(LANE), par(RW_RKV), par(LANE),
                  par(RW_WIDTH), mat, par(RW_WIDTH), mat,
                  par(RW_WIDTH), par(RW_WIDTH), par(RW_WIDTH),
                  pl.BlockSpec((LANE, LANE), lambda b, j: (0, 0))],
        out_specs=(row_spec,) * 7 + (st(RW_RKV), st(LANE)),
        scratch_shapes=[pltpu.VMEM((1, RW_RKV), F32), pltpu.VMEM((1, LANE), F32)],
        compiler_params=_cparams(("parallel", "arbitrary")),
        name="rw_prep",
    )(p3, p3, sh0_rkv, sh0_wa, lp["mu_rkv"], lp["mu_wa"], lp["rw_w0"], lp["rw_w2p"],
      lp["rw_a0"], lp["rw_a2p"], lp["rw_kk"], lp["rw_ka"], lp["rw_rk"], lp["ones_blk"])


def _rw_scan_kernel(r_ref, k_ref, v_ref, nkk_ref, kka_ref, w_ref, bon_ref, gate_ref, s0_ref,
                    lng_ref, lnb_ref, ones_ref, y_ref, s_out, s_scr, y_scr, *, bb_n):
    j = pl.program_id(1)

    @pl.when(j == 0)
    def _():
        s_scr[...] = s0_ref[...]

    lane = lax.broadcasted_iota(jnp.int32, (RW_HEAD_DIM, LANE), 1)
    sub = lax.broadcasted_iota(jnp.int32, (RW_HEAD_DIM, LANE), 0)
    lo = lane < RW_HEAD_DIM
    eye0 = lane == sub
    eye1 = lane == sub + RW_HEAD_DIM

    def seg_sum(x):
        s0 = jnp.sum(jnp.where(lo, x, 0.0), axis=1, keepdims=True)
        s1 = jnp.sum(jnp.where(lo, 0.0, x), axis=1, keepdims=True)
        return s0, s1

    sub8 = lax.broadcasted_iota(jnp.int32, (SUBLANES, LANE), 0)

    def row_group(g, carry):
        base = pl.multiple_of(g * SUBLANES, SUBLANES)
        for bb in range(bb_n):
            for p in range(RW_PAIRS):
                sl = slice(p * LANE, (p + 1) * LANE)
                tile = lambda ref: ref[bb, pl.ds(base, SUBLANES), sl]
                nkk8, v8, w8, kka8, k8, r8 = (tile(ref) for ref in
                                              (nkk_ref, v_ref, w_ref, kka_ref, k_ref, r_ref))
                s = s_scr[bb, p]
                y8 = jnp.zeros((SUBLANES, LANE), F32)
                for i in range(SUBLANES):
                    row = lambda x: x[i:i + 1, :]
                    sa0, sa1 = seg_sum(s * row(nkk8))
                    vb = jnp.broadcast_to(row(v8), (RW_HEAD_DIM, LANE))
                    v0 = jnp.sum(jnp.where(eye0, vb, 0.0), axis=1, keepdims=True)
                    v1 = jnp.sum(jnp.where(eye1, vb, 0.0), axis=1, keepdims=True)
                    s = (s * row(w8) + jnp.where(lo, sa0, sa1) * row(kka8)
                         + jnp.where(lo, v0, v1) * row(k8))
                    y0, y1 = seg_sum(s * row(r8))
                    ycol = jnp.where(eye0, y0, jnp.where(eye1, y1, 0.0))
                    y8 = jnp.where(sub8 == i, jnp.sum(ycol, axis=0, keepdims=True), y8)
                s_scr[bb, p] = s
                y_scr[bb, pl.ds(base, SUBLANES), sl] = y8
        return carry

    lax.fori_loop(0, BLK // SUBLANES, row_group, 0)

    ones_blk = ones_ref[...]
    inv = 1.0 / RW_HEAD_DIM
    for bb in range(bb_n):
        y = y_scr[bb]
        mu = _head_sum(y, ones_blk) * inv
        d = y - mu
        var = _head_sum(d * d, ones_blk) * inv
        yn = d * lax.rsqrt(var + RW_GN_EPS) * lng_ref[...] + lnb_ref[...]
        y_ref[bb] = (yn + bon_ref[bb]) * _silu(gate_ref[bb])

    @pl.when(j == pl.num_programs(1) - 1)
    def _():
        s_out[...] = s_scr[...]


def _rw_scan(prep, p3, s0, lp, *, bb_n):
    r, k, v, nkk, kka, w, bon = prep
    bsz, seq_rows, _ = r.shape
    nblk = seq_rows // BLK
    row_spec = pl.BlockSpec((bb_n, BLK, RW_WIDTH), lambda b, j: (b, j, 0))
    par = pl.BlockSpec((1, RW_WIDTH), lambda b, j: (0, 0))
    st_spec = pl.BlockSpec((bb_n, RW_PAIRS, RW_HEAD_DIM, LANE), lambda b, j: (b, 0, 0, 0))
    return pl.pallas_call(
        functools.partial(_rw_scan_kernel, bb_n=bb_n),
        out_shape=(jax.ShapeDtypeStruct((bsz, seq_rows, RW_WIDTH), F32),
                   jax.ShapeDtypeStruct((bsz, RW_PAIRS, RW_HEAD_DIM, LANE), F32)),
        grid=(bsz // bb_n, nblk),
        in_specs=[row_spec] * 7
        + [pl.BlockSpec((bb_n, BLK, W_RWG), lambda b, j: (b, j, OFF_RWG // W_RWG)),
           st_spec, par, par, pl.BlockSpec((LANE, LANE), lambda b, j: (0, 0))],
        out_specs=(row_spec, st_spec),
        scratch_shapes=[pltpu.VMEM((bb_n, RW_PAIRS, RW_HEAD_DIM, LANE), F32),
                        pltpu.VMEM((bb_n, BLK, RW_WIDTH), F32)],
        compiler_params=_cparams(("parallel", "arbitrary")),
        name="rw_scan",
    )(r, k, v, nkk, kka, w, bon, p3, s0, lp["rw_ln_g"], lp["rw_ln_b"], lp["ones_blk"])


def _cmul(ar, ai, br, bi):
    return ar * br - ai * bi, ar * bi + ai * br


def _s5_prep_kernel(are_ref, aim_ref, ldt_ref, bre_ref, bim_ref,
                    bbre_out, bbim_out, lvre_out, lvim_out, pwre_out, pwim_out):
    ar, ai = are_ref[...], aim_ref[...]
    dt = jnp.exp(ldt_ref[...])
    mag = jnp.exp(ar * dt)
    lr, li = mag * jnp.cos(ai * dt), mag * jnp.sin(ai * dt)
    nr, ni = lr - 1.0, li
    den = ar * ar + ai * ai
    qr, qi = (nr * ar + ni * ai) / den, (ni * ar - nr * ai) / den
    br, bi = bre_ref[...], bim_ref[...]
    bbr, bbi = _cmul(qr[:, None, :], qi[:, None, :], br, bi)
    bbre_out[...] = bbr
    bbim_out[...] = bbi
    sq_r, sq_i = lr, li
    pows = [(lr, li)]
    for lvl in range(SCAN_LEVELS):
        lvre_out[lvl] = sq_r
        lvim_out[lvl] = sq_i
        pows = pows + [_cmul(pr, pi, sq_r, sq_i) for pr, pi in pows]
        sq_r, sq_i = _cmul(sq_r, sq_i, sq_r, sq_i)
    for t in range(BLK):
        pwre_out[t] = pows[t][0]
        pwim_out[t] = pows[t][1]


def _s5_prep(a_re, a_im, log_dt, b_re_t, b_im_t):
    ghp =jax.ShapeDtypeStruct((S5_GROUPS, S5_GROUP_CH, S5_STATE), F32)
    lv = jax.ShapeDtypeStruct((SCAN_LEVELS, S5_GROUPS, S5_STATE), F32)
    pw = jax.ShapeDtypeStruct((BLK, S5_GROUPS, S5_STATE), F32)
    return pl.pallas_call(
        _s5_prep_kernel,
        out_shape=(ghp, ghp, lv, lv, pw, pw),
        name="s5_prep",
    )(a_re, a_im, log_dt, b_re_t, b_im_t)


def _gelu_tanh(x):
    return 0.5 * x * (1.0 + jnp.tanh(math.sqrt(2.0 / math.pi) * (x + 0.044715 * (x * x * x))))


def _s5_kernel(p_ref, h0_ref, wb_ref, wc_ref, lv_ref, pw_ref, d_ref, wglu_ref, bglu_ref,
               y_ref, h_out, h_scr, xr_scr, xi_scr, hs_scr, *, tb):
    j = pl.program_id(1)

    @pl.when(j == 0)
    def _():
        h_scr[...] = h0_ref[...]

    zeros_halo = jnp.zeros((SCAN_HALO, S5_BLK_STATE), F32)
    xr_scr[0:SCAN_HALO, :] = zeros_halo
    xi_scr[0:SCAN_HALO, :] = zeros_halo

    u = p_ref[:, 0:S5_WIDTH]
    gate = p_ref[:, S5_WIDTH:2 * S5_WIDTH]
    y_parts = []
    for jb in range(S5_NBLK):
        bu = _bdot(u[:, jb * LANE:(jb + 1) * LANE], wb_ref[jb])
        for s in range(tb // BLK):
            xr = bu[s * BLK:(s + 1) * BLK, 0:S5_BLK_STATE]
            xi = bu[s * BLK:(s + 1) * BLK, S5_BLK_STATE:2 * S5_BLK_STATE]
            for lvl in range(SCAN_LEVELS):
                d = 1 << lvl
                xr_scr[SCAN_HALO:SCAN_HALO + BLK, :] = xr
                xi_scr[SCAN_HALO:SCAN_HALO + BLK, :] = xi
                sr = xr_scr[SCAN_HALO - d:SCAN_HALO - d + BLK, :]
                si = xi_scr[SCAN_HALO - d:SCAN_HALO - d + BLK, :]
                lr = lv_ref[lvl, 2 * jb:2 * jb + 1, :]
                li = lv_ref[lvl, 2 * jb + 1:2 * jb + 2, :]
                xr, xi = xr + (lr * sr - li * si), xi + (lr * si + li * sr)
            h_r = h_scr[2 * jb:2 * jb + 1, :]
            h_i = h_scr[2 * jb + 1:2 * jb + 2, :]
            pr, pi = pw_ref[2 * jb], pw_ref[2 * jb + 1]
            hr = xr + (pr * h_r - pi * h_i)
            hi = xi + (pr * h_i + pi * h_r)
            h_scr[2 * jb:2 * jb + 1, :] = hr[BLK - 1:BLK, :]
            h_scr[2 * jb + 1:2 * jb + 2, :] = hi[BLK - 1:BLK, :]
            hs_scr[s * BLK:(s + 1) * BLK, 0:S5_BLK_STATE] = hr
            hs_scr[s * BLK:(s + 1) * BLK, S5_BLK_STATE:2 * S5_BLK_STATE] = hi
        y_parts.append(_bdot(hs_scr[...], wc_ref[jb]))
    y = jnp.concatenate(y_parts, axis=-1) + d_ref[...] * u
    y = _gelu_tanh(y)
    y = y * _sigmoid(_bdot(y, wglu_ref[...]) + bglu_ref[...])
    y_ref[...] = y * _silu(gate)

    @pl.when(j == pl.num_programs(1) - 1)
    def _():
        h_out[...] = h_scr[...]


def _s5(p3, h0, lp, *, tb):
    bsz, seq_rows, _ = p3.shape
    st_spec = pl.BlockSpec((None, 2 * S5_NBLK, S5_BLK_STATE), lambda b, j: (b, 0, 0))
    full = lambda shape: pl.BlockSpec(shape, lambda b, j: (0,) * len(shape))
    return pl.pallas_call(
        functools.partial(_s5_kernel, tb=tb),
        out_shape=(jax.ShapeDtypeStruct((bsz, seq_rows, S5_WIDTH), F32),
                   jax.ShapeDtypeStruct((bsz, 2 * S5_NBLK, S5_BLK_STATE), F32)),
        grid=(bsz, seq_rows // tb),
        in_specs=[pl.BlockSpec((None, tb, W_S5), lambda b, j: (b, j, OFF_S5 // W_S5)),
                  st_spec,
                  full((S5_NBLK, LANE, 2 * S5_BLK_STATE)),
                  full((S5_NBLK, 2 * S5_BLK_STATE, LANE)),
                  full((SCAN_LEVELS, 2 * S5_NBLK, S5_BLK_STATE)),
                  full((2 * S5_NBLK, BLK, S5_BLK_STATE)),
                  full((1, S5_WIDTH)), full((S5_WIDTH, S5_WIDTH)), full((1, S5_WIDTH))],
        out_specs=(pl.BlockSpec((None, tb, S5_WIDTH), lambda b, j: (b, j, 0)), st_spec),
        scratch_shapes=[pltpu.VMEM((2 * S5_NBLK, S5_BLK_STATE), F32),
                        pltpu.VMEM((SCAN_HALO + BLK, S5_BLK_STATE), F32),
                        pltpu.VMEM((SCAN_HALO + BLK, S5_BLK_STATE), F32),
                        pltpu.VMEM((tb, 2 * S5_BLK_STATE), F32)],
        compiler_params=_cparams(("parallel", "arbitrary")),
        name="s5",
    )(p3, h0, lp["s5_wb"], lp["s5_wc"], lp["s5_lv"], lp["s5_pw"], lp["s5_d"],
      lp["s5_w_glu"], lp["s5_b_glu"])


def _mlstm_kernel(qk_ref, oz_ref, v_ref, gi_ref, gf_ref, conv0_ref, c0_ref, n0_ref, m0_ref,
                  cw_ref, cb_ref, bi_ref, bf_ref, lng_ref,
                  y_ref, conv_out, c_out, n_out, m_out,
                  xp_scr, c_scr, n_scr, m_scr, *, pad):
    j = pl.program_id(1)
    halo = 8

    @pl.when(j == 0)
    def _():
        xp_scr[0:halo, :] = jnp.zeros((halo, W_QK), F32)
        xp_scr[halo - (ML_CONV - 1):halo, :] = conv0_ref[...]
        c_scr[...] = jnp.zeros(c_scr.shape, F32)
        c_scr[:, 0:ML_HEAD_DIM, 0:ML_HEAD_DIM] = c0_ref[...]
        n_scr[...] = jnp.zeros(n_scr.shape, F32)
        n_scr[:, :, 0:ML_HEAD_DIM] = n0_ref[...]
        m_scr[...] = m0_ref[...]

    xp_scr[halo:halo + BLK, :] = qk_ref[...]
    conv = cb_ref[...] + xp_scr[halo - 3:halo - 3 + BLK, :] * cw_ref[0:1, :]
    for tap in range(1, ML_CONV):
        conv = conv + xp_scr[halo - 3 + tap:halo - 3 + tap + BLK, :] * cw_ref[tap:tap + 1, :]
    tail = xp_scr[halo + BLK - (ML_CONV - 1):halo + BLK, :]
    xp_scr[halo - (ML_CONV - 1):halo, :] = tail
    conv_out[...] = tail
    act = _silu(conv)
    q = act[:, 0:ML_WIDTH_PAD]
    k = act[:, ML_WIDTH_PAD:2 * ML_WIDTH_PAD] * (1.0 / math.sqrt(ML_HEAD_DIM))
    v = v_ref[...]

    t0 = jnp.where(j == 0, pad, 0) if pad else 0
    row1 = lax.broadcasted_iota(jnp.int32, (BLK, 1), 0)
    row_ok = row1 >= t0
    ti = lax.broadcasted_iota(jnp.int32, (BLK, BLK), 0)
    si = lax.broadcasted_iota(jnp.int32, (BLK, BLK), 1)
    pair_ok = (si <= ti) & (si >= t0)
    tril = jnp.where(si <= ti, 1.0, 0.0)
    eye = jnp.where(si == ti, 1.0, 0.0)
    ones_sq = jnp.ones((BLK, BLK), F32)

    log_i = gi_ref[...] + bi_ref[...]
    f_pre = gf_ref[...] + bf_ref[...]
    log_f = jnp.where(row_ok, -_softplus(-f_pre), 0.0)
    b = _hdot(tril, log_f)
    m_prev = m_scr[...]
    g = b + m_prev
    b_end = b[BLK - 1:BLK, :]
    e_log = jnp.where(row_ok, b_end - b + log_i, -jnp.inf)
    m_new = jnp.maximum(b_end + m_prev, jnp.max(e_log, axis=0, keepdims=True))
    we = jnp.exp(e_log - m_new)
    keep = jnp.exp(b_end + m_prev - m_new)
    m_scr[...] = m_new

    lane_ok = lax.broadcasted_iota(jnp.int32, (1, ML_HEAD_PAD), 1) < ML_HEAD_DIM
    o = oz_ref[:, 0:ML_WIDTH_PAD]
    z = oz_ref[:, ML_WIDTH_PAD:2 * ML_WIDTH_PAD]
    for h in range(ML_HEADS):
        hs = slice(h * ML_HEAD_PAD, (h + 1) * ML_HEAD_PAD)
        qh, kh, vh = q[:, hs], k[:, hs], v[:, hs]
        b_col = b[:, h:h + 1]
        x_col = log_i[:, h:h + 1] - b_col
        d = b_col + _hdot(ones_sq, eye * x_col)
        d = jnp.where(pair_ok, d, -jnp.inf)
        g_col = g[:, h:h + 1]
        m_row = jnp.maximum(g_col, jnp.max(d, axis=1, keepdims=True))
        qk = lax.dot_general(qh.astype(BF16), kh.astype(BF16), (((1,), (1,)), ((), ())),
                             preferred_element_type=F32)
        s = qk * jnp.exp(d - m_row)
        w_inter = jnp.exp(g_col - m_row)
        c_h = c_scr[h]
        n_h = n_scr[h]
        num = _bdot(s, vh) + w_inter * _bdot(qh, c_h)
        den = jnp.sum(s, axis=1, keepdims=True) + w_inter * jnp.sum(qh * n_h, axis=1, keepdims=True)
        hh = num / jnp.maximum(jnp.abs(den), jnp.exp(-m_row))
        mu = jnp.sum(hh, axis=1, keepdims=True) * (1.0 / ML_HEAD_DIM)
        dv = jnp.where(lane_ok, hh - mu, 0.0)
        var = jnp.sum(dv * dv, axis=1, keepdims=True) * (1.0 / ML_HEAD_DIM)
        hn = dv * lax.rsqrt(var + LN_EPS) * lng_ref[:, hs]
        y_ref[:, hs] = _sigmoid(o[:, hs]) * hn * _silu(z[:, hs])
        we_col = we[:, h:h + 1]
        keep_h = keep[:, h:h + 1]
        kv = lax.dot_general(kh.astype(BF16), (we_col * vh).astype(BF16), (((0,), (0,)), ((), ())),
                             preferred_element_type=F32)
        c_scr[h] = keep_h * c_h + kv
        n_scr[h] = keep_h * n_h + jnp.sum(we_col * kh, axis=0, keepdims=True)

    @pl.when(j == pl.num_programs(1) - 1)
    def _():
        c_out[...] = c_scr[:, 0:ML_HEAD_DIM, 0:ML_HEAD_DIM]
        n_out[...] = n_scr[:, :, 0:ML_HEAD_DIM]
        m_out[...] = m_scr[...]


def _mlstm(p3, conv0, c0, n0, m0, lp, *, pad):
    bsz, seq_rows, _ = p3.shape
    nblk = seq_rows // BLK
    blk = lambda w, off: pl.BlockSpec((None, BLK, w), lambda b, j: (b, j, off // w))
    par = lambda r, w: pl.BlockSpec((r, w), lambda b, j: (0, 0))
    conv_spec = pl.BlockSpec((None, ML_CONV - 1, W_QK), lambda b, j: (b, 0, 0))
    c_spec = pl.BlockSpec((None, ML_HEADS, ML_HEAD_DIM, ML_HEAD_DIM), lambda b, j: (b, 0, 0, 0))
    n_spec = pl.BlockSpec((None, ML_HEADS, 1, ML_HEAD_DIM), lambda b, j: (b, 0, 0, 0))
    m_spec = pl.BlockSpec((None, 1, LANE), lambda b, j: (b, 0, 0))
    return pl.pallas_call(
        functools.partial(_mlstm_kernel, pad=pad),
        out_shape=(jax.ShapeDtypeStruct((bsz, seq_rows, ML_WIDTH_PAD), F32),
                   jax.ShapeDtypeStruct((bsz, ML_CONV - 1, W_QK), F32),
                   jax.ShapeDtypeStruct((bsz, ML_HEADS, ML_HEAD_DIM, ML_HEAD_DIM), F32),
                   jax.ShapeDtypeStruct((bsz, ML_HEADS, 1, ML_HEAD_DIM), F32),
                   jax.ShapeDtypeStruct((bsz, 1, LANE), F32)),
        grid=(bsz, nblk),
        in_specs=[blk(W_QK, OFF_QK), blk(W_OZ, OFF_OZ), blk(W_MV, OFF_MV),
                  blk(W_MI, OFF_MI), blk(W_MF, OFF_MF),
                  conv_spec, c_spec, n_spec, m_spec,
                  par(ML_CONV, W_QK), par(1, W_QK), par(1, LANE), par(1, LANE),
                  par(1, ML_WIDTH_PAD)],
        out_specs=(pl.BlockSpec((None, BLK, ML_WIDTH_PAD), lambda b, j: (b, j, 0)),
                   conv_spec, c_spec, n_spec, m_spec),
        scratch_shapes=[pltpu.VMEM((8 + BLK, W_QK), F32),
                        pltpu.VMEM((ML_HEADS, ML_HEAD_PAD, ML_HEAD_PAD), F32),
                        pltpu.VMEM((ML_HEADS, 1, ML_HEAD_PAD), F32),
                        pltpu.VMEM((1, LANE), F32)],
        compiler_params=_cparams(("parallel", "arbitrary")),
        name="mlstm",
    )(p3, p3, p3, p3, p3, conv0, c0, n0, m0, lp["ml_cw"], lp["ml_cb"], lp["ml_bi"], lp["ml_bf"],
      lp["ml_ln_g"])


def _merge_kernel(x_ref, mg_ref, yrw_ref, ys5_ref, yml_ref, bmg_ref, wrw_ref, ws5_ref, wml_ref,
                  wout_ref, g_ref, b_ref, o_ref, *, tm, seq_rows, pad):
    gates = _sigmoid(mg_ref[...] + bmg_ref[...])
    merged = (gates[:, 0:D_MODEL] * _bdot(yrw_ref[...], wrw_ref[...])
              + gates[:, D_MODEL:2 * D_MODEL] * _bdot(ys5_ref[...], ws5_ref[...])
              + gates[:, 2 * D_MODEL:3 * D_MODEL] * _bdot(yml_ref[...], wml_ref[...]))
    out = _bdot(merged, wout_ref[...])
    y = _layer_norm_rows(DN_ALPHA * x_ref[...] + out, g_ref[...], b_ref[...])
    if pad:
        y = jnp.where(_pad_row_mask(pl.program_id(0), tm, seq_rows, pad), 0.0, y)
    o_ref[...] = y


def _merge(x, p, y_rw, y_s5, y_ml, lp, *, tm, seq_rows, pad):
    n = x.shape[0]
    rows = lambda w: pl.BlockSpec((tm, w), lambda i: (i, 0))
    full = lambda r, w: pl.BlockSpec((r, w), lambda i: (0, 0))
    return pl.pallas_call(
        functools.partial(_merge_kernel, tm=tm, seq_rows=seq_rows, pad=pad),
        out_shape=jax.ShapeDtypeStruct((n, D_MODEL), F32),
        grid=(n // tm,),
        in_specs=[rows(D_MODEL), pl.BlockSpec((tm, W_MG), lambda i: (i, OFF_MG // W_MG)),
                  rows(RW_WIDTH), rows(S5_WIDTH), rows(ML_WIDTH_PAD),
                  full(1, W_MG), full(RW_WIDTH, D_MODEL), full(S5_WIDTH, D_MODEL),
                  full(ML_WIDTH_PAD, D_MODEL), full(D_MODEL, D_MODEL),
                  full(1, D_MODEL), full(1, D_MODEL)],
        out_specs=rows(D_MODEL),
        compiler_params=_cparams(("parallel",)),
        name="merge",
    )(x, p, y_rw, y_s5, y_ml, lp["b_merge"], lp["w_br_rw"], lp["w_br_s5"], lp["w_br_ml"],
      lp["w_out"], lp["ln_g"], lp["ln_b"])


def _pad_heads(w):
    lead = w.shape[:-1]
    w = w.reshape(lead + (ML_HEADS, ML_HEAD_DIM))
    w = jnp.pad(w, [(0, 0)] * len(lead) + [(0, 0), (0, ML_HEAD_PAD - ML_HEAD_DIM)])
    return w.reshape(lead + (ML_WIDTH_PAD,))


def _unpad_heads(w):
    lead = w.shape[:-1]
    return w.reshape(lead + (ML_HEADS, ML_HEAD_PAD))[..., :ML_HEAD_DIM].reshape(lead + (ML_WIDTH,))


def _pad_lanes(w, width=LANE):
    return jnp.pad(w, [(0, 0)] * (w.ndim - 1) + [(0, width - w.shape[-1])])


def _qk_pad(w):
    return jnp.concatenate([_pad_heads(w[..., :ML_WIDTH]), _pad_heads(w[..., ML_WIDTH:])], axis=-1)


def _layer_params(l, w_in, rw_mu, rw_w0, rw_w2, rw_a0, rw_a2, rw_kk, rw_ka, rw_rk, rw_ln_g, rw_ln_b,
                  s5_a_re, s5_a_im, s5_b_re, s5_b_im, s5_c_re, s5_c_im, s5_d, s5_log_dt, s5_w_glu,
                  s5_b_glu, ml_conv_w, ml_conv_b, ml_b_if, ml_ln_g, b_merge, w_br_rw, w_br_s5,
                  w_br_ml, w_out, ln_g, ln_b):
    w = w_in[l]
    o = 0
    cols = {}
    for name, size in (("rwc", RW_SHIFT), ("rwg", RW_WIDTH), ("s5", 2 * S5_WIDTH),
                       ("qk", 2 * ML_WIDTH), ("mv", ML_WIDTH), ("mi", ML_HEADS), ("mf", ML_HEADS),
                       ("mo", ML_WIDTH), ("mz", ML_WIDTH), ("mg", 3 * D_MODEL)):
        cols[name] = w[:, o:o + size]
        o += size
    w_all = jnp.concatenate([
        cols["mg"], cols["s5"], _qk_pad(cols["qk"]),
        _pad_heads(cols["mo"]), _pad_heads(cols["mz"]), _pad_heads(cols["mv"]),
        cols["rwc"][:, :RW_RKV], cols["rwg"], cols["rwc"][:, RW_RKV:],
        _pad_lanes(cols["mi"]), _pad_lanes(cols["mf"])], axis=1).astype(BF16)
    row = lambda a: a.reshape(1, -1)
    zeros_lora = jnp.zeros((RW_LORA, RW_WIDTH), F32)
    lane = jnp.arange(LANE)
    ones_blk = (lane[:, None] // RW_HEAD_DIM == lane[None, :] // RW_HEAD_DIM).astype(F32)

    bb_re, bb_im, lv_re, lv_im, pw_re, pw_im = _s5_prep(
        s5_a_re[l], s5_a_im[l], s5_log_dt[l].reshape(S5_GROUPS, 1),
        jnp.swapaxes(s5_b_re[l], 1, 2), jnp.swapaxes(s5_b_im[l], 1, 2))
    eye8 = jnp.eye(8, dtype=F32)
    blocked = lambda a: a.reshape((S5_NBLK, 8) + a.shape[1:])
    bb = jnp.stack([blocked(bb_re), blocked(bb_im)])
    wb = jnp.einsum("cjghp,gk->jghckp", bb, eye8).reshape(S5_NBLK, LANE, 2 * S5_BLK_STATE)
    cc = jnp.stack([blocked(s5_c_re[l]), -blocked(s5_c_im[l])])
    wc = jnp.einsum("cjghp,gk->jcgpkh", cc, eye8).reshape(S5_NBLK, 2 * S5_BLK_STATE, LANE)
    state_rows = lambda re, im: jnp.stack(
        [re.reshape(re.shape[:-2] + (S5_NBLK, S5_BLK_STATE)),
         im.reshape(im.shape[:-2] + (S5_NBLK, S5_BLK_STATE))], axis=-2)
    lv = state_rows(lv_re, lv_im).reshape(SCAN_LEVELS, 2 * S5_NBLK, S5_BLK_STATE)
    pw = jnp.moveaxis(state_rows(pw_re, pw_im).reshape(BLK, 2 * S5_NBLK, S5_BLK_STATE), 0, 1)

    return dict(
        w_all=w_all,
        mu_rkv=row(rw_mu[l][:RW_RKV]), mu_wa=row(rw_mu[l][RW_RKV:]),
        rw_w0=row(rw_w0[l]), rw_a0=row(rw_a0[l]),
        rw_w2p=jnp.concatenate([rw_w2[l], zeros_lora], axis=0),
        rw_a2p=jnp.concatenate([zeros_lora, rw_a2[l]], axis=0),
        rw_kk=row(rw_kk[l]), rw_ka=row(rw_ka[l]), rw_rk=row(rw_rk[l]),
        rw_ln_g=row(rw_ln_g[l]), rw_ln_b=row(rw_ln_b[l]), ones_blk=ones_blk,
        s5_wb=wb.astype(BF16), s5_wc=wc.astype(BF16), s5_lv=lv, s5_pw=pw,
        s5_d=row(s5_d[l]), s5_w_glu=s5_w_glu[l].astype(BF16), s5_b_glu=row(s5_b_glu[l]),
        ml_cw=_qk_pad(ml_conv_w[l]), ml_cb=row(_qk_pad(ml_conv_b[l])),
        ml_bi=row(_pad_lanes(ml_b_if[l][:ML_HEADS])), ml_bf=row(_pad_lanes(ml_b_if[l][ML_HEADS:])),
        ml_ln_g=row(_pad_heads(ml_ln_g[l])),
        b_merge=row(b_merge[l]), w_br_rw=w_br_rw[l].astype(BF16), w_br_s5=w_br_s5[l].astype(BF16),
        w_br_ml=jnp.pad(w_br_ml[l].reshape(ML_HEADS, ML_HEAD_DIM, D_MODEL),
                        ((0, 0), (0, ML_HEAD_PAD - ML_HEAD_DIM), (0, 0))
                        ).reshape(ML_WIDTH_PAD, D_MODEL).astype(BF16),
        w_out=w_out[l].astype(BF16), ln_g=row(ln_g[l]), ln_b=row(ln_b[l]))


def _pack_wkv(s):
    b = s.shape[0]
    s = s.reshape(b, RW_PAIRS, 2, RW_HEAD_DIM, RW_HEAD_DIM)
    return jnp.swapaxes(s, 2, 3).reshape(b, RW_PAIRS, RW_HEAD_DIM, LANE)


def _unpack_wkv(s):
    b = s.shape[0]
    s = s.reshape(b, RW_PAIRS, RW_HEAD_DIM, 2, RW_HEAD_DIM)
    return jnp.swapaxes(s, 2, 3).reshape(b, RW_HEADS, RW_HEAD_DIM, RW_HEAD_DIM)


def _pack_s5(re, im):
    b = re.shape[0]
    return jnp.stack([re.reshape(b, S5_NBLK, S5_BLK_STATE), im.reshape(b, S5_NBLK, S5_BLK_STATE)],
                     axis=2).reshape(b, 2 * S5_NBLK, S5_BLK_STATE)


def _unpack_s5(h):
    b = h.shape[0]
    h = h.reshape(b, S5_NBLK, 2, S5_BLK_STATE)
    return (h[:, :, 0].reshape(b, S5_GROUPS, S5_STATE), h[:, :, 1].reshape(b, S5_GROUPS, S5_STATE))


def _row_tile(n, seq_rows, target):
    best = 8
    for t in range(8, min(n, target) + 1, 8):
        if seq_rows % t == 0 or (t % seq_rows == 0 and n % t == 0):
            best = t
    return best


def _trunk_layer(x, st, lp, *, bsz, seq_rows, pad, bb_n, s5_tb, tm_proj, tm_merge):
    rw_shift0, rw_wkv0, s5_re0, s5_im0, ml_conv0, ml_c0, ml_n0, ml_m0 = st
    p = _proj(x, lp["w_all"], tm=tm_proj)
    p3 = p.reshape(bsz, seq_rows, P_COLS)

    prep = _rw_prep(p3, rw_shift0[:, None, :RW_RKV], rw_shift0[:, None, RW_RKV:], lp)
    y_rw, wkv1 = _rw_scan(prep[:7], p3, _pack_wkv(rw_wkv0), lp, bb_n=bb_n)
    rw_shift1 = jnp.concatenate([prep[7][:, 0], prep[8][:, 0]], axis=-1)

    y_s5, h1 = _s5(p3, _pack_s5(s5_re0, s5_im0), lp, tb=s5_tb)
    s5_re1, s5_im1 = _unpack_s5(h1)

    y_ml, conv1, c1, n1, m1 = _mlstm(
        p3, _qk_pad(ml_conv0), ml_c0, ml_n0[:, :, None, :], _pad_lanes(ml_m0)[:, None, :], lp, pad=pad)
    ml_conv1 = jnp.concatenate([_unpad_heads(conv1[..., :ML_WIDTH_PAD]),
                                _unpad_heads(conv1[..., ML_WIDTH_PAD:])], axis=-1)

    n = bsz * seq_rows
    x_new = _merge(x, p, y_rw.reshape(n, RW_WIDTH), y_s5.reshape(n, S5_WIDTH),
                   y_ml.reshape(n, ML_WIDTH_PAD), lp, tm=tm_merge, seq_rows=seq_rows, pad=pad)
    return x_new, (rw_shift1, _unpack_wkv(wkv1), s5_re1, s5_im1, ml_conv1, c1, n1[:, :, 0, :],
                   m1[:, 0, :ML_HEADS])


def _run_group(x_rows, states, lps, *, bsz, seq_rows, pad, in_ln_g, in_ln_b):
    n = bsz * seq_rows
    bb_n = 2 if bsz % 2 == 0 else 1
    s5_tb = max(t for t in (BLK, 5 * BLK) if seq_rows % t == 0)
    tm_ln = _row_tile(n, seq_rows, 1024)
    tm_proj = _row_tile(n, seq_rows, 2080)
    tm_merge = _row_tile(n, seq_rows, 320)
    x = _ln_in(x_rows, in_ln_g, in_ln_b, tm=tm_ln, seq_rows=seq_rows, pad=pad)
    outs = []
    for l in range(DEPTH):
        x, st = _trunk_layer(x, states[l], lps[l], bsz=bsz, seq_rows=seq_rows, pad=pad, bb_n=bb_n,
                             s5_tb=s5_tb, tm_proj=tm_proj, tm_merge=tm_merge)
        outs.append(st)
    return x, outs


def kernel(x_prompt, x_sample, state_rwkv_shift, state_rwkv_wkv, state_s5_re, state_s5_im, state_mlstm_conv, state_mlstm_c, state_mlstm_n, state_mlstm_m, meta, in_ln_g, in_ln_b, w_in, rw_mu, rw_w0, rw_w2, rw_a0, rw_a2, rw_kk, rw_ka, rw_rk, rw_ln_g, rw_ln_b, s5_a_re, s5_a_im, s5_b_re, s5_b_im, s5_c_re, s5_c_im, s5_d, s5_log_dt, s5_w_glu, s5_b_glu, ml_conv_w, ml_conv_b, ml_b_if, ml_ln_g, b_merge, w_br_rw, w_br_s5, w_br_ml, w_out, ln_g, ln_b):
    lps = [_layer_params(l, w_in, rw_mu, rw_w0, rw_w2, rw_a0, rw_a2, rw_kk, rw_ka, rw_rk, rw_ln_g,
                         rw_ln_b, s5_a_re, s5_a_im, s5_b_re, s5_b_im, s5_c_re, s5_c_im, s5_d,
                         s5_log_dt, s5_w_glu, s5_b_glu, ml_conv_w, ml_conv_b, ml_b_if, ml_ln_g,
                         b_merge, w_br_rw, w_br_s5, w_br_ml, w_out, ln_g, ln_b)
           for l in range(DEPTH)]
    g_in, b_in = in_ln_g.reshape(1, D_MODEL), in_ln_b.reshape(1, D_MODEL)

    bp, sp = x_prompt.shape[0], x_prompt.shape[1]
    lp_rows = PAD + N_META + sp
    xp = jnp.concatenate([jnp.zeros((bp, PAD, D_MODEL), F32),
                          jnp.broadcast_to(meta[None], (bp, N_META, D_MODEL)), x_prompt], axis=1)
    z = lambda *shape: jnp.zeros((bp,) + shape, F32)
    zero_state = (z(RW_SHIFT), z(RW_HEADS, RW_HEAD_DIM, RW_HEAD_DIM), z(S5_GROUPS, S5_STATE),
                  z(S5_GROUPS, S5_STATE), z(ML_CONV - 1, 2 * ML_WIDTH),
                  z(ML_HEADS, ML_HEAD_DIM, ML_HEAD_DIM), z(ML_HEADS, ML_HEAD_DIM), z(ML_HEADS))
    yp, p_states = _run_group(xp.reshape(bp * lp_rows, D_MODEL), [zero_state] * DEPTH, lps,
                              bsz=bp, seq_rows=lp_rows, pad=PAD, in_ln_g=g_in, in_ln_b=b_in)
    y_prompt = yp.reshape(bp, lp_rows, D_MODEL)[:, PAD + N_META:]

    bs, ds = x_sample.shape[0], x_sample.shape[1]
    s_in = [(state_rwkv_shift[l], state_rwkv_wkv[l], state_s5_re[l], state_s5_im[l],
             state_mlstm_conv[l], state_mlstm_c[l], state_mlstm_n[l], state_mlstm_m[l])
            for l in range(DEPTH)]
    ys, s_states = _run_group(x_sample.reshape(bs * ds, D_MODEL), s_in, lps,
                              bsz=bs, seq_rows=ds, pad=0, in_ln_g=g_in, in_ln_b=b_in)
    y_sample = ys.reshape(bs, ds, D_MODEL)

    stack = lambda sts: tuple(jnp.stack(s, 0) for s in zip(*sts))
    return (y_prompt, y_sample) + stack(p_states) + stack(s_states)
```

```python
import functools
import math

import jax
import jax.numpy as jnp
from jax import lax
from jax.experimental import pallas as pl
from jax.experimental.pallas import tpu as pltpu

F32 = jnp.float32
BF16 = jnp.bfloat16

D_MODEL = 1024
DEPTH = 2
N_META = 16
RW_HEADS = 12
RW_HEAD_DIM = 64
RW_WIDTH = RW_HEADS * RW_HEAD_DIM
RW_PAIRS = RW_HEADS // 2
RW_LORA = 64
RW_RKV = 3 * RW_WIDTH
RW_SHIFT = RW_RKV + 2 * RW_LORA
S5_GROUPS = 32
S5_GROUP_CH = 16
S5_WIDTH = S5_GROUPS * S5_GROUP_CH
S5_STATE = 64
S5_NBLK = 4
S5_BLK_STATE = 512
ML_HEADS = 4
ML_HEAD_DIM = 192
ML_HEAD_PAD = 256
ML_WIDTH = ML_HEADS * ML_HEAD_DIM
ML_WIDTH_PAD = ML_HEADS * ML_HEAD_PAD
ML_CONV = 4
DN_ALPHA = (2 * DEPTH) ** 0.25
LN_EPS = 1e-5
RW_GN_EPS = 64e-5

LANE = 128
SUBLANES = 8
BLK = 64
PAD = BLK - N_META
SCAN_LEVELS = 6
SCAN_HALO = 32

OFF_MG, W_MG = 0, 3 * D_MODEL
OFF_S5, W_S5 = 3072, 2 * S5_WIDTH
OFF_QK, W_QK = 4096, 2 * ML_WIDTH_PAD
OFF_OZ, W_OZ = 6144, 2 * ML_WIDTH_PAD
OFF_MV, W_MV = 8192, ML_WIDTH_PAD
OFF_RKV, W_RKV = 9216, RW_RKV
OFF_RWG, W_RWG = 11520, RW_WIDTH
OFF_WA, W_WA = 12288, LANE
OFF_MI, W_MI = 12416, LANE
OFF_MF, W_MF = 12544, LANE
P_COLS = 12672
P_TN = 1152

VMEM_LIMIT = 56 * 1024 * 1024


def _cparams(sem):
    return pltpu.CompilerParams(dimension_semantics=sem, vmem_limit_bytes=VMEM_LIMIT)


def _bdot(a, b):
    return jnp.dot(a.astype(BF16), b.astype(BF16), preferred_element_type=F32)


def _hdot(a, b):
    return jnp.dot(a, b, precision=lax.Precision.HIGHEST, preferred_element_type=F32)


def _sigmoid(x):
    return 1.0 / (1.0 + jnp.exp(-x))


def _silu(x):
    return x * _sigmoid(x)


def _softplus(x):
    return jnp.maximum(x, 0.0) + jnp.log1p(jnp.exp(-jnp.abs(x)))


def _pad_row_mask(tile_idx, tm, seq_rows, pad):
    pos0 = lax.rem(tile_idx * tm, seq_rows)
    row = lax.broadcasted_iota(jnp.int32, (tm, 1), 0) + pos0
    return row < pad


def _layer_norm_rows(x, g, b):
    mu = jnp.mean(x, axis=-1, keepdims=True)
    d = x - mu
    var = jnp.mean(d * d, axis=-1, keepdims=True)
    return d * lax.rsqrt(var + LN_EPS) * g + b


def _ln_in_kernel(x_ref, g_ref, b_ref, o_ref, *, tm, seq_rows, pad):
    y = _layer_norm_rows(x_ref[...], g_ref[...], b_ref[...])
    if pad:
        y = jnp.where(_pad_row_mask(pl.program_id(0), tm, seq_rows, pad), 0.0, y)
    o_ref[...] = y


def _ln_in(x, g, b, *, tm, seq_rows, pad):
    n = x.shape[0]
    return pl.pallas_call(
        functools.partial(_ln_in_kernel, tm=tm, seq_rows=seq_rows, pad=pad),
        out_shape=jax.ShapeDtypeStruct((n, D_MODEL), F32),
        grid=(n // tm,),
        in_specs=[pl.BlockSpec((tm, D_MODEL), lambda i: (i, 0)),
                  pl.BlockSpec((1, D_MODEL), lambda i: (0, 0)),
                  pl.BlockSpec((1, D_MODEL), lambda i: (0, 0))],
        out_specs=pl.BlockSpec((tm, D_MODEL), lambda i: (i, 0)),
        compiler_params=_cparams(("parallel",)),
        name="ln_in",
    )(x, g, b)


def _proj_kernel(x_ref, w_ref, o_ref):
    o_ref[...] = jnp.dot(x_ref[...].astype(BF16), w_ref[...], preferred_element_type=F32)


def _proj(x, w_all, *, tm):
    n = x.shape[0]
    return pl.pallas_call(
        _proj_kernel,
        out_shape=jax.ShapeDtypeStruct((n, P_COLS), F32),
        grid=(n // tm, P_COLS // P_TN),
        in_specs=[pl.BlockSpec((tm, D_MODEL), lambda i, j: (i, 0)),
                  pl.BlockSpec((D_MODEL, P_TN), lambda i, j: (0, j))],
        out_specs=pl.BlockSpec((tm, P_TN), lambda i, j: (i, j)),
        compiler_params=_cparams(("parallel", "arbitrary")),
        name="proj",
    )(x, w_all)


def _head_sum(x, ones_blk):
    parts = [_hdot(x[:, i * LANE:(i + 1) * LANE], ones_blk) for i in range(RW_WIDTH // LANE)]
    return jnp.concatenate(parts, axis=-1)


def _shift_rows(u, carry):
    rolled = pltpu.roll(u, 1, 0)
    row = lax.broadcasted_iota(jnp.int32, u.shape, 0)
    return jnp.where(row == 0, carry, rolled)


def _rw_prep_kernel(rkv_ref, wa_ref, sh0_rkv_ref, sh0_wa_ref, mu_rkv_ref, mu_wa_ref,
                    w0_ref, w2_ref, a0_ref, a2_ref, kk_ref, ka_ref, rk_ref, ones_ref,
                    r_out, k_out, v_out, nkk_out, kka_out, w_out, bon_out, sh_rkv_out, sh_wa_out,
                    c_rkv, c_wa):
    j = pl.program_id(1)

    @pl.when(j == 0)
    def _():
        c_rkv[...] = sh0_rkv_ref[...]
        c_wa[...] = sh0_wa_ref[...]

    u = rkv_ref[...]
    uw = wa_ref[...]
    xs = u + (_shift_rows(u, c_rkv[...]) - u) * mu_rkv_ref[...]
    xwa = uw + (_shift_rows(uw, c_wa[...]) - uw) * mu_wa_ref[...]
    c_rkv[...] = u[BLK - 1:BLK, :]
    c_wa[...] = uw[BLK - 1:BLK, :]
    sh_rkv_out[...] = u[BLK - 1:BLK, :]
    sh_wa_out[...] = uw[BLK - 1:BLK, :]

    r = xs[:, 0:RW_WIDTH]
    k = xs[:, RW_WIDTH:2 * RW_WIDTH]
    v = xs[:, 2 * RW_WIDTH:3 * RW_WIDTH]
    w_log = -_softplus(-(w0_ref[...] + _bdot(jnp.tanh(xwa), w2_ref[...]))) - 0.5
    decay = jnp.exp(-jnp.exp(w_log))
    a = _sigmoid(a0_ref[...] + _bdot(xwa, a2_ref[...]))
    ones_blk = ones_ref[...]
    kk = k * kk_ref[...]
    kk = kk * lax.rsqrt(_head_sum(kk * kk, ones_blk) + 1e-12)
    k = k * (1.0 + (a - 1.0) * ka_ref[...])
    r_out[...] = r
    k_out[...] = k
    v_out[...] = v
    nkk_out[...] = -kk
    kka_out[...] = kk * a
    w_out[...] = decay
    bon_out[...] = _head_sum(r * k * rk_ref[...], ones_blk) * v


def _rw_prep(p3, sh0_rkv, sh0_wa, lp):
    bsz, seq_rows, _ = p3.shape
    nblk = seq_rows // BLK
    row_spec = pl.BlockSpec((None, BLK, RW_WIDTH), lambda b, j: (b, j, 0))
    par = lambda w: pl.BlockSpec((1, w), lambda b, j: (0, 0))
    mat = pl.BlockSpec((LANE, RW_WIDTH), lambda b, j: (0, 0))
    st = lambda w: pl.BlockSpec((None, 1, w), lambda b, j: (b, 0, 0))
    row_shape = jax.ShapeDtypeStruct((bsz, seq_rows, RW_WIDTH), F32)
    return pl.pallas_call(
        _rw_prep_kernel,
        out_shape=(row_shape,) * 7 + (jax.ShapeDtypeStruct((bsz, 1, RW_RKV), F32),
                                      jax.ShapeDtypeStruct((bsz, 1, LANE), F32)),
        grid=(bsz, nblk),
        in_specs=[pl.BlockSpec((None, BLK, W_RKV), lambda b, j: (b, j, OFF_RKV // W_RKV)),
                  pl.BlockSpec((None, BLK, W_WA), lambda b, j: (b, j, OFF_WA // W_WA)),
                  st(RW_RKV), st(LANE), par(RW_RKV), par(LANE),
                  par(RW_WIDTH), mat, par(RW_WIDTH), mat,
                  par(RW_WIDTH), par(RW_WIDTH), par(RW_WIDTH),
                  pl.BlockSpec((LANE, LANE), lambda b, j: (0, 0))],
        out_specs=(row_spec,) * 7 + (st(RW_RKV), st(LANE)),
        scratch_shapes=[pltpu.VMEM((1, RW_RKV), F32), pltpu.VMEM((1, LANE), F32)],
        compiler_params=_cparams(("parallel", "arbitrary")),
        name="rw_prep",
    )(p3, p3, sh0_rkv, sh0_wa, lp["mu_rkv"], lp["mu_wa"], lp["rw_w0"], lp["rw_w2p"],
      lp["rw_a0"], lp["rw_a2p"], lp["rw_kk"], lp["rw_ka"], lp["rw_rk"], lp["ones_blk"])


def _rw_scan_kernel(r_ref, k_ref, v_ref, nkk_ref, kka_ref, w_ref, bon_ref, gate_ref, s0_ref,
                    lng_ref, lnb_ref, ones_ref, ones2_ref, y_ref, s_out, s_scr, y_scr, *, bb_n):
    j = pl.program_id(1)

    @pl.when(j == 0)
    def _():
        s_scr[...] = s0_ref[...]

    lane = lax.broadcasted_iota(jnp.int32, (RW_HEAD_DIM, LANE), 1)
    sub = lax.broadcasted_iota(jnp.int32, (RW_HEAD_DIM, LANE), 0)
    lo = lane < RW_HEAD_DIM
    diag = (lane & (RW_HEAD_DIM - 1)) == sub
    sub8 = lax.broadcasted_iota(jnp.int32, (SUBLANES, LANE), 0)
    ones2 = ones2_ref[...]

    def split_lhs(hi_f32, lo_f32):
        return jnp.concatenate([hi_f32.astype(BF16), lo_f32.astype(BF16)], axis=1)

    def seg_lhs(x):
        hi = x.astype(BF16).astype(F32)
        return split_lhs(hi, x - hi)

    def seg_dot(lhs_tiles):
        out = jnp.dot(jnp.concatenate(lhs_tiles, axis=0), ones2, preferred_element_type=F32)
        return [out[c * RW_HEAD_DIM:(c + 1) * RW_HEAD_DIM, :] for c in range(len(lhs_tiles))]

    chains = [(bb, p) for bb in range(bb_n) for p in range(RW_PAIRS)]

    def row_group(g, carry):
        base = pl.multiple_of(g * SUBLANES, SUBLANES)
        tiles = []
        for bb, p in chains:
            sl = slice(p * LANE, (p + 1) * LANE)
            nkk8, v8, w8, kka8, k8, r8 = (ref[bb, pl.ds(base, SUBLANES), sl] for ref in
                                          (nkk_ref, v_ref, w_ref, kka_ref, k_ref, r_ref))
            v8h = v8.astype(BF16).astype(F32)
            tiles.append((nkk8, v8h, v8 - v8h, w8, kka8, k8, r8))
        y8 = [jnp.zeros((SUBLANES, LANE), F32) for _ in chains]
        for i in range(SUBLANES):
            rows = [tuple(x[i:i + 1, :] for x in t) for t in tiles]
            vcol = seg_dot([split_lhs(jnp.where(diag, vh_t, 0.0), jnp.where(diag, vl_t, 0.0))
                            for _, vh_t, vl_t, _, _, _, _ in rows])
            s_old = [s_scr[bb, p] for bb, p in chains]
            sa = seg_dot([seg_lhs(s * row[0]) for s, row in zip(s_old, rows)])
            s_new = []
            for c, (bb, p) in enumerate(chains):
                _, _, _, w_t, kka_t, k_t, _ = rows[c]
                s = s_old[c] * w_t + sa[c] * kka_t + vcol[c] * k_t
                s_scr[bb, p] = s
                s_new.append(s)
            for c in range(len(chains)):
                sr = s_new[c] * rows[c][6]
                y_lo = jnp.sum(jnp.where(lo, sr, 0.0), axis=1, keepdims=True)
                y_hi = jnp.sum(jnp.where(lo, 0.0, sr), axis=1, keepdims=True)
                ycol = jnp.where(diag, jnp.where(lo, y_lo, y_hi), 0.0)
                y8[c] = jnp.where(sub8 == i, jnp.sum(ycol, axis=0, keepdims=True), y8[c])
        for c, (bb, p) in enumerate(chains):
            y_scr[bb, pl.ds(base, SUBLANES), p * LANE:(p + 1) * LANE] = y8[c]
        return carry

    lax.fori_loop(0, BLK // SUBLANES, row_group, 0)

    ones_blk = ones_ref[...]
    inv = 1.0 / RW_HEAD_DIM
    for bb in range(bb_n):
        y = y_scr[bb]
        mu = _head_sum(y, ones_blk) * inv
        d = y - mu
        var = _head_sum(d * d, ones_blk) * inv
        yn = d * lax.rsqrt(var + RW_GN_EPS) * lng_ref[...] + lnb_ref[...]
        y_ref[bb] = (yn + bon_ref[bb]) * _silu(gate_ref[bb])

    @pl.when(j == pl.num_programs(1) - 1)
    def _():
        s_out[...] = s_scr[...]


def _rw_scan(prep, p3, s0, lp, *, bb_n):
    r, k, v, nkk, kka, w, bon = prep
    bsz, seq_rows, _ = r.shape
    nblk = seq_rows // BLK
    row_spec = pl.BlockSpec((bb_n, BLK, RW_WIDTH), lambda b, j: (b, j, 0))
    par = pl.BlockSpec((1, RW_WIDTH), lambda b, j: (0, 0))
    st_spec = pl.BlockSpec((bb_n, RW_PAIRS, RW_HEAD_DIM, LANE), lambda b, j: (b, 0, 0, 0))
    return pl.pallas_call(
        functools.partial(_rw_scan_kernel, bb_n=bb_n),
        out_shape=(jax.ShapeDtypeStruct((bsz, seq_rows, RW_WIDTH), F32),
                   jax.ShapeDtypeStruct((bsz, RW_PAIRS, RW_HEAD_DIM, LANE), F32)),
        grid=(bsz // bb_n, nblk),
        in_specs=[row_spec] * 7
        + [pl.BlockSpec((bb_n, BLK, W_RWG), lambda b, j: (b, j, OFF_RWG // W_RWG)),
           st_spec, par, par, pl.BlockSpec((LANE, LANE), lambda b, j: (0, 0)),
           pl.BlockSpec((2 * LANE, LANE), lambda b, j: (0, 0))],
        out_specs=(row_spec, st_spec),
        scratch_shapes=[pltpu.VMEM((bb_n, RW_PAIRS, RW_HEAD_DIM, LANE), F32),
                        pltpu.VMEM((bb_n, BLK, RW_WIDTH), F32)],
        compiler_params=_cparams(("parallel", "arbitrary")),
        name="rw_scan",
    )(r, k, v, nkk, kka, w, bon, p3, s0, lp["rw_ln_g"], lp["rw_ln_b"], lp["ones_blk"],
      jnp.concatenate([lp["ones_blk"], lp["ones_blk"]], axis=0).astype(BF16))


def _cmul(ar, ai, br, bi):
    return ar * br - ai * bi, ar * bi + ai * br


def _s5_prep_kernel(are_ref, aim_ref, ldt_ref, bre_ref, bim_ref,
                    bbre_out, bbim_out, lvre_out, lvim_out, pwre_out, pwim_out):
    ar, ai = are_ref[...], aim_ref[...]
    dt = jnp.exp(ldt_ref[...])
    mag = jnp.exp(ar * dt)
    lr, li = mag * jnp.cos(ai * dt), mag * jnp.sin(ai * dt)
    nr, ni = lr - 1.0, li
    den = ar * ar + ai * ai
    qr, qi = (nr * ar + ni * ai) / den, (ni * ar - nr * ai) / den
    br, bi = bre_ref[...], bim_ref[...]
    bbr, bbi = _cmul(qr[:, None, :], qi[:, None, :], br, bi)
    bbre_out[...] = bbr
    bbim_out[...] = bbi
    sq_r, sq_i = lr, li
    pows = [(lr, li)]
    for lvl in range(SCAN_LEVELS):
        lvre_out[lvl] = sq_r
        lvim_out[lvl] = sq_i
        pows = pows + [_cmul(pr, pi, sq_r, sq_i) for pr, pi in pows]
        sq_r, sq_i = _cmul(sq_r, sq_i, sq_r, sq_i)
    for t in range(BLK):
        pwre_out[t] = pows[t][0]
        pwim_out[t] = pows[t][1]


def _s5_prep(a_re, a_im, log_dt, b_re_t, b_im_t):
    ghp =jax.ShapeDtypeStruct((S5_GROUPS, S5_GROUP_CH, S5_STATE), F32)
    lv = jax.ShapeDtypeStruct((SCAN_LEVELS, S5_GROUPS, S5_STATE), F32)
    pw = jax.ShapeDtypeStruct((BLK, S5_GROUPS, S5_STATE), F32)
    return pl.pallas_call(
        _s5_prep_kernel,
        out_shape=(ghp, ghp, lv, lv, pw, pw),
        name="s5_prep",
    )(a_re, a_im, log_dt, b_re_t, b_im_t)


def _gelu_tanh(x):
    return 0.5 * x * (1.0 + jnp.tanh(math.sqrt(2.0 / math.pi) * (x + 0.044715 * (x * x * x))))


def _s5_kernel(p_ref, h0_ref, wb_ref, wc_ref, lv_ref, pw_ref, d_ref, wglu_ref, bglu_ref,
               y_ref, h_out, h_scr, xr_scr, xi_scr, hs_scr, *, tb):
    j = pl.program_id(1)

    @pl.when(j == 0)
    def _():
        h_scr[...] = h0_ref[...]

    zeros_halo = jnp.zeros((SCAN_HALO, S5_BLK_STATE), F32)
    xr_scr[0:SCAN_HALO, :] = zeros_halo
    xi_scr[0:SCAN_HALO, :] = zeros_halo

    u = p_ref[:, 0:S5_WIDTH]
    gate = p_ref[:, S5_WIDTH:2 * S5_WIDTH]
    y_parts = []
    for jb in range(S5_NBLK):
        bu = _bdot(u[:, jb * LANE:(jb + 1) * LANE], wb_ref[jb])
        for s in range(tb // BLK):
            xr = bu[s * BLK:(s + 1) * BLK, 0:S5_BLK_STATE]
            xi = bu[s * BLK:(s + 1) * BLK, S5_BLK_STATE:2 * S5_BLK_STATE]
            for lvl in range(SCAN_LEVELS):
                d = 1 << lvl
                xr_scr[SCAN_HALO:SCAN_HALO + BLK, :] = xr
                xi_scr[SCAN_HALO:SCAN_HALO + BLK, :] = xi
                sr = xr_scr[SCAN_HALO - d:SCAN_HALO - d + BLK, :]
                si = xi_scr[SCAN_HALO - d:SCAN_HALO - d + BLK, :]
                lr = lv_ref[lvl, 2 * jb:2 * jb + 1, :]
                li = lv_ref[lvl, 2 * jb + 1:2 * jb + 2, :]
                xr, xi = xr + (lr * sr - li * si), xi + (lr * si + li * sr)
            h_r = h_scr[2 * jb:2 * jb + 1, :]
            h_i = h_scr[2 * jb + 1:2 * jb + 2, :]
            pr, pi = pw_ref[2 * jb], pw_ref[2 * jb + 1]
            hr = xr + (pr * h_r - pi * h_i)
            hi = xi + (pr * h_i + pi * h_r)
            h_scr[2 * jb:2 * jb + 1, :] = hr[BLK - 1:BLK, :]
            h_scr[2 * jb + 1:2 * jb + 2, :] = hi[BLK - 1:BLK, :]
            hs_scr[s * BLK:(s + 1) * BLK, 0:S5_BLK_STATE] = hr
            hs_scr[s * BLK:(s + 1) * BLK, S5_BLK_STATE:2 * S5_BLK_STATE] = hi
        y_parts.append(_bdot(hs_scr[...], wc_ref[jb]))
    y = jnp.concatenate(y_parts, axis=-1) + d_ref[...] * u
    y = _gelu_tanh(y)
    y = y * _sigmoid(_bdot(y, wglu_ref[...]) + bglu_ref[...])
    y_ref[...] = y * _silu(gate)

    @pl.when(j == pl.num_programs(1) - 1)
    def _():
        h_out[...] = h_scr[...]


def _s5(p3, h0, lp, *, tb):
    bsz, seq_rows, _ = p3.shape
    st_spec = pl.BlockSpec((None, 2 * S5_NBLK, S5_BLK_STATE), lambda b, j: (b, 0, 0))
    full = lambda shape: pl.BlockSpec(shape, lambda b, j: (0,) * len(shape))
    return pl.pallas_call(
        functools.partial(_s5_kernel, tb=tb),
        out_shape=(jax.ShapeDtypeStruct((bsz, seq_rows, S5_WIDTH), F32),
                   jax.ShapeDtypeStruct((bsz, 2 * S5_NBLK, S5_BLK_STATE), F32)),
        grid=(bsz, seq_rows // tb),
        in_specs=[pl.BlockSpec((None, tb, W_S5), lambda b, j: (b, j, OFF_S5 // W_S5)),
                  st_spec,
                  full((S5_NBLK, LANE, 2 * S5_BLK_STATE)),
                  full((S5_NBLK, 2 * S5_BLK_STATE, LANE)),
                  full((SCAN_LEVELS, 2 * S5_NBLK, S5_BLK_STATE)),
                  full((2 * S5_NBLK, BLK, S5_BLK_STATE)),
                  full((1, S5_WIDTH)), full((S5_WIDTH, S5_WIDTH)), full((1, S5_WIDTH))],
        out_specs=(pl.BlockSpec((None, tb, S5_WIDTH), lambda b, j: (b, j, 0)), st_spec),
        scratch_shapes=[pltpu.VMEM((2 * S5_NBLK, S5_BLK_STATE), F32),
                        pltpu.VMEM((SCAN_HALO + BLK, S5_BLK_STATE), F32),
                        pltpu.VMEM((SCAN_HALO + BLK, S5_BLK_STATE), F32),
                        pltpu.VMEM((tb, 2 * S5_BLK_STATE), F32)],
        compiler_params=_cparams(("parallel", "arbitrary")),
        name="s5",
    )(p3, h0, lp["s5_wb"], lp["s5_wc"], lp["s5_lv"], lp["s5_pw"], lp["s5_d"],
      lp["s5_w_glu"], lp["s5_b_glu"])


def _mlstm_kernel(qk_ref, oz_ref, v_ref, gi_ref, gf_ref, conv0_ref, c0_ref, n0_ref, m0_ref,
                  cw_ref, cb_ref, bi_ref, bf_ref, lng_ref,
                  y_ref, conv_out, c_out, n_out, m_out,
                  xp_scr, c_scr, n_scr, m_scr, *, pad):
    j = pl.program_id(1)
    halo = 8

    @pl.when(j == 0)
    def _():
        xp_scr[0:halo, :] = jnp.zeros((halo, W_QK), F32)
        xp_scr[halo - (ML_CONV - 1):halo, :] = conv0_ref[...]
        c_scr[...] = jnp.zeros(c_scr.shape, F32)
        c_scr[:, 0:ML_HEAD_DIM, 0:ML_HEAD_DIM] = c0_ref[...]
        n_scr[...] = jnp.zeros(n_scr.shape, F32)
        n_scr[:, :, 0:ML_HEAD_DIM] = n0_ref[...]
        m_scr[...] = m0_ref[...]

    xp_scr[halo:halo + BLK, :] = qk_ref[...]
    conv = cb_ref[...] + xp_scr[halo - 3:halo - 3 + BLK, :] * cw_ref[0:1, :]
    for tap in range(1, ML_CONV):
        conv = conv + xp_scr[halo - 3 + tap:halo - 3 + tap + BLK, :] * cw_ref[tap:tap + 1, :]
    tail = xp_scr[halo + BLK - (ML_CONV - 1):halo + BLK, :]
    xp_scr[halo - (ML_CONV - 1):halo, :] = tail
    conv_out[...] = tail
    act = _silu(conv)
    q = act[:, 0:ML_WIDTH_PAD]
    k = act[:, ML_WIDTH_PAD:2 * ML_WIDTH_PAD] * (1.0 / math.sqrt(ML_HEAD_DIM))
    v = v_ref[...]

    t0 = jnp.where(j == 0, pad, 0) if pad else 0
    row1 = lax.broadcasted_iota(jnp.int32, (BLK, 1), 0)
    row_ok = row1 >= t0
    ti = lax.broadcasted_iota(jnp.int32, (BLK, BLK), 0)
    si = lax.broadcasted_iota(jnp.int32, (BLK, BLK), 1)
    pair_ok = (si <= ti) & (si >= t0)
    tril = jnp.where(si <= ti, 1.0, 0.0)
    eye = jnp.where(si == ti, 1.0, 0.0)
    ones_sq = jnp.ones((BLK, BLK), F32)

    log_i = gi_ref[...] + bi_ref[...]
    f_pre = gf_ref[...] + bf_ref[...]
    log_f = jnp.where(row_ok, -_softplus(-f_pre), 0.0)
    b = _hdot(tril, log_f)
    m_prev = m_scr[...]
    g = b + m_prev
    b_end = b[BLK - 1:BLK, :]
    e_log = jnp.where(row_ok, b_end - b + log_i, -jnp.inf)
    m_new = jnp.maximum(b_end + m_prev, jnp.max(e_log, axis=0, keepdims=True))
    we = jnp.exp(e_log - m_new)
    keep = jnp.exp(b_end + m_prev - m_new)
    m_scr[...] = m_new

    lane_ok = lax.broadcasted_iota(jnp.int32, (1, ML_HEAD_PAD), 1) < ML_HEAD_DIM
    o = oz_ref[:, 0:ML_WIDTH_PAD]
    z = oz_ref[:, ML_WIDTH_PAD:2 * ML_WIDTH_PAD]
    for h in range(ML_HEADS):
        hs = slice(h * ML_HEAD_PAD, (h + 1) * ML_HEAD_PAD)
        qh, kh, vh = q[:, hs], k[:, hs], v[:, hs]
        b_col = b[:, h:h + 1]
        x_col = log_i[:, h:h + 1] - b_col
        d = b_col + _hdot(ones_sq, eye * x_col)
        d = jnp.where(pair_ok, d, -jnp.inf)
        g_col = g[:, h:h + 1]
        m_row = jnp.maximum(g_col, jnp.max(d, axis=1, keepdims=True))
        qk = lax.dot_general(qh.astype(BF16), kh.astype(BF16), (((1,), (1,)), ((), ())),
                             preferred_element_type=F32)
        s = qk * jnp.exp(d - m_row)
        w_inter = jnp.exp(g_col - m_row)
        c_h = c_scr[h]
        n_h = n_scr[h]
        num = _bdot(s, vh) + w_inter * _bdot(qh, c_h)
        den = jnp.sum(s, axis=1, keepdims=True) + w_inter * jnp.sum(qh * n_h, axis=1, keepdims=True)
        hh = num / jnp.maximum(jnp.abs(den), jnp.exp(-m_row))
        mu = jnp.sum(hh, axis=1, keepdims=True) * (1.0 / ML_HEAD_DIM)
        dv = jnp.where(lane_ok, hh - mu, 0.0)
        var = jnp.sum(dv * dv, axis=1, keepdims=True) * (1.0 / ML_HEAD_DIM)
        hn = dv * lax.rsqrt(var + LN_EPS) * lng_ref[:, hs]
        y_ref[:, hs] = _sigmoid(o[:, hs]) * hn * _silu(z[:, hs])
        we_col = we[:, h:h + 1]
        keep_h = keep[:, h:h + 1]
        kv = lax.dot_general(kh.astype(BF16), (we_col * vh).astype(BF16), (((0,), (0,)), ((), ())),
                             preferred_element_type=F32)
        c_scr[h] = keep_h * c_h + kv
        n_scr[h] = keep_h * n_h + jnp.sum(we_col * kh, axis=0, keepdims=True)

    @pl.when(j == pl.num_programs(1) - 1)
    def _():
        c_out[...] = c_scr[:, 0:ML_HEAD_DIM, 0:ML_HEAD_DIM]
        n_out[...] = n_scr[:, :, 0:ML_HEAD_DIM]
        m_out[...] = m_scr[...]


def _mlstm(p3, conv0, c0, n0, m0, lp, *, pad):
    bsz, seq_rows, _ = p3.shape
    nblk = seq_rows // BLK
    blk = lambda w, off: pl.BlockSpec((None, BLK, w), lambda b, j: (b, j, off // w))
    par = lambda r, w: pl.BlockSpec((r, w), lambda b, j: (0, 0))
    conv_spec = pl.BlockSpec((None, ML_CONV - 1, W_QK), lambda b, j: (b, 0, 0))
    c_spec = pl.BlockSpec((None, ML_HEADS, ML_HEAD_DIM, ML_HEAD_DIM), lambda b, j: (b, 0, 0, 0))
    n_spec = pl.BlockSpec((None, ML_HEADS, 1, ML_HEAD_DIM), lambda b, j: (b, 0, 0, 0))
    m_spec = pl.BlockSpec((None, 1, LANE), lambda b, j: (b, 0, 0))
    return pl.pallas_call(
        functools.partial(_mlstm_kernel, pad=pad),
        out_shape=(jax.ShapeDtypeStruct((bsz, seq_rows, ML_WIDTH_PAD), F32),
                   jax.ShapeDtypeStruct((bsz, ML_CONV - 1, W_QK), F32),
                   jax.ShapeDtypeStruct((bsz, ML_HEADS, ML_HEAD_DIM, ML_HEAD_DIM), F32),
                   jax.ShapeDtypeStruct((bsz, ML_HEADS, 1, ML_HEAD_DIM), F32),
                   jax.ShapeDtypeStruct((bsz, 1, LANE), F32)),
        grid=(bsz, nblk),
        in_specs=[blk(W_QK, OFF_QK), blk(W_OZ, OFF_OZ), blk(W_MV, OFF_MV),
                  blk(W_MI, OFF_MI), blk(W_MF, OFF_MF),
                  conv_spec, c_spec, n_spec, m_spec,
                  par(ML_CONV, W_QK), par(1, W_QK), par(1, LANE), par(1, LANE),
                  par(1, ML_WIDTH_PAD)],
        out_specs=(pl.BlockSpec((None, BLK, ML_WIDTH_PAD), lambda b, j: (b, j, 0)),
                   conv_spec, c_spec, n_spec, m_spec),
        scratch_shapes=[pltpu.VMEM((8 + BLK, W_QK), F32),
                        pltpu.VMEM((ML_HEADS, ML_HEAD_PAD, ML_HEAD_PAD), F32),
                        pltpu.VMEM((ML_HEADS, 1, ML_HEAD_PAD), F32),
                        pltpu.VMEM((1, LANE), F32)],
        compiler_params=_cparams(("parallel", "arbitrary")),
        name="mlstm",
    )(p3, p3, p3, p3, p3, conv0, c0, n0, m0, lp["ml_cw"], lp["ml_cb"], lp["ml_bi"], lp["ml_bf"],
      lp["ml_ln_g"])


def _merge_kernel(x_ref, mg_ref, yrw_ref, ys5_ref, yml_ref, bmg_ref, wrw_ref, ws5_ref, wml_ref,
                  wout_ref, g_ref, b_ref, o_ref, *, tm, seq_rows, pad):
    gates = _sigmoid(mg_ref[...] + bmg_ref[...])
    merged = (gates[:, 0:D_MODEL] * _bdot(yrw_ref[...], wrw_ref[...])
              + gates[:, D_MODEL:2 * D_MODEL] * _bdot(ys5_ref[...], ws5_ref[...])
              + gates[:, 2 * D_MODEL:3 * D_MODEL] * _bdot(yml_ref[...], wml_ref[...]))
    out = _bdot(merged, wout_ref[...])
    y = _layer_norm_rows(DN_ALPHA * x_ref[...] + out, g_ref[...], b_ref[...])
    if pad:
        y = jnp.where(_pad_row_mask(pl.program_id(0), tm, seq_rows, pad), 0.0, y)
    o_ref[...] = y


def _merge(x, p, y_rw, y_s5, y_ml, lp, *, tm, seq_rows, pad):
    n = x.shape[0]
    rows = lambda w: pl.BlockSpec((tm, w), lambda i: (i, 0))
    full = lambda r, w: pl.BlockSpec((r, w), lambda i: (0, 0))
    return pl.pallas_call(
        functools.partial(_merge_kernel, tm=tm, seq_rows=seq_rows, pad=pad),
        out_shape=jax.ShapeDtypeStruct((n, D_MODEL), F32),
        grid=(n // tm,),
        in_specs=[rows(D_MODEL), pl.BlockSpec((tm, W_MG), lambda i: (i, OFF_MG // W_MG)),
                  rows(RW_WIDTH), rows(S5_WIDTH), rows(ML_WIDTH_PAD),
                  full(1, W_MG), full(RW_WIDTH, D_MODEL), full(S5_WIDTH, D_MODEL),
                  full(ML_WIDTH_PAD, D_MODEL), full(D_MODEL, D_MODEL),
                  full(1, D_MODEL), full(1, D_MODEL)],
        out_specs=rows(D_MODEL),
        compiler_params=_cparams(("parallel",)),
        name="merge",
    )(x, p, y_rw, y_s5, y_ml, lp["b_merge"], lp["w_br_rw"], lp["w_br_s5"], lp["w_br_ml"],
      lp["w_out"], lp["ln_g"], lp["ln_b"])


def _pad_heads(w):
    lead = w.shape[:-1]
    w = w.reshape(lead + (ML_HEADS, ML_HEAD_DIM))
    w = jnp.pad(w, [(0, 0)] * len(lead) + [(0, 0), (0, ML_HEAD_PAD - ML_HEAD_DIM)])
    return w.reshape(lead + (ML_WIDTH_PAD,))


def _unpad_heads(w):
    lead = w.shape[:-1]
    return w.reshape(lead + (ML_HEADS, ML_HEAD_PAD))[..., :ML_HEAD_DIM].reshape(lead + (ML_WIDTH,))


def _pad_lanes(w, width=LANE):
    return jnp.pad(w, [(0, 0)] * (w.ndim - 1) + [(0, width - w.shape[-1])])


def _qk_pad(w):
    return jnp.concatenate([_pad_heads(w[..., :ML_WIDTH]), _pad_heads(w[..., ML_WIDTH:])], axis=-1)


def _layer_params(l, w_in, rw_mu, rw_w0, rw_w2, rw_a0, rw_a2, rw_kk, rw_ka, rw_rk, rw_ln_g, rw_ln_b,
                  s5_a_re, s5_a_im, s5_b_re, s5_b_im, s5_c_re, s5_c_im, s5_d, s5_log_dt, s5_w_glu,
                  s5_b_glu, ml_conv_w, ml_conv_b, ml_b_if, ml_ln_g, b_merge, w_br_rw, w_br_s5,
                  w_br_ml, w_out, ln_g, ln_b):
    w = w_in[l]
    o = 0
    cols = {}
    for name, size in (("rwc", RW_SHIFT), ("rwg", RW_WIDTH), ("s5", 2 * S5_WIDTH),
                       ("qk", 2 * ML_WIDTH), ("mv", ML_WIDTH), ("mi", ML_HEADS), ("mf", ML_HEADS),
                       ("mo", ML_WIDTH), ("mz", ML_WIDTH), ("mg", 3 * D_MODEL)):
        cols[name] = w[:, o:o + size]
        o += size
    w_all = jnp.concatenate([
        cols["mg"], cols["s5"], _qk_pad(cols["qk"]),
        _pad_heads(cols["mo"]), _pad_heads(cols["mz"]), _pad_heads(cols["mv"]),
        cols["rwc"][:, :RW_RKV], cols["rwg"], cols["rwc"][:, RW_RKV:],
        _pad_lanes(cols["mi"]), _pad_lanes(cols["mf"])], axis=1).astype(BF16)
    row = lambda a: a.reshape(1, -1)
    zeros_lora = jnp.zeros((RW_LORA, RW_WIDTH), F32)
    lane = jnp.arange(LANE)
    ones_blk = (lane[:, None] // RW_HEAD_DIM == lane[None, :] // RW_HEAD_DIM).astype(F32)

    bb_re, bb_im, lv_re, lv_im, pw_re, pw_im = _s5_prep(
        s5_a_re[l], s5_a_im[l], s5_log_dt[l].reshape(S5_GROUPS, 1),
        jnp.swapaxes(s5_b_re[l], 1, 2), jnp.swapaxes(s5_b_im[l], 1, 2))
    eye8 = jnp.eye(8, dtype=F32)
    blocked = lambda a: a.reshape((S5_NBLK, 8) + a.shape[1:])
    bb = jnp.stack([blocked(bb_re), blocked(bb_im)])
    wb = jnp.einsum("cjghp,gk->jghckp", bb, eye8).reshape(S5_NBLK, LANE, 2 * S5_BLK_STATE)
    cc = jnp.stack([blocked(s5_c_re[l]), -blocked(s5_c_im[l])])
    wc = jnp.einsum("cjghp,gk->jcgpkh", cc, eye8).reshape(S5_NBLK, 2 * S5_BLK_STATE, LANE)
    state_rows = lambda re, im: jnp.stack(
        [re.reshape(re.shape[:-2] + (S5_NBLK, S5_BLK_STATE)),
         im.reshape(im.shape[:-2] + (S5_NBLK, S5_BLK_STATE))], axis=-2)
    lv = state_rows(lv_re, lv_im).reshape(SCAN_LEVELS, 2 * S5_NBLK, S5_BLK_STATE)
    pw = jnp.moveaxis(state_rows(pw_re, pw_im).reshape(BLK, 2 * S5_NBLK, S5_BLK_STATE), 0, 1)

    return dict(
        w_all=w_all,
        mu_rkv=row(rw_mu[l][:RW_RKV]), mu_wa=row(rw_mu[l][RW_RKV:]),
        rw_w0=row(rw_w0[l]), rw_a0=row(rw_a0[l]),
        rw_w2p=jnp.concatenate([rw_w2[l], zeros_lora], axis=0),
        rw_a2p=jnp.concatenate([zeros_lora, rw_a2[l]], axis=0),
        rw_kk=row(rw_kk[l]), rw_ka=row(rw_ka[l]), rw_rk=row(rw_rk[l]),
        rw_ln_g=row(rw_ln_g[l]), rw_ln_b=row(rw_ln_b[l]), ones_blk=ones_blk,
        s5_wb=wb.astype(BF16), s5_wc=wc.astype(BF16), s5_lv=lv, s5_pw=pw,
        s5_d=row(s5_d[l]), s5_w_glu=s5_w_glu[l].astype(BF16), s5_b_glu=row(s5_b_glu[l]),
        ml_cw=_qk_pad(ml_conv_w[l]), ml_cb=row(_qk_pad(ml_conv_b[l])),
        ml_bi=row(_pad_lanes(ml_b_if[l][:ML_HEADS])), ml_bf=row(_pad_lanes(ml_b_if[l][ML_HEADS:])),
        ml_ln_g=row(_pad_heads(ml_ln_g[l])),
        b_merge=row(b_merge[l]), w_br_rw=w_br_rw[l].astype(BF16), w_br_s5=w_br_s5[l].astype(BF16),
        w_br_ml=jnp.pad(w_br_ml[l].reshape(ML_HEADS, ML_HEAD_DIM, D_MODEL),
                        ((0, 0), (0, ML_HEAD_PAD - ML_HEAD_DIM), (0, 0))
                        ).reshape(ML_WIDTH_PAD, D_MODEL).astype(BF16),
        w_out=w_out[l].astype(BF16), ln_g=row(ln_g[l]), ln_b=row(ln_b[l]))


def _pack_wkv(s):
    b = s.shape[0]
    s = s.reshape(b, RW_PAIRS, 2, RW_HEAD_DIM, RW_HEAD_DIM)
    return jnp.swapaxes(s, 2, 3).reshape(b, RW_PAIRS, RW_HEAD_DIM, LANE)


def _unpack_wkv(s):
    b = s.shape[0]
    s = s.reshape(b, RW_PAIRS, RW_HEAD_DIM, 2, RW_HEAD_DIM)
    return jnp.swapaxes(s, 2, 3).reshape(b, RW_HEADS, RW_HEAD_DIM, RW_HEAD_DIM)


def _pack_s5(re, im):
    b = re.shape[0]
    return jnp.stack([re.reshape(b, S5_NBLK, S5_BLK_STATE), im.reshape(b, S5_NBLK, S5_BLK_STATE)],
                     axis=2).reshape(b, 2 * S5_NBLK, S5_BLK_STATE)


def _unpack_s5(h):
    b = h.shape[0]
    h = h.reshape(b, S5_NBLK, 2, S5_BLK_STATE)
    return (h[:, :, 0].reshape(b, S5_GROUPS, S5_STATE), h[:, :, 1].reshape(b, S5_GROUPS, S5_STATE))


def _row_tile(n, seq_rows, target):
    best = 8
    for t in range(8, min(n, target) + 1, 8):
        if seq_rows % t == 0 or (t % seq_rows == 0 and n % t == 0):
            best = t
    return best


def _trunk_layer(x, st, lp, *, bsz, seq_rows, pad, bb_n, s5_tb, tm_proj, tm_merge):
    rw_shift0, rw_wkv0, s5_re0, s5_im0, ml_conv0, ml_c0, ml_n0, ml_m0 = st
    p = _proj(x, lp["w_all"], tm=tm_proj)
    p3 = p.reshape(bsz, seq_rows, P_COLS)

    prep = _rw_prep(p3, rw_shift0[:, None, :RW_RKV], rw_shift0[:, None, RW_RKV:], lp)
    y_rw, wkv1 = _rw_scan(prep[:7], p3, _pack_wkv(rw_wkv0), lp, bb_n=bb_n)
    rw_shift1 = jnp.concatenate([prep[7][:, 0], prep[8][:, 0]], axis=-1)

    y_s5, h1 = _s5(p3, _pack_s5(s5_re0, s5_im0), lp, tb=s5_tb)
    s5_re1, s5_im1 = _unpack_s5(h1)

    y_ml, conv1, c1, n1, m1 = _mlstm(
        p3, _qk_pad(ml_conv0), ml_c0, ml_n0[:, :, None, :], _pad_lanes(ml_m0)[:, None, :], lp, pad=pad)
    ml_conv1 = jnp.concatenate([_unpad_heads(conv1[..., :ML_WIDTH_PAD]),
                                _unpad_heads(conv1[..., ML_WIDTH_PAD:])], axis=-1)

    n = bsz * seq_rows
    x_new = _merge(x, p, y_rw.reshape(n, RW_WIDTH), y_s5.reshape(n, S5_WIDTH),
                   y_ml.reshape(n, ML_WIDTH_PAD), lp, tm=tm_merge, seq_rows=seq_rows, pad=pad)
    return x_new, (rw_shift1, _unpack_wkv(wkv1), s5_re1, s5_im1, ml_conv1, c1, n1[:, :, 0, :],
                   m1[:, 0, :ML_HEADS])


def _run_group(x_rows, states, lps, *, bsz, seq_rows, pad, in_ln_g, in_ln_b):
    n = bsz * seq_rows
    bb_n = 2 if bsz % 2 == 0 else 1
    s5_tb = max(t for t in (BLK, 5 * BLK) if seq_rows % t == 0)
    tm_ln = _row_tile(n, seq_rows, 1024)
    tm_proj = _row_tile(n, seq_rows, 2080)
    tm_merge = _row_tile(n, seq_rows, 320)
    x = _ln_in(x_rows, in_ln_g, in_ln_b, tm=tm_ln, seq_rows=seq_rows, pad=pad)
    outs = []
    for l in range(DEPTH):
        x, st = _trunk_layer(x, states[l], lps[l], bsz=bsz, seq_rows=seq_rows, pad=pad, bb_n=bb_n,
                             s5_tb=s5_tb, tm_proj=tm_proj, tm_merge=tm_merge)
        outs.append(st)
    return x, outs


def kernel(x_prompt, x_sample, state_rwkv_shift, state_rwkv_wkv, state_s5_re, state_s5_im, state_mlstm_conv, state_mlstm_c, state_mlstm_n, state_mlstm_m, meta, in_ln_g, in_ln_b, w_in, rw_mu, rw_w0, rw_w2, rw_a0, rw_a2, rw_kk, rw_ka, rw_rk, rw_ln_g, rw_ln_b, s5_a_re, s5_a_im, s5_b_re, s5_b_im, s5_c_re, s5_c_im, s5_d, s5_log_dt, s5_w_glu, s5_b_glu, ml_conv_w, ml_conv_b, ml_b_if, ml_ln_g, b_merge, w_br_rw, w_br_s5, w_br_ml, w_out, ln_g, ln_b):
    lps = [_layer_params(l, w_in, rw_mu, rw_w0, rw_w2, rw_a0, rw_a2, rw_kk, rw_ka, rw_rk, rw_ln_g,
                         rw_ln_b, s5_a_re, s5_a_im, s5_b_re, s5_b_im, s5_c_re, s5_c_im, s5_d,
                         s5_log_dt, s5_w_glu, s5_b_glu, ml_conv_w, ml_conv_b, ml_b_if, ml_ln_g,
                         b_merge, w_br_rw, w_br_s5, w_br_ml, w_out, ln_g, ln_b)
           for l in range(DEPTH)]
    g_in, b_in = in_ln_g.reshape(1, D_MODEL), in_ln_b.reshape(1, D_MODEL)

    bp, sp = x_prompt.shape[0], x_prompt.shape[1]
    lp_rows = PAD + N_META + sp
    xp = jnp.concatenate([jnp.zeros((bp, PAD, D_MODEL), F32),
                          jnp.broadcast_to(meta[None], (bp, N_META, D_MODEL)), x_prompt], axis=1)
    z = lambda *shape: jnp.zeros((bp,) + shape, F32)
    zero_state = (z(RW_SHIFT), z(RW_HEADS, RW_HEAD_DIM, RW_HEAD_DIM), z(S5_GROUPS, S5_STATE),
                  z(S5_GROUPS, S5_STATE), z(ML_CONV - 1, 2 * ML_WIDTH),
                  z(ML_HEADS, ML_HEAD_DIM, ML_HEAD_DIM), z(ML_HEADS, ML_HEAD_DIM), z(ML_HEADS))
    yp, p_states = _run_group(xp.reshape(bp * lp_rows, D_MODEL), [zero_state] * DEPTH, lps,
                              bsz=bp, seq_rows=lp_rows, pad=PAD, in_ln_g=g_in, in_ln_b=b_in)
    y_prompt = yp.reshape(bp, lp_rows, D_MODEL)[:, PAD + N_META:]

    bs, ds = x_sample.shape[0], x_sample.shape[1]
    s_in = [(state_rwkv_shift[l], state_rwkv_wkv[l], state_s5_re[l], state_s5_im[l],
             state_mlstm_conv[l], state_mlstm_c[l], state_mlstm_n[l], state_mlstm_m[l])
            for l in range(DEPTH)]
    ys, s_states = _run_group(x_sample.reshape(bs * ds, D_MODEL), s_in, lps,
                              bsz=bs, seq_rows=ds, pad=0, in_ln_g=g_in, in_ln_b=b_in)
    y_sample = ys.reshape(bs, ds, D_MODEL)

    stack = lambda sts: tuple(jnp.stack(s, 0) for s in zip(*sts))
    return (y_prompt, y_sample) + stack(p_states) + stack(s_states)
```

```python
import functools
import math

import jax
import jax.numpy as jnp
from jax import lax
from jax.experimental import pallas as pl
from jax.experimental.pallas import tpu as pltpu

F32 = jnp.float32
BF16 = jnp.bfloat16

D_MODEL = 1024
DEPTH = 2
N_META = 16
RW_HEADS = 12
RW_HEAD_DIM = 64
RW_WIDTH = RW_HEADS * RW_HEAD_DIM
RW_PAIRS = RW_HEADS // 2
RW_LORA = 64
RW_RKV = 3 * RW_WIDTH
RW_SHIFT = RW_RKV + 2 * RW_LORA
S5_GROUPS = 32
S5_GROUP_CH = 16
S5_WIDTH = S5_GROUPS * S5_GROUP_CH
S5_STATE = 64
S5_NBLK = 4
S5_BLK_STATE = 512
ML_HEADS = 4
ML_HEAD_DIM = 192
ML_HEAD_PAD = 256
ML_WIDTH = ML_HEADS * ML_HEAD_DIM
ML_WIDTH_PAD = ML_HEADS * ML_HEAD_PAD
ML_CONV = 4
DN_ALPHA = (2 * DEPTH) ** 0.25
LN_EPS = 1e-5
RW_GN_EPS = 64e-5

LANE = 128
SUBLANES = 8
BLK = 64
PAD = BLK - N_META
SCAN_LEVELS = 3

OFF_MG, W_MG = 0, 3 * D_MODEL
OFF_S5, W_S5 = 3072, 2 * S5_WIDTH
OFF_QK, W_QK = 4096, 2 * ML_WIDTH_PAD
OFF_OZ, W_OZ = 6144, 2 * ML_WIDTH_PAD
OFF_MV, W_MV = 8192, ML_WIDTH_PAD
OFF_RKV, W_RKV = 9216, RW_RKV
OFF_RWG, W_RWG = 11520, RW_WIDTH
OFF_WA, W_WA = 12288, LANE
OFF_MI, W_MI = 12416, LANE
OFF_MF, W_MF = 12544, LANE
P_COLS = 12672
P_TN = 1152

VMEM_LIMIT = 56 * 1024 * 1024


def _cparams(sem):
    return pltpu.CompilerParams(dimension_semantics=sem, vmem_limit_bytes=VMEM_LIMIT)


def _bdot(a, b):
    return jnp.dot(a.astype(BF16), b.astype(BF16), preferred_element_type=F32)


def _hdot(a, b):
    return jnp.dot(a, b, precision=lax.Precision.HIGHEST, preferred_element_type=F32)


def _sigmoid(x):
    return 1.0 / (1.0 + jnp.exp(-x))


def _silu(x):
    return x * _sigmoid(x)


def _softplus(x):
    return jnp.maximum(x, 0.0) + jnp.log1p(jnp.exp(-jnp.abs(x)))


def _pad_row_mask(tile_idx, tm, seq_rows, pad):
    pos0 = lax.rem(tile_idx * tm, seq_rows)
    row = lax.broadcasted_iota(jnp.int32, (tm, 1), 0) + pos0
    return row < pad


def _layer_norm_rows(x, g, b):
    mu = jnp.mean(x, axis=-1, keepdims=True)
    d = x - mu
    var = jnp.mean(d * d, axis=-1, keepdims=True)
    return d * lax.rsqrt(var + LN_EPS) * g + b


def _ln_in_kernel(x_ref, g_ref, b_ref, o_ref, *, tm, seq_rows, pad):
    y = _layer_norm_rows(x_ref[...], g_ref[...], b_ref[...])
    if pad:
        y = jnp.where(_pad_row_mask(pl.program_id(0), tm, seq_rows, pad), 0.0, y)
    o_ref[...] = y


def _ln_in(x, g, b, *, tm, seq_rows, pad):
    n = x.shape[0]
    return pl.pallas_call(
        functools.partial(_ln_in_kernel, tm=tm, seq_rows=seq_rows, pad=pad),
        out_shape=jax.ShapeDtypeStruct((n, D_MODEL), F32),
        grid=(n // tm,),
        in_specs=[pl.BlockSpec((tm, D_MODEL), lambda i: (i, 0)),
                  pl.BlockSpec((1, D_MODEL), lambda i: (0, 0)),
                  pl.BlockSpec((1, D_MODEL), lambda i: (0, 0))],
        out_specs=pl.BlockSpec((tm, D_MODEL), lambda i: (i, 0)),
        compiler_params=_cparams(("parallel",)),
        name="ln_in",
    )(x, g, b)


def _proj_kernel(x_ref, w_ref, o_ref):
    o_ref[...] = jnp.dot(x_ref[...].astype(BF16), w_ref[...], preferred_element_type=F32)


def _proj(x, w_all, *, tm):
    n = x.shape[0]
    return pl.pallas_call(
        _proj_kernel,
        out_shape=jax.ShapeDtypeStruct((n, P_COLS), F32),
        grid=(n // tm, P_COLS // P_TN),
        in_specs=[pl.BlockSpec((tm, D_MODEL), lambda i, j: (i, 0)),
                  pl.BlockSpec((D_MODEL, P_TN), lambda i, j: (0, j))],
        out_specs=pl.BlockSpec((tm, P_TN), lambda i, j: (i, j)),
        compiler_params=_cparams(("parallel", "arbitrary")),
        name="proj",
    )(x, w_all)


def _split_lhs(hi_f32, lo_f32):
    return jnp.concatenate([hi_f32.astype(BF16), lo_f32.astype(BF16)], axis=1)


def _seg_lhs(x):
    hi = x.astype(BF16).astype(F32)
    return _split_lhs(hi, x - hi)


def _head_sum(x, ones2):
    parts = [jnp.dot(_seg_lhs(x[:, i * LANE:(i + 1) * LANE]), ones2, preferred_element_type=F32)
             for i in range(RW_WIDTH // LANE)]
    return jnp.concatenate(parts, axis=-1)


def _shift_rows(u, carry):
    rolled = pltpu.roll(u, 1, 0)
    row = lax.broadcasted_iota(jnp.int32, u.shape, 0)
    return jnp.where(row == 0, carry, rolled)


def _rw_prep_kernel(rkv_ref, wa_ref, sh0_rkv_ref, sh0_wa_ref, mu_rkv_ref, mu_wa_ref,
                    w0_ref, w2_ref, a0_ref, a2_ref, kk_ref, ka_ref, rk_ref, ones_ref,
                    r_out, k_out, v_out, nkk_out, kka_out, w_out, bon_out, sh_rkv_out, sh_wa_out,
                    c_rkv, c_wa):
    j = pl.program_id(1)

    @pl.when(j == 0)
    def _():
        c_rkv[...] = sh0_rkv_ref[...]
        c_wa[...] = sh0_wa_ref[...]

    u = rkv_ref[...]
    uw = wa_ref[...]
    xs = u + (_shift_rows(u, c_rkv[...]) - u) * mu_rkv_ref[...]
    xwa = uw + (_shift_rows(uw, c_wa[...]) - uw) * mu_wa_ref[...]
    c_rkv[...] = u[BLK - 1:BLK, :]
    c_wa[...] = uw[BLK - 1:BLK, :]
    sh_rkv_out[...] = u[BLK - 1:BLK, :]
    sh_wa_out[...] = uw[BLK - 1:BLK, :]

    r = xs[:, 0:RW_WIDTH]
    k = xs[:, RW_WIDTH:2 * RW_WIDTH]
    v = xs[:, 2 * RW_WIDTH:3 * RW_WIDTH]
    w_log = -_softplus(-(w0_ref[...] + _bdot(jnp.tanh(xwa), w2_ref[...]))) - 0.5
    decay = jnp.exp(-jnp.exp(w_log))
    a = _sigmoid(a0_ref[...] + _bdot(xwa, a2_ref[...]))
    ones2 = ones_ref[...]
    kk = k * kk_ref[...]
    kk = kk * lax.rsqrt(_head_sum(kk * kk, ones2) + 1e-12)
    k = k * (1.0 + (a - 1.0) * ka_ref[...])
    r_out[...] = r
    k_out[...] = k
    v_out[...] = v
    nkk_out[...] = -kk
    kka_out[...] = kk * a
    w_out[...] = decay
    bon_out[...] = _head_sum(r * k * rk_ref[...], ones2) * v


def _rw_prep(p3, sh0_rkv, sh0_wa, lp):
    bsz, seq_rows, _ = p3.shape
    nblk = seq_rows // BLK
    row_spec = pl.BlockSpec((None, BLK, RW_WIDTH), lambda b, j: (b, j, 0))
    par = lambda w: pl.BlockSpec((1, w), lambda b, j: (0, 0))
    mat = pl.BlockSpec((LANE, RW_WIDTH), lambda b, j: (0, 0))
    st = lambda w: pl.BlockSpec((None, 1, w), lambda b, j: (b, 0, 0))
    row_shape = jax.ShapeDtypeStruct((bsz, seq_rows, RW_WIDTH), F32)
    return pl.pallas_call(
        _rw_prep_kernel,
        out_shape=(row_shape,) * 7 + (jax.ShapeDtypeStruct((bsz, 1, RW_RKV), F32),
                                      jax.ShapeDtypeStruct((bsz, 1, LANE), F32)),
        grid=(bsz, nblk),
        in_specs=[pl.BlockSpec((None, BLK, W_RKV), lambda b, j: (b, j, OFF_RKV // W_RKV)),
                  pl.BlockSpec((None, BLK, W_WA), lambda b, j: (b, j, OFF_WA // W_WA)),
                  st(RW_RKV), st(LANE), par(RW_RKV), par(LANE),
                  par(RW_WIDTH), mat, par(RW_WIDTH), mat,
                  par(RW_WIDTH), par(RW_WIDTH), par(RW_WIDTH),
                  pl.BlockSpec((2 * LANE, LANE), lambda b, j: (0, 0))],
        out_specs=(row_spec,) * 7 + (st(RW_RKV), st(LANE)),
        scratch_shapes=[pltpu.VMEM((1, RW_RKV), F32), pltpu.VMEM((1, LANE), F32)],
        compiler_params=_cparams(("parallel", "arbitrary")),
        name="rw_prep",
    )(p3, p3, sh0_rkv, sh0_wa, lp["mu_rkv"], lp["mu_wa"], lp["rw_w0"], lp["rw_w2p"],
      lp["rw_a0"], lp["rw_a2p"], lp["rw_kk"], lp["rw_ka"], lp["rw_rk"], lp["ones2"])


def _rw_scan_kernel(r_ref, k_ref, v_ref, nkk_ref, kka_ref, w_ref, bon_ref, gate_ref, s0_ref,
                    lng_ref, lnb_ref, ones2_ref, y_ref, s_out, s_scr, y_scr, *, bb_n):
    j = pl.program_id(1)

    @pl.when(j == 0)
    def _():
        s_scr[...] = s0_ref[...]

    lane = lax.broadcasted_iota(jnp.int32, (RW_HEAD_DIM, LANE), 1)
    sub = lax.broadcasted_iota(jnp.int32, (RW_HEAD_DIM, LANE), 0)
    lo = lane < RW_HEAD_DIM
    diag = (lane & (RW_HEAD_DIM - 1)) == sub
    ones2 = ones2_ref[...]

    def seg_dot(lhs_tiles):
        out = jnp.dot(jnp.concatenate(lhs_tiles, axis=0), ones2, preferred_element_type=F32)
        return [out[c * RW_HEAD_DIM:(c + 1) * RW_HEAD_DIM, :] for c in range(len(lhs_tiles))]

    chains = [(bb, p) for bb in range(bb_n) for p in range(RW_PAIRS)]
    n_ch = len(chains)

    def row_group(g, carry):
        for i in range(SUBLANES):
            row = lambda ref, c: ref[chains[c][0], g, i:i + 1,
                                     chains[c][1] * LANE:(chains[c][1] + 1) * LANE]
            v_lhs = []
            for c in range(n_ch):
                v_t = row(v_ref, c)
                vh_t = v_t.astype(BF16).astype(F32)
                v_lhs.append(_split_lhs(jnp.where(diag, vh_t, 0.0), jnp.where(diag, v_t - vh_t, 0.0)))
            vcol = seg_dot(v_lhs)
            s_old = [s_scr[bb, p] for bb, p in chains]
            sa = seg_dot([_seg_lhs(s_old[c] * row(nkk_ref, c)) for c in range(n_ch)])
            s_new = []
            for c, (bb, p) in enumerate(chains):
                s = s_old[c] * row(w_ref, c) + sa[c] * row(kka_ref, c) + vcol[c] * row(k_ref, c)
                s_scr[bb, p] = s
                s_new.append(s)
            for c, (bb, p) in enumerate(chains):
                sr = s_new[c] * row(r_ref, c)
                y_lo = jnp.sum(jnp.where(lo, sr, 0.0), axis=1, keepdims=True)
                y_hi = jnp.sum(jnp.where(lo, 0.0, sr), axis=1, keepdims=True)
                ycol = jnp.where(diag, jnp.where(lo, y_lo, y_hi), 0.0)
                y_scr[bb, g, i:i + 1, p * LANE:(p + 1) * LANE] = jnp.sum(ycol, axis=0, keepdims=True)
        return carry

    lax.fori_loop(0, BLK // SUBLANES, row_group, 0)

    inv = 1.0 / RW_HEAD_DIM
    for bb in range(bb_n):
        y = y_scr[bb].reshape(BLK, RW_WIDTH)
        mu = _head_sum(y, ones2) * inv
        d = y - mu
        var = _head_sum(d * d, ones2) * inv
        yn = d * lax.rsqrt(var + RW_GN_EPS) * lng_ref[...] + lnb_ref[...]
        y_ref[bb] = (yn + bon_ref[bb]) * _silu(gate_ref[bb])

    @pl.when(j == pl.num_programs(1) - 1)
    def _():
        s_out[...] = s_scr[...]


def _rw_scan(prep, p3, s0, lp, *, bb_n):
    r, k, v, nkk, kka, w, bon = prep
    bsz, seq_rows, _ = r.shape
    nblk = seq_rows // BLK
    grp = BLK // SUBLANES
    by_group = lambda a: a.reshape(bsz, seq_rows // SUBLANES, SUBLANES, RW_WIDTH)
    grp_spec = pl.BlockSpec((bb_n, grp, SUBLANES, RW_WIDTH), lambda b, j: (b, j, 0, 0))
    row_spec = pl.BlockSpec((bb_n, BLK, RW_WIDTH), lambda b, j: (b, j, 0))
    par = pl.BlockSpec((1, RW_WIDTH), lambda b, j: (0, 0))
    st_spec = pl.BlockSpec((bb_n, RW_PAIRS, RW_HEAD_DIM, LANE), lambda b, j: (b, 0, 0, 0))
    return pl.pallas_call(
        functools.partial(_rw_scan_kernel, bb_n=bb_n),
        out_shape=(jax.ShapeDtypeStruct((bsz, seq_rows, RW_WIDTH), F32),
                   jax.ShapeDtypeStruct((bsz, RW_PAIRS, RW_HEAD_DIM, LANE), F32)),
        grid=(bsz // bb_n, nblk),
        in_specs=[grp_spec] * 6 + [row_spec]
        + [pl.BlockSpec((bb_n, BLK, W_RWG), lambda b, j: (b, j, OFF_RWG // W_RWG)),
           st_spec, par, par, pl.BlockSpec((2 * LANE, LANE), lambda b, j: (0, 0))],
        out_specs=(row_spec, st_spec),
        scratch_shapes=[pltpu.VMEM((bb_n, RW_PAIRS, RW_HEAD_DIM, LANE), F32),
                        pltpu.VMEM((bb_n, grp, SUBLANES, RW_WIDTH), F32)],
        compiler_params=_cparams(("parallel", "arbitrary")),
        name="rw_scan",
    )(by_group(r), by_group(k), by_group(v), by_group(nkk), by_group(kka), by_group(w), bon, p3, s0,
      lp["rw_ln_g"], lp["rw_ln_b"], lp["ones2"])


def _cmul(ar, ai, br, bi):
    return ar * br - ai * bi, ar * bi + ai * br


def _s5_prep_kernel(are_ref, aim_ref, ldt_ref, bre_ref, bim_ref,
                    bbre_out, bbim_out, lvre_out, lvim_out, pwre_out, pwim_out):
    ar, ai = are_ref[...], aim_ref[...]
    dt = jnp.exp(ldt_ref[...])
    mag = jnp.exp(ar * dt)
    lr, li = mag * jnp.cos(ai * dt), mag * jnp.sin(ai * dt)
    nr, ni = lr - 1.0, li
    den = ar * ar + ai * ai
    qr, qi = (nr * ar + ni * ai) / den, (ni * ar - nr * ai) / den
    br, bi = bre_ref[...], bim_ref[...]
    bbr, bbi = _cmul(qr[:, None, :], qi[:, None, :], br, bi)
    bbre_out[...] = bbr
    bbim_out[...] = bbi
    sq_r, sq_i = lr, li
    pows = [(lr, li)]
    for lvl in range(SCAN_LEVELS):
        lvre_out[lvl] = sq_r
        lvim_out[lvl] = sq_i
        pows = pows + [_cmul(pr, pi, sq_r, sq_i) for pr, pi in pows]
        sq_r, sq_i = _cmul(sq_r, sq_i, sq_r, sq_i)
    for t in range(SUBLANES):
        pwre_out[t] = pows[t][0]
        pwim_out[t] = pows[t][1]


def _s5_prep(a_re, a_im, log_dt, b_re_t, b_im_t):
    ghp =jax.ShapeDtypeStruct((S5_GROUPS, S5_GROUP_CH, S5_STATE), F32)
    lv = jax.ShapeDtypeStruct((SCAN_LEVELS, S5_GROUPS, S5_STATE), F32)
    pw = jax.ShapeDtypeStruct((SUBLANES, S5_GROUPS, S5_STATE), F32)
    return pl.pallas_call(
        _s5_prep_kernel,
        out_shape=(ghp, ghp, lv, lv, pw, pw),
        name="s5_prep",
    )(a_re, a_im, log_dt, b_re_t, b_im_t)


def _gelu_tanh(x):
    return 0.5 * x * (1.0 + jnp.tanh(math.sqrt(2.0 / math.pi) * (x + 0.044715 * (x * x * x))))


def _s5_kernel(p_ref, h0_ref, wb_ref, wc_ref, lv_ref, pw_ref, d_ref, wglu_ref, bglu_ref,
               y_ref, h_out, h_scr, hs_scr, *, tb):
    j = pl.program_id(1)

    @pl.when(j == 0)
    def _():
        h_scr[...] = h0_ref[...]

    u = p_ref[:, 0:S5_WIDTH]
    gate = p_ref[:, S5_WIDTH:2 * S5_WIDTH]
    y_parts = []
    for jb in range(S5_NBLK):
        bu = _bdot(u[:, jb * LANE:(jb + 1) * LANE], wb_ref[jb])
        for s in range(tb // BLK):
            xr = bu[s * BLK:(s + 1) * BLK, 0:S5_BLK_STATE]
            xi = bu[s * BLK:(s + 1) * BLK, S5_BLK_STATE:2 * S5_BLK_STATE]
            xr = xr.reshape(BLK // SUBLANES, SUBLANES, S5_BLK_STATE)
            xi = xi.reshape(BLK // SUBLANES, SUBLANES, S5_BLK_STATE)
            for lvl in range(SCAN_LEVELS):
                sr = pltpu.roll(xr, 1 << lvl, 1)
                si = pltpu.roll(xi, 1 << lvl, 1)
                lr, li = lv_ref[lvl, 2 * jb], lv_ref[lvl, 2 * jb + 1]
                xr, xi = xr + (lr * sr - li * si), xi + (lr * si + li * sr)
            xr = xr.reshape(BLK, S5_BLK_STATE)
            xi = xi.reshape(BLK, S5_BLK_STATE)
            c_r = h_scr[2 * jb:2 * jb + 1, :]
            c_i = h_scr[2 * jb + 1:2 * jb + 2, :]
            pr, pi = pw_ref[2 * jb], pw_ref[2 * jb + 1]
            for grp in range(BLK // SUBLANES):
                rows = slice(grp * SUBLANES, (grp + 1) * SUBLANES)
                hr = xr[rows, :] + (pr * c_r - pi * c_i)
                hi = xi[rows, :] + (pr * c_i + pi * c_r)
                c_r, c_i = hr[SUBLANES - 1:SUBLANES, :], hi[SUBLANES - 1:SUBLANES, :]
                out_rows = slice(s * BLK + grp * SUBLANES, s * BLK + (grp + 1) * SUBLANES)
                hs_scr[out_rows, 0:S5_BLK_STATE] = hr
                hs_scr[out_rows, S5_BLK_STATE:2 * S5_BLK_STATE] = hi
            h_scr[2 * jb:2 * jb + 1, :] = c_r
            h_scr[2 * jb + 1:2 * jb + 2, :] = c_i
        y_parts.append(_bdot(hs_scr[...], wc_ref[jb]))
    y = jnp.concatenate(y_parts, axis=-1) + d_ref[...] * u
    y = _gelu_tanh(y)
    y = y * _sigmoid(_bdot(y, wglu_ref[...]) + bglu_ref[...])
    y_ref[...] = y * _silu(gate)

    @pl.when(j == pl.num_programs(1) - 1)
    def _():
        h_out[...] = h_scr[...]


def _s5(p3, h0, lp, *, tb):
    bsz, seq_rows, _ = p3.shape
    st_spec = pl.BlockSpec((None, 2 * S5_NBLK, S5_BLK_STATE), lambda b, j: (b, 0, 0))
    full = lambda shape: pl.BlockSpec(shape, lambda b, j: (0,) * len(shape))
    return pl.pallas_call(
        functools.partial(_s5_kernel, tb=tb),
        out_shape=(jax.ShapeDtypeStruct((bsz, seq_rows, S5_WIDTH), F32),
                   jax.ShapeDtypeStruct((bsz, 2 * S5_NBLK, S5_BLK_STATE), F32)),
        grid=(bsz, seq_rows // tb),
        in_specs=[pl.BlockSpec((None, tb, W_S5), lambda b, j: (b, j, OFF_S5 // W_S5)),
                  st_spec,
                  full((S5_NBLK, LANE, 2 * S5_BLK_STATE)),
                  full((S5_NBLK, 2 * S5_BLK_STATE, LANE)),
                  full((SCAN_LEVELS, 2 * S5_NBLK, SUBLANES, S5_BLK_STATE)),
                  full((2 * S5_NBLK, SUBLANES, S5_BLK_STATE)),
                  full((1, S5_WIDTH)), full((S5_WIDTH, S5_WIDTH)), full((1, S5_WIDTH))],
        out_specs=(pl.BlockSpec((None, tb, S5_WIDTH), lambda b, j: (b, j, 0)), st_spec),
        scratch_shapes=[pltpu.VMEM((2 * S5_NBLK, S5_BLK_STATE), F32),
                        pltpu.VMEM((tb, 2 * S5_BLK_STATE), F32)],
        compiler_params=_cparams(("parallel", "arbitrary")),
        name="s5",
    )(p3, h0, lp["s5_wb"], lp["s5_wc"], lp["s5_lv"], lp["s5_pw"], lp["s5_d"],
      lp["s5_w_glu"], lp["s5_b_glu"])


def _mlstm_kernel(qk_ref, oz_ref, v_ref, gi_ref, gf_ref, conv0_ref, c0_ref, n0_ref, m0_ref,
                  cw_ref, cb_ref, bi_ref, bf_ref, lng_ref,
                  y_ref, conv_out, c_out, n_out, m_out,
                  xp_scr, c_scr, n_scr, m_scr, *, pad):
    j = pl.program_id(1)
    halo = 8

    @pl.when(j == 0)
    def _():
        xp_scr[0:halo, :] = jnp.zeros((halo, W_QK), F32)
        xp_scr[halo - (ML_CONV - 1):halo, :] = conv0_ref[...]
        c_scr[...] = jnp.zeros(c_scr.shape, F32)
        c_scr[:, 0:ML_HEAD_DIM, 0:ML_HEAD_DIM] = c0_ref[...]
        n_scr[...] = jnp.zeros(n_scr.shape, F32)
        n_scr[:, :, 0:ML_HEAD_DIM] = n0_ref[...]
        m_scr[...] = m0_ref[...]

    xp_scr[halo:halo + BLK, :] = qk_ref[...]
    conv = cb_ref[...] + xp_scr[halo - 3:halo - 3 + BLK, :] * cw_ref[0:1, :]
    for tap in range(1, ML_CONV):
        conv = conv + xp_scr[halo - 3 + tap:halo - 3 + tap + BLK, :] * cw_ref[tap:tap + 1, :]
    tail = xp_scr[halo + BLK - (ML_CONV - 1):halo + BLK, :]
    xp_scr[halo - (ML_CONV - 1):halo, :] = tail
    conv_out[...] = tail
    act = _silu(conv)
    q = act[:, 0:ML_WIDTH_PAD]
    k = act[:, ML_WIDTH_PAD:2 * ML_WIDTH_PAD] * (1.0 / math.sqrt(ML_HEAD_DIM))
    v = v_ref[...]

    t0 = jnp.where(j == 0, pad, 0) if pad else 0
    row1 = lax.broadcasted_iota(jnp.int32, (BLK, 1), 0)
    row_ok = row1 >= t0
    ti = lax.broadcasted_iota(jnp.int32, (BLK, BLK), 0)
    si = lax.broadcasted_iota(jnp.int32, (BLK, BLK), 1)
    pair_ok = (si <= ti) & (si >= t0)
    tril = jnp.where(si <= ti, 1.0, 0.0)
    eye = jnp.where(si == ti, 1.0, 0.0)
    ones_sq = jnp.ones((BLK, BLK), F32)

    log_i = gi_ref[...] + bi_ref[...]
    f_pre = gf_ref[...] + bf_ref[...]
    log_f = jnp.where(row_ok, -_softplus(-f_pre), 0.0)
    b = _hdot(tril, log_f)
    m_prev = m_scr[...]
    g = b + m_prev
    b_end = b[BLK - 1:BLK, :]
    e_log = jnp.where(row_ok, b_end - b + log_i, -jnp.inf)
    m_new = jnp.maximum(b_end + m_prev, jnp.max(e_log, axis=0, keepdims=True))
    we = jnp.exp(e_log - m_new)
    keep = jnp.exp(b_end + m_prev - m_new)
    m_scr[...] = m_new

    lane_ok = lax.broadcasted_iota(jnp.int32, (1, ML_HEAD_PAD), 1) < ML_HEAD_DIM
    o = oz_ref[:, 0:ML_WIDTH_PAD]
    z = oz_ref[:, ML_WIDTH_PAD:2 * ML_WIDTH_PAD]
    for h in range(ML_HEADS):
        hs = slice(h * ML_HEAD_PAD, (h + 1) * ML_HEAD_PAD)
        qh, kh, vh = q[:, hs], k[:, hs], v[:, hs]
        b_col = b[:, h:h + 1]
        x_col = log_i[:, h:h + 1] - b_col
        d = b_col + _hdot(ones_sq, eye * x_col)
        d = jnp.where(pair_ok, d, -jnp.inf)
        g_col = g[:, h:h + 1]
        m_row = jnp.maximum(g_col, jnp.max(d, axis=1, keepdims=True))
        qk = lax.dot_general(qh.astype(BF16), kh.astype(BF16), (((1,), (1,)), ((), ())),
                             preferred_element_type=F32)
        s = qk * jnp.exp(d - m_row)
        w_inter = jnp.exp(g_col - m_row)
        c_h = c_scr[h]
        n_h = n_scr[h]
        num = _bdot(s, vh) + w_inter * _bdot(qh, c_h)
        den = jnp.sum(s, axis=1, keepdims=True) + w_inter * jnp.sum(qh * n_h, axis=1, keepdims=True)
        hh = num / jnp.maximum(jnp.abs(den), jnp.exp(-m_row))
        mu = jnp.sum(hh, axis=1, keepdims=True) * (1.0 / ML_HEAD_DIM)
        dv = jnp.where(lane_ok, hh - mu, 0.0)
        var = jnp.sum(dv * dv, axis=1, keepdims=True) * (1.0 / ML_HEAD_DIM)
        hn = dv * lax.rsqrt(var + LN_EPS) * lng_ref[:, hs]
        y_ref[:, hs] = _sigmoid(o[:, hs]) * hn * _silu(z[:, hs])
        we_col = we[:, h:h + 1]
        keep_h = keep[:, h:h + 1]
        kv = lax.dot_general(kh.astype(BF16), (we_col * vh).astype(BF16), (((0,), (0,)), ((), ())),
                             preferred_element_type=F32)
        c_scr[h] = keep_h * c_h + kv
        n_scr[h] = keep_h * n_h + jnp.sum(we_col * kh, axis=0, keepdims=True)

    @pl.when(j == pl.num_programs(1) - 1)
    def _():
        c_out[...] = c_scr[:, 0:ML_HEAD_DIM, 0:ML_HEAD_DIM]
        n_out[...] = n_scr[:, :, 0:ML_HEAD_DIM]
        m_out[...] = m_scr[...]


def _mlstm(p3, conv0, c0, n0, m0, lp, *, pad):
    bsz, seq_rows, _ = p3.shape
    nblk = seq_rows // BLK
    blk = lambda w, off: pl.BlockSpec((None, BLK, w), lambda b, j: (b, j, off // w))
    par = lambda r, w: pl.BlockSpec((r, w), lambda b, j: (0, 0))
    conv_spec = pl.BlockSpec((None, ML_CONV - 1, W_QK), lambda b, j: (b, 0, 0))
    c_spec = pl.BlockSpec((None, ML_HEADS, ML_HEAD_DIM, ML_HEAD_DIM), lambda b, j: (b, 0, 0, 0))
    n_spec = pl.BlockSpec((None, ML_HEADS, 1, ML_HEAD_DIM), lambda b, j: (b, 0, 0, 0))
    m_spec = pl.BlockSpec((None, 1, LANE), lambda b, j: (b, 0, 0))
    return pl.pallas_call(
        functools.partial(_mlstm_kernel, pad=pad),
        out_shape=(jax.ShapeDtypeStruct((bsz, seq_rows, ML_WIDTH_PAD), F32),
                   jax.ShapeDtypeStruct((bsz, ML_CONV - 1, W_QK), F32),
                   jax.ShapeDtypeStruct((bsz, ML_HEADS, ML_HEAD_DIM, ML_HEAD_DIM), F32),
                   jax.ShapeDtypeStruct((bsz, ML_HEADS, 1, ML_HEAD_DIM), F32),
                   jax.ShapeDtypeStruct((bsz, 1, LANE), F32)),
        grid=(bsz, nblk),
        in_specs=[blk(W_QK, OFF_QK), blk(W_OZ, OFF_OZ), blk(W_MV, OFF_MV),
                  blk(W_MI, OFF_MI), blk(W_MF, OFF_MF),
                  conv_spec, c_spec, n_spec, m_spec,
                  par(ML_CONV, W_QK), par(1, W_QK), par(1, LANE), par(1, LANE),
                  par(1, ML_WIDTH_PAD)],
        out_specs=(pl.BlockSpec((None, BLK, ML_WIDTH_PAD), lambda b, j: (b, j, 0)),
                   conv_spec, c_spec, n_spec, m_spec),
        scratch_shapes=[pltpu.VMEM((8 + BLK, W_QK), F32),
                        pltpu.VMEM((ML_HEADS, ML_HEAD_PAD, ML_HEAD_PAD), F32),
                        pltpu.VMEM((ML_HEADS, 1, ML_HEAD_PAD), F32),
                        pltpu.VMEM((1, LANE), F32)],
        compiler_params=_cparams(("parallel", "arbitrary")),
        name="mlstm",
    )(p3, p3, p3, p3, p3, conv0, c0, n0, m0, lp["ml_cw"], lp["ml_cb"], lp["ml_bi"], lp["ml_bf"],
      lp["ml_ln_g"])


def _merge_kernel(x_ref, mg_ref, yrw_ref, ys5_ref, yml_ref, bmg_ref, wrw_ref, ws5_ref, wml_ref,
                  wout_ref, g_ref, b_ref, o_ref, *, tm, seq_rows, pad):
    gates = _sigmoid(mg_ref[...] + bmg_ref[...])
    merged = (gates[:, 0:D_MODEL] * _bdot(yrw_ref[...], wrw_ref[...])
              + gates[:, D_MODEL:2 * D_MODEL] * _bdot(ys5_ref[...], ws5_ref[...])
              + gates[:, 2 * D_MODEL:3 * D_MODEL] * _bdot(yml_ref[...], wml_ref[...]))
    out = _bdot(merged, wout_ref[...])
    y = _layer_norm_rows(DN_ALPHA * x_ref[...] + out, g_ref[...], b_ref[...])
    if pad:
        y = jnp.where(_pad_row_mask(pl.program_id(0), tm, seq_rows, pad), 0.0, y)
    o_ref[...] = y


def _merge(x, p, y_rw, y_s5, y_ml, lp, *, tm, seq_rows, pad):
    n = x.shape[0]
    rows = lambda w: pl.BlockSpec((tm, w), lambda i: (i, 0))
    full = lambda r, w: pl.BlockSpec((r, w), lambda i: (0, 0))
    return pl.pallas_call(
        functools.partial(_merge_kernel, tm=tm, seq_rows=seq_rows, pad=pad),
        out_shape=jax.ShapeDtypeStruct((n, D_MODEL), F32),
        grid=(n // tm,),
        in_specs=[rows(D_MODEL), pl.BlockSpec((tm, W_MG), lambda i: (i, OFF_MG // W_MG)),
                  rows(RW_WIDTH), rows(S5_WIDTH), rows(ML_WIDTH_PAD),
                  full(1, W_MG), full(RW_WIDTH, D_MODEL), full(S5_WIDTH, D_MODEL),
                  full(ML_WIDTH_PAD, D_MODEL), full(D_MODEL, D_MODEL),
                  full(1, D_MODEL), full(1, D_MODEL)],
        out_specs=rows(D_MODEL),
        compiler_params=_cparams(("parallel",)),
        name="merge",
    )(x, p, y_rw, y_s5, y_ml, lp["b_merge"], lp["w_br_rw"], lp["w_br_s5"], lp["w_br_ml"],
      lp["w_out"], lp["ln_g"], lp["ln_b"])


def _pad_heads(w):
    lead = w.shape[:-1]
    w = w.reshape(lead + (ML_HEADS, ML_HEAD_DIM))
    w = jnp.pad(w, [(0, 0)] * len(lead) + [(0, 0), (0, ML_HEAD_PAD - ML_HEAD_DIM)])
    return w.reshape(lead + (ML_WIDTH_PAD,))


def _unpad_heads(w):
    lead = w.shape[:-1]
    return w.reshape(lead + (ML_HEADS, ML_HEAD_PAD))[..., :ML_HEAD_DIM].reshape(lead + (ML_WIDTH,))


def _pad_lanes(w, width=LANE):
    return jnp.pad(w, [(0, 0)] * (w.ndim - 1) + [(0, width - w.shape[-1])])


def _qk_pad(w):
    return jnp.concatenate([_pad_heads(w[..., :ML_WIDTH]), _pad_heads(w[..., ML_WIDTH:])], axis=-1)


def _layer_params(l, w_in, rw_mu, rw_w0, rw_w2, rw_a0, rw_a2, rw_kk, rw_ka, rw_rk, rw_ln_g, rw_ln_b,
                  s5_a_re, s5_a_im, s5_b_re, s5_b_im, s5_c_re, s5_c_im, s5_d, s5_log_dt, s5_w_glu,
                  s5_b_glu, ml_conv_w, ml_conv_b, ml_b_if, ml_ln_g, b_merge, w_br_rw, w_br_s5,
                  w_br_ml, w_out, ln_g, ln_b):
    w = w_in[l]
    o = 0
    cols = {}
    for name, size in (("rwc", RW_SHIFT), ("rwg", RW_WIDTH), ("s5", 2 * S5_WIDTH),
                       ("qk", 2 * ML_WIDTH), ("mv", ML_WIDTH), ("mi", ML_HEADS), ("mf", ML_HEADS),
                       ("mo", ML_WIDTH), ("mz", ML_WIDTH), ("mg", 3 * D_MODEL)):
        cols[name] = w[:, o:o + size]
        o += size
    w_all = jnp.concatenate([
        cols["mg"], cols["s5"], _qk_pad(cols["qk"]),
        _pad_heads(cols["mo"]), _pad_heads(cols["mz"]), _pad_heads(cols["mv"]),
        cols["rwc"][:, :RW_RKV], cols["rwg"], cols["rwc"][:, RW_RKV:],
        _pad_lanes(cols["mi"]), _pad_lanes(cols["mf"])], axis=1).astype(BF16)
    row = lambda a: a.reshape(1, -1)
    zeros_lora = jnp.zeros((RW_LORA, RW_WIDTH), F32)
    lane = jnp.arange(LANE)
    ones_blk = (lane[:, None] // RW_HEAD_DIM == lane[None, :] // RW_HEAD_DIM).astype(F32)

    bb_re, bb_im, lv_re, lv_im, pw_re, pw_im = _s5_prep(
        s5_a_re[l], s5_a_im[l], s5_log_dt[l].reshape(S5_GROUPS, 1),
        jnp.swapaxes(s5_b_re[l], 1, 2), jnp.swapaxes(s5_b_im[l], 1, 2))
    eye8 = jnp.eye(8, dtype=F32)
    blocked = lambda a: a.reshape((S5_NBLK, 8) + a.shape[1:])
    bb = jnp.stack([blocked(bb_re), blocked(bb_im)])
    wb = jnp.einsum("cjghp,gk->jghckp", bb, eye8).reshape(S5_NBLK, LANE, 2 * S5_BLK_STATE)
    cc = jnp.stack([blocked(s5_c_re[l]), -blocked(s5_c_im[l])])
    wc = jnp.einsum("cjghp,gk->jcgpkh", cc, eye8).reshape(S5_NBLK, 2 * S5_BLK_STATE, LANE)
    state_rows = lambda re, im: jnp.stack(
        [re.reshape(re.shape[:-2] + (S5_NBLK, S5_BLK_STATE)),
         im.reshape(im.shape[:-2] + (S5_NBLK, S5_BLK_STATE))], axis=-2)
    lv = state_rows(lv_re, lv_im).reshape(SCAN_LEVELS, 2 * S5_NBLK, 1, S5_BLK_STATE)
    row_in_group = jnp.arange(SUBLANES)[None, None, :, None]
    lv = jnp.where(row_in_group >= (1 << jnp.arange(SCAN_LEVELS))[:, None, None, None], lv, 0.0)
    pw = jnp.moveaxis(state_rows(pw_re, pw_im).reshape(SUBLANES, 2 * S5_NBLK, S5_BLK_STATE), 0, 1)

    return dict(
        w_all=w_all,
        mu_rkv=row(rw_mu[l][:RW_RKV]), mu_wa=row(rw_mu[l][RW_RKV:]),
        rw_w0=row(rw_w0[l]), rw_a0=row(rw_a0[l]),
        rw_w2p=jnp.concatenate([rw_w2[l], zeros_lora], axis=0),
        rw_a2p=jnp.concatenate([zeros_lora, rw_a2[l]], axis=0),
        rw_kk=row(rw_kk[l]), rw_ka=row(rw_ka[l]), rw_rk=row(rw_rk[l]),
        rw_ln_g=row(rw_ln_g[l]), rw_ln_b=row(rw_ln_b[l]),
        ones2=jnp.concatenate([ones_blk, ones_blk], axis=0).astype(BF16),
        s5_wb=wb.astype(BF16), s5_wc=wc.astype(BF16), s5_lv=lv, s5_pw=pw,
        s5_d=row(s5_d[l]), s5_w_glu=s5_w_glu[l].astype(BF16), s5_b_glu=row(s5_b_glu[l]),
        ml_cw=_qk_pad(ml_conv_w[l]), ml_cb=row(_qk_pad(ml_conv_b[l])),
        ml_bi=row(_pad_lanes(ml_b_if[l][:ML_HEADS])), ml_bf=row(_pad_lanes(ml_b_if[l][ML_HEADS:])),
        ml_ln_g=row(_pad_heads(ml_ln_g[l])),
        b_merge=row(b_merge[l]), w_br_rw=w_br_rw[l].astype(BF16), w_br_s5=w_br_s5[l].astype(BF16),
        w_br_ml=jnp.pad(w_br_ml[l].reshape(ML_HEADS, ML_HEAD_DIM, D_MODEL),
                        ((0, 0), (0, ML_HEAD_PAD - ML_HEAD_DIM), (0, 0))
                        ).reshape(ML_WIDTH_PAD, D_MODEL).astype(BF16),
        w_out=w_out[l].astype(BF16), ln_g=row(ln_g[l]), ln_b=row(ln_b[l]))


def _pack_wkv(s):
    b = s.shape[0]
    s = s.reshape(b, RW_PAIRS, 2, RW_HEAD_DIM, RW_HEAD_DIM)
    return jnp.swapaxes(s, 2, 3).reshape(b, RW_PAIRS, RW_HEAD_DIM, LANE)


def _unpack_wkv(s):
    b = s.shape[0]
    s = s.reshape(b, RW_PAIRS, RW_HEAD_DIM, 2, RW_HEAD_DIM)
    return jnp.swapaxes(s, 2, 3).reshape(b, RW_HEADS, RW_HEAD_DIM, RW_HEAD_DIM)


def _pack_s5(re, im):
    b = re.shape[0]
    return jnp.stack([re.reshape(b, S5_NBLK, S5_BLK_STATE), im.reshape(b, S5_NBLK, S5_BLK_STATE)],
                     axis=2).reshape(b, 2 * S5_NBLK, S5_BLK_STATE)


def _unpack_s5(h):
    b = h.shape[0]
    h = h.reshape(b, S5_NBLK, 2, S5_BLK_STATE)
    return (h[:, :, 0].reshape(b, S5_GROUPS, S5_STATE), h[:, :, 1].reshape(b, S5_GROUPS, S5_STATE))


def _row_tile(n, seq_rows, target):
    best = 8
    for t in range(8, min(n, target) + 1, 8):
        if seq_rows % t == 0 or (t % seq_rows == 0 and n % t == 0):
            best = t
    return best


def _trunk_layer(x, st, lp, *, bsz, seq_rows, pad, bb_n, s5_tb, tm_proj, tm_merge):
    rw_shift0, rw_wkv0, s5_re0, s5_im0, ml_conv0, ml_c0, ml_n0, ml_m0 = st
    p = _proj(x, lp["w_all"], tm=tm_proj)
    p3 = p.reshape(bsz, seq_rows, P_COLS)

    prep = _rw_prep(p3, rw_shift0[:, None, :RW_RKV], rw_shift0[:, None, RW_RKV:], lp)
    y_rw, wkv1 = _rw_scan(prep[:7], p3, _pack_wkv(rw_wkv0), lp, bb_n=bb_n)
    rw_shift1 = jnp.concatenate([prep[7][:, 0], prep[8][:, 0]], axis=-1)

    y_s5, h1 = _s5(p3, _pack_s5(s5_re0, s5_im0), lp, tb=s5_tb)
    s5_re1, s5_im1 = _unpack_s5(h1)

    y_ml, conv1, c1, n1, m1 = _mlstm(
        p3, _qk_pad(ml_conv0), ml_c0, ml_n0[:, :, None, :], _pad_lanes(ml_m0)[:, None, :], lp, pad=pad)
    ml_conv1 = jnp.concatenate([_unpad_heads(conv1[..., :ML_WIDTH_PAD]),
                                _unpad_heads(conv1[..., ML_WIDTH_PAD:])], axis=-1)

    n = bsz * seq_rows
    x_new = _merge(x, p, y_rw.reshape(n, RW_WIDTH), y_s5.reshape(n, S5_WIDTH),
                   y_ml.reshape(n, ML_WIDTH_PAD), lp, tm=tm_merge, seq_rows=seq_rows, pad=pad)
    return x_new, (rw_shift1, _unpack_wkv(wkv1), s5_re1, s5_im1, ml_conv1, c1, n1[:, :, 0, :],
                   m1[:, 0, :ML_HEADS])


def _run_group(x_rows, states, lps, *, bsz, seq_rows, pad, in_ln_g, in_ln_b):
    n = bsz * seq_rows
    bb_n = 2 if bsz % 2 == 0 else 1
    s5_tb = max(t for t in (BLK, 5 * BLK) if seq_rows % t == 0)
    tm_ln = _row_tile(n, seq_rows, 1024)
    tm_proj = _row_tile(n, seq_rows, 2080)
    tm_merge = _row_tile(n, seq_rows, 320)
    x = _ln_in(x_rows, in_ln_g, in_ln_b, tm=tm_ln, seq_rows=seq_rows, pad=pad)
    outs = []
    for l in range(DEPTH):
        x, st = _trunk_layer(x, states[l], lps[l], bsz=bsz, seq_rows=seq_rows, pad=pad, bb_n=bb_n,
                             s5_tb=s5_tb, tm_proj=tm_proj, tm_merge=tm_merge)
        outs.append(st)
    return x, outs


def kernel(x_prompt, x_sample, state_rwkv_shift, state_rwkv_wkv, state_s5_re, state_s5_im, state_mlstm_conv, state_mlstm_c, state_mlstm_n, state_mlstm_m, meta, in_ln_g, in_ln_b, w_in, rw_mu, rw_w0, rw_w2, rw_a0, rw_a2, rw_kk, rw_ka, rw_rk, rw_ln_g, rw_ln_b, s5_a_re, s5_a_im, s5_b_re, s5_b_im, s5_c_re, s5_c_im, s5_d, s5_log_dt, s5_w_glu, s5_b_glu, ml_conv_w, ml_conv_b, ml_b_if, ml_ln_g, b_merge, w_br_rw, w_br_s5, w_br_ml, w_out, ln_g, ln_b):
    lps = [_layer_params(l, w_in, rw_mu, rw_w0, rw_w2, rw_a0, rw_a2, rw_kk, rw_ka, rw_rk, rw_ln_g,
                         rw_ln_b, s5_a_re, s5_a_im, s5_b_re, s5_b_im, s5_c_re, s5_c_im, s5_d,
                         s5_log_dt, s5_w_glu, s5_b_glu, ml_conv_w, ml_conv_b, ml_b_if, ml_ln_g,
                         b_merge, w_br_rw, w_br_s5, w_br_ml, w_out, ln_g, ln_b)
           for l in range(DEPTH)]
    g_in, b_in = in_ln_g.reshape(1, D_MODEL), in_ln_b.reshape(1, D_MODEL)

    bp, sp = x_prompt.shape[0], x_prompt.shape[1]
    lp_rows = PAD + N_META + sp
    xp = jnp.concatenate([jnp.zeros((bp, PAD, D_MODEL), F32),
                          jnp.broadcast_to(meta[None], (bp, N_META, D_MODEL)), x_prompt], axis=1)
    z = lambda *shape: jnp.zeros((bp,) + shape, F32)
    zero_state = (z(RW_SHIFT), z(RW_HEADS, RW_HEAD_DIM, RW_HEAD_DIM), z(S5_GROUPS, S5_STATE),
                  z(S5_GROUPS, S5_STATE), z(ML_CONV - 1, 2 * ML_WIDTH),
                  z(ML_HEADS, ML_HEAD_DIM, ML_HEAD_DIM), z(ML_HEADS, ML_HEAD_DIM), z(ML_HEADS))
    yp, p_states = _run_group(xp.reshape(bp * lp_rows, D_MODEL), [zero_state] * DEPTH, lps,
                              bsz=bp, seq_rows=lp_rows, pad=PAD, in_ln_g=g_in, in_ln_b=b_in)
    y_prompt = yp.reshape(bp, lp_rows, D_MODEL)[:, PAD + N_META:]

    bs, ds = x_sample.shape[0], x_sample.shape[1]
    s_in = [(state_rwkv_shift[l], state_rwkv_wkv[l], state_s5_re[l], state_s5_im[l],
             state_mlstm_conv[l], state_mlstm_c[l], state_mlstm_n[l], state_mlstm_m[l])
            for l in range(DEPTH)]
    ys, s_states = _run_group(x_sample.reshape(bs * ds, D_MODEL), s_in, lps,
                              bsz=bs, seq_rows=ds, pad=0, in_ln_g=g_in, in_ln_b=b_in)
    y_sample = ys.reshape(bs, ds, D_MODEL)

    stack = lambda sts: tuple(jnp.stack(s, 0) for s in zip(*sts))
    return (y_prompt, y_sample) + stack(p_states) + stack(s_states)
```

```python
import functools
import math

import jax
import jax.numpy as jnp
from jax import lax
from jax.experimental import pallas as pl
from jax.experimental.pallas import tpu as pltpu

F32 = jnp.float32
BF16 = jnp.bfloat16

D_MODEL = 1024
DEPTH = 2
N_META = 16
RW_HEADS = 12
RW_HEAD_DIM = 64
RW_WIDTH = RW_HEADS * RW_HEAD_DIM
RW_PAIRS = RW_HEADS // 2
RW_LORA = 64
RW_RKV = 3 * RW_WIDTH
RW_SHIFT = RW_RKV + 2 * RW_LORA
S5_GROUPS = 32
S5_GROUP_CH = 16
S5_WIDTH = S5_GROUPS * S5_GROUP_CH
S5_STATE = 64
S5_NBLK = 4
S5_BLK_STATE = 512
ML_HEADS = 4
ML_HEAD_DIM = 192
ML_HEAD_PAD = 256
ML_WIDTH = ML_HEADS * ML_HEAD_DIM
ML_WIDTH_PAD = ML_HEADS * ML_HEAD_PAD
ML_CONV = 4
DN_ALPHA = (2 * DEPTH) ** 0.25
LN_EPS = 1e-5
RW_GN_EPS = 64e-5

LANE = 128
SUBLANES = 8
BLK = 64
PAD = BLK - N_META
SCAN_LEVELS = 3

OFF_MG, W_MG = 0, 3 * D_MODEL
OFF_S5, W_S5 = 3072, 2 * S5_WIDTH
OFF_QK, W_QK = 4096, 2 * ML_WIDTH_PAD
OFF_OZ, W_OZ = 6144, 2 * ML_WIDTH_PAD
OFF_MV, W_MV = 8192, ML_WIDTH_PAD
OFF_RKV, W_RKV = 9216, RW_RKV
OFF_RWG, W_RWG = 11520, RW_WIDTH
OFF_WA, W_WA = 12288, LANE
OFF_MI, W_MI = 12416, LANE
OFF_MF, W_MF = 12544, LANE
P_COLS = 12672
P_TN = 1152

VMEM_LIMIT = 56 * 1024 * 1024


def _cparams(sem):
    return pltpu.CompilerParams(dimension_semantics=sem, vmem_limit_bytes=VMEM_LIMIT)


def _bdot(a, b):
    return jnp.dot(a.astype(BF16), b.astype(BF16), preferred_element_type=F32)


def _hdot(a, b):
    return jnp.dot(a, b, precision=lax.Precision.HIGHEST, preferred_element_type=F32)


def _sigmoid(x):
    return 1.0 / (1.0 + jnp.exp(-x))


def _silu(x):
    return x * _sigmoid(x)


def _softplus(x):
    return jnp.maximum(x, 0.0) + jnp.log1p(jnp.exp(-jnp.abs(x)))


def _pad_row_mask(tile_idx, tm, seq_rows, pad):
    pos0 = lax.rem(tile_idx * tm, seq_rows)
    row = lax.broadcasted_iota(jnp.int32, (tm, 1), 0) + pos0
    return row < pad


def _layer_norm_rows(x, g, b):
    mu = jnp.mean(x, axis=-1, keepdims=True)
    d = x - mu
    var = jnp.mean(d * d, axis=-1, keepdims=True)
    return d * lax.rsqrt(var + LN_EPS) * g + b


def _ln_in_kernel(x_ref, g_ref, b_ref, o_ref, *, tm, seq_rows, pad):
    y = _layer_norm_rows(x_ref[...], g_ref[...], b_ref[...])
    if pad:
        y = jnp.where(_pad_row_mask(pl.program_id(0), tm, seq_rows, pad), 0.0, y)
    o_ref[...] = y


def _ln_in(x, g, b, *, tm, seq_rows, pad):
    n = x.shape[0]
    return pl.pallas_call(
        functools.partial(_ln_in_kernel, tm=tm, seq_rows=seq_rows, pad=pad),
        out_shape=jax.ShapeDtypeStruct((n, D_MODEL), F32),
        grid=(n // tm,),
        in_specs=[pl.BlockSpec((tm, D_MODEL), lambda i: (i, 0)),
                  pl.BlockSpec((1, D_MODEL), lambda i: (0, 0)),
                  pl.BlockSpec((1, D_MODEL), lambda i: (0, 0))],
        out_specs=pl.BlockSpec((tm, D_MODEL), lambda i: (i, 0)),
        compiler_params=_cparams(("parallel",)),
        name="ln_in",
    )(x, g, b)


def _proj_kernel(x_ref, w_ref, o_ref):
    o_ref[...] = jnp.dot(x_ref[...].astype(BF16), w_ref[...], preferred_element_type=F32)


def _proj(x, w_all, *, tm):
    n = x.shape[0]
    return pl.pallas_call(
        _proj_kernel,
        out_shape=jax.ShapeDtypeStruct((n, P_COLS), F32),
        grid=(n // tm, P_COLS // P_TN),
        in_specs=[pl.BlockSpec((tm, D_MODEL), lambda i, j: (i, 0)),
                  pl.BlockSpec((D_MODEL, P_TN), lambda i, j: (0, j))],
        out_specs=pl.BlockSpec((tm, P_TN), lambda i, j: (i, j)),
        compiler_params=_cparams(("parallel", "arbitrary")),
        name="proj",
    )(x, w_all)


def _split_lhs(hi_f32, lo_f32):
    return jnp.concatenate([hi_f32.astype(BF16), lo_f32.astype(BF16)], axis=1)


def _seg_lhs(x):
    hi = x.astype(BF16).astype(F32)
    return _split_lhs(hi, x - hi)


def _head_sum(x, ones2):
    parts = [jnp.dot(_seg_lhs(x[:, i * LANE:(i + 1) * LANE]), ones2, preferred_element_type=F32)
             for i in range(RW_WIDTH // LANE)]
    return jnp.concatenate(parts, axis=-1)


def _shift_rows(u, carry):
    rolled = pltpu.roll(u, 1, 0)
    row = lax.broadcasted_iota(jnp.int32, u.shape, 0)
    return jnp.where(row == 0, carry, rolled)


def _rw_prep_kernel(rkv_ref, wa_ref, sh0_rkv_ref, sh0_wa_ref, mu_rkv_ref, mu_wa_ref,
                    w0_ref, w2_ref, a0_ref, a2_ref, kk_ref, ka_ref, rk_ref, ones_ref,
                    r_out, k_out, v_out, nkk_out, kka_out, w_out, bon_out, sh_rkv_out, sh_wa_out,
                    c_rkv, c_wa):
    j = pl.program_id(1)

    @pl.when(j == 0)
    def _():
        c_rkv[...] = sh0_rkv_ref[...]
        c_wa[...] = sh0_wa_ref[...]

    u = rkv_ref[...]
    uw = wa_ref[...]
    xs = u + (_shift_rows(u, c_rkv[...]) - u) * mu_rkv_ref[...]
    xwa = uw + (_shift_rows(uw, c_wa[...]) - uw) * mu_wa_ref[...]
    c_rkv[...] = u[BLK - 1:BLK, :]
    c_wa[...] = uw[BLK - 1:BLK, :]
    sh_rkv_out[...] = u[BLK - 1:BLK, :]
    sh_wa_out[...] = uw[BLK - 1:BLK, :]

    r = xs[:, 0:RW_WIDTH]
    k = xs[:, RW_WIDTH:2 * RW_WIDTH]
    v = xs[:, 2 * RW_WIDTH:3 * RW_WIDTH]
    w_log = -_softplus(-(w0_ref[...] + _bdot(jnp.tanh(xwa), w2_ref[...]))) - 0.5
    decay = jnp.exp(-jnp.exp(w_log))
    a = _sigmoid(a0_ref[...] + _bdot(xwa, a2_ref[...]))
    ones2 = ones_ref[...]
    kk = k * kk_ref[...]
    kk = kk * lax.rsqrt(_head_sum(kk * kk, ones2) + 1e-12)
    k = k * (1.0 + (a - 1.0) * ka_ref[...])
    r_out[...] = r
    k_out[...] = k
    v_out[...] = v
    nkk_out[...] = -kk
    kka_out[...] = kk * a
    w_out[...] = decay
    bon_out[...] = _head_sum(r * k * rk_ref[...], ones2) * v


def _rw_prep(p3, sh0_rkv, sh0_wa, lp):
    bsz, seq_rows, _ = p3.shape
    nblk = seq_rows // BLK
    row_spec = pl.BlockSpec((None, BLK, RW_WIDTH), lambda b, j: (b, j, 0))
    par = lambda w: pl.BlockSpec((1, w), lambda b, j: (0, 0))
    mat = pl.BlockSpec((LANE, RW_WIDTH), lambda b, j: (0, 0))
    st = lambda w: pl.BlockSpec((None, 1, w), lambda b, j: (b, 0, 0))
    row_shape = jax.ShapeDtypeStruct((bsz, seq_rows, RW_WIDTH), F32)
    return pl.pallas_call(
        _rw_prep_kernel,
        out_shape=(row_shape,) * 7 + (jax.ShapeDtypeStruct((bsz, 1, RW_RKV), F32),
                                      jax.ShapeDtypeStruct((bsz, 1, LANE), F32)),
        grid=(bsz, nblk),
        in_specs=[pl.BlockSpec((None, BLK, W_RKV), lambda b, j: (b, j, OFF_RKV // W_RKV)),
                  pl.BlockSpec((None, BLK, W_WA), lambda b, j: (b, j, OFF_WA // W_WA)),
                  st(RW_RKV), st(LANE), par(RW_RKV), par(LANE),
                  par(RW_WIDTH), mat, par(RW_WIDTH), mat,
                  par(RW_WIDTH), par(RW_WIDTH), par(RW_WIDTH),
                  pl.BlockSpec((2 * LANE, LANE), lambda b, j: (0, 0))],
        out_specs=(row_spec,) * 7 + (st(RW_RKV), st(LANE)),
        scratch_shapes=[pltpu.VMEM((1, RW_RKV), F32), pltpu.VMEM((1, LANE), F32)],
        compiler_params=_cparams(("parallel", "arbitrary")),
        name="rw_prep",
    )(p3, p3, sh0_rkv, sh0_wa, lp["mu_rkv"], lp["mu_wa"], lp["rw_w0"], lp["rw_w2p"],
      lp["rw_a0"], lp["rw_a2p"], lp["rw_kk"], lp["rw_ka"], lp["rw_rk"], lp["ones2"])


def _rw_scan_kernel(r_ref, k_ref, v_ref, nkk_ref, kka_ref, w_ref, bon_ref, gate_ref, s0_ref,
                    lng_ref, lnb_ref, ones2_ref, y_ref, s_out, s_scr, y_scr, *, bb_n):
    j = pl.program_id(1)

    @pl.when(j == 0)
    def _():
        s_scr[...] = s0_ref[...]

    lane = lax.broadcasted_iota(jnp.int32, (RW_HEAD_DIM, LANE), 1)
    sub = lax.broadcasted_iota(jnp.int32, (RW_HEAD_DIM, LANE), 0)
    lo = lane < RW_HEAD_DIM
    lo_row = lax.broadcasted_iota(jnp.int32, (1, LANE), 1) < RW_HEAD_DIM
    diag =(lane & (RW_HEAD_DIM - 1)) == sub
    ones2 = ones2_ref[...]

    def seg_dot(lhs_tiles):
        out = jnp.dot(jnp.concatenate(lhs_tiles, axis=0), ones2, preferred_element_type=F32)
        return [out[c * RW_HEAD_DIM:(c + 1) * RW_HEAD_DIM, :] for c in range(len(lhs_tiles))]

    chains = [(bb, p) for bb in range(bb_n) for p in range(RW_PAIRS)]
    n_ch = len(chains)

    def row_group(g, carry):
        for i in range(SUBLANES):
            row = lambda ref, c: ref[chains[c][0], g, i:i + 1,
                                     chains[c][1] * LANE:(chains[c][1] + 1) * LANE]
            v_lhs = []
            for c in range(n_ch):
                v_t = row(v_ref, c)
                vh_t = v_t.astype(BF16).astype(F32)
                v_lhs.append(_split_lhs(jnp.where(diag, vh_t, 0.0), jnp.where(diag, v_t - vh_t, 0.0)))
            vcol = seg_dot(v_lhs)
            s_old = [s_scr[bb, p] for bb, p in chains]
            sa = seg_dot([_seg_lhs(s_old[c] * row(nkk_ref, c)) for c in range(n_ch)])
            s_new = []
            for c, (bb, p) in enumerate(chains):
                s = s_old[c] * row(w_ref, c) + sa[c] * row(kka_ref, c) + vcol[c] * row(k_ref, c)
                s_scr[bb, p] = s
                s_new.append(s)
            for c, (bb, p) in enumerate(chains):
                r_t = row(r_ref, c)
                y_lo = jnp.sum(s_new[c] * jnp.where(lo_row, r_t, 0.0), axis=1, keepdims=True)
                y_hi = jnp.sum(s_new[c] * jnp.where(lo_row, 0.0, r_t), axis=1, keepdims=True)
                ycol = jnp.where(diag, jnp.where(lo, y_lo, y_hi), 0.0)
                y_scr[bb, g, i:i + 1, p * LANE:(p + 1) * LANE] = jnp.sum(ycol, axis=0, keepdims=True)
        return carry

    lax.fori_loop(0, BLK // SUBLANES, row_group, 0)

    inv = 1.0 / RW_HEAD_DIM
    for bb in range(bb_n):
        y = y_scr[bb].reshape(BLK, RW_WIDTH)
        mu = _head_sum(y, ones2) * inv
        d = y - mu
        var = _head_sum(d * d, ones2) * inv
        yn = d * lax.rsqrt(var + RW_GN_EPS) * lng_ref[...] + lnb_ref[...]
        y_ref[bb] = (yn + bon_ref[bb]) * _silu(gate_ref[bb])

    @pl.when(j == pl.num_programs(1) - 1)
    def _():
        s_out[...] = s_scr[...]


def _rw_scan(prep, p3, s0, lp, *, bb_n):
    r, k, v, nkk, kka, w, bon = prep
    bsz, seq_rows, _ = r.shape
    nblk = seq_rows // BLK
    grp = BLK // SUBLANES
    by_group = lambda a: a.reshape(bsz, seq_rows // SUBLANES, SUBLANES, RW_WIDTH)
    grp_spec = pl.BlockSpec((bb_n, grp, SUBLANES, RW_WIDTH), lambda b, j: (b, j, 0, 0))
    row_spec = pl.BlockSpec((bb_n, BLK, RW_WIDTH), lambda b, j: (b, j, 0))
    par = pl.BlockSpec((1, RW_WIDTH), lambda b, j: (0, 0))
    st_spec = pl.BlockSpec((bb_n, RW_PAIRS, RW_HEAD_DIM, LANE), lambda b, j: (b, 0, 0, 0))
    return pl.pallas_call(
        functools.partial(_rw_scan_kernel, bb_n=bb_n),
        out_shape=(jax.ShapeDtypeStruct((bsz, seq_rows, RW_WIDTH), F32),
                   jax.ShapeDtypeStruct((bsz, RW_PAIRS, RW_HEAD_DIM, LANE), F32)),
        grid=(bsz // bb_n, nblk),
        in_specs=[grp_spec] * 6 + [row_spec]
        + [pl.BlockSpec((bb_n, BLK, W_RWG), lambda b, j: (b, j, OFF_RWG // W_RWG)),
           st_spec, par, par, pl.BlockSpec((2 * LANE, LANE), lambda b, j: (0, 0))],
        out_specs=(row_spec, st_spec),
        scratch_shapes=[pltpu.VMEM((bb_n, RW_PAIRS, RW_HEAD_DIM, LANE), F32),
                        pltpu.VMEM((bb_n, grp, SUBLANES, RW_WIDTH), F32)],
        compiler_params=_cparams(("parallel", "arbitrary")),
        name="rw_scan",
    )(by_group(r), by_group(k), by_group(v), by_group(nkk), by_group(kka), by_group(w), bon, p3, s0,
      lp["rw_ln_g"], lp["rw_ln_b"], lp["ones2"])


def _cmul(ar, ai, br, bi):
    return ar * br - ai * bi, ar * bi + ai * br


def _s5_prep_kernel(are_ref, aim_ref, ldt_ref, bre_ref, bim_ref,
                    bbre_out, bbim_out, lvre_out, lvim_out, pwre_out, pwim_out):
    ar, ai = are_ref[...], aim_ref[...]
    dt = jnp.exp(ldt_ref[...])
    mag = jnp.exp(ar * dt)
    lr, li = mag * jnp.cos(ai * dt), mag * jnp.sin(ai * dt)
    nr, ni = lr - 1.0, li
    den = ar * ar + ai * ai
    qr, qi = (nr * ar + ni * ai) / den, (ni * ar - nr * ai) / den
    br, bi = bre_ref[...], bim_ref[...]
    bbr, bbi = _cmul(qr[:, None, :], qi[:, None, :], br, bi)
    bbre_out[...] = bbr
    bbim_out[...] = bbi
    sq_r, sq_i = lr, li
    pows = [(lr, li)]
    for lvl in range(SCAN_LEVELS):
        lvre_out[lvl] = sq_r
        lvim_out[lvl] = sq_i
        pows = pows + [_cmul(pr, pi, sq_r, sq_i) for pr, pi in pows]
        sq_r, sq_i = _cmul(sq_r, sq_i, sq_r, sq_i)
    for t in range(SUBLANES):
        pwre_out[t] = pows[t][0]
        pwim_out[t] = pows[t][1]


def _s5_prep(a_re, a_im, log_dt, b_re_t, b_im_t):
    ghp =jax.ShapeDtypeStruct((S5_GROUPS, S5_GROUP_CH, S5_STATE), F32)
    lv = jax.ShapeDtypeStruct((SCAN_LEVELS, S5_GROUPS, S5_STATE), F32)
    pw = jax.ShapeDtypeStruct((SUBLANES, S5_GROUPS, S5_STATE), F32)
    return pl.pallas_call(
        _s5_prep_kernel,
        out_shape=(ghp, ghp, lv, lv, pw, pw),
        name="s5_prep",
    )(a_re, a_im, log_dt, b_re_t, b_im_t)


def _gelu_tanh(x):
    return 0.5 * x * (1.0 + jnp.tanh(math.sqrt(2.0 / math.pi) * (x + 0.044715 * (x * x * x))))


def _s5_kernel(p_ref, h0_ref, wb_ref, wc_ref, lv_ref, pw_ref, d_ref, wglu_ref, bglu_ref,
               y_ref, h_out, h_scr, hs_scr, *, tb):
    j = pl.program_id(1)

    @pl.when(j == 0)
    def _():
        h_scr[...] = h0_ref[...]

    u = p_ref[:, 0:S5_WIDTH]
    gate = p_ref[:, S5_WIDTH:2 * S5_WIDTH]
    y_parts = []
    for jb in range(S5_NBLK):
        bu = _bdot(u[:, jb * LANE:(jb + 1) * LANE], wb_ref[jb])
        for s in range(tb // BLK):
            xr = bu[s * BLK:(s + 1) * BLK, 0:S5_BLK_STATE]
            xi = bu[s * BLK:(s + 1) * BLK, S5_BLK_STATE:2 * S5_BLK_STATE]
            xr = xr.reshape(BLK // SUBLANES, SUBLANES, S5_BLK_STATE)
            xi = xi.reshape(BLK // SUBLANES, SUBLANES, S5_BLK_STATE)
            for lvl in range(SCAN_LEVELS):
                sr = pltpu.roll(xr, 1 << lvl, 1)
                si = pltpu.roll(xi, 1 << lvl, 1)
                lr, li = lv_ref[lvl, 2 * jb], lv_ref[lvl, 2 * jb + 1]
                xr, xi = xr + (lr * sr - li * si), xi + (lr * si + li * sr)
            xr = xr.reshape(BLK, S5_BLK_STATE)
            xi = xi.reshape(BLK, S5_BLK_STATE)
            c_r = h_scr[2 * jb:2 * jb + 1, :]
            c_i = h_scr[2 * jb + 1:2 * jb + 2, :]
            pr, pi = pw_ref[2 * jb], pw_ref[2 * jb + 1]
            for grp in range(BLK // SUBLANES):
                rows = slice(grp * SUBLANES, (grp + 1) * SUBLANES)
                hr = xr[rows, :] + (pr * c_r - pi * c_i)
                hi = xi[rows, :] + (pr * c_i + pi * c_r)
                c_r, c_i = hr[SUBLANES - 1:SUBLANES, :], hi[SUBLANES - 1:SUBLANES, :]
                out_rows = slice(s * BLK + grp * SUBLANES, s * BLK + (grp + 1) * SUBLANES)
                hs_scr[out_rows, 0:S5_BLK_STATE] = hr
                hs_scr[out_rows, S5_BLK_STATE:2 * S5_BLK_STATE] = hi
            h_scr[2 * jb:2 * jb + 1, :] = c_r
            h_scr[2 * jb + 1:2 * jb + 2, :] = c_i
        y_parts.append(_bdot(hs_scr[...], wc_ref[jb]))
    y = jnp.concatenate(y_parts, axis=-1) + d_ref[...] * u
    y = _gelu_tanh(y)
    y = y * _sigmoid(_bdot(y, wglu_ref[...]) + bglu_ref[...])
    y_ref[...] = y * _silu(gate)

    @pl.when(j == pl.num_programs(1) - 1)
    def _():
        h_out[...] = h_scr[...]


def _s5(p3, h0, lp, *, tb):
    bsz, seq_rows, _ = p3.shape
    st_spec = pl.BlockSpec((None, 2 * S5_NBLK, S5_BLK_STATE), lambda b, j: (b, 0, 0))
    full = lambda shape: pl.BlockSpec(shape, lambda b, j: (0,) * len(shape))
    return pl.pallas_call(
        functools.partial(_s5_kernel, tb=tb),
        out_shape=(jax.ShapeDtypeStruct((bsz, seq_rows, S5_WIDTH), F32),
                   jax.ShapeDtypeStruct((bsz, 2 * S5_NBLK, S5_BLK_STATE), F32)),
        grid=(bsz, seq_rows // tb),
        in_specs=[pl.BlockSpec((None, tb, W_S5), lambda b, j: (b, j, OFF_S5 // W_S5)),
                  st_spec,
                  full((S5_NBLK, LANE, 2 * S5_BLK_STATE)),
                  full((S5_NBLK, 2 * S5_BLK_STATE, LANE)),
                  full((SCAN_LEVELS, 2 * S5_NBLK, SUBLANES, S5_BLK_STATE)),
                  full((2 * S5_NBLK, SUBLANES, S5_BLK_STATE)),
                  full((1, S5_WIDTH)), full((S5_WIDTH, S5_WIDTH)), full((1, S5_WIDTH))],
        out_specs=(pl.BlockSpec((None, tb, S5_WIDTH), lambda b, j: (b, j, 0)), st_spec),
        scratch_shapes=[pltpu.VMEM((2 * S5_NBLK, S5_BLK_STATE), F32),
                        pltpu.VMEM((tb, 2 * S5_BLK_STATE), F32)],
        compiler_params=_cparams(("parallel", "arbitrary")),
        name="s5",
    )(p3, h0, lp["s5_wb"], lp["s5_wc"], lp["s5_lv"], lp["s5_pw"], lp["s5_d"],
      lp["s5_w_glu"], lp["s5_b_glu"])


def _mlstm_kernel(qk_ref, oz_ref, v_ref, gi_ref, gf_ref, conv0_ref, c0_ref, n0_ref, m0_ref,
                  cw_ref, cb_ref, bi_ref, bf_ref, lng_ref,
                  y_ref, conv_out, c_out, n_out, m_out,
                  xp_scr, c_scr, n_scr, m_scr, *, pad, bb_n):
    j = pl.program_id(1)
    halo = 8

    @pl.when(j == 0)
    def _():
        xp_scr[:, 0:halo, :] = jnp.zeros((bb_n, halo, W_QK), F32)
        xp_scr[:, halo - (ML_CONV - 1):halo, :] = conv0_ref[...]
        c_scr[...] = jnp.zeros(c_scr.shape, F32)
        c_scr[:, :, 0:ML_HEAD_DIM, 0:ML_HEAD_DIM] = c0_ref[...]
        n_scr[...] = jnp.zeros(n_scr.shape, F32)
        n_scr[:, :, :, 0:ML_HEAD_DIM] = n0_ref[...]
        m_scr[...] = m0_ref[...]

    t0 = jnp.where(j == 0, pad, 0) if pad else 0
    row1 = lax.broadcasted_iota(jnp.int32, (BLK, 1), 0)
    row_ok = row1 >= t0
    ti = lax.broadcasted_iota(jnp.int32, (BLK, BLK), 0)
    si = lax.broadcasted_iota(jnp.int32, (BLK, BLK), 1)
    pair_ok = (si <= ti) & (si >= t0)
    tril = jnp.where(si <= ti, 1.0, 0.0)
    eye = jnp.where(si == ti, 1.0, 0.0)
    ones_sq = jnp.ones((BLK, BLK), F32)
    lane_ok = lax.broadcasted_iota(jnp.int32, (1, ML_HEAD_PAD), 1) < ML_HEAD_DIM
    head = lambda a, h: a[:, h * ML_HEAD_PAD:(h + 1) * ML_HEAD_PAD]
    chains = [(bb, h) for bb in range(bb_n) for h in range(ML_HEADS)]

    log_i, log_f, b = [], [], []
    for bb in range(bb_n):
        log_i.append(gi_ref[bb] + bi_ref[...])
        log_f.append(jnp.where(row_ok, -_softplus(-(gf_ref[bb] + bf_ref[...])), 0.0))
        b.append(_hdot(tril, log_f[bb]))
    d_row = {}
    for bb, h in chains:
        x_col = log_i[bb][:, h:h + 1] - b[bb][:, h:h + 1]
        d_row[bb, h] = _hdot(ones_sq, eye * x_col)

    q, k, v = [], [], []
    for bb in range(bb_n):
        xp_scr[bb, halo:halo + BLK, :] = qk_ref[bb]
        conv = cb_ref[...] + xp_scr[bb, halo - 3:halo - 3 + BLK, :] * cw_ref[0:1, :]
        for tap in range(1, ML_CONV):
            conv = conv + xp_scr[bb, halo - 3 + tap:halo - 3 + tap + BLK, :] * cw_ref[tap:tap + 1, :]
        tail = xp_scr[bb, halo + BLK - (ML_CONV - 1):halo + BLK, :]
        xp_scr[bb, halo - (ML_CONV - 1):halo, :] = tail
        conv_out[bb] = tail
        act = _silu(conv)
        q.append(act[:, 0:ML_WIDTH_PAD])
        k.append(act[:, ML_WIDTH_PAD:2 * ML_WIDTH_PAD] * (1.0 / math.sqrt(ML_HEAD_DIM)))
        v.append(v_ref[bb])

    qk, q_c = {}, {}
    for bb, h in chains:
        qh = head(q[bb], h).astype(BF16)
        qk[bb, h] = lax.dot_general(qh, head(k[bb], h).astype(BF16), (((1,), (1,)), ((), ())),
                                    preferred_element_type=F32)
        q_c[bb, h] = jnp.dot(qh, c_scr[bb, h].astype(BF16), preferred_element_type=F32)

    g, we, keep = [], [], []
    for bb in range(bb_n):
        m_prev = m_scr[bb]
        g.append(b[bb] + m_prev)
        b_end = b[bb][BLK - 1:BLK, :]
        e_log = jnp.where(row_ok, b_end - b[bb] + log_i[bb], -jnp.inf)
        m_new = jnp.maximum(b_end + m_prev, jnp.max(e_log, axis=0, keepdims=True))
        we.append(jnp.exp(e_log - m_new))
        keep.append(jnp.exp(b_end + m_prev - m_new))
        m_scr[bb] = m_new

    s_mat, m_row, w_inter = {}, {}, {}
    for bb, h in chains:
        d = b[bb][:, h:h + 1] + d_row[bb, h]
        d = jnp.where(pair_ok, d, -jnp.inf)
        g_col = g[bb][:, h:h + 1]
        m_row[bb, h] = jnp.maximum(g_col, jnp.max(d, axis=1, keepdims=True))
        s_mat[bb, h] = qk[bb, h] * jnp.exp(d - m_row[bb, h])
        w_inter[bb, h] = jnp.exp(g_col - m_row[bb, h])

    s_v, k_v = {}, {}
    for bb, h in chains:
        vh = head(v[bb], h)
        s_v[bb, h] = _bdot(s_mat[bb, h], vh)
        k_v[bb, h] = lax.dot_general(head(k[bb], h).astype(BF16),
                                     (we[bb][:, h:h + 1] * vh).astype(BF16),
                                     (((0,), (0,)), ((), ())), preferred_element_type=F32)

    for bb, h in chains:
        hs = slice(h * ML_HEAD_PAD, (h + 1) * ML_HEAD_PAD)
        kh = head(k[bb], h)
        n_h = n_scr[bb, h]
        num = s_v[bb, h] + w_inter[bb, h] * q_c[bb, h]
        qn = jnp.sum(head(q[bb], h) * n_h, axis=1, keepdims=True)
        den = jnp.sum(s_mat[bb, h], axis=1, keepdims=True) + w_inter[bb, h] * qn
        hh = num / jnp.maximum(jnp.abs(den), jnp.exp(-m_row[bb, h]))
        mu = jnp.sum(hh, axis=1, keepdims=True) * (1.0 / ML_HEAD_DIM)
        dv = jnp.where(lane_ok, hh - mu, 0.0)
        var = jnp.sum(dv * dv, axis=1, keepdims=True) * (1.0 / ML_HEAD_DIM)
        hn = dv * lax.rsqrt(var + LN_EPS) * lng_ref[:, hs]
        y_ref[bb, :, hs] = (_sigmoid(oz_ref[bb, :, hs]) * hn
                            * _silu(oz_ref[bb, :, ML_WIDTH_PAD + h * ML_HEAD_PAD:
                                           ML_WIDTH_PAD + (h + 1) * ML_HEAD_PAD]))
        keep_h = keep[bb][:, h:h + 1]
        c_scr[bb, h] = keep_h * c_scr[bb, h] + k_v[bb, h]
        n_scr[bb, h] = keep_h * n_h + jnp.sum(we[bb][:, h:h + 1] * kh, axis=0, keepdims=True)

    @pl.when(j == pl.num_programs(1) - 1)
    def _():
        c_out[...] = c_scr[:, :, 0:ML_HEAD_DIM, 0:ML_HEAD_DIM]
        n_out[...] = n_scr[:, :, :, 0:ML_HEAD_DIM]
        m_out[...] = m_scr[...]


def _mlstm(p3, conv0, c0, n0, m0, lp, *, pad, bb_n):
    bsz, seq_rows, _ = p3.shape
    nblk = seq_rows // BLK
    blk = lambda w, off: pl.BlockSpec((bb_n, BLK, w), lambda b, j: (b, j, off // w))
    par = lambda r, w: pl.BlockSpec((r, w), lambda b, j: (0, 0))
    conv_spec = pl.BlockSpec((bb_n, ML_CONV - 1, W_QK), lambda b, j: (b, 0, 0))
    c_spec = pl.BlockSpec((bb_n, ML_HEADS, ML_HEAD_DIM, ML_HEAD_DIM), lambda b, j: (b, 0, 0, 0))
    n_spec = pl.BlockSpec((bb_n, ML_HEADS, 1, ML_HEAD_DIM), lambda b, j: (b, 0, 0, 0))
    m_spec = pl.BlockSpec((bb_n, 1, LANE), lambda b, j: (b, 0, 0))
    return pl.pallas_call(
        functools.partial(_mlstm_kernel, pad=pad, bb_n=bb_n),
        out_shape=(jax.ShapeDtypeStruct((bsz, seq_rows, ML_WIDTH_PAD), F32),
                   jax.ShapeDtypeStruct((bsz, ML_CONV - 1, W_QK), F32),
                   jax.ShapeDtypeStruct((bsz, ML_HEADS, ML_HEAD_DIM, ML_HEAD_DIM), F32),
                   jax.ShapeDtypeStruct((bsz, ML_HEADS, 1, ML_HEAD_DIM), F32),
                   jax.ShapeDtypeStruct((bsz, 1, LANE), F32)),
        grid=(bsz // bb_n, nblk),
        in_specs=[blk(W_QK, OFF_QK), blk(W_OZ, OFF_OZ), blk(W_MV, OFF_MV),
                  blk(W_MI, OFF_MI), blk(W_MF, OFF_MF),
                  conv_spec, c_spec, n_spec, m_spec,
                  par(ML_CONV, W_QK), par(1, W_QK), par(1, LANE), par(1, LANE),
                  par(1, ML_WIDTH_PAD)],
        out_specs=(pl.BlockSpec((bb_n, BLK, ML_WIDTH_PAD), lambda b, j: (b, j, 0)),
                   conv_spec, c_spec, n_spec, m_spec),
        scratch_shapes=[pltpu.VMEM((bb_n, 8 + BLK, W_QK), F32),
                        pltpu.VMEM((bb_n, ML_HEADS, ML_HEAD_PAD, ML_HEAD_PAD), F32),
                        pltpu.VMEM((bb_n, ML_HEADS, 1, ML_HEAD_PAD), F32),
                        pltpu.VMEM((bb_n, 1, LANE), F32)],
        compiler_params=_cparams(("parallel", "arbitrary")),
        name="mlstm",
    )(p3, p3, p3, p3, p3, conv0, c0, n0, m0, lp["ml_cw"], lp["ml_cb"], lp["ml_bi"], lp["ml_bf"],
      lp["ml_ln_g"])


def _merge_kernel(x_ref, mg_ref, yrw_ref, ys5_ref, yml_ref, bmg_ref, wrw_ref, ws5_ref, wml_ref,
                  wout_ref, g_ref, b_ref, o_ref, *, tm, seq_rows, pad):
    gates = _sigmoid(mg_ref[...] + bmg_ref[...])
    merged = (gates[:, 0:D_MODEL] * _bdot(yrw_ref[...], wrw_ref[...])
              + gates[:, D_MODEL:2 * D_MODEL] * _bdot(ys5_ref[...], ws5_ref[...])
              + gates[:, 2 * D_MODEL:3 * D_MODEL] * _bdot(yml_ref[...], wml_ref[...]))
    out = _bdot(merged, wout_ref[...])
    y = _layer_norm_rows(DN_ALPHA * x_ref[...] + out, g_ref[...], b_ref[...])
    if pad:
        y = jnp.where(_pad_row_mask(pl.program_id(0), tm, seq_rows, pad), 0.0, y)
    o_ref[...] = y


def _merge(x, p, y_rw, y_s5, y_ml, lp, *, tm, seq_rows, pad):
    n = x.shape[0]
    rows = lambda w: pl.BlockSpec((tm, w), lambda i: (i, 0))
    full = lambda r, w: pl.BlockSpec((r, w), lambda i: (0, 0))
    return pl.pallas_call(
        functools.partial(_merge_kernel, tm=tm, seq_rows=seq_rows, pad=pad),
        out_shape=jax.ShapeDtypeStruct((n, D_MODEL), F32),
        grid=(n // tm,),
        in_specs=[rows(D_MODEL), pl.BlockSpec((tm, W_MG), lambda i: (i, OFF_MG // W_MG)),
                  rows(RW_WIDTH), rows(S5_WIDTH), rows(ML_WIDTH_PAD),
                  full(1, W_MG), full(RW_WIDTH, D_MODEL), full(S5_WIDTH, D_MODEL),
                  full(ML_WIDTH_PAD, D_MODEL), full(D_MODEL, D_MODEL),
                  full(1, D_MODEL), full(1, D_MODEL)],
        out_specs=rows(D_MODEL),
        compiler_params=_cparams(("parallel",)),
        name="merge",
    )(x, p, y_rw, y_s5, y_ml, lp["b_merge"], lp["w_br_rw"], lp["w_br_s5"], lp["w_br_ml"],
      lp["w_out"], lp["ln_g"], lp["ln_b"])


def _pad_heads(w):
    lead = w.shape[:-1]
    w = w.reshape(lead + (ML_HEADS, ML_HEAD_DIM))
    w = jnp.pad(w, [(0, 0)] * len(lead) + [(0, 0), (0, ML_HEAD_PAD - ML_HEAD_DIM)])
    return w.reshape(lead + (ML_WIDTH_PAD,))


def _unpad_heads(w):
    lead = w.shape[:-1]
    return w.reshape(lead + (ML_HEADS, ML_HEAD_PAD))[..., :ML_HEAD_DIM].reshape(lead + (ML_WIDTH,))


def _pad_lanes(w, width=LANE):
    return jnp.pad(w, [(0, 0)] * (w.ndim - 1) + [(0, width - w.shape[-1])])


def _qk_pad(w):
    return jnp.concatenate([_pad_heads(w[..., :ML_WIDTH]), _pad_heads(w[..., ML_WIDTH:])], axis=-1)


def _layer_params(l, w_in, rw_mu, rw_w0, rw_w2, rw_a0, rw_a2, rw_kk, rw_ka, rw_rk, rw_ln_g, rw_ln_b,
                  s5_a_re, s5_a_im, s5_b_re, s5_b_im, s5_c_re, s5_c_im, s5_d, s5_log_dt, s5_w_glu,
                  s5_b_glu, ml_conv_w, ml_conv_b, ml_b_if, ml_ln_g, b_merge, w_br_rw, w_br_s5,
                  w_br_ml, w_out, ln_g, ln_b):
    w = w_in[l]
    o = 0
    cols = {}
    for name, size in (("rwc", RW_SHIFT), ("rwg", RW_WIDTH), ("s5", 2 * S5_WIDTH),
                       ("qk", 2 * ML_WIDTH), ("mv", ML_WIDTH), ("mi", ML_HEADS), ("mf", ML_HEADS),
                       ("mo", ML_WIDTH), ("mz", ML_WIDTH), ("mg", 3 * D_MODEL)):
        cols[name] = w[:, o:o + size]
        o += size
    w_all = jnp.concatenate([
        cols["mg"], cols["s5"], _qk_pad(cols["qk"]),
        _pad_heads(cols["mo"]), _pad_heads(cols["mz"]), _pad_heads(cols["mv"]),
        cols["rwc"][:, :RW_RKV], cols["rwg"], cols["rwc"][:, RW_RKV:],
        _pad_lanes(cols["mi"]), _pad_lanes(cols["mf"])], axis=1).astype(BF16)
    row = lambda a: a.reshape(1, -1)
    zeros_lora = jnp.zeros((RW_LORA, RW_WIDTH), F32)
    lane = jnp.arange(LANE)
    ones_blk = (lane[:, None] // RW_HEAD_DIM == lane[None, :] // RW_HEAD_DIM).astype(F32)

    bb_re, bb_im, lv_re, lv_im, pw_re, pw_im = _s5_prep(
        s5_a_re[l], s5_a_im[l], s5_log_dt[l].reshape(S5_GROUPS, 1),
        jnp.swapaxes(s5_b_re[l], 1, 2), jnp.swapaxes(s5_b_im[l], 1, 2))
    eye8 = jnp.eye(8, dtype=F32)
    blocked = lambda a: a.reshape((S5_NBLK, 8) + a.shape[1:])
    bb = jnp.stack([blocked(bb_re), blocked(bb_im)])
    wb = jnp.einsum("cjghp,gk->jghckp", bb, eye8).reshape(S5_NBLK, LANE, 2 * S5_BLK_STATE)
    cc = jnp.stack([blocked(s5_c_re[l]), -blocked(s5_c_im[l])])
    wc = jnp.einsum("cjghp,gk->jcgpkh", cc, eye8).reshape(S5_NBLK, 2 * S5_BLK_STATE, LANE)
    state_rows = lambda re, im: jnp.stack(
        [re.reshape(re.shape[:-2] + (S5_NBLK, S5_BLK_STATE)),
         im.reshape(im.shape[:-2] + (S5_NBLK, S5_BLK_STATE))], axis=-2)
    lv = state_rows(lv_re, lv_im).reshape(SCAN_LEVELS, 2 * S5_NBLK, 1, S5_BLK_STATE)
    row_in_group = jnp.arange(SUBLANES)[None, None, :, None]
    lv = jnp.where(row_in_group >= (1 << jnp.arange(SCAN_LEVELS))[:, None, None, None], lv, 0.0)
    pw = jnp.moveaxis(state_rows(pw_re, pw_im).reshape(SUBLANES, 2 * S5_NBLK, S5_BLK_STATE), 0, 1)

    return dict(
        w_all=w_all,
        mu_rkv=row(rw_mu[l][:RW_RKV]), mu_wa=row(rw_mu[l][RW_RKV:]),
        rw_w0=row(rw_w0[l]), rw_a0=row(rw_a0[l]),
        rw_w2p=jnp.concatenate([rw_w2[l], zeros_lora], axis=0),
        rw_a2p=jnp.concatenate([zeros_lora, rw_a2[l]], axis=0),
        rw_kk=row(rw_kk[l]), rw_ka=row(rw_ka[l]), rw_rk=row(rw_rk[l]),
        rw_ln_g=row(rw_ln_g[l]), rw_ln_b=row(rw_ln_b[l]),
        ones2=jnp.concatenate([ones_blk, ones_blk], axis=0).astype(BF16),
        s5_wb=wb.astype(BF16), s5_wc=wc.astype(BF16), s5_lv=lv, s5_pw=pw,
        s5_d=row(s5_d[l]), s5_w_glu=s5_w_glu[l].astype(BF16), s5_b_glu=row(s5_b_glu[l]),
        ml_cw=_qk_pad(ml_conv_w[l]), ml_cb=row(_qk_pad(ml_conv_b[l])),
        ml_bi=row(_pad_lanes(ml_b_if[l][:ML_HEADS])), ml_bf=row(_pad_lanes(ml_b_if[l][ML_HEADS:])),
        ml_ln_g=row(_pad_heads(ml_ln_g[l])),
        b_merge=row(b_merge[l]), w_br_rw=w_br_rw[l].astype(BF16), w_br_s5=w_br_s5[l].astype(BF16),
        w_br_ml=jnp.pad(w_br_ml[l].reshape(ML_HEADS, ML_HEAD_DIM, D_MODEL),
                        ((0, 0), (0, ML_HEAD_PAD - ML_HEAD_DIM), (0, 0))
                        ).reshape(ML_WIDTH_PAD, D_MODEL).astype(BF16),
        w_out=w_out[l].astype(BF16), ln_g=row(ln_g[l]), ln_b=row(ln_b[l]))


def _pack_wkv(s):
    b = s.shape[0]
    s = s.reshape(b, RW_PAIRS, 2, RW_HEAD_DIM, RW_HEAD_DIM)
    return jnp.swapaxes(s, 2, 3).reshape(b, RW_PAIRS, RW_HEAD_DIM, LANE)


def _unpack_wkv(s):
    b = s.shape[0]
    s = s.reshape(b, RW_PAIRS, RW_HEAD_DIM, 2, RW_HEAD_DIM)
    return jnp.swapaxes(s, 2, 3).reshape(b, RW_HEADS, RW_HEAD_DIM, RW_HEAD_DIM)


def _pack_s5(re, im):
    b = re.shape[0]
    return jnp.stack([re.reshape(b, S5_NBLK, S5_BLK_STATE), im.reshape(b, S5_NBLK, S5_BLK_STATE)],
                     axis=2).reshape(b, 2 * S5_NBLK, S5_BLK_STATE)


def _unpack_s5(h):
    b = h.shape[0]
    h = h.reshape(b, S5_NBLK, 2, S5_BLK_STATE)
    return (h[:, :, 0].reshape(b, S5_GROUPS, S5_STATE), h[:, :, 1].reshape(b, S5_GROUPS, S5_STATE))


def _row_tile(n, seq_rows, target):
    best = 8
    for t in range(8, min(n, target) + 1, 8):
        if seq_rows % t == 0 or (t % seq_rows == 0 and n % t == 0):
            best = t
    return best


def _trunk_layer(x, st, lp, *, bsz, seq_rows, pad, bb_n, s5_tb, tm_proj, tm_merge):
    rw_shift0, rw_wkv0, s5_re0, s5_im0, ml_conv0, ml_c0, ml_n0, ml_m0 = st
    p = _proj(x, lp["w_all"], tm=tm_proj)
    p3 = p.reshape(bsz, seq_rows, P_COLS)

    prep = _rw_prep(p3, rw_shift0[:, None, :RW_RKV], rw_shift0[:, None, RW_RKV:], lp)
    y_rw, wkv1 = _rw_scan(prep[:7], p3, _pack_wkv(rw_wkv0), lp, bb_n=bb_n)
    rw_shift1 = jnp.concatenate([prep[7][:, 0], prep[8][:, 0]], axis=-1)

    y_s5, h1 = _s5(p3, _pack_s5(s5_re0, s5_im0), lp, tb=s5_tb)
    s5_re1, s5_im1 = _unpack_s5(h1)

    y_ml, conv1, c1, n1, m1 = _mlstm(
        p3, _qk_pad(ml_conv0), ml_c0, ml_n0[:, :, None, :], _pad_lanes(ml_m0)[:, None, :], lp, pad=pad,
        bb_n=min(bb_n, 2))
    ml_conv1 = jnp.concatenate([_unpad_heads(conv1[..., :ML_WIDTH_PAD]),
                                _unpad_heads(conv1[..., ML_WIDTH_PAD:])], axis=-1)

    n = bsz * seq_rows
    x_new = _merge(x, p, y_rw.reshape(n, RW_WIDTH), y_s5.reshape(n, S5_WIDTH),
                   y_ml.reshape(n, ML_WIDTH_PAD), lp, tm=tm_merge, seq_rows=seq_rows, pad=pad)
    return x_new, (rw_shift1, _unpack_wkv(wkv1), s5_re1, s5_im1, ml_conv1, c1, n1[:, :, 0, :],
                   m1[:, 0, :ML_HEADS])


def _run_group(x_rows, states, lps, *, bsz, seq_rows, pad, in_ln_g, in_ln_b):
    n = bsz * seq_rows
    bb_n = 4 if bsz % 4 == 0 else (2 if bsz % 2 == 0 else 1)
    s5_tb = max(t for t in (BLK, 5 * BLK) if seq_rows % t == 0)
    tm_ln = _row_tile(n, seq_rows, 1024)
    tm_proj = _row_tile(n, seq_rows, 2080)
    tm_merge = _row_tile(n, seq_rows, 320)
    x = _ln_in(x_rows, in_ln_g, in_ln_b, tm=tm_ln, seq_rows=seq_rows, pad=pad)
    outs = []
    for l in range(DEPTH):
        x, st = _trunk_layer(x, states[l], lps[l], bsz=bsz, seq_rows=seq_rows, pad=pad, bb_n=bb_n,
                             s5_tb=s5_tb, tm_proj=tm_proj, tm_merge=tm_merge)
        outs.append(st)
    return x, outs


def kernel(x_prompt, x_sample, state_rwkv_shift, state_rwkv_wkv, state_s5_re, state_s5_im, state_mlstm_conv, state_mlstm_c, state_mlstm_n, state_mlstm_m, meta, in_ln_g, in_ln_b, w_in, rw_mu, rw_w0, rw_w2, rw_a0, rw_a2, rw_kk, rw_ka, rw_rk, rw_ln_g, rw_ln_b, s5_a_re, s5_a_im, s5_b_re, s5_b_im, s5_c_re, s5_c_im, s5_d, s5_log_dt, s5_w_glu, s5_b_glu, ml_conv_w, ml_conv_b, ml_b_if, ml_ln_g, b_merge, w_br_rw, w_br_s5, w_br_ml, w_out, ln_g, ln_b):
    lps = [_layer_params(l, w_in, rw_mu, rw_w0, rw_w2, rw_a0, rw_a2, rw_kk, rw_ka, rw_rk, rw_ln_g,
                         rw_ln_b, s5_a_re, s5_a_im, s5_b_re, s5_b_im, s5_c_re, s5_c_im, s5_d,
                         s5_log_dt, s5_w_glu, s5_b_glu, ml_conv_w, ml_conv_b, ml_b_if, ml_ln_g,
                         b_merge, w_br_rw, w_br_s5, w_br_ml, w_out, ln_g, ln_b)
           for l in range(DEPTH)]
    g_in, b_in = in_ln_g.reshape(1, D_MODEL), in_ln_b.reshape(1, D_MODEL)

    bp, sp = x_prompt.shape[0], x_prompt.shape[1]
    lp_rows = PAD + N_META + sp
    xp = jnp.concatenate([jnp.zeros((bp, PAD, D_MODEL), F32),
                          jnp.broadcast_to(meta[None], (bp, N_META, D_MODEL)), x_prompt], axis=1)
    z = lambda *shape: jnp.zeros((bp,) + shape, F32)
    zero_state = (z(RW_SHIFT), z(RW_HEADS, RW_HEAD_DIM, RW_HEAD_DIM), z(S5_GROUPS, S5_STATE),
                  z(S5_GROUPS, S5_STATE), z(ML_CONV - 1, 2 * ML_WIDTH),
                  z(ML_HEADS, ML_HEAD_DIM, ML_HEAD_DIM), z(ML_HEADS, ML_HEAD_DIM), z(ML_HEADS))
    yp, p_states = _run_group(xp.reshape(bp * lp_rows, D_MODEL), [zero_state] * DEPTH, lps,
                              bsz=bp, seq_rows=lp_rows, pad=PAD, in_ln_g=g_in, in_ln_b=b_in)
    y_prompt = yp.reshape(bp, lp_rows, D_MODEL)[:, PAD + N_META:]

    bs, ds = x_sample.shape[0], x_sample.shape[1]
    s_in = [(state_rwkv_shift[l], state_rwkv_wkv[l], state_s5_re[l], state_s5_im[l],
             state_mlstm_conv[l], state_mlstm_c[l], state_mlstm_n[l], state_mlstm_m[l])
            for l in range(DEPTH)]
    ys, s_states = _run_group(x_sample.reshape(bs * ds, D_MODEL), s_in, lps,
                              bsz=bs, seq_rows=ds, pad=0, in_ln_g=g_in, in_ln_b=b_in)
    y_sample = ys.reshape(bs, ds, D_MODEL)

    stack = lambda sts: tuple(jnp.stack(s, 0) for s in zip(*sts))
    return (y_prompt, y_sample) + stack(p_states) + stack(s_states)
```

```python
import functools
import math

import jax
import jax.numpy as jnp
from jax import lax
from jax.experimental import pallas as pl
from jax.experimental.pallas import tpu as pltpu

F32 = jnp.float32
BF16 = jnp.bfloat16

D_MODEL = 1024
DEPTH = 2
N_META = 16
RW_HEADS = 12
RW_HEAD_DIM = 64
RW_WIDTH = RW_HEADS * RW_HEAD_DIM
RW_PAIRS = RW_HEADS // 2
RW_LORA = 64
RW_RKV = 3 * RW_WIDTH
RW_SHIFT = RW_RKV + 2 * RW_LORA
S5_GROUPS = 32
S5_GROUP_CH = 16
S5_WIDTH = S5_GROUPS * S5_GROUP_CH
S5_STATE = 64
S5_NBLK = 4
S5_BLK_STATE = 512
ML_HEADS = 4
ML_HEAD_DIM = 192
ML_HEAD_PAD = 256
ML_WIDTH = ML_HEADS * ML_HEAD_DIM
ML_WIDTH_PAD = ML_HEADS * ML_HEAD_PAD
ML_CONV = 4
DN_ALPHA = (2 * DEPTH) ** 0.25
LN_EPS = 1e-5
RW_GN_EPS = 64e-5

LANE = 128
SUBLANES = 8
BLK = 64
PAD = BLK - N_META
SCAN_LEVELS = 3

OFF_MG, W_MG = 0, 3 * D_MODEL
OFF_S5, W_S5 = 3072, 2 * S5_WIDTH
OFF_QK, W_QK = 4096, 2 * ML_WIDTH_PAD
OFF_OZ, W_OZ = 6144, 2 * ML_WIDTH_PAD
OFF_MV, W_MV = 8192, ML_WIDTH_PAD
OFF_RKV, W_RKV = 9216, RW_RKV
OFF_RWG, W_RWG = 11520, RW_WIDTH
OFF_WA, W_WA = 12288, LANE
OFF_MI, W_MI = 12416, LANE
OFF_MF, W_MF = 12544, LANE
P_COLS = 12672
P_TN = 1152

VMEM_LIMIT = 56 * 1024 * 1024


def _cparams(sem):
    return pltpu.CompilerParams(dimension_semantics=sem, vmem_limit_bytes=VMEM_LIMIT)


def _bdot(a, b):
    return jnp.dot(a.astype(BF16), b.astype(BF16), preferred_element_type=F32)


def _hdot(a, b):
    return jnp.dot(a, b, precision=lax.Precision.HIGHEST, preferred_element_type=F32)


def _sigmoid(x):
    return 1.0 / (1.0 + jnp.exp(-x))


def _silu(x):
    return x * _sigmoid(x)


def _softplus(x):
    return jnp.maximum(x, 0.0) + jnp.log1p(jnp.exp(-jnp.abs(x)))


def _pad_row_mask(tile_idx, tm, seq_rows, pad):
    pos0 = lax.rem(tile_idx * tm, seq_rows)
    row = lax.broadcasted_iota(jnp.int32, (tm, 1), 0) + pos0
    return row < pad


def _layer_norm_rows(x, g, b):
    mu = jnp.mean(x, axis=-1, keepdims=True)
    d = x - mu
    var = jnp.mean(d * d, axis=-1, keepdims=True)
    return d * lax.rsqrt(var + LN_EPS) * g + b


def _ln_in_kernel(x_ref, g_ref, b_ref, o_ref, *, tm, seq_rows, pad):
    y = _layer_norm_rows(x_ref[...], g_ref[...], b_ref[...])
    if pad:
        y = jnp.where(_pad_row_mask(pl.program_id(0), tm, seq_rows, pad), 0.0, y)
    o_ref[...] = y


def _ln_in(x, g, b, *, tm, seq_rows, pad):
    n = x.shape[0]
    return pl.pallas_call(
        functools.partial(_ln_in_kernel, tm=tm, seq_rows=seq_rows, pad=pad),
        out_shape=jax.ShapeDtypeStruct((n, D_MODEL), F32),
        grid=(n // tm,),
        in_specs=[pl.BlockSpec((tm, D_MODEL), lambda i: (i, 0)),
                  pl.BlockSpec((1, D_MODEL), lambda i: (0, 0)),
                  pl.BlockSpec((1, D_MODEL), lambda i: (0, 0))],
        out_specs=pl.BlockSpec((tm, D_MODEL), lambda i: (i, 0)),
        compiler_params=_cparams(("parallel",)),
        name="ln_in",
    )(x, g, b)


def _proj_kernel(x_ref, w_ref, o_ref):
    o_ref[...] = jnp.dot(x_ref[...].astype(BF16), w_ref[...], preferred_element_type=F32)


def _proj(x, w_all, *, tm):
    n = x.shape[0]
    return pl.pallas_call(
        _proj_kernel,
        out_shape=jax.ShapeDtypeStruct((n, P_COLS), F32),
        grid=(n // tm, P_COLS // P_TN),
        in_specs=[pl.BlockSpec((tm, D_MODEL), lambda i, j: (i, 0)),
                  pl.BlockSpec((D_MODEL, P_TN), lambda i, j: (0, j))],
        out_specs=pl.BlockSpec((tm, P_TN), lambda i, j: (i, j)),
        compiler_params=_cparams(("parallel", "arbitrary")),
        name="proj",
    )(x, w_all)


def _split_lhs(hi_f32, lo_f32):
    return jnp.concatenate([hi_f32.astype(BF16), lo_f32.astype(BF16)], axis=1)


def _seg_lhs(x):
    hi = x.astype(BF16).astype(F32)
    return _split_lhs(hi, x - hi)


def _head_sum(x, ones2):
    parts = [jnp.dot(_seg_lhs(x[:, i * LANE:(i + 1) * LANE]), ones2, preferred_element_type=F32)
             for i in range(RW_WIDTH // LANE)]
    return jnp.concatenate(parts, axis=-1)


def _shift_rows(u, carry):
    rolled = pltpu.roll(u, 1, 0)
    row = lax.broadcasted_iota(jnp.int32, u.shape, 0)
    return jnp.where(row == 0, carry, rolled)


def _rw_prep_kernel(rkv_ref, wa_ref, sh0_rkv_ref, sh0_wa_ref, mu_rkv_ref, mu_wa_ref,
                    w0_ref, w2_ref, a0_ref, a2_ref, kk_ref, ka_ref, rk_ref, ones_ref,
                    at_out, rt_out, bh_out, kh_out, bc_out, kc_out, v_out, bon_out, wt_out,
                    sh_rkv_out, sh_wa_out, c_rkv, c_wa):
    j = pl.program_id(1)

    @pl.when(j == 0)
    def _():
        c_rkv[...] = sh0_rkv_ref[...]
        c_wa[...] = sh0_wa_ref[...]

    u = rkv_ref[...]
    uw = wa_ref[...]
    xs = u + (_shift_rows(u, c_rkv[...]) - u) * mu_rkv_ref[...]
    xwa = uw + (_shift_rows(uw, c_wa[...]) - uw) * mu_wa_ref[...]
    c_rkv[...] = u[BLK - 1:BLK, :]
    c_wa[...] = uw[BLK - 1:BLK, :]
    sh_rkv_out[...] = u[BLK - 1:BLK, :]
    sh_wa_out[...] = uw[BLK - 1:BLK, :]

    r = xs[:, 0:RW_WIDTH]
    k = xs[:, RW_WIDTH:2 * RW_WIDTH]
    v = xs[:, 2 * RW_WIDTH:3 * RW_WIDTH]
    w_log = -_softplus(-(w0_ref[...] + _bdot(jnp.tanh(xwa), w2_ref[...]))) - 0.5
    log_decay = -jnp.exp(w_log)
    a = _sigmoid(a0_ref[...] + _bdot(xwa, a2_ref[...]))
    ones2 = ones_ref[...]
    kk = k * kk_ref[...]
    kk = kk * lax.rsqrt(_head_sum(kk * kk, ones2) + 1e-12)
    k = k * (1.0 + (a - 1.0) * ka_ref[...])
    kka = kk * a
    ti = lax.broadcasted_iota(jnp.int32, (BLK, BLK), 0)
    si = lax.broadcasted_iota(jnp.int32, (BLK, BLK), 1)
    cum = _hdot(jnp.where(si <= ti, 1.0, 0.0), log_decay)
    cum_end = cum[BLK - 1:BLK, :]
    inv_w = jnp.exp(-cum)
    tail_w = jnp.exp(cum_end - cum)
    at_out[...] = -kk * jnp.exp(cum - log_decay)
    rt_out[...] = r * jnp.exp(cum)
    bh_out[...] = kka * inv_w
    kh_out[...] = k * inv_w
    bc_out[...] = kka * tail_w
    kc_out[...] = k * tail_w
    v_out[...] = v
    bon_out[...] = _head_sum(r * k * rk_ref[...], ones2) * v
    wt_out[...] = jnp.exp(cum_end)


def _rw_prep(p3, sh0_rkv, sh0_wa, lp):
    bsz, seq_rows, _ = p3.shape
    nblk = seq_rows // BLK
    row_spec = pl.BlockSpec((None, BLK, RW_WIDTH), lambda b, j: (b, j, 0))
    par = lambda w: pl.BlockSpec((1, w), lambda b, j: (0, 0))
    mat = pl.BlockSpec((LANE, RW_WIDTH), lambda b, j: (0, 0))
    st = lambda w: pl.BlockSpec((None, 1, w), lambda b, j: (b, 0, 0))
    row_shape = jax.ShapeDtypeStruct((bsz, seq_rows, RW_WIDTH), F32)
    wt_spec = pl.BlockSpec((None, None, 1, RW_WIDTH), lambda b, j: (b, j, 0, 0))
    return pl.pallas_call(
        _rw_prep_kernel,
        out_shape=(row_shape,) * 8 + (jax.ShapeDtypeStruct((bsz, nblk, 1, RW_WIDTH), F32),
                                      jax.ShapeDtypeStruct((bsz, 1, RW_RKV), F32),
                                      jax.ShapeDtypeStruct((bsz, 1, LANE), F32)),
        grid=(bsz, nblk),
        in_specs=[pl.BlockSpec((None, BLK, W_RKV), lambda b, j: (b, j, OFF_RKV // W_RKV)),
                  pl.BlockSpec((None, BLK, W_WA), lambda b, j: (b, j, OFF_WA // W_WA)),
                  st(RW_RKV), st(LANE), par(RW_RKV), par(LANE),
                  par(RW_WIDTH), mat, par(RW_WIDTH), mat,
                  par(RW_WIDTH), par(RW_WIDTH), par(RW_WIDTH),
                  pl.BlockSpec((2 * LANE, LANE), lambda b, j: (0, 0))],
        out_specs=(row_spec,) * 8 + (wt_spec, st(RW_RKV), st(LANE)),
        scratch_shapes=[pltpu.VMEM((1, RW_RKV), F32), pltpu.VMEM((1, LANE), F32)],
        compiler_params=_cparams(("parallel", "arbitrary")),
        name="rw_prep",
    )(p3, p3, sh0_rkv, sh0_wa, lp["mu_rkv"], lp["mu_wa"], lp["rw_w0"], lp["rw_w2p"],
      lp["rw_a0"], lp["rw_a2p"], lp["rw_kk"], lp["rw_ka"], lp["rw_rk"], lp["ones2"])


def _rw_scan_kernel(r_ref, k_ref, v_ref, nkk_ref, kka_ref, w_ref, bon_ref, gate_ref, s0_ref,
                    lng_ref, lnb_ref, ones2_ref, y_ref, s_out, s_scr, y_scr, *, bb_n):
    j = pl.program_id(1)

    @pl.when(j == 0)
    def _():
        s_scr[...] = s0_ref[...]

    lane = lax.broadcasted_iota(jnp.int32, (RW_HEAD_DIM, LANE), 1)
    sub = lax.broadcasted_iota(jnp.int32, (RW_HEAD_DIM, LANE), 0)
    lo = lane < RW_HEAD_DIM
    lo_row = lax.broadcasted_iota(jnp.int32, (1, LANE), 1) < RW_HEAD_DIM
    diag =(lane & (RW_HEAD_DIM - 1)) == sub
    ones2 = ones2_ref[...]

    def seg_dot(lhs_tiles):
        out = jnp.dot(jnp.concatenate(lhs_tiles, axis=0), ones2, preferred_element_type=F32)
        return [out[c * RW_HEAD_DIM:(c + 1) * RW_HEAD_DIM, :] for c in range(len(lhs_tiles))]

    chains = [(bb, p) for bb in range(bb_n) for p in range(RW_PAIRS)]
    n_ch = len(chains)

    def row_group(g, carry):
        for i in range(SUBLANES):
            row = lambda ref, c: ref[chains[c][0], g, i:i + 1,
                                     chains[c][1] * LANE:(chains[c][1] + 1) * LANE]
            v_lhs = []
            for c in range(n_ch):
                v_t = row(v_ref, c)
                vh_t = v_t.astype(BF16).astype(F32)
                v_lhs.append(_split_lhs(jnp.where(diag, vh_t, 0.0), jnp.where(diag, v_t - vh_t, 0.0)))
            vcol = seg_dot(v_lhs)
            s_old = [s_scr[bb, p] for bb, p in chains]
            sa = seg_dot([_seg_lhs(s_old[c] * row(nkk_ref, c)) for c in range(n_ch)])
            s_new = []
            for c, (bb, p) in enumerate(chains):
                s = s_old[c] * row(w_ref, c) + sa[c] * row(kka_ref, c) + vcol[c] * row(k_ref, c)
                s_scr[bb, p] = s
                s_new.append(s)
            for c, (bb, p) in enumerate(chains):
                r_t = row(r_ref, c)
                y_lo = jnp.sum(s_new[c] * jnp.where(lo_row, r_t, 0.0), axis=1, keepdims=True)
                y_hi = jnp.sum(s_new[c] * jnp.where(lo_row, 0.0, r_t), axis=1, keepdims=True)
                ycol = jnp.where(diag, jnp.where(lo, y_lo, y_hi), 0.0)
                y_scr[bb, g, i:i + 1, p * LANE:(p + 1) * LANE] = jnp.sum(ycol, axis=0, keepdims=True)
        return carry

    lax.fori_loop(0, BLK // SUBLANES, row_group, 0)

    inv = 1.0 / RW_HEAD_DIM
    for bb in range(bb_n):
        y = y_scr[bb].reshape(BLK, RW_WIDTH)
        mu = _head_sum(y, ones2) * inv
        d = y - mu
        var = _head_sum(d * d, ones2) * inv
        yn = d * lax.rsqrt(var + RW_GN_EPS) * lng_ref[...] + lnb_ref[...]
        y_ref[bb] = (yn + bon_ref[bb]) * _silu(gate_ref[bb])

    @pl.when(j == pl.num_programs(1) - 1)
    def _():
        s_out[...] = s_scr[...]


def _rw_scan(prep, p3, s0, lp, *, bb_n):
    r, k, v, nkk, kka, w, bon = prep
    bsz, seq_rows, _ = r.shape
    nblk = seq_rows // BLK
    grp = BLK // SUBLANES
    by_group = lambda a: a.reshape(bsz, seq_rows // SUBLANES, SUBLANES, RW_WIDTH)
    grp_spec = pl.BlockSpec((bb_n, grp, SUBLANES, RW_WIDTH), lambda b, j: (b, j, 0, 0))
    row_spec = pl.BlockSpec((bb_n, BLK, RW_WIDTH), lambda b, j: (b, j, 0))
    par = pl.BlockSpec((1, RW_WIDTH), lambda b, j: (0, 0))
    st_spec = pl.BlockSpec((bb_n, RW_PAIRS, RW_HEAD_DIM, LANE), lambda b, j: (b, 0, 0, 0))
    return pl.pallas_call(
        functools.partial(_rw_scan_kernel, bb_n=bb_n),
        out_shape=(jax.ShapeDtypeStruct((bsz, seq_rows, RW_WIDTH), F32),
                   jax.ShapeDtypeStruct((bsz, RW_PAIRS, RW_HEAD_DIM, LANE), F32)),
        grid=(bsz // bb_n, nblk),
        in_specs=[grp_spec] * 6 + [row_spec]
        + [pl.BlockSpec((bb_n, BLK, W_RWG), lambda b, j: (b, j, OFF_RWG // W_RWG)),
           st_spec, par, par, pl.BlockSpec((2 * LANE, LANE), lambda b, j: (0, 0))],
        out_specs=(row_spec, st_spec),
        scratch_shapes=[pltpu.VMEM((bb_n, RW_PAIRS, RW_HEAD_DIM, LANE), F32),
                        pltpu.VMEM((bb_n, grp, SUBLANES, RW_WIDTH), F32)],
        compiler_params=_cparams(("parallel", "arbitrary")),
        name="rw_scan",
    )(by_group(r), by_group(k), by_group(v), by_group(nkk), by_group(kka), by_group(w), bon, p3, s0,
      lp["rw_ln_g"], lp["rw_ln_b"], lp["ones2"])


def _split3_lhs(a):
    hi = a.astype(BF16)
    lo = (a - hi.astype(F32)).astype(BF16)
    return jnp.concatenate([hi, lo, hi], axis=1)


def _split3_rhs_rows(b):
    hi = b.astype(BF16)
    lo = (b - hi.astype(F32)).astype(BF16)
    return jnp.concatenate([hi, hi, lo], axis=1)


def _split3_rhs(b):
    hi = b.astype(BF16)
    lo = (b - hi.astype(F32)).astype(BF16)
    return jnp.concatenate([hi, hi, lo], axis=0)


def _dot_nt(lhs3, rhs3_rows):
    return lax.dot_general(lhs3, rhs3_rows, (((1,), (1,)), ((), ())), preferred_element_type=F32)


def _dot_nn(lhs3, rhs3):
    return jnp.dot(lhs3, rhs3, preferred_element_type=F32)


def _rw_chunk_kernel(at_ref, rt_ref, bh_ref, kh_ref, bc_ref, kc_ref, wt_ref, vt_ref, s0_ref,
                     y_ref, s_out, s_scr, *, bb_n):
    j = pl.program_id(1)

    @pl.when(j == 0)
    def _():
        s_scr[...] = s0_ref[...]

    lo_lane = lax.broadcasted_iota(jnp.int32, (1, LANE), 1) < RW_HEAD_DIM
    row_i = lax.broadcasted_iota(jnp.int32, (LANE, LANE), 0)
    col_i = lax.broadcasted_iota(jnp.int32, (LANE, LANE), 1)
    same_head = (row_i < RW_HEAD_DIM) == (col_i < RW_HEAD_DIM)
    strict = same_head & ((row_i & (RW_HEAD_DIM - 1)) < (col_i & (RW_HEAD_DIM - 1)))
    incl = same_head & ((row_i & (RW_HEAD_DIM - 1)) <= (col_i & (RW_HEAD_DIM - 1)))

    def by_head(x):
        return jnp.concatenate([jnp.where(lo_lane, x, 0.0), jnp.where(lo_lane, 0.0, x)], axis=0)

    chains = [(bb, p) for bb in range(bb_n) for p in range(RW_PAIRS)]
    n_ch = len(chains)
    tile = lambda ref, c: ref[chains[c][0], :, chains[c][1] * LANE:(chains[c][1] + 1) * LANE]

    ar_rows, gb, hb3, gk_hk_kc3, bc3 = [], [], [], [], []
    for c in range(n_ch):
        ar = jnp.concatenate([by_head(tile(at_ref, c)), by_head(tile(rt_ref, c))], axis=0)
        bk = jnp.concatenate([by_head(tile(bh_ref, c)), by_head(tile(kh_ref, c))], axis=0)
        ar_rows.append(_split3_rhs_rows(ar))
        gram = _dot_nt(_split3_lhs(bk), ar_rows[c])
        gb.append(jnp.where(strict, gram[0:LANE, 0:LANE], 0.0))
        hb = jnp.where(incl, gram[0:LANE, LANE:2 * LANE], 0.0)
        gk = jnp.where(strict, gram[LANE:2 * LANE, 0:LANE], 0.0)
        hk = jnp.where(incl, gram[LANE:2 * LANE, LANE:2 * LANE], 0.0)
        kc = by_head(tile(kc_ref, c))
        gk_hk_kc3.append(_split3_rhs(jnp.concatenate([gk, hk, kc], axis=1)))
        bc3.append(_split3_rhs(jnp.concatenate([hb, by_head(tile(bc_ref, c))], axis=1)))
    v_terms = [_dot_nn(_split3_lhs(vt_ref[bb, p]), gk_hk_kc3[c]) for c, (bb, p) in enumerate(chains)]
    powers = [[_split3_rhs(g) for g in gb]]
    cur = gb
    for _ in range(5):
        cur = [_dot_nn(_split3_lhs(cur[c]), powers[-1][c]) for c in range(n_ch)]
        powers.append([_split3_rhs(g) for g in cur])

    s_old = [s_scr[bb, p] for bb, p in chains]
    uy = [_dot_nt(_split3_lhs(s_old[c]), ar_rows[c]) for c in range(n_ch)]
    x = [uy[c][:, 0:LANE] + v_terms[c][:, 0:LANE] for c in range(n_ch)]
    for lvl in range(6):
        x = [x[c] + _dot_nn(_split3_lhs(x[c]), powers[lvl][c]) for c in range(n_ch)]
    out2 = [_dot_nn(_split3_lhs(x[c]), bc3[c]) for c in range(n_ch)]
    for c, (bb, p) in enumerate(chains):
        y_ref[bb, p] = uy[c][:, LANE:2 * LANE] + out2[c][:, 0:LANE] + v_terms[c][:, LANE:2 * LANE]
        wt = wt_ref[bb, :, p * LANE:(p + 1) * LANE]
        s_scr[bb, p] = s_old[c] * wt + out2[c][:, LANE:2 * LANE] + v_terms[c][:, 2 * LANE:3 * LANE]

    @pl.when(j == pl.num_programs(1) - 1)
    def _():
        s_out[...] = s_scr[...]


def _rw_chunk(prep, s0, *, bb_n):
    at, rt, bh, kh, bc, kc, v, _, wt = prep
    bsz, seq_rows, _ = at.shape
    nblk = seq_rows // BLK
    vt = v.reshape(bsz, nblk, BLK, RW_PAIRS, 2, RW_HEAD_DIM)
    vt = jnp.transpose(vt, (0, 1, 3, 5, 4, 2)).reshape(bsz, nblk, RW_PAIRS, RW_HEAD_DIM, LANE)
    row_spec = pl.BlockSpec((bb_n, BLK, RW_WIDTH), lambda b, j: (b, j, 0))
    col_spec = pl.BlockSpec((bb_n, None, RW_PAIRS, RW_HEAD_DIM, LANE), lambda b, j: (b, j, 0, 0, 0))
    st_spec = pl.BlockSpec((bb_n, RW_PAIRS, RW_HEAD_DIM, LANE), lambda b, j: (b, 0, 0, 0))
    y_col, s1 = pl.pallas_call(
        functools.partial(_rw_chunk_kernel, bb_n=bb_n),
        out_shape=(jax.ShapeDtypeStruct((bsz, nblk, RW_PAIRS, RW_HEAD_DIM, LANE), F32),
                   jax.ShapeDtypeStruct((bsz, RW_PAIRS, RW_HEAD_DIM, LANE), F32)),
        grid=(bsz // bb_n, nblk),
        in_specs=[row_spec] * 6
        + [pl.BlockSpec((bb_n, None, 1, RW_WIDTH), lambda b, j: (b, j, 0, 0)), col_spec, st_spec],
        out_specs=(col_spec, st_spec),
        scratch_shapes=[pltpu.VMEM((bb_n, RW_PAIRS, RW_HEAD_DIM, LANE), F32)],
        compiler_params=_cparams(("parallel", "arbitrary")),
        name="rw_chunk",
    )(at, rt, bh, kh, bc, kc, wt, vt, s0)
    y = y_col.reshape(bsz, nblk, RW_PAIRS, RW_HEAD_DIM, 2, BLK)
    y = jnp.transpose(y, (0, 1, 5, 2, 4, 3)).reshape(bsz * seq_rows, RW_WIDTH)
    return y, s1


def _rw_post_kernel(y_ref, bon_ref, gate_ref, lng_ref, lnb_ref, ones2_ref, o_ref):
    ones2 = ones2_ref[...]
    y = y_ref[...]
    inv = 1.0 / RW_HEAD_DIM
    mu = _head_sum(y, ones2) * inv
    d = y - mu
    var = _head_sum(d * d, ones2) * inv
    yn = d * lax.rsqrt(var + RW_GN_EPS) * lng_ref[...] + lnb_ref[...]
    o_ref[...] = (yn + bon_ref[...]) * _silu(gate_ref[...])


def _rw_post(y, bon, p, lp, *, tm):
    n = y.shape[0]
    rows = pl.BlockSpec((tm, RW_WIDTH), lambda i: (i, 0))
    par = pl.BlockSpec((1, RW_WIDTH), lambda i: (0, 0))
    return pl.pallas_call(
        _rw_post_kernel,
        out_shape=jax.ShapeDtypeStruct((n, RW_WIDTH), F32),
        grid=(n // tm,),
        in_specs=[rows, rows, pl.BlockSpec((tm, W_RWG), lambda i: (i, OFF_RWG // W_RWG)),
                  par, par, pl.BlockSpec((2 * LANE, LANE), lambda i: (0, 0))],
        out_specs=rows,
        compiler_params=_cparams(("parallel",)),
        name="rw_post",
    )(y, bon, p, lp["rw_ln_g"], lp["rw_ln_b"], lp["ones2"])


def _cmul(ar, ai, br, bi):
    return ar * br - ai * bi, ar * bi + ai * br


def _s5_prep_kernel(are_ref, aim_ref, ldt_ref, bre_ref, bim_ref,
                    bbre_out, bbim_out, lvre_out, lvim_out, pwre_out, pwim_out):
    ar, ai = are_ref[...], aim_ref[...]
    dt = jnp.exp(ldt_ref[...])
    mag = jnp.exp(ar * dt)
    lr, li = mag * jnp.cos(ai * dt), mag * jnp.sin(ai * dt)
    nr, ni = lr - 1.0, li
    den = ar * ar + ai * ai
    qr, qi = (nr * ar + ni * ai) / den, (ni * ar - nr * ai) / den
    br, bi = bre_ref[...], bim_ref[...]
    bbr, bbi = _cmul(qr[:, None, :], qi[:, None, :], br, bi)
    bbre_out[...] = bbr
    bbim_out[...] = bbi
    sq_r, sq_i = lr, li
    pows = [(lr, li)]
    for lvl in range(SCAN_LEVELS):
        lvre_out[lvl] = sq_r
        lvim_out[lvl] = sq_i
        pows = pows + [_cmul(pr, pi, sq_r, sq_i) for pr, pi in pows]
        sq_r, sq_i = _cmul(sq_r, sq_i, sq_r, sq_i)
    for t in range(SUBLANES):
        pwre_out[t] = pows[t][0]
        pwim_out[t] = pows[t][1]


def _s5_prep(a_re, a_im, log_dt, b_re_t, b_im_t):
    ghp =jax.ShapeDtypeStruct((S5_GROUPS, S5_GROUP_CH, S5_STATE), F32)
    lv = jax.ShapeDtypeStruct((SCAN_LEVELS, S5_GROUPS, S5_STATE), F32)
    pw = jax.ShapeDtypeStruct((SUBLANES, S5_GROUPS, S5_STATE), F32)
    return pl.pallas_call(
        _s5_prep_kernel,
        out_shape=(ghp, ghp, lv, lv, pw, pw),
        name="s5_prep",
    )(a_re, a_im, log_dt, b_re_t, b_im_t)


def _gelu_tanh(x):
    return 0.5 * x * (1.0 + jnp.tanh(math.sqrt(2.0 / math.pi) * (x + 0.044715 * (x * x * x))))


def _s5_kernel(p_ref, h0_ref, wb_ref, wc_ref, lv_ref, pw_ref, d_ref, wglu_ref, bglu_ref,
               y_ref, h_out, h_scr, hs_scr, *, tb):
    j = pl.program_id(1)

    @pl.when(j == 0)
    def _():
        h_scr[...] = h0_ref[...]

    u = p_ref[:, 0:S5_WIDTH]
    gate = p_ref[:, S5_WIDTH:2 * S5_WIDTH]
    y_parts = []
    for jb in range(S5_NBLK):
        bu = _bdot(u[:, jb * LANE:(jb + 1) * LANE], wb_ref[jb])
        for s in range(tb // BLK):
            xr = bu[s * BLK:(s + 1) * BLK, 0:S5_BLK_STATE]
            xi = bu[s * BLK:(s + 1) * BLK, S5_BLK_STATE:2 * S5_BLK_STATE]
            xr = xr.reshape(BLK // SUBLANES, SUBLANES, S5_BLK_STATE)
            xi = xi.reshape(BLK // SUBLANES, SUBLANES, S5_BLK_STATE)
            for lvl in range(SCAN_LEVELS):
                sr = pltpu.roll(xr, 1 << lvl, 1)
                si = pltpu.roll(xi, 1 << lvl, 1)
                lr, li = lv_ref[lvl, 2 * jb], lv_ref[lvl, 2 * jb + 1]
                xr, xi = xr + (lr * sr - li * si), xi + (lr * si + li * sr)
            xr = xr.reshape(BLK, S5_BLK_STATE)
            xi = xi.reshape(BLK, S5_BLK_STATE)
            c_r = h_scr[2 * jb:2 * jb + 1, :]
            c_i = h_scr[2 * jb + 1:2 * jb + 2, :]
            pr, pi = pw_ref[2 * jb], pw_ref[2 * jb + 1]
            for grp in range(BLK // SUBLANES):
                rows = slice(grp * SUBLANES, (grp + 1) * SUBLANES)
                hr = xr[rows, :] + (pr * c_r - pi * c_i)
                hi = xi[rows, :] + (pr * c_i + pi * c_r)
                c_r, c_i = hr[SUBLANES - 1:SUBLANES, :], hi[SUBLANES - 1:SUBLANES, :]
                out_rows = slice(s * BLK + grp * SUBLANES, s * BLK + (grp + 1) * SUBLANES)
                hs_scr[out_rows, 0:S5_BLK_STATE] = hr
                hs_scr[out_rows, S5_BLK_STATE:2 * S5_BLK_STATE] = hi
            h_scr[2 * jb:2 * jb + 1, :] = c_r
            h_scr[2 * jb + 1:2 * jb + 2, :] = c_i
        y_parts.append(_bdot(hs_scr[...], wc_ref[jb]))
    y = jnp.concatenate(y_parts, axis=-1) + d_ref[...] * u
    y = _gelu_tanh(y)
    y = y * _sigmoid(_bdot(y, wglu_ref[...]) + bglu_ref[...])
    y_ref[...] = y * _silu(gate)

    @pl.when(j == pl.num_programs(1) - 1)
    def _():
        h_out[...] = h_scr[...]


def _s5(p3, h0, lp, *, tb):
    bsz, seq_rows, _ = p3.shape
    st_spec = pl.BlockSpec((None, 2 * S5_NBLK, S5_BLK_STATE), lambda b, j: (b, 0, 0))
    full = lambda shape: pl.BlockSpec(shape, lambda b, j: (0,) * len(shape))
    return pl.pallas_call(
        functools.partial(_s5_kernel, tb=tb),
        out_shape=(jax.ShapeDtypeStruct((bsz, seq_rows, S5_WIDTH), F32),
                   jax.ShapeDtypeStruct((bsz, 2 * S5_NBLK, S5_BLK_STATE), F32)),
        grid=(bsz, seq_rows // tb),
        in_specs=[pl.BlockSpec((None, tb, W_S5), lambda b, j: (b, j, OFF_S5 // W_S5)),
                  st_spec,
                  full((S5_NBLK, LANE, 2 * S5_BLK_STATE)),
                  full((S5_NBLK, 2 * S5_BLK_STATE, LANE)),
                  full((SCAN_LEVELS, 2 * S5_NBLK, SUBLANES, S5_BLK_STATE)),
                  full((2 * S5_NBLK, SUBLANES, S5_BLK_STATE)),
                  full((1, S5_WIDTH)), full((S5_WIDTH, S5_WIDTH)), full((1, S5_WIDTH))],
        out_specs=(pl.BlockSpec((None, tb, S5_WIDTH), lambda b, j: (b, j, 0)), st_spec),
        scratch_shapes=[pltpu.VMEM((2 * S5_NBLK, S5_BLK_STATE), F32),
                        pltpu.VMEM((tb, 2 * S5_BLK_STATE), F32)],
        compiler_params=_cparams(("parallel", "arbitrary")),
        name="s5",
    )(p3, h0, lp["s5_wb"], lp["s5_wc"], lp["s5_lv"], lp["s5_pw"], lp["s5_d"],
      lp["s5_w_glu"], lp["s5_b_glu"])


def _mlstm_kernel(qk_ref, oz_ref, v_ref, gi_ref, gf_ref, conv0_ref, c0_ref, n0_ref, m0_ref,
                  cw_ref, cb_ref, bi_ref, bf_ref, lng_ref,
                  y_ref, conv_out, c_out, n_out, m_out,
                  xp_scr, c_scr, n_scr, m_scr, *, pad, bb_n):
    j = pl.program_id(1)
    halo = 8

    @pl.when(j == 0)
    def _():
        xp_scr[:, 0:halo, :] = jnp.zeros((bb_n, halo, W_QK), F32)
        xp_scr[:, halo - (ML_CONV - 1):halo, :] = conv0_ref[...]
        c_scr[...] = jnp.zeros(c_scr.shape, F32)
        c_scr[:, :, 0:ML_HEAD_DIM, 0:ML_HEAD_DIM] = c0_ref[...]
        n_scr[...] = jnp.zeros(n_scr.shape, F32)
        n_scr[:, :, :, 0:ML_HEAD_DIM] = n0_ref[...]
        m_scr[...] = m0_ref[...]

    t0 = jnp.where(j == 0, pad, 0) if pad else 0
    row1 = lax.broadcasted_iota(jnp.int32, (BLK, 1), 0)
    row_ok = row1 >= t0
    ti = lax.broadcasted_iota(jnp.int32, (BLK, BLK), 0)
    si = lax.broadcasted_iota(jnp.int32, (BLK, BLK), 1)
    pair_ok = (si <= ti) & (si >= t0)
    tril = jnp.where(si <= ti, 1.0, 0.0)
    eye = jnp.where(si == ti, 1.0, 0.0)
    ones_sq = jnp.ones((BLK, BLK), F32)
    lane_ok = lax.broadcasted_iota(jnp.int32, (1, ML_HEAD_PAD), 1) < ML_HEAD_DIM
    head = lambda a, h: a[:, h * ML_HEAD_PAD:(h + 1) * ML_HEAD_PAD]
    chains = [(bb, h) for bb in range(bb_n) for h in range(ML_HEADS)]

    log_i, log_f, b = [], [], []
    for bb in range(bb_n):
        log_i.append(gi_ref[bb] + bi_ref[...])
        log_f.append(jnp.where(row_ok, -_softplus(-(gf_ref[bb] + bf_ref[...])), 0.0))
        b.append(_hdot(tril, log_f[bb]))
    d_row = {}
    for bb, h in chains:
        x_col = log_i[bb][:, h:h + 1] - b[bb][:, h:h + 1]
        d_row[bb, h] = _hdot(ones_sq, eye * x_col)

    q, k, v = [], [], []
    for bb in range(bb_n):
        xp_scr[bb, halo:halo + BLK, :] = qk_ref[bb]
        conv = cb_ref[...] + xp_scr[bb, halo - 3:halo - 3 + BLK, :] * cw_ref[0:1, :]
        for tap in range(1, ML_CONV):
            conv = conv + xp_scr[bb, halo - 3 + tap:halo - 3 + tap + BLK, :] * cw_ref[tap:tap + 1, :]
        tail = xp_scr[bb, halo + BLK - (ML_CONV - 1):halo + BLK, :]
        xp_scr[bb, halo - (ML_CONV - 1):halo, :] = tail
        conv_out[bb] = tail
        act = _silu(conv)
        q.append(act[:, 0:ML_WIDTH_PAD])
        k.append(act[:, ML_WIDTH_PAD:2 * ML_WIDTH_PAD] * (1.0 / math.sqrt(ML_HEAD_DIM)))
        v.append(v_ref[bb])

    qk, q_c = {}, {}
    for bb, h in chains:
        qh = head(q[bb], h).astype(BF16)
        qk[bb, h] = lax.dot_general(qh, head(k[bb], h).astype(BF16), (((1,), (1,)), ((), ())),
                                    preferred_element_type=F32)
        q_c[bb, h] = jnp.dot(qh, c_scr[bb, h].astype(BF16), preferred_element_type=F32)

    g, we, keep = [], [], []
    for bb in range(bb_n):
        m_prev = m_scr[bb]
        g.append(b[bb] + m_prev)
        b_end = b[bb][BLK - 1:BLK, :]
        e_log = jnp.where(row_ok, b_end - b[bb] + log_i[bb], -jnp.inf)
        m_new = jnp.maximum(b_end + m_prev, jnp.max(e_log, axis=0, keepdims=True))
        we.append(jnp.exp(e_log - m_new))
        keep.append(jnp.exp(b_end + m_prev - m_new))
        m_scr[bb] = m_new

    s_mat, m_row, w_inter = {}, {}, {}
    for bb, h in chains:
        d = b[bb][:, h:h + 1] + d_row[bb, h]
        d = jnp.where(pair_ok, d, -jnp.inf)
        g_col = g[bb][:, h:h + 1]
        m_row[bb, h] = jnp.maximum(g_col, jnp.max(d, axis=1, keepdims=True))
        s_mat[bb, h] = qk[bb, h] * jnp.exp(d - m_row[bb, h])
        w_inter[bb, h] = jnp.exp(g_col - m_row[bb, h])

    s_v, k_v = {}, {}
    for bb, h in chains:
        vh = head(v[bb], h)
        s_v[bb, h] = _bdot(s_mat[bb, h], vh)
        k_v[bb, h] = lax.dot_general(head(k[bb], h).astype(BF16),
                                     (we[bb][:, h:h + 1] * vh).astype(BF16),
                                     (((0,), (0,)), ((), ())), preferred_element_type=F32)

    for bb, h in chains:
        hs = slice(h * ML_HEAD_PAD, (h + 1) * ML_HEAD_PAD)
        kh = head(k[bb], h)
        n_h = n_scr[bb, h]
        num = s_v[bb, h] + w_inter[bb, h] * q_c[bb, h]
        qn = jnp.sum(head(q[bb], h) * n_h, axis=1, keepdims=True)
        den = jnp.sum(s_mat[bb, h], axis=1, keepdims=True) + w_inter[bb, h] * qn
        hh = num / jnp.maximum(jnp.abs(den), jnp.exp(-m_row[bb, h]))
        mu = jnp.sum(hh, axis=1, keepdims=True) * (1.0 / ML_HEAD_DIM)
        dv = jnp.where(lane_ok, hh - mu, 0.0)
        var = jnp.sum(dv * dv, axis=1, keepdims=True) * (1.0 / ML_HEAD_DIM)
        hn = dv * lax.rsqrt(var + LN_EPS) * lng_ref[:, hs]
        y_ref[bb, :, hs] = (_sigmoid(oz_ref[bb, :, hs]) * hn
                            * _silu(oz_ref[bb, :, ML_WIDTH_PAD + h * ML_HEAD_PAD:
                                           ML_WIDTH_PAD + (h + 1) * ML_HEAD_PAD]))
        keep_h = keep[bb][:, h:h + 1]
        c_scr[bb, h] = keep_h * c_scr[bb, h] + k_v[bb, h]
        n_scr[bb, h] = keep_h * n_h + jnp.sum(we[bb][:, h:h + 1] * kh, axis=0, keepdims=True)

    @pl.when(j == pl.num_programs(1) - 1)
    def _():
        c_out[...] = c_scr[:, :, 0:ML_HEAD_DIM, 0:ML_HEAD_DIM]
        n_out[...] = n_scr[:, :, :, 0:ML_HEAD_DIM]
        m_out[...] = m_scr[...]


def _mlstm(p3, conv0, c0, n0, m0, lp, *, pad, bb_n):
    bsz, seq_rows, _ = p3.shape
    nblk = seq_rows // BLK
    blk = lambda w, off: pl.BlockSpec((bb_n, BLK, w), lambda b, j: (b, j, off // w))
    par = lambda r, w: pl.BlockSpec((r, w), lambda b, j: (0, 0))
    conv_spec = pl.BlockSpec((bb_n, ML_CONV - 1, W_QK), lambda b, j: (b, 0, 0))
    c_spec = pl.BlockSpec((bb_n, ML_HEADS, ML_HEAD_DIM, ML_HEAD_DIM), lambda b, j: (b, 0, 0, 0))
    n_spec = pl.BlockSpec((bb_n, ML_HEADS, 1, ML_HEAD_DIM), lambda b, j: (b, 0, 0, 0))
    m_spec = pl.BlockSpec((bb_n, 1, LANE), lambda b, j: (b, 0, 0))
    return pl.pallas_call(
        functools.partial(_mlstm_kernel, pad=pad, bb_n=bb_n),
        out_shape=(jax.ShapeDtypeStruct((bsz, seq_rows, ML_WIDTH_PAD), F32),
                   jax.ShapeDtypeStruct((bsz, ML_CONV - 1, W_QK), F32),
                   jax.ShapeDtypeStruct((bsz, ML_HEADS, ML_HEAD_DIM, ML_HEAD_DIM), F32),
                   jax.ShapeDtypeStruct((bsz, ML_HEADS, 1, ML_HEAD_DIM), F32),
                   jax.ShapeDtypeStruct((bsz, 1, LANE), F32)),
        grid=(bsz // bb_n, nblk),
        in_specs=[blk(W_QK, OFF_QK), blk(W_OZ, OFF_OZ), blk(W_MV, OFF_MV),
                  blk(W_MI, OFF_MI), blk(W_MF, OFF_MF),
                  conv_spec, c_spec, n_spec, m_spec,
                  par(ML_CONV, W_QK), par(1, W_QK), par(1, LANE), par(1, LANE),
                  par(1, ML_WIDTH_PAD)],
        out_specs=(pl.BlockSpec((bb_n, BLK, ML_WIDTH_PAD), lambda b, j: (b, j, 0)),
                   conv_spec, c_spec, n_spec, m_spec),
        scratch_shapes=[pltpu.VMEM((bb_n, 8 + BLK, W_QK), F32),
                        pltpu.VMEM((bb_n, ML_HEADS, ML_HEAD_PAD, ML_HEAD_PAD), F32),
                        pltpu.VMEM((bb_n, ML_HEADS, 1, ML_HEAD_PAD), F32),
                        pltpu.VMEM((bb_n, 1, LANE), F32)],
        compiler_params=_cparams(("parallel", "arbitrary")),
        name="mlstm",
    )(p3, p3, p3, p3, p3, conv0, c0, n0, m0, lp["ml_cw"], lp["ml_cb"], lp["ml_bi"], lp["ml_bf"],
      lp["ml_ln_g"])


def _merge_kernel(x_ref, mg_ref, yrw_ref, ys5_ref, yml_ref, bmg_ref, wrw_ref, ws5_ref, wml_ref,
                  wout_ref, g_ref, b_ref, o_ref, *, tm, seq_rows, pad):
    gates = _sigmoid(mg_ref[...] + bmg_ref[...])
    merged = (gates[:, 0:D_MODEL] * _bdot(yrw_ref[...], wrw_ref[...])
              + gates[:, D_MODEL:2 * D_MODEL] * _bdot(ys5_ref[...], ws5_ref[...])
              + gates[:, 2 * D_MODEL:3 * D_MODEL] * _bdot(yml_ref[...], wml_ref[...]))
    out = _bdot(merged, wout_ref[...])
    y = _layer_norm_rows(DN_ALPHA * x_ref[...] + out, g_ref[...], b_ref[...])
    if pad:
        y = jnp.where(_pad_row_mask(pl.program_id(0), tm, seq_rows, pad), 0.0, y)
    o_ref[...] = y


def _merge(x, p, y_rw, y_s5, y_ml, lp, *, tm, seq_rows, pad):
    n = x.shape[0]
    rows = lambda w: pl.BlockSpec((tm, w), lambda i: (i, 0))
    full = lambda r, w: pl.BlockSpec((r, w), lambda i: (0, 0))
    return pl.pallas_call(
        functools.partial(_merge_kernel, tm=tm, seq_rows=seq_rows, pad=pad),
        out_shape=jax.ShapeDtypeStruct((n, D_MODEL), F32),
        grid=(n // tm,),
        in_specs=[rows(D_MODEL), pl.BlockSpec((tm, W_MG), lambda i: (i, OFF_MG // W_MG)),
                  rows(RW_WIDTH), rows(S5_WIDTH), rows(ML_WIDTH_PAD),
                  full(1, W_MG), full(RW_WIDTH, D_MODEL), full(S5_WIDTH, D_MODEL),
                  full(ML_WIDTH_PAD, D_MODEL), full(D_MODEL, D_MODEL),
                  full(1, D_MODEL), full(1, D_MODEL)],
        out_specs=rows(D_MODEL),
        compiler_params=_cparams(("parallel",)),
        name="merge",
    )(x, p, y_rw, y_s5, y_ml, lp["b_merge"], lp["w_br_rw"], lp["w_br_s5"], lp["w_br_ml"],
      lp["w_out"], lp["ln_g"], lp["ln_b"])


def _pad_heads(w):
    lead = w.shape[:-1]
    w = w.reshape(lead + (ML_HEADS, ML_HEAD_DIM))
    w = jnp.pad(w, [(0, 0)] * len(lead) + [(0, 0), (0, ML_HEAD_PAD - ML_HEAD_DIM)])
    return w.reshape(lead + (ML_WIDTH_PAD,))


def _unpad_heads(w):
    lead = w.shape[:-1]
    return w.reshape(lead + (ML_HEADS, ML_HEAD_PAD))[..., :ML_HEAD_DIM].reshape(lead + (ML_WIDTH,))


def _pad_lanes(w, width=LANE):
    return jnp.pad(w, [(0, 0)] * (w.ndim - 1) + [(0, width - w.shape[-1])])


def _qk_pad(w):
    return jnp.concatenate([_pad_heads(w[..., :ML_WIDTH]), _pad_heads(w[..., ML_WIDTH:])], axis=-1)


def _layer_params(l, w_in, rw_mu, rw_w0, rw_w2, rw_a0, rw_a2, rw_kk, rw_ka, rw_rk, rw_ln_g, rw_ln_b,
                  s5_a_re, s5_a_im, s5_b_re, s5_b_im, s5_c_re, s5_c_im, s5_d, s5_log_dt, s5_w_glu,
                  s5_b_glu, ml_conv_w, ml_conv_b, ml_b_if, ml_ln_g, b_merge, w_br_rw, w_br_s5,
                  w_br_ml, w_out, ln_g, ln_b):
    w = w_in[l]
    o = 0
    cols = {}
    for name, size in (("rwc", RW_SHIFT), ("rwg", RW_WIDTH), ("s5", 2 * S5_WIDTH),
                       ("qk", 2 * ML_WIDTH), ("mv", ML_WIDTH), ("mi", ML_HEADS), ("mf", ML_HEADS),
                       ("mo", ML_WIDTH), ("mz", ML_WIDTH), ("mg", 3 * D_MODEL)):
        cols[name] = w[:, o:o + size]
        o += size
    w_all = jnp.concatenate([
        cols["mg"], cols["s5"], _qk_pad(cols["qk"]),
        _pad_heads(cols["mo"]), _pad_heads(cols["mz"]), _pad_heads(cols["mv"]),
        cols["rwc"][:, :RW_RKV], cols["rwg"], cols["rwc"][:, RW_RKV:],
        _pad_lanes(cols["mi"]), _pad_lanes(cols["mf"])], axis=1).astype(BF16)
    row = lambda a: a.reshape(1, -1)
    zeros_lora = jnp.zeros((RW_LORA, RW_WIDTH), F32)
    lane = jnp.arange(LANE)
    ones_blk = (lane[:, None] // RW_HEAD_DIM == lane[None, :] // RW_HEAD_DIM).astype(F32)

    bb_re, bb_im, lv_re, lv_im, pw_re, pw_im = _s5_prep(
        s5_a_re[l], s5_a_im[l], s5_log_dt[l].reshape(S5_GROUPS, 1),
        jnp.swapaxes(s5_b_re[l], 1, 2), jnp.swapaxes(s5_b_im[l], 1, 2))
    eye8 = jnp.eye(8, dtype=F32)
    blocked = lambda a: a.reshape((S5_NBLK, 8) + a.shape[1:])
    bb = jnp.stack([blocked(bb_re), blocked(bb_im)])
    wb = jnp.einsum("cjghp,gk->jghckp", bb, eye8).reshape(S5_NBLK, LANE, 2 * S5_BLK_STATE)
    cc = jnp.stack([blocked(s5_c_re[l]), -blocked(s5_c_im[l])])
    wc = jnp.einsum("cjghp,gk->jcgpkh", cc, eye8).reshape(S5_NBLK, 2 * S5_BLK_STATE, LANE)
    state_rows = lambda re, im: jnp.stack(
        [re.reshape(re.shape[:-2] + (S5_NBLK, S5_BLK_STATE)),
         im.reshape(im.shape[:-2] + (S5_NBLK, S5_BLK_STATE))], axis=-2)
    lv = state_rows(lv_re, lv_im).reshape(SCAN_LEVELS, 2 * S5_NBLK, 1, S5_BLK_STATE)
    row_in_group = jnp.arange(SUBLANES)[None, None, :, None]
    lv = jnp.where(row_in_group >= (1 << jnp.arange(SCAN_LEVELS))[:, None, None, None], lv, 0.0)
    pw = jnp.moveaxis(state_rows(pw_re, pw_im).reshape(SUBLANES, 2 * S5_NBLK, S5_BLK_STATE), 0, 1)

    return dict(
        w_all=w_all,
        mu_rkv=row(rw_mu[l][:RW_RKV]), mu_wa=row(rw_mu[l][RW_RKV:]),
        rw_w0=row(rw_w0[l]), rw_a0=row(rw_a0[l]),
        rw_w2p=jnp.concatenate([rw_w2[l], zeros_lora], axis=0),
        rw_a2p=jnp.concatenate([zeros_lora, rw_a2[l]], axis=0),
        rw_kk=row(rw_kk[l]), rw_ka=row(rw_ka[l]), rw_rk=row(rw_rk[l]),
        rw_ln_g=row(rw_ln_g[l]), rw_ln_b=row(rw_ln_b[l]),
        ones2=jnp.concatenate([ones_blk, ones_blk], axis=0).astype(BF16),
        s5_wb=wb.astype(BF16), s5_wc=wc.astype(BF16), s5_lv=lv, s5_pw=pw,
        s5_d=row(s5_d[l]), s5_w_glu=s5_w_glu[l].astype(BF16), s5_b_glu=row(s5_b_glu[l]),
        ml_cw=_qk_pad(ml_conv_w[l]), ml_cb=row(_qk_pad(ml_conv_b[l])),
        ml_bi=row(_pad_lanes(ml_b_if[l][:ML_HEADS])), ml_bf=row(_pad_lanes(ml_b_if[l][ML_HEADS:])),
        ml_ln_g=row(_pad_heads(ml_ln_g[l])),
        b_merge=row(b_merge[l]), w_br_rw=w_br_rw[l].astype(BF16), w_br_s5=w_br_s5[l].astype(BF16),
        w_br_ml=jnp.pad(w_br_ml[l].reshape(ML_HEADS, ML_HEAD_DIM, D_MODEL),
                        ((0, 0), (0, ML_HEAD_PAD - ML_HEAD_DIM), (0, 0))
                        ).reshape(ML_WIDTH_PAD, D_MODEL).astype(BF16),
        w_out=w_out[l].astype(BF16), ln_g=row(ln_g[l]), ln_b=row(ln_b[l]))


def _pack_wkv(s):
    b = s.shape[0]
    s = s.reshape(b, RW_PAIRS, 2, RW_HEAD_DIM, RW_HEAD_DIM)
    return jnp.swapaxes(s, 2, 3).reshape(b, RW_PAIRS, RW_HEAD_DIM, LANE)


def _unpack_wkv(s):
    b = s.shape[0]
    s = s.reshape(b, RW_PAIRS, RW_HEAD_DIM, 2, RW_HEAD_DIM)
    return jnp.swapaxes(s, 2, 3).reshape(b, RW_HEADS, RW_HEAD_DIM, RW_HEAD_DIM)


def _pack_s5(re, im):
    b = re.shape[0]
    return jnp.stack([re.reshape(b, S5_NBLK, S5_BLK_STATE), im.reshape(b, S5_NBLK, S5_BLK_STATE)],
                     axis=2).reshape(b, 2 * S5_NBLK, S5_BLK_STATE)


def _unpack_s5(h):
    b = h.shape[0]
    h = h.reshape(b, S5_NBLK, 2, S5_BLK_STATE)
    return (h[:, :, 0].reshape(b, S5_GROUPS, S5_STATE), h[:, :, 1].reshape(b, S5_GROUPS, S5_STATE))


def _row_tile(n, seq_rows, target):
    best = 8
    for t in range(8, min(n, target) + 1, 8):
        if seq_rows % t == 0 or (t % seq_rows == 0 and n % t == 0):
            best = t
    return best


def _trunk_layer(x, st, lp, *, bsz, seq_rows, pad, bb_n, s5_tb, tm_proj, tm_merge):
    rw_shift0, rw_wkv0, s5_re0, s5_im0, ml_conv0, ml_c0, ml_n0, ml_m0 = st
    p = _proj(x, lp["w_all"], tm=tm_proj)
    p3 = p.reshape(bsz, seq_rows, P_COLS)

    prep = _rw_prep(p3, rw_shift0[:, None, :RW_RKV], rw_shift0[:, None, RW_RKV:], lp)
    n = bsz * seq_rows
    y_raw, wkv1 = _rw_chunk(prep[:9], _pack_wkv(rw_wkv0), bb_n=min(bb_n, 2))
    y_rw = _rw_post(y_raw, prep[7].reshape(n, RW_WIDTH), p, lp, tm=tm_merge)
    rw_shift1 = jnp.concatenate([prep[9][:, 0], prep[10][:, 0]], axis=-1)

    y_s5, h1 = _s5(p3, _pack_s5(s5_re0, s5_im0), lp, tb=s5_tb)
    s5_re1, s5_im1 = _unpack_s5(h1)

    y_ml, conv1, c1, n1, m1 = _mlstm(
        p3, _qk_pad(ml_conv0), ml_c0, ml_n0[:, :, None, :], _pad_lanes(ml_m0)[:, None, :], lp, pad=pad,
        bb_n=min(bb_n, 2))
    ml_conv1 = jnp.concatenate([_unpad_heads(conv1[..., :ML_WIDTH_PAD]),
                                _unpad_heads(conv1[..., ML_WIDTH_PAD:])], axis=-1)

    x_new = _merge(x, p, y_rw, y_s5.reshape(n, S5_WIDTH),
                   y_ml.reshape(n, ML_WIDTH_PAD), lp, tm=tm_merge, seq_rows=seq_rows, pad=pad)
    return x_new, (rw_shift1, _unpack_wkv(wkv1), s5_re1, s5_im1, ml_conv1, c1, n1[:, :, 0, :],
                   m1[:, 0, :ML_HEADS])


def _run_group(x_rows, states, lps, *, bsz, seq_rows, pad, in_ln_g, in_ln_b):
    n = bsz * seq_rows
    bb_n = 4 if bsz % 4 == 0 else (2 if bsz % 2 == 0 else 1)
    s5_tb = max(t for t in (BLK, 5 * BLK) if seq_rows % t == 0)
    tm_ln = _row_tile(n, seq_rows, 1024)
    tm_proj = _row_tile(n, seq_rows, 2080)
    tm_merge = _row_tile(n, seq_rows, 320)
    x = _ln_in(x_rows, in_ln_g, in_ln_b, tm=tm_ln, seq_rows=seq_rows, pad=pad)
    outs = []
    for l in range(DEPTH):
        x, st = _trunk_layer(x, states[l], lps[l], bsz=bsz, seq_rows=seq_rows, pad=pad, bb_n=bb_n,
                             s5_tb=s5_tb, tm_proj=tm_proj, tm_merge=tm_merge)
        outs.append(st)
    return x, outs


def kernel(x_prompt, x_sample, state_rwkv_shift, state_rwkv_wkv, state_s5_re, state_s5_im, state_mlstm_conv, state_mlstm_c, state_mlstm_n, state_mlstm_m, meta, in_ln_g, in_ln_b, w_in, rw_mu, rw_w0, rw_w2, rw_a0, rw_a2, rw_kk, rw_ka, rw_rk, rw_ln_g, rw_ln_b, s5_a_re, s5_a_im, s5_b_re, s5_b_im, s5_c_re, s5_c_im, s5_d, s5_log_dt, s5_w_glu, s5_b_glu, ml_conv_w, ml_conv_b, ml_b_if, ml_ln_g, b_merge, w_br_rw, w_br_s5, w_br_ml, w_out, ln_g, ln_b):
    lps = [_layer_params(l, w_in, rw_mu, rw_w0, rw_w2, rw_a0, rw_a2, rw_kk, rw_ka, rw_rk, rw_ln_g,
                         rw_ln_b, s5_a_re, s5_a_im, s5_b_re, s5_b_im, s5_c_re, s5_c_im, s5_d,
                         s5_log_dt, s5_w_glu, s5_b_glu, ml_conv_w, ml_conv_b, ml_b_if, ml_ln_g,
                         b_merge, w_br_rw, w_br_s5, w_br_ml, w_out, ln_g, ln_b)
           for l in range(DEPTH)]
    g_in, b_in = in_ln_g.reshape(1, D_MODEL), in_ln_b.reshape(1, D_MODEL)

    bp, sp = x_prompt.shape[0], x_prompt.shape[1]
    lp_rows = PAD + N_META + sp
    xp = jnp.concatenate([jnp.zeros((bp, PAD, D_MODEL), F32),
                          jnp.broadcast_to(meta[None], (bp, N_META, D_MODEL)), x_prompt], axis=1)
    z = lambda *shape: jnp.zeros((bp,) + shape, F32)
    zero_state = (z(RW_SHIFT), z(RW_HEADS, RW_HEAD_DIM, RW_HEAD_DIM), z(S5_GROUPS, S5_STATE),
                  z(S5_GROUPS, S5_STATE), z(ML_CONV - 1, 2 * ML_WIDTH),
                  z(ML_HEADS, ML_HEAD_DIM, ML_HEAD_DIM), z(ML_HEADS, ML_HEAD_DIM), z(ML_HEADS))
    yp, p_states = _run_group(xp.reshape(bp * lp_rows, D_MODEL), [zero_state] * DEPTH, lps,
                              bsz=bp, seq_rows=lp_rows, pad=PAD, in_ln_g=g_in, in_ln_b=b_in)
    y_prompt = yp.reshape(bp, lp_rows, D_MODEL)[:, PAD + N_META:]

    bs, ds = x_sample.shape[0], x_sample.shape[1]
    s_in = [(state_rwkv_shift[l], state_rwkv_wkv[l], state_s5_re[l], state_s5_im[l],
             state_mlstm_conv[l], state_mlstm_c[l], state_mlstm_n[l], state_mlstm_m[l])
            for l in range(DEPTH)]
    ys, s_states = _run_group(x_sample.reshape(bs * ds, D_MODEL), s_in, lps,
                              bsz=bs, seq_rows=ds, pad=0, in_ln_g=g_in, in_ln_b=b_in)
    y_sample = ys.reshape(bs, ds, D_MODEL)

    stack = lambda sts: tuple(jnp.stack(s, 0) for s in zip(*sts))
    return (y_prompt, y_sample) + stack(p_states) + stack(s_states)
```

```python
import functools
import math

import jax
import jax.numpy as jnp
from jax import lax
from jax.experimental import pallas as pl
from jax.experimental.pallas import tpu as pltpu

F32 = jnp.float32
BF16 = jnp.bfloat16

D_MODEL = 1024
DEPTH = 2
N_META = 16
RW_HEADS = 12
RW_HEAD_DIM = 64
RW_WIDTH = RW_HEADS * RW_HEAD_DIM
RW_PAIRS = RW_HEADS // 2
RW_LORA = 64
RW_RKV = 3 * RW_WIDTH
RW_SHIFT = RW_RKV + 2 * RW_LORA
S5_GROUPS = 32
S5_GROUP_CH = 16
S5_WIDTH = S5_GROUPS * S5_GROUP_CH
S5_STATE = 64
S5_NBLK = 4
S5_BLK_STATE = 512
ML_HEADS = 4
ML_HEAD_DIM = 192
ML_HEAD_PAD = 256
ML_WIDTH = ML_HEADS * ML_HEAD_DIM
ML_WIDTH_PAD = ML_HEADS * ML_HEAD_PAD
ML_CONV = 4
DN_ALPHA = (2 * DEPTH) ** 0.25
LN_EPS = 1e-5
RW_GN_EPS = 64e-5

LANE = 128
SUBLANES = 8
BLK = 64
PAD = BLK - N_META
SCAN_LEVELS = 3

OFF_MG, W_MG = 0, 3 * D_MODEL
OFF_S5, W_S5 = 3072, 2 * S5_WIDTH
OFF_QK, W_QK = 4096, 2 * ML_WIDTH_PAD
OFF_OZ, W_OZ = 6144, 2 * ML_WIDTH_PAD
OFF_MV, W_MV = 8192, ML_WIDTH_PAD
OFF_RKV, W_RKV = 9216, RW_RKV
OFF_RWG, W_RWG = 11520, RW_WIDTH
OFF_WA, W_WA = 12288, LANE
OFF_MI, W_MI = 12416, LANE
OFF_MF, W_MF = 12544, LANE
P_COLS = 12672
P_TN = 1152

VMEM_LIMIT = 56 * 1024 * 1024


def _cparams(sem):
    return pltpu.CompilerParams(dimension_semantics=sem, vmem_limit_bytes=VMEM_LIMIT)


def _bdot(a, b):
    return jnp.dot(a.astype(BF16), b.astype(BF16), preferred_element_type=F32)


def _hdot(a, b):
    return jnp.dot(a, b, precision=lax.Precision.HIGHEST, preferred_element_type=F32)


def _sigmoid(x):
    return 1.0 / (1.0 + jnp.exp(-x))


def _silu(x):
    return x * _sigmoid(x)


def _softplus(x):
    return jnp.maximum(x, 0.0) + jnp.log1p(jnp.exp(-jnp.abs(x)))


def _pad_row_mask(tile_idx, tm, seq_rows, pad):
    pos0 = lax.rem(tile_idx * tm, seq_rows)
    row = lax.broadcasted_iota(jnp.int32, (tm, 1), 0) + pos0
    return row < pad


def _layer_norm_rows(x, g, b):
    mu = jnp.mean(x, axis=-1, keepdims=True)
    d = x - mu
    var = jnp.mean(d * d, axis=-1, keepdims=True)
    return d * lax.rsqrt(var + LN_EPS) * g + b


def _ln_in_kernel(x_ref, g_ref, b_ref, o_ref, *, tm, seq_rows, pad):
    y = _layer_norm_rows(x_ref[...], g_ref[...], b_ref[...])
    if pad:
        y = jnp.where(_pad_row_mask(pl.program_id(0), tm, seq_rows, pad), 0.0, y)
    o_ref[...] = y


def _ln_in(x, g, b, *, tm, seq_rows, pad):
    n = x.shape[0]
    return pl.pallas_call(
        functools.partial(_ln_in_kernel, tm=tm, seq_rows=seq_rows, pad=pad),
        out_shape=jax.ShapeDtypeStruct((n, D_MODEL), F32),
        grid=(n // tm,),
        in_specs=[pl.BlockSpec((tm, D_MODEL), lambda i: (i, 0)),
                  pl.BlockSpec((1, D_MODEL), lambda i: (0, 0)),
                  pl.BlockSpec((1, D_MODEL), lambda i: (0, 0))],
        out_specs=pl.BlockSpec((tm, D_MODEL), lambda i: (i, 0)),
        compiler_params=_cparams(("parallel",)),
        name="ln_in",
    )(x, g, b)


def _proj_kernel(x_ref, w_ref, o_ref):
    o_ref[...] = jnp.dot(x_ref[...].astype(BF16), w_ref[...], preferred_element_type=F32)


def _proj(x, w_all, *, tm):
    n = x.shape[0]
    return pl.pallas_call(
        _proj_kernel,
        out_shape=jax.ShapeDtypeStruct((n, P_COLS), F32),
        grid=(n // tm, P_COLS // P_TN),
        in_specs=[pl.BlockSpec((tm, D_MODEL), lambda i, j: (i, 0)),
                  pl.BlockSpec((D_MODEL, P_TN), lambda i, j: (0, j))],
        out_specs=pl.BlockSpec((tm, P_TN), lambda i, j: (i, j)),
        compiler_params=_cparams(("parallel", "arbitrary")),
        name="proj",
    )(x, w_all)


def _split_lhs(hi_f32, lo_f32):
    return jnp.concatenate([hi_f32.astype(BF16), lo_f32.astype(BF16)], axis=1)


def _seg_lhs(x):
    hi = x.astype(BF16).astype(F32)
    return _split_lhs(hi, x - hi)


def _head_sum(x, ones2):
    parts = [jnp.dot(_seg_lhs(x[:, i * LANE:(i + 1) * LANE]), ones2, preferred_element_type=F32)
             for i in range(RW_WIDTH // LANE)]
    return jnp.concatenate(parts, axis=-1)


def _shift_rows(u, carry):
    rolled = pltpu.roll(u, 1, 0)
    row = lax.broadcasted_iota(jnp.int32, u.shape, 0)
    return jnp.where(row == 0, carry, rolled)


def _rw_prep_kernel(rkv_ref, wa_ref, sh0_rkv_ref, sh0_wa_ref, mu_rkv_ref, mu_wa_ref,
                    w0_ref, w2_ref, a0_ref, a2_ref, kk_ref, ka_ref, rk_ref, ones_ref,
                    at_out, rt_out, bh_out, kh_out, bc_out, kc_out, v_out, bon_out, wt_out,
                    sh_rkv_out, sh_wa_out, c_rkv, c_wa):
    j = pl.program_id(1)

    @pl.when(j == 0)
    def _():
        c_rkv[...] = sh0_rkv_ref[...]
        c_wa[...] = sh0_wa_ref[...]

    u = rkv_ref[...]
    uw = wa_ref[...]
    xs = u + (_shift_rows(u, c_rkv[...]) - u) * mu_rkv_ref[...]
    xwa = uw + (_shift_rows(uw, c_wa[...]) - uw) * mu_wa_ref[...]
    c_rkv[...] = u[BLK - 1:BLK, :]
    c_wa[...] = uw[BLK - 1:BLK, :]
    sh_rkv_out[...] = u[BLK - 1:BLK, :]
    sh_wa_out[...] = uw[BLK - 1:BLK, :]

    r = xs[:, 0:RW_WIDTH]
    k = xs[:, RW_WIDTH:2 * RW_WIDTH]
    v = xs[:, 2 * RW_WIDTH:3 * RW_WIDTH]
    w_log = -_softplus(-(w0_ref[...] + _bdot(jnp.tanh(xwa), w2_ref[...]))) - 0.5
    log_decay = -jnp.exp(w_log)
    a = _sigmoid(a0_ref[...] + _bdot(xwa, a2_ref[...]))
    ones2 = ones_ref[...]
    kk = k * kk_ref[...]
    kk = kk * lax.rsqrt(_head_sum(kk * kk, ones2) + 1e-12)
    k = k * (1.0 + (a - 1.0) * ka_ref[...])
    kka = kk * a
    ti = lax.broadcasted_iota(jnp.int32, (BLK, BLK), 0)
    si = lax.broadcasted_iota(jnp.int32, (BLK, BLK), 1)
    cum = _hdot(jnp.where(si <= ti, 1.0, 0.0), log_decay)
    cum_end = cum[BLK - 1:BLK, :]
    inv_w = jnp.exp(-cum)
    tail_w = jnp.exp(cum_end - cum)
    at_out[...] = -kk * jnp.exp(cum - log_decay)
    rt_out[...] = r * jnp.exp(cum)
    bh_out[...] = kka * inv_w
    kh_out[...] = k * inv_w
    bc_out[...] = kka * tail_w
    kc_out[...] = k * tail_w
    v_out[...] = v
    bon_out[...] = _head_sum(r * k * rk_ref[...], ones2) * v
    wt_out[...] = jnp.exp(cum_end)


def _rw_prep(p3, sh0_rkv, sh0_wa, lp):
    bsz, seq_rows, _ = p3.shape
    nblk = seq_rows // BLK
    row_spec = pl.BlockSpec((None, BLK, RW_WIDTH), lambda b, j: (b, j, 0))
    par = lambda w: pl.BlockSpec((1, w), lambda b, j: (0, 0))
    mat = pl.BlockSpec((LANE, RW_WIDTH), lambda b, j: (0, 0))
    st = lambda w: pl.BlockSpec((None, 1, w), lambda b, j: (b, 0, 0))
    row_shape = jax.ShapeDtypeStruct((bsz, seq_rows, RW_WIDTH), F32)
    wt_spec = pl.BlockSpec((None, None, 1, RW_WIDTH), lambda b, j: (b, j, 0, 0))
    return pl.pallas_call(
        _rw_prep_kernel,
        out_shape=(row_shape,) * 8 + (jax.ShapeDtypeStruct((bsz, nblk, 1, RW_WIDTH), F32),
                                      jax.ShapeDtypeStruct((bsz, 1, RW_RKV), F32),
                                      jax.ShapeDtypeStruct((bsz, 1, LANE), F32)),
        grid=(bsz, nblk),
        in_specs=[pl.BlockSpec((None, BLK, W_RKV), lambda b, j: (b, j, OFF_RKV // W_RKV)),
                  pl.BlockSpec((None, BLK, W_WA), lambda b, j: (b, j, OFF_WA // W_WA)),
                  st(RW_RKV), st(LANE), par(RW_RKV), par(LANE),
                  par(RW_WIDTH), mat, par(RW_WIDTH), mat,
                  par(RW_WIDTH), par(RW_WIDTH), par(RW_WIDTH),
                  pl.BlockSpec((2 * LANE, LANE), lambda b, j: (0, 0))],
        out_specs=(row_spec,) * 8 + (wt_spec, st(RW_RKV), st(LANE)),
        scratch_shapes=[pltpu.VMEM((1, RW_RKV), F32), pltpu.VMEM((1, LANE), F32)],
        compiler_params=_cparams(("parallel", "arbitrary")),
        name="rw_prep",
    )(p3, p3, sh0_rkv, sh0_wa, lp["mu_rkv"], lp["mu_wa"], lp["rw_w0"], lp["rw_w2p"],
      lp["rw_a0"], lp["rw_a2p"], lp["rw_kk"], lp["rw_ka"], lp["rw_rk"], lp["ones2"])


def _rw_scan_kernel(r_ref, k_ref, v_ref, nkk_ref, kka_ref, w_ref, bon_ref, gate_ref, s0_ref,
                    lng_ref, lnb_ref, ones2_ref, y_ref, s_out, s_scr, y_scr, *, bb_n):
    j = pl.program_id(1)

    @pl.when(j == 0)
    def _():
        s_scr[...] = s0_ref[...]

    lane = lax.broadcasted_iota(jnp.int32, (RW_HEAD_DIM, LANE), 1)
    sub = lax.broadcasted_iota(jnp.int32, (RW_HEAD_DIM, LANE), 0)
    lo = lane < RW_HEAD_DIM
    lo_row = lax.broadcasted_iota(jnp.int32, (1, LANE), 1) < RW_HEAD_DIM
    diag =(lane & (RW_HEAD_DIM - 1)) == sub
    ones2 = ones2_ref[...]

    def seg_dot(lhs_tiles):
        out = jnp.dot(jnp.concatenate(lhs_tiles, axis=0), ones2, preferred_element_type=F32)
        return [out[c * RW_HEAD_DIM:(c + 1) * RW_HEAD_DIM, :] for c in range(len(lhs_tiles))]

    chains = [(bb, p) for bb in range(bb_n) for p in range(RW_PAIRS)]
    n_ch = len(chains)

    def row_group(g, carry):
        for i in range(SUBLANES):
            row = lambda ref, c: ref[chains[c][0], g, i:i + 1,
                                     chains[c][1] * LANE:(chains[c][1] + 1) * LANE]
            v_lhs = []
            for c in range(n_ch):
                v_t = row(v_ref, c)
                vh_t = v_t.astype(BF16).astype(F32)
                v_lhs.append(_split_lhs(jnp.where(diag, vh_t, 0.0), jnp.where(diag, v_t - vh_t, 0.0)))
            vcol = seg_dot(v_lhs)
            s_old = [s_scr[bb, p] for bb, p in chains]
            sa = seg_dot([_seg_lhs(s_old[c] * row(nkk_ref, c)) for c in range(n_ch)])
            s_new = []
            for c, (bb, p) in enumerate(chains):
                s = s_old[c] * row(w_ref, c) + sa[c] * row(kka_ref, c) + vcol[c] * row(k_ref, c)
                s_scr[bb, p] = s
                s_new.append(s)
            for c, (bb, p) in enumerate(chains):
                r_t = row(r_ref, c)
                y_lo = jnp.sum(s_new[c] * jnp.where(lo_row, r_t, 0.0), axis=1, keepdims=True)
                y_hi = jnp.sum(s_new[c] * jnp.where(lo_row, 0.0, r_t), axis=1, keepdims=True)
                ycol = jnp.where(diag, jnp.where(lo, y_lo, y_hi), 0.0)
                y_scr[bb, g, i:i + 1, p * LANE:(p + 1) * LANE] = jnp.sum(ycol, axis=0, keepdims=True)
        return carry

    lax.fori_loop(0, BLK // SUBLANES, row_group, 0)

    inv = 1.0 / RW_HEAD_DIM
    for bb in range(bb_n):
        y = y_scr[bb].reshape(BLK, RW_WIDTH)
        mu = _head_sum(y, ones2) * inv
        d = y - mu
        var = _head_sum(d * d, ones2) * inv
        yn = d * lax.rsqrt(var + RW_GN_EPS) * lng_ref[...] + lnb_ref[...]
        y_ref[bb] = (yn + bon_ref[bb]) * _silu(gate_ref[bb])

    @pl.when(j == pl.num_programs(1) - 1)
    def _():
        s_out[...] = s_scr[...]


def _rw_scan(prep, p3, s0, lp, *, bb_n):
    r, k, v, nkk, kka, w, bon = prep
    bsz, seq_rows, _ = r.shape
    nblk = seq_rows // BLK
    grp = BLK // SUBLANES
    by_group = lambda a: a.reshape(bsz, seq_rows // SUBLANES, SUBLANES, RW_WIDTH)
    grp_spec = pl.BlockSpec((bb_n, grp, SUBLANES, RW_WIDTH), lambda b, j: (b, j, 0, 0))
    row_spec = pl.BlockSpec((bb_n, BLK, RW_WIDTH), lambda b, j: (b, j, 0))
    par = pl.BlockSpec((1, RW_WIDTH), lambda b, j: (0, 0))
    st_spec = pl.BlockSpec((bb_n, RW_PAIRS, RW_HEAD_DIM, LANE), lambda b, j: (b, 0, 0, 0))
    return pl.pallas_call(
        functools.partial(_rw_scan_kernel, bb_n=bb_n),
        out_shape=(jax.ShapeDtypeStruct((bsz, seq_rows, RW_WIDTH), F32),
                   jax.ShapeDtypeStruct((bsz, RW_PAIRS, RW_HEAD_DIM, LANE), F32)),
        grid=(bsz // bb_n, nblk),
        in_specs=[grp_spec] * 6 + [row_spec]
        + [pl.BlockSpec((bb_n, BLK, W_RWG), lambda b, j: (b, j, OFF_RWG // W_RWG)),
           st_spec, par, par, pl.BlockSpec((2 * LANE, LANE), lambda b, j: (0, 0))],
        out_specs=(row_spec, st_spec),
        scratch_shapes=[pltpu.VMEM((bb_n, RW_PAIRS, RW_HEAD_DIM, LANE), F32),
                        pltpu.VMEM((bb_n, grp, SUBLANES, RW_WIDTH), F32)],
        compiler_params=_cparams(("parallel", "arbitrary")),
        name="rw_scan",
    )(by_group(r), by_group(k), by_group(v), by_group(nkk), by_group(kka), by_group(w), bon, p3, s0,
      lp["rw_ln_g"], lp["rw_ln_b"], lp["ones2"])


def _split3_lhs(a):
    hi = a.astype(BF16)
    lo = (a - hi.astype(F32)).astype(BF16)
    return jnp.concatenate([hi, lo, hi], axis=1)


def _split3_rhs_rows(b):
    hi = b.astype(BF16)
    lo = (b - hi.astype(F32)).astype(BF16)
    return jnp.concatenate([hi, hi, lo], axis=1)


def _split3_rhs(b):
    hi = b.astype(BF16)
    lo = (b - hi.astype(F32)).astype(BF16)
    return jnp.concatenate([hi, hi, lo], axis=0)


def _dot_nt(lhs3, rhs3_rows):
    return lax.dot_general(lhs3, rhs3_rows, (((1,), (1,)), ((), ())), preferred_element_type=F32)


def _dot_nn(lhs3, rhs3):
    return jnp.dot(lhs3, rhs3, preferred_element_type=F32)


def _split3_exact_rows(b):
    hi = b.astype(BF16)
    rest = b - hi.astype(F32)
    mid = rest.astype(BF16)
    lo = (rest - mid.astype(F32)).astype(BF16)
    return jnp.concatenate([hi, mid, lo], axis=1)


def _rw_chunk_kernel(at_ref, rt_ref, bh_ref, kh_ref, bc_ref, kc_ref, wt_ref, v_ref, s0_ref,
                     y_ref, s_out, s_scr, *, bb_n):
    j = pl.program_id(1)

    @pl.when(j == 0)
    def _():
        s_scr[...] = s0_ref[...]

    lo_lane = lax.broadcasted_iota(jnp.int32, (1, LANE), 1) < RW_HEAD_DIM
    row_i = lax.broadcasted_iota(jnp.int32, (LANE, LANE), 0)
    col_i = lax.broadcasted_iota(jnp.int32, (LANE, LANE), 1)
    same_head = (row_i < RW_HEAD_DIM) == (col_i < RW_HEAD_DIM)
    strict = same_head & ((row_i & (RW_HEAD_DIM - 1)) < (col_i & (RW_HEAD_DIM - 1)))
    incl = same_head & ((row_i & (RW_HEAD_DIM - 1)) <= (col_i & (RW_HEAD_DIM - 1)))

    def by_head(x):
        return jnp.concatenate([jnp.where(lo_lane, x, 0.0), jnp.where(lo_lane, 0.0, x)], axis=0)

    lane64 = lax.broadcasted_iota(jnp.int32, (RW_HEAD_DIM, LANE), 1) & (RW_HEAD_DIM - 1)
    pick = jnp.where(lane64 == lax.broadcasted_iota(jnp.int32, (RW_HEAD_DIM, LANE), 0), 1.0, 0.0)
    pick3 = jnp.concatenate([pick.astype(BF16)] * 3, axis=1)

    def pair_transpose(x):
        return _dot_nt(pick3, _split3_exact_rows(by_head(x)))

    chains = [(bb, p) for bb in range(bb_n) for p in range(RW_PAIRS)]
    n_ch = len(chains)
    tile = lambda ref, c: ref[chains[c][0], :, chains[c][1] * LANE:(chains[c][1] + 1) * LANE]

    ar_rows, gb, hb3, gk_hk_kc3, bc3 = [], [], [], [], []
    for c in range(n_ch):
        ar = jnp.concatenate([by_head(tile(at_ref, c)), by_head(tile(rt_ref, c))], axis=0)
        bk = jnp.concatenate([by_head(tile(bh_ref, c)), by_head(tile(kh_ref, c))], axis=0)
        ar_rows.append(_split3_rhs_rows(ar))
        gram = _dot_nt(_split3_lhs(bk), ar_rows[c])
        gb.append(jnp.where(strict, gram[0:LANE, 0:LANE], 0.0))
        hb = jnp.where(incl, gram[0:LANE, LANE:2 * LANE], 0.0)
        gk = jnp.where(strict, gram[LANE:2 * LANE, 0:LANE], 0.0)
        hk = jnp.where(incl, gram[LANE:2 * LANE, LANE:2 * LANE], 0.0)
        kc = by_head(tile(kc_ref, c))
        gk_hk_kc3.append(_split3_rhs(jnp.concatenate([gk, hk, kc], axis=1)))
        bc3.append(_split3_rhs(jnp.concatenate([hb, by_head(tile(bc_ref, c))], axis=1)))
    vt = [pair_transpose(tile(v_ref, c)) for c in range(n_ch)]
    v_terms = [_dot_nn(_split3_lhs(vt[c]), gk_hk_kc3[c]) for c in range(n_ch)]
    powers = [[_split3_rhs(g) for g in gb]]
    cur = gb
    for _ in range(5):
        cur = [_dot_nn(_split3_lhs(cur[c]), powers[-1][c]) for c in range(n_ch)]
        powers.append([_split3_rhs(g) for g in cur])

    s_old = [s_scr[bb, p] for bb, p in chains]
    uy = [_dot_nt(_split3_lhs(s_old[c]), ar_rows[c]) for c in range(n_ch)]
    x = [uy[c][:, 0:LANE] + v_terms[c][:, 0:LANE] for c in range(n_ch)]
    for lvl in range(6):
        x = [x[c] + _dot_nn(_split3_lhs(x[c]), powers[lvl][c]) for c in range(n_ch)]
    out2 = [_dot_nn(_split3_lhs(x[c]), bc3[c]) for c in range(n_ch)]
    for c, (bb, p) in enumerate(chains):
        y_col = uy[c][:, LANE:2 * LANE] + out2[c][:, 0:LANE] + v_terms[c][:, LANE:2 * LANE]
        y_ref[bb, :, p * LANE:(p + 1) * LANE] = pair_transpose(y_col)
        wt = wt_ref[bb, :, p * LANE:(p + 1) * LANE]
        s_scr[bb, p] = s_old[c] * wt + out2[c][:, LANE:2 * LANE] + v_terms[c][:, 2 * LANE:3 * LANE]

    @pl.when(j == pl.num_programs(1) - 1)
    def _():
        s_out[...] = s_scr[...]


def _rw_chunk(prep, s0, *, bb_n):
    at, rt, bh, kh, bc, kc, v, _, wt = prep
    bsz, seq_rows, _ = at.shape
    nblk = seq_rows // BLK
    row_spec = pl.BlockSpec((bb_n, BLK, RW_WIDTH), lambda b, j: (b, j, 0))
    st_spec = pl.BlockSpec((bb_n, RW_PAIRS, RW_HEAD_DIM, LANE), lambda b, j: (b, 0, 0, 0))
    y, s1 = pl.pallas_call(
        functools.partial(_rw_chunk_kernel, bb_n=bb_n),
        out_shape=(jax.ShapeDtypeStruct((bsz, seq_rows, RW_WIDTH), F32),
                   jax.ShapeDtypeStruct((bsz, RW_PAIRS, RW_HEAD_DIM, LANE), F32)),
        grid=(bsz // bb_n, nblk),
        in_specs=[row_spec] * 6
        + [pl.BlockSpec((bb_n, None, 1, RW_WIDTH), lambda b, j: (b, j, 0, 0)), row_spec, st_spec],
        out_specs=(row_spec, st_spec),
        scratch_shapes=[pltpu.VMEM((bb_n, RW_PAIRS, RW_HEAD_DIM, LANE), F32)],
        compiler_params=_cparams(("parallel", "arbitrary")),
        name="rw_chunk",
    )(at, rt, bh, kh, bc, kc, wt, v, s0)
    return y.reshape(bsz * seq_rows, RW_WIDTH), s1


def _rw_post_kernel(y_ref, bon_ref, gate_ref, lng_ref, lnb_ref, ones2_ref, o_ref):
    ones2 = ones2_ref[...]
    y = y_ref[...]
    inv = 1.0 / RW_HEAD_DIM
    mu = _head_sum(y, ones2) * inv
    d = y - mu
    var = _head_sum(d * d, ones2) * inv
    yn = d * lax.rsqrt(var + RW_GN_EPS) * lng_ref[...] + lnb_ref[...]
    o_ref[...] = (yn + bon_ref[...]) * _silu(gate_ref[...])


def _rw_post(y, bon, p, lp, *, tm):
    n = y.shape[0]
    rows = pl.BlockSpec((tm, RW_WIDTH), lambda i: (i, 0))
    par = pl.BlockSpec((1, RW_WIDTH), lambda i: (0, 0))
    return pl.pallas_call(
        _rw_post_kernel,
        out_shape=jax.ShapeDtypeStruct((n, RW_WIDTH), F32),
        grid=(n // tm,),
        in_specs=[rows, rows, pl.BlockSpec((tm, W_RWG), lambda i: (i, OFF_RWG // W_RWG)),
                  par, par, pl.BlockSpec((2 * LANE, LANE), lambda i: (0, 0))],
        out_specs=rows,
        compiler_params=_cparams(("parallel",)),
        name="rw_post",
    )(y, bon, p, lp["rw_ln_g"], lp["rw_ln_b"], lp["ones2"])


def _cmul(ar, ai, br, bi):
    return ar * br - ai * bi, ar * bi + ai * br


def _s5_prep_kernel(are_ref, aim_ref, ldt_ref, bre_ref, bim_ref,
                    bbre_out, bbim_out, lvre_out, lvim_out, pwre_out, pwim_out):
    ar, ai = are_ref[...], aim_ref[...]
    dt = jnp.exp(ldt_ref[...])
    mag = jnp.exp(ar * dt)
    lr, li = mag * jnp.cos(ai * dt), mag * jnp.sin(ai * dt)
    nr, ni = lr - 1.0, li
    den = ar * ar + ai * ai
    qr, qi = (nr * ar + ni * ai) / den, (ni * ar - nr * ai) / den
    br, bi = bre_ref[...], bim_ref[...]
    bbr, bbi = _cmul(qr[:, None, :], qi[:, None, :], br, bi)
    bbre_out[...] = bbr
    bbim_out[...] = bbi
    sq_r, sq_i = lr, li
    pows = [(lr, li)]
    for lvl in range(SCAN_LEVELS):
        lvre_out[lvl] = sq_r
        lvim_out[lvl] = sq_i
        pows = pows + [_cmul(pr, pi, sq_r, sq_i) for pr, pi in pows]
        sq_r, sq_i = _cmul(sq_r, sq_i, sq_r, sq_i)
    for t in range(SUBLANES):
        pwre_out[t] = pows[t][0]
        pwim_out[t] = pows[t][1]


def _s5_prep(a_re, a_im, log_dt, b_re_t, b_im_t):
    ghp =jax.ShapeDtypeStruct((S5_GROUPS, S5_GROUP_CH, S5_STATE), F32)
    lv = jax.ShapeDtypeStruct((SCAN_LEVELS, S5_GROUPS, S5_STATE), F32)
    pw = jax.ShapeDtypeStruct((SUBLANES, S5_GROUPS, S5_STATE), F32)
    return pl.pallas_call(
        _s5_prep_kernel,
        out_shape=(ghp, ghp, lv, lv, pw, pw),
        name="s5_prep",
    )(a_re, a_im, log_dt, b_re_t, b_im_t)


def _gelu_tanh(x):
    return 0.5 * x * (1.0 + jnp.tanh(math.sqrt(2.0 / math.pi) * (x + 0.044715 * (x * x * x))))


def _s5_kernel(p_ref, h0_ref, wb_ref, wc_ref, lv_ref, pw_ref, d_ref, wglu_ref, bglu_ref,
               y_ref, h_out, h_scr, hs_scr, *, tb):
    j = pl.program_id(1)

    @pl.when(j == 0)
    def _():
        h_scr[...] = h0_ref[...]

    u = p_ref[:, 0:S5_WIDTH]
    gate = p_ref[:, S5_WIDTH:2 * S5_WIDTH]
    y_parts = []
    for jb in range(S5_NBLK):
        bu = _bdot(u[:, jb * LANE:(jb + 1) * LANE], wb_ref[jb])
        for s in range(tb // BLK):
            xr = bu[s * BLK:(s + 1) * BLK, 0:S5_BLK_STATE]
            xi = bu[s * BLK:(s + 1) * BLK, S5_BLK_STATE:2 * S5_BLK_STATE]
            xr = xr.reshape(BLK // SUBLANES, SUBLANES, S5_BLK_STATE)
            xi = xi.reshape(BLK // SUBLANES, SUBLANES, S5_BLK_STATE)
            for lvl in range(SCAN_LEVELS):
                sr = pltpu.roll(xr, 1 << lvl, 1)
                si = pltpu.roll(xi, 1 << lvl, 1)
                lr, li = lv_ref[lvl, 2 * jb], lv_ref[lvl, 2 * jb + 1]
                xr, xi = xr + (lr * sr - li * si), xi + (lr * si + li * sr)
            xr = xr.reshape(BLK, S5_BLK_STATE)
            xi = xi.reshape(BLK, S5_BLK_STATE)
            c_r = h_scr[2 * jb:2 * jb + 1, :]
            c_i = h_scr[2 * jb + 1:2 * jb + 2, :]
            pr, pi = pw_ref[2 * jb], pw_ref[2 * jb + 1]
            for grp in range(BLK // SUBLANES):
                rows = slice(grp * SUBLANES, (grp + 1) * SUBLANES)
                hr = xr[rows, :] + (pr * c_r - pi * c_i)
                hi = xi[rows, :] + (pr * c_i + pi * c_r)
                c_r, c_i = hr[SUBLANES - 1:SUBLANES, :], hi[SUBLANES - 1:SUBLANES, :]
                out_rows = slice(s * BLK + grp * SUBLANES, s * BLK + (grp + 1) * SUBLANES)
                hs_scr[out_rows, 0:S5_BLK_STATE] = hr
                hs_scr[out_rows, S5_BLK_STATE:2 * S5_BLK_STATE] = hi
            h_scr[2 * jb:2 * jb + 1, :] = c_r
            h_scr[2 * jb + 1:2 * jb + 2, :] = c_i
        y_parts.append(_bdot(hs_scr[...], wc_ref[jb]))
    y = jnp.concatenate(y_parts, axis=-1) + d_ref[...] * u
    y = _gelu_tanh(y)
    y = y * _sigmoid(_bdot(y, wglu_ref[...]) + bglu_ref[...])
    y_ref[...] = y * _silu(gate)

    @pl.when(j == pl.num_programs(1) - 1)
    def _():
        h_out[...] = h_scr[...]


def _s5(p3, h0, lp, *, tb):
    bsz, seq_rows, _ = p3.shape
    st_spec = pl.BlockSpec((None, 2 * S5_NBLK, S5_BLK_STATE), lambda b, j: (b, 0, 0))
    full = lambda shape: pl.BlockSpec(shape, lambda b, j: (0,) * len(shape))
    return pl.pallas_call(
        functools.partial(_s5_kernel, tb=tb),
        out_shape=(jax.ShapeDtypeStruct((bsz, seq_rows, S5_WIDTH), F32),
                   jax.ShapeDtypeStruct((bsz, 2 * S5_NBLK, S5_BLK_STATE), F32)),
        grid=(bsz, seq_rows // tb),
        in_specs=[pl.BlockSpec((None, tb, W_S5), lambda b, j: (b, j, OFF_S5 // W_S5)),
                  st_spec,
                  full((S5_NBLK, LANE, 2 * S5_BLK_STATE)),
                  full((S5_NBLK, 2 * S5_BLK_STATE, LANE)),
                  full((SCAN_LEVELS, 2 * S5_NBLK, SUBLANES, S5_BLK_STATE)),
                  full((2 * S5_NBLK, SUBLANES, S5_BLK_STATE)),
                  full((1, S5_WIDTH)), full((S5_WIDTH, S5_WIDTH)), full((1, S5_WIDTH))],
        out_specs=(pl.BlockSpec((None, tb, S5_WIDTH), lambda b, j: (b, j, 0)), st_spec),
        scratch_shapes=[pltpu.VMEM((2 * S5_NBLK, S5_BLK_STATE), F32),
                        pltpu.VMEM((tb, 2 * S5_BLK_STATE), F32)],
        compiler_params=_cparams(("parallel", "arbitrary")),
        name="s5",
    )(p3, h0, lp["s5_wb"], lp["s5_wc"], lp["s5_lv"], lp["s5_pw"], lp["s5_d"],
      lp["s5_w_glu"], lp["s5_b_glu"])


def _mlstm_kernel(qk_ref, oz_ref, v_ref, gi_ref, gf_ref, conv0_ref, c0_ref, n0_ref, m0_ref,
                  cw_ref, cb_ref, bi_ref, bf_ref, lng_ref,
                  y_ref, conv_out, c_out, n_out, m_out,
                  xp_scr, c_scr, n_scr, m_scr, *, pad, bb_n):
    j = pl.program_id(1)
    halo = 8

    @pl.when(j == 0)
    def _():
        xp_scr[:, 0:halo, :] = jnp.zeros((bb_n, halo, W_QK), F32)
        xp_scr[:, halo - (ML_CONV - 1):halo, :] = conv0_ref[...]
        c_scr[...] = jnp.zeros(c_scr.shape, F32)
        c_scr[:, :, 0:ML_HEAD_DIM, 0:ML_HEAD_DIM] = c0_ref[...]
        n_scr[...] = jnp.zeros(n_scr.shape, F32)
        n_scr[:, :, :, 0:ML_HEAD_DIM] = n0_ref[...]
        m_scr[...] = m0_ref[...]

    t0 = jnp.where(j == 0, pad, 0) if pad else 0
    row1 = lax.broadcasted_iota(jnp.int32, (BLK, 1), 0)
    row_ok = row1 >= t0
    ti = lax.broadcasted_iota(jnp.int32, (BLK, BLK), 0)
    si = lax.broadcasted_iota(jnp.int32, (BLK, BLK), 1)
    pair_ok = (si <= ti) & (si >= t0)
    tril = jnp.where(si <= ti, 1.0, 0.0)
    eye = jnp.where(si == ti, 1.0, 0.0)
    ones_sq = jnp.ones((BLK, BLK), F32)
    lane_ok = lax.broadcasted_iota(jnp.int32, (1, ML_HEAD_PAD), 1) < ML_HEAD_DIM
    head = lambda a, h: a[:, h * ML_HEAD_PAD:(h + 1) * ML_HEAD_PAD]
    chains = [(bb, h) for bb in range(bb_n) for h in range(ML_HEADS)]

    log_i, log_f, b = [], [], []
    for bb in range(bb_n):
        log_i.append(gi_ref[bb] + bi_ref[...])
        log_f.append(jnp.where(row_ok, -_softplus(-(gf_ref[bb] + bf_ref[...])), 0.0))
        b.append(_hdot(tril, log_f[bb]))
    d_row = {}
    for bb, h in chains:
        x_col = log_i[bb][:, h:h + 1] - b[bb][:, h:h + 1]
        d_row[bb, h] = _hdot(ones_sq, eye * x_col)

    q, k, v = [], [], []
    for bb in range(bb_n):
        xp_scr[bb, halo:halo + BLK, :] = qk_ref[bb]
        conv = cb_ref[...] + xp_scr[bb, halo - 3:halo - 3 + BLK, :] * cw_ref[0:1, :]
        for tap in range(1, ML_CONV):
            conv = conv + xp_scr[bb, halo - 3 + tap:halo - 3 + tap + BLK, :] * cw_ref[tap:tap + 1, :]
        tail = xp_scr[bb, halo + BLK - (ML_CONV - 1):halo + BLK, :]
        xp_scr[bb, halo - (ML_CONV - 1):halo, :] = tail
        conv_out[bb] = tail
        act = _silu(conv)
        q.append(act[:, 0:ML_WIDTH_PAD])
        k.append(act[:, ML_WIDTH_PAD:2 * ML_WIDTH_PAD] * (1.0 / math.sqrt(ML_HEAD_DIM)))
        v.append(v_ref[bb])

    qk, q_c = {}, {}
    for bb, h in chains:
        qh = head(q[bb], h).astype(BF16)
        qk[bb, h] = lax.dot_general(qh, head(k[bb], h).astype(BF16), (((1,), (1,)), ((), ())),
                                    preferred_element_type=F32)
        q_c[bb, h] = jnp.dot(qh, c_scr[bb, h].astype(BF16), preferred_element_type=F32)

    g, we, keep = [], [], []
    for bb in range(bb_n):
        m_prev = m_scr[bb]
        g.append(b[bb] + m_prev)
        b_end = b[bb][BLK - 1:BLK, :]
        e_log = jnp.where(row_ok, b_end - b[bb] + log_i[bb], -jnp.inf)
        m_new = jnp.maximum(b_end + m_prev, jnp.max(e_log, axis=0, keepdims=True))
        we.append(jnp.exp(e_log - m_new))
        keep.append(jnp.exp(b_end + m_prev - m_new))
        m_scr[bb] = m_new

    s_mat, m_row, w_inter = {}, {}, {}
    for bb, h in chains:
        d = b[bb][:, h:h + 1] + d_row[bb, h]
        d = jnp.where(pair_ok, d, -jnp.inf)
        g_col = g[bb][:, h:h + 1]
        m_row[bb, h] = jnp.maximum(g_col, jnp.max(d, axis=1, keepdims=True))
        s_mat[bb, h] = qk[bb, h] * jnp.exp(d - m_row[bb, h])
        w_inter[bb, h] = jnp.exp(g_col - m_row[bb, h])

    s_v, k_v = {}, {}
    for bb, h in chains:
        vh = head(v[bb], h)
        s_v[bb, h] = _bdot(s_mat[bb, h], vh)
        k_v[bb, h] = lax.dot_general(head(k[bb], h).astype(BF16),
                                     (we[bb][:, h:h + 1] * vh).astype(BF16),
                                     (((0,), (0,)), ((), ())), preferred_element_type=F32)

    for bb, h in chains:
        hs = slice(h * ML_HEAD_PAD, (h + 1) * ML_HEAD_PAD)
        kh = head(k[bb], h)
        n_h = n_scr[bb, h]
        num = s_v[bb, h] + w_inter[bb, h] * q_c[bb, h]
        qn = jnp.sum(head(q[bb], h) * n_h, axis=1, keepdims=True)
        den = jnp.sum(s_mat[bb, h], axis=1, keepdims=True) + w_inter[bb, h] * qn
        hh = num / jnp.maximum(jnp.abs(den), jnp.exp(-m_row[bb, h]))
        mu = jnp.sum(hh, axis=1, keepdims=True) * (1.0 / ML_HEAD_DIM)
        dv = jnp.where(lane_ok, hh - mu, 0.0)
        var = jnp.sum(dv * dv, axis=1, keepdims=True) * (1.0 / ML_HEAD_DIM)
        hn = dv * lax.rsqrt(var + LN_EPS) * lng_ref[:, hs]
        y_ref[bb, :, hs] = (_sigmoid(oz_ref[bb, :, hs]) * hn
                            * _silu(oz_ref[bb, :, ML_WIDTH_PAD + h * ML_HEAD_PAD:
                                           ML_WIDTH_PAD + (h + 1) * ML_HEAD_PAD]))
        keep_h = keep[bb][:, h:h + 1]
        c_scr[bb, h] = keep_h * c_scr[bb, h] + k_v[bb, h]
        n_scr[bb, h] = keep_h * n_h + jnp.sum(we[bb][:, h:h + 1] * kh, axis=0, keepdims=True)

    @pl.when(j == pl.num_programs(1) - 1)
    def _():
        c_out[...] = c_scr[:, :, 0:ML_HEAD_DIM, 0:ML_HEAD_DIM]
        n_out[...] = n_scr[:, :, :, 0:ML_HEAD_DIM]
        m_out[...] = m_scr[...]


def _mlstm(p3, conv0, c0, n0, m0, lp, *, pad, bb_n):
    bsz, seq_rows, _ = p3.shape
    nblk = seq_rows // BLK
    blk = lambda w, off: pl.BlockSpec((bb_n, BLK, w), lambda b, j: (b, j, off // w))
    par = lambda r, w: pl.BlockSpec((r, w), lambda b, j: (0, 0))
    conv_spec = pl.BlockSpec((bb_n, ML_CONV - 1, W_QK), lambda b, j: (b, 0, 0))
    c_spec = pl.BlockSpec((bb_n, ML_HEADS, ML_HEAD_DIM, ML_HEAD_DIM), lambda b, j: (b, 0, 0, 0))
    n_spec = pl.BlockSpec((bb_n, ML_HEADS, 1, ML_HEAD_DIM), lambda b, j: (b, 0, 0, 0))
    m_spec = pl.BlockSpec((bb_n, 1, LANE), lambda b, j: (b, 0, 0))
    return pl.pallas_call(
        functools.partial(_mlstm_kernel, pad=pad, bb_n=bb_n),
        out_shape=(jax.ShapeDtypeStruct((bsz, seq_rows, ML_WIDTH_PAD), F32),
                   jax.ShapeDtypeStruct((bsz, ML_CONV - 1, W_QK), F32),
                   jax.ShapeDtypeStruct((bsz, ML_HEADS, ML_HEAD_DIM, ML_HEAD_DIM), F32),
                   jax.ShapeDtypeStruct((bsz, ML_HEADS, 1, ML_HEAD_DIM), F32),
                   jax.ShapeDtypeStruct((bsz, 1, LANE), F32)),
        grid=(bsz // bb_n, nblk),
        in_specs=[blk(W_QK, OFF_QK), blk(W_OZ, OFF_OZ), blk(W_MV, OFF_MV),
                  blk(W_MI, OFF_MI), blk(W_MF, OFF_MF),
                  conv_spec, c_spec, n_spec, m_spec,
                  par(ML_CONV, W_QK), par(1, W_QK), par(1, LANE), par(1, LANE),
                  par(1, ML_WIDTH_PAD)],
        out_specs=(pl.BlockSpec((bb_n, BLK, ML_WIDTH_PAD), lambda b, j: (b, j, 0)),
                   conv_spec, c_spec, n_spec, m_spec),
        scratch_shapes=[pltpu.VMEM((bb_n, 8 + BLK, W_QK), F32),
                        pltpu.VMEM((bb_n, ML_HEADS, ML_HEAD_PAD, ML_HEAD_PAD), F32),
                        pltpu.VMEM((bb_n, ML_HEADS, 1, ML_HEAD_PAD), F32),
                        pltpu.VMEM((bb_n, 1, LANE), F32)],
        compiler_params=_cparams(("parallel", "arbitrary")),
        name="mlstm",
    )(p3, p3, p3, p3, p3, conv0, c0, n0, m0, lp["ml_cw"], lp["ml_cb"], lp["ml_bi"], lp["ml_bf"],
      lp["ml_ln_g"])


def _merge_kernel(x_ref, mg_ref, yrw_ref, ys5_ref, yml_ref, bmg_ref, wrw_ref, ws5_ref, wml_ref,
                  wout_ref, g_ref, b_ref, o_ref, *, tm, seq_rows, pad):
    gates = _sigmoid(mg_ref[...] + bmg_ref[...])
    merged = (gates[:, 0:D_MODEL] * _bdot(yrw_ref[...], wrw_ref[...])
              + gates[:, D_MODEL:2 * D_MODEL] * _bdot(ys5_ref[...], ws5_ref[...])
              + gates[:, 2 * D_MODEL:3 * D_MODEL] * _bdot(yml_ref[...], wml_ref[...]))
    out = _bdot(merged, wout_ref[...])
    y = _layer_norm_rows(DN_ALPHA * x_ref[...] + out, g_ref[...], b_ref[...])
    if pad:
        y = jnp.where(_pad_row_mask(pl.program_id(0), tm, seq_rows, pad), 0.0, y)
    o_ref[...] = y


def _merge(x, p, y_rw, y_s5, y_ml, lp, *, tm, seq_rows, pad):
    n = x.shape[0]
    rows = lambda w: pl.BlockSpec((tm, w), lambda i: (i, 0))
    full = lambda r, w: pl.BlockSpec((r, w), lambda i: (0, 0))
    return pl.pallas_call(
        functools.partial(_merge_kernel, tm=tm, seq_rows=seq_rows, pad=pad),
        out_shape=jax.ShapeDtypeStruct((n, D_MODEL), F32),
        grid=(n // tm,),
        in_specs=[rows(D_MODEL), pl.BlockSpec((tm, W_MG), lambda i: (i, OFF_MG // W_MG)),
                  rows(RW_WIDTH), rows(S5_WIDTH), rows(ML_WIDTH_PAD),
                  full(1, W_MG), full(RW_WIDTH, D_MODEL), full(S5_WIDTH, D_MODEL),
                  full(ML_WIDTH_PAD, D_MODEL), full(D_MODEL, D_MODEL),
                  full(1, D_MODEL), full(1, D_MODEL)],
        out_specs=rows(D_MODEL),
        compiler_params=_cparams(("parallel",)),
        name="merge",
    )(x, p, y_rw, y_s5, y_ml, lp["b_merge"], lp["w_br_rw"], lp["w_br_s5"], lp["w_br_ml"],
      lp["w_out"], lp["ln_g"], lp["ln_b"])


def _pad_heads(w):
    lead = w.shape[:-1]
    w = w.reshape(lead + (ML_HEADS, ML_HEAD_DIM))
    w = jnp.pad(w, [(0, 0)] * len(lead) + [(0, 0), (0, ML_HEAD_PAD - ML_HEAD_DIM)])
    return w.reshape(lead + (ML_WIDTH_PAD,))


def _unpad_heads(w):
    lead = w.shape[:-1]
    return w.reshape(lead + (ML_HEADS, ML_HEAD_PAD))[..., :ML_HEAD_DIM].reshape(lead + (ML_WIDTH,))


def _pad_lanes(w, width=LANE):
    return jnp.pad(w, [(0, 0)] * (w.ndim - 1) + [(0, width - w.shape[-1])])


def _qk_pad(w):
    return jnp.concatenate([_pad_heads(w[..., :ML_WIDTH]), _pad_heads(w[..., ML_WIDTH:])], axis=-1)


def _layer_params(l, w_in, rw_mu, rw_w0, rw_w2, rw_a0, rw_a2, rw_kk, rw_ka, rw_rk, rw_ln_g, rw_ln_b,
                  s5_a_re, s5_a_im, s5_b_re, s5_b_im, s5_c_re, s5_c_im, s5_d, s5_log_dt, s5_w_glu,
                  s5_b_glu, ml_conv_w, ml_conv_b, ml_b_if, ml_ln_g, b_merge, w_br_rw, w_br_s5,
                  w_br_ml, w_out, ln_g, ln_b):
    w = w_in[l]
    o = 0
    cols = {}
    for name, size in (("rwc", RW_SHIFT), ("rwg", RW_WIDTH), ("s5", 2 * S5_WIDTH),
                       ("qk", 2 * ML_WIDTH), ("mv", ML_WIDTH), ("mi", ML_HEADS), ("mf", ML_HEADS),
                       ("mo", ML_WIDTH), ("mz", ML_WIDTH), ("mg", 3 * D_MODEL)):
        cols[name] = w[:, o:o + size]
        o += size
    w_all = jnp.concatenate([
        cols["mg"], cols["s5"], _qk_pad(cols["qk"]),
        _pad_heads(cols["mo"]), _pad_heads(cols["mz"]), _pad_heads(cols["mv"]),
        cols["rwc"][:, :RW_RKV], cols["rwg"], cols["rwc"][:, RW_RKV:],
        _pad_lanes(cols["mi"]), _pad_lanes(cols["mf"])], axis=1).astype(BF16)
    row = lambda a: a.reshape(1, -1)
    zeros_lora = jnp.zeros((RW_LORA, RW_WIDTH), F32)
    lane = jnp.arange(LANE)
    ones_blk = (lane[:, None] // RW_HEAD_DIM == lane[None, :] // RW_HEAD_DIM).astype(F32)

    bb_re, bb_im, lv_re, lv_im, pw_re, pw_im = _s5_prep(
        s5_a_re[l], s5_a_im[l], s5_log_dt[l].reshape(S5_GROUPS, 1),
        jnp.swapaxes(s5_b_re[l], 1, 2), jnp.swapaxes(s5_b_im[l], 1, 2))
    eye8 = jnp.eye(8, dtype=F32)
    blocked = lambda a: a.reshape((S5_NBLK, 8) + a.shape[1:])
    bb = jnp.stack([blocked(bb_re), blocked(bb_im)])
    wb = jnp.einsum("cjghp,gk->jghckp", bb, eye8).reshape(S5_NBLK, LANE, 2 * S5_BLK_STATE)
    cc = jnp.stack([blocked(s5_c_re[l]), -blocked(s5_c_im[l])])
    wc = jnp.einsum("cjghp,gk->jcgpkh", cc, eye8).reshape(S5_NBLK, 2 * S5_BLK_STATE, LANE)
    state_rows = lambda re, im: jnp.stack(
        [re.reshape(re.shape[:-2] + (S5_NBLK, S5_BLK_STATE)),
         im.reshape(im.shape[:-2] + (S5_NBLK, S5_BLK_STATE))], axis=-2)
    lv = state_rows(lv_re, lv_im).reshape(SCAN_LEVELS, 2 * S5_NBLK, 1, S5_BLK_STATE)
    row_in_group = jnp.arange(SUBLANES)[None, None, :, None]
    lv = jnp.where(row_in_group >= (1 << jnp.arange(SCAN_LEVELS))[:, None, None, None], lv, 0.0)
    pw = jnp.moveaxis(state_rows(pw_re, pw_im).reshape(SUBLANES, 2 * S5_NBLK, S5_BLK_STATE), 0, 1)

    return dict(
        w_all=w_all,
        mu_rkv=row(rw_mu[l][:RW_RKV]), mu_wa=row(rw_mu[l][RW_RKV:]),
        rw_w0=row(rw_w0[l]), rw_a0=row(rw_a0[l]),
        rw_w2p=jnp.concatenate([rw_w2[l], zeros_lora], axis=0),
        rw_a2p=jnp.concatenate([zeros_lora, rw_a2[l]], axis=0),
        rw_kk=row(rw_kk[l]), rw_ka=row(rw_ka[l]), rw_rk=row(rw_rk[l]),
        rw_ln_g=row(rw_ln_g[l]), rw_ln_b=row(rw_ln_b[l]),
        ones2=jnp.concatenate([ones_blk, ones_blk], axis=0).astype(BF16),
        s5_wb=wb.astype(BF16), s5_wc=wc.astype(BF16), s5_lv=lv, s5_pw=pw,
        s5_d=row(s5_d[l]), s5_w_glu=s5_w_glu[l].astype(BF16), s5_b_glu=row(s5_b_glu[l]),
        ml_cw=_qk_pad(ml_conv_w[l]), ml_cb=row(_qk_pad(ml_conv_b[l])),
        ml_bi=row(_pad_lanes(ml_b_if[l][:ML_HEADS])), ml_bf=row(_pad_lanes(ml_b_if[l][ML_HEADS:])),
        ml_ln_g=row(_pad_heads(ml_ln_g[l])),
        b_merge=row(b_merge[l]), w_br_rw=w_br_rw[l].astype(BF16), w_br_s5=w_br_s5[l].astype(BF16),
        w_br_ml=jnp.pad(w_br_ml[l].reshape(ML_HEADS, ML_HEAD_DIM, D_MODEL),
                        ((0, 0), (0, ML_HEAD_PAD - ML_HEAD_DIM), (0, 0))
                        ).reshape(ML_WIDTH_PAD, D_MODEL).astype(BF16),
        w_out=w_out[l].astype(BF16), ln_g=row(ln_g[l]), ln_b=row(ln_b[l]))


def _pack_wkv(s):
    b = s.shape[0]
    s = s.reshape(b, RW_PAIRS, 2, RW_HEAD_DIM, RW_HEAD_DIM)
    return jnp.swapaxes(s, 2, 3).reshape(b, RW_PAIRS, RW_HEAD_DIM, LANE)


def _unpack_wkv(s):
    b = s.shape[0]
    s = s.reshape(b, RW_PAIRS, RW_HEAD_DIM, 2, RW_HEAD_DIM)
    return jnp.swapaxes(s, 2, 3).reshape(b, RW_HEADS, RW_HEAD_DIM, RW_HEAD_DIM)


def _pack_s5(re, im):
    b = re.shape[0]
    return jnp.stack([re.reshape(b, S5_NBLK, S5_BLK_STATE), im.reshape(b, S5_NBLK, S5_BLK_STATE)],
                     axis=2).reshape(b, 2 * S5_NBLK, S5_BLK_STATE)


def _unpack_s5(h):
    b = h.shape[0]
    h = h.reshape(b, S5_NBLK, 2, S5_BLK_STATE)
    return (h[:, :, 0].reshape(b, S5_GROUPS, S5_STATE), h[:, :, 1].reshape(b, S5_GROUPS, S5_STATE))


def _row_tile(n, seq_rows, target):
    best = 8
    for t in range(8, min(n, target) + 1, 8):
        if seq_rows % t == 0 or (t % seq_rows == 0 and n % t == 0):
            best = t
    return best


def _trunk_layer(x, st, lp, *, bsz, seq_rows, pad, bb_n, s5_tb, tm_proj, tm_merge):
    rw_shift0, rw_wkv0, s5_re0, s5_im0, ml_conv0, ml_c0, ml_n0, ml_m0 = st
    p = _proj(x, lp["w_all"], tm=tm_proj)
    p3 = p.reshape(bsz, seq_rows, P_COLS)

    prep = _rw_prep(p3, rw_shift0[:, None, :RW_RKV], rw_shift0[:, None, RW_RKV:], lp)
    n = bsz * seq_rows
    y_raw, wkv1 = _rw_chunk(prep[:9], _pack_wkv(rw_wkv0), bb_n=min(bb_n, 2))
    y_rw = _rw_post(y_raw, prep[7].reshape(n, RW_WIDTH), p, lp, tm=tm_merge)
    rw_shift1 = jnp.concatenate([prep[9][:, 0], prep[10][:, 0]], axis=-1)

    y_s5, h1 = _s5(p3, _pack_s5(s5_re0, s5_im0), lp, tb=s5_tb)
    s5_re1, s5_im1 = _unpack_s5(h1)

    y_ml, conv1, c1, n1, m1 = _mlstm(
        p3, _qk_pad(ml_conv0), ml_c0, ml_n0[:, :, None, :], _pad_lanes(ml_m0)[:, None, :], lp, pad=pad,
        bb_n=min(bb_n, 2))
    ml_conv1 = jnp.concatenate([_unpad_heads(conv1[..., :ML_WIDTH_PAD]),
                                _unpad_heads(conv1[..., ML_WIDTH_PAD:])], axis=-1)

    x_new = _merge(x, p, y_rw, y_s5.reshape(n, S5_WIDTH),
                   y_ml.reshape(n, ML_WIDTH_PAD), lp, tm=tm_merge, seq_rows=seq_rows, pad=pad)
    return x_new, (rw_shift1, _unpack_wkv(wkv1), s5_re1, s5_im1, ml_conv1, c1, n1[:, :, 0, :],
                   m1[:, 0, :ML_HEADS])


def _run_group(x_rows, states, lps, *, bsz, seq_rows, pad, in_ln_g, in_ln_b):
    n = bsz * seq_rows
    bb_n = 4 if bsz % 4 == 0 else (2 if bsz % 2 == 0 else 1)
    s5_tb = max(t for t in (BLK, 5 * BLK) if seq_rows % t == 0)
    tm_ln = _row_tile(n, seq_rows, 1024)
    tm_proj = _row_tile(n, seq_rows, 2080)
    tm_merge = _row_tile(n, seq_rows, 320)
    x = _ln_in(x_rows, in_ln_g, in_ln_b, tm=tm_ln, seq_rows=seq_rows, pad=pad)
    outs = []
    for l in range(DEPTH):
        x, st = _trunk_layer(x, states[l], lps[l], bsz=bsz, seq_rows=seq_rows, pad=pad, bb_n=bb_n,
                             s5_tb=s5_tb, tm_proj=tm_proj, tm_merge=tm_merge)
        outs.append(st)
    return x, outs


def kernel(x_prompt, x_sample, state_rwkv_shift, state_rwkv_wkv, state_s5_re, state_s5_im, state_mlstm_conv, state_mlstm_c, state_mlstm_n, state_mlstm_m, meta, in_ln_g, in_ln_b, w_in, rw_mu, rw_w0, rw_w2, rw_a0, rw_a2, rw_kk, rw_ka, rw_rk, rw_ln_g, rw_ln_b, s5_a_re, s5_a_im, s5_b_re, s5_b_im, s5_c_re, s5_c_im, s5_d, s5_log_dt, s5_w_glu, s5_b_glu, ml_conv_w, ml_conv_b, ml_b_if, ml_ln_g, b_merge, w_br_rw, w_br_s5, w_br_ml, w_out, ln_g, ln_b):
    lps = [_layer_params(l, w_in, rw_mu, rw_w0, rw_w2, rw_a0, rw_a2, rw_kk, rw_ka, rw_rk, rw_ln_g,
                         rw_ln_b, s5_a_re, s5_a_im, s5_b_re, s5_b_im, s5_c_re, s5_c_im, s5_d,
                         s5_log_dt, s5_w_glu, s5_b_glu, ml_conv_w, ml_conv_b, ml_b_if, ml_ln_g,
                         b_merge, w_br_rw, w_br_s5, w_br_ml, w_out, ln_g, ln_b)
           for l in range(DEPTH)]
    g_in, b_in = in_ln_g.reshape(1, D_MODEL), in_ln_b.reshape(1, D_MODEL)

    bp, sp = x_prompt.shape[0], x_prompt.shape[1]
    lp_rows = PAD + N_META + sp
    xp = jnp.concatenate([jnp.zeros((bp, PAD, D_MODEL), F32),
                          jnp.broadcast_to(meta[None], (bp, N_META, D_MODEL)), x_prompt], axis=1)
    z = lambda *shape: jnp.zeros((bp,) + shape, F32)
    zero_state = (z(RW_SHIFT), z(RW_HEADS, RW_HEAD_DIM, RW_HEAD_DIM), z(S5_GROUPS, S5_STATE),
                  z(S5_GROUPS, S5_STATE), z(ML_CONV - 1, 2 * ML_WIDTH),
                  z(ML_HEADS, ML_HEAD_DIM, ML_HEAD_DIM), z(ML_HEADS, ML_HEAD_DIM), z(ML_HEADS))
    yp, p_states = _run_group(xp.reshape(bp * lp_rows, D_MODEL), [zero_state] * DEPTH, lps,
                              bsz=bp, seq_rows=lp_rows, pad=PAD, in_ln_g=g_in, in_ln_b=b_in)
    y_prompt = yp.reshape(bp, lp_rows, D_MODEL)[:, PAD + N_META:]

    bs, ds = x_sample.shape[0], x_sample.shape[1]
    s_in = [(state_rwkv_shift[l], state_rwkv_wkv[l], state_s5_re[l], state_s5_im[l],
             state_mlstm_conv[l], state_mlstm_c[l], state_mlstm_n[l], state_mlstm_m[l])
            for l in range(DEPTH)]
    ys, s_states = _run_group(x_sample.reshape(bs * ds, D_MODEL), s_in, lps,
                              bsz=bs, seq_rows=ds, pad=0, in_ln_g=g_in, in_ln_b=b_in)
    y_sample = ys.reshape(bs, ds, D_MODEL)

    stack = lambda sts: tuple(jnp.stack(s, 0) for s in zip(*sts))
    return (y_prompt, y_sample) + stack(p_states) + stack(s_states)
```

```python
import functools
import math

import jax
import jax.numpy as jnp
from jax import lax
from jax.experimental import pallas as pl
from jax.experimental.pallas import tpu as pltpu

F32 = jnp.float32
BF16 = jnp.bfloat16

D_MODEL = 1024
DEPTH = 2
N_META = 16
RW_HEADS = 12
RW_HEAD_DIM = 64
RW_WIDTH = RW_HEADS * RW_HEAD_DIM
RW_PAIRS = RW_HEADS // 2
RW_LORA = 64
RW_RKV = 3 * RW_WIDTH
RW_SHIFT = RW_RKV + 2 * RW_LORA
S5_GROUPS = 32
S5_GROUP_CH = 16
S5_WIDTH = S5_GROUPS * S5_GROUP_CH
S5_STATE = 64
S5_NBLK = 4
S5_BLK_STATE = 512
ML_HEADS = 4
ML_HEAD_DIM = 192
ML_HEAD_PAD = 256
ML_WIDTH = ML_HEADS * ML_HEAD_DIM
ML_WIDTH_PAD = ML_HEADS * ML_HEAD_PAD
ML_CONV = 4
DN_ALPHA = (2 * DEPTH) ** 0.25
LN_EPS = 1e-5
RW_GN_EPS = 64e-5

LANE = 128
SUBLANES = 8
BLK = 64
PAD = BLK - N_META
SCAN_LEVELS = 3

OFF_MG, W_MG = 0, 3 * D_MODEL
OFF_S5, W_S5 = 3072, 2 * S5_WIDTH
OFF_QK, W_QK = 4096, 2 * ML_WIDTH_PAD
OFF_OZ, W_OZ = 6144, 2 * ML_WIDTH_PAD
OFF_MV, W_MV = 8192, ML_WIDTH_PAD
OFF_RKV, W_RKV = 9216, RW_RKV
OFF_RWG, W_RWG = 11520, RW_WIDTH
OFF_WA, W_WA = 12288, LANE
OFF_MI, W_MI = 12416, LANE
OFF_MF, W_MF = 12544, LANE
P_COLS = 12672
P_TN = 1152

VMEM_LIMIT = 56 * 1024 * 1024


def _cparams(sem):
    return pltpu.CompilerParams(dimension_semantics=sem, vmem_limit_bytes=VMEM_LIMIT)


def _bdot(a, b):
    return jnp.dot(a.astype(BF16), b.astype(BF16), preferred_element_type=F32)


def _hdot(a, b):
    return jnp.dot(a, b, precision=lax.Precision.HIGHEST, preferred_element_type=F32)


def _sigmoid(x):
    return 1.0 / (1.0 + jnp.exp(-x))


def _silu(x):
    return x * _sigmoid(x)


def _softplus(x):
    return jnp.maximum(x, 0.0) + jnp.log1p(jnp.exp(-jnp.abs(x)))


def _pad_row_mask(tile_idx, tm, seq_rows, pad):
    pos0 = lax.rem(tile_idx * tm, seq_rows)
    row = lax.broadcasted_iota(jnp.int32, (tm, 1), 0) + pos0
    return row < pad


def _layer_norm_rows(x, g, b):
    mu = jnp.mean(x, axis=-1, keepdims=True)
    d = x - mu
    var = jnp.mean(d * d, axis=-1, keepdims=True)
    return d * lax.rsqrt(var + LN_EPS) * g + b


def _ln_in_kernel(x_ref, g_ref, b_ref, o_ref, *, tm, seq_rows, pad):
    y = _layer_norm_rows(x_ref[...], g_ref[...], b_ref[...])
    if pad:
        y = jnp.where(_pad_row_mask(pl.program_id(0), tm, seq_rows, pad), 0.0, y)
    o_ref[...] = y


def _ln_in(x, g, b, *, tm, seq_rows, pad):
    n = x.shape[0]
    return pl.pallas_call(
        functools.partial(_ln_in_kernel, tm=tm, seq_rows=seq_rows, pad=pad),
        out_shape=jax.ShapeDtypeStruct((n, D_MODEL), F32),
        grid=(n // tm,),
        in_specs=[pl.BlockSpec((tm, D_MODEL), lambda i: (i, 0)),
                  pl.BlockSpec((1, D_MODEL), lambda i: (0, 0)),
                  pl.BlockSpec((1, D_MODEL), lambda i: (0, 0))],
        out_specs=pl.BlockSpec((tm, D_MODEL), lambda i: (i, 0)),
        compiler_params=_cparams(("parallel",)),
        name="ln_in",
    )(x, g, b)


def _proj_kernel(x_ref, w_ref, o_ref):
    o_ref[...] = jnp.dot(x_ref[...].astype(BF16), w_ref[...], preferred_element_type=F32)


def _proj(x, w_all, *, tm):
    n = x.shape[0]
    return pl.pallas_call(
        _proj_kernel,
        out_shape=jax.ShapeDtypeStruct((n, P_COLS), F32),
        grid=(n // tm, P_COLS // P_TN),
        in_specs=[pl.BlockSpec((tm, D_MODEL), lambda i, j: (i, 0)),
                  pl.BlockSpec((D_MODEL, P_TN), lambda i, j: (0, j))],
        out_specs=pl.BlockSpec((tm, P_TN), lambda i, j: (i, j)),
        compiler_params=_cparams(("parallel", "arbitrary")),
        name="proj",
    )(x, w_all)


def _split_lhs(hi_f32, lo_f32):
    return jnp.concatenate([hi_f32.astype(BF16), lo_f32.astype(BF16)], axis=1)


def _seg_lhs(x):
    hi = x.astype(BF16).astype(F32)
    return _split_lhs(hi, x - hi)


def _head_sum(x, ones2):
    parts = [jnp.dot(_seg_lhs(x[:, i * LANE:(i + 1) * LANE]), ones2, preferred_element_type=F32)
             for i in range(RW_WIDTH // LANE)]
    return jnp.concatenate(parts, axis=-1)


def _shift_rows(u, carry):
    rolled = pltpu.roll(u, 1, 0)
    row = lax.broadcasted_iota(jnp.int32, u.shape, 0)
    return jnp.where(row == 0, carry, rolled)


def _split3_lhs(a):
    hi = a.astype(BF16)
    lo = (a - hi.astype(F32)).astype(BF16)
    return jnp.concatenate([hi, lo, hi], axis=1)


def _split3_rhs_rows(b):
    hi = b.astype(BF16)
    lo = (b - hi.astype(F32)).astype(BF16)
    return jnp.concatenate([hi, hi, lo], axis=1)


def _split3_rhs(b):
    hi = b.astype(BF16)
    lo = (b - hi.astype(F32)).astype(BF16)
    return jnp.concatenate([hi, hi, lo], axis=0)


def _split3_exact_rows(b):
    hi = b.astype(BF16)
    rest = b - hi.astype(F32)
    mid = rest.astype(BF16)
    lo = (rest - mid.astype(F32)).astype(BF16)
    return jnp.concatenate([hi, mid, lo], axis=1)


def _dot_nt(lhs3, rhs3_rows):
    return lax.dot_general(lhs3, rhs3_rows, (((1,), (1,)), ((), ())), preferred_element_type=F32)


def _dot_nn(lhs3, rhs3):
    return jnp.dot(lhs3, rhs3, preferred_element_type=F32)


def _rw_prepare(u, uw, carry_rkv, carry_wa, mu_rkv, mu_wa, w0, w2p, a0, a2p, kk_gain, ka_gain,
                rk_gain, ones2):
    xs = u + (_shift_rows(u, carry_rkv) - u) * mu_rkv
    xwa = uw + (_shift_rows(uw, carry_wa) - uw) * mu_wa
    r = xs[:, 0:RW_WIDTH]
    k = xs[:, RW_WIDTH:2 * RW_WIDTH]
    v = xs[:, 2 * RW_WIDTH:3 * RW_WIDTH]
    w_log = -_softplus(-(w0 + _bdot(jnp.tanh(xwa), w2p))) - 0.5
    log_decay = -jnp.exp(w_log)
    a = _sigmoid(a0 + _bdot(xwa, a2p))
    kk = k * kk_gain
    kk = kk * lax.rsqrt(_head_sum(kk * kk, ones2) + 1e-12)
    k = k * (1.0 + (a - 1.0) * ka_gain)
    kka = kk * a
    ti = lax.broadcasted_iota(jnp.int32, (BLK, BLK), 0)
    si = lax.broadcasted_iota(jnp.int32, (BLK, BLK), 1)
    cum = _hdot(jnp.where(si <= ti, 1.0, 0.0), log_decay)
    cum_end = cum[BLK - 1:BLK, :]
    inv_w = jnp.exp(-cum)
    tail_w = jnp.exp(cum_end - cum)
    return dict(at=-kk * jnp.exp(cum - log_decay), rt=r * jnp.exp(cum), bh=kka * inv_w, kh=k * inv_w,
                bc=kka * tail_w, kc=k * tail_w, v=v, wt=jnp.exp(cum_end),
                bonus=_head_sum(r * k * rk_gain, ones2) * v)


def _rw_kernel(rkv_ref, wa_ref, gate_ref, sh0_rkv_ref, sh0_wa_ref, s0_ref,
               mu_rkv_ref, mu_wa_ref, w0_ref, w2_ref, a0_ref, a2_ref, kk_ref, ka_ref, rk_ref,
               lng_ref, lnb_ref, ones_ref,
               y_ref, s_out, sh_rkv_out, sh_wa_out, s_scr, c_rkv, c_wa, *, bb_n):
    j = pl.program_id(1)

    @pl.when(j == 0)
    def _():
        s_scr[...] = s0_ref[...]
        c_rkv[...] = sh0_rkv_ref[...]
        c_wa[...] = sh0_wa_ref[...]

    ones2 = ones_ref[...]
    rows = []
    for bb in range(bb_n):
        u, uw = rkv_ref[bb], wa_ref[bb]
        rows.append(_rw_prepare(u, uw, c_rkv[bb], c_wa[bb], mu_rkv_ref[...], mu_wa_ref[...],
                                w0_ref[...], w2_ref[...], a0_ref[...], a2_ref[...], kk_ref[...],
                                ka_ref[...], rk_ref[...], ones2))
        for carry, out, src in ((c_rkv, sh_rkv_out, u), (c_wa, sh_wa_out, uw)):
            carry[bb] = src[BLK - 1:BLK, :]
            out[bb] = src[BLK - 1:BLK, :]

    lo_lane = lax.broadcasted_iota(jnp.int32, (1, LANE), 1) < RW_HEAD_DIM
    row_i = lax.broadcasted_iota(jnp.int32, (LANE, LANE), 0)
    col_i = lax.broadcasted_iota(jnp.int32, (LANE, LANE), 1)
    same_head = (row_i < RW_HEAD_DIM) == (col_i < RW_HEAD_DIM)
    strict = same_head & ((row_i & (RW_HEAD_DIM - 1)) < (col_i & (RW_HEAD_DIM - 1)))
    incl = same_head & ((row_i & (RW_HEAD_DIM - 1)) <= (col_i & (RW_HEAD_DIM - 1)))

    def by_head(x):
        return jnp.concatenate([jnp.where(lo_lane, x, 0.0), jnp.where(lo_lane, 0.0, x)], axis=0)

    lane64 = lax.broadcasted_iota(jnp.int32, (RW_HEAD_DIM, LANE), 1) & (RW_HEAD_DIM - 1)
    pick = jnp.where(lane64 == lax.broadcasted_iota(jnp.int32, (RW_HEAD_DIM, LANE), 0), 1.0, 0.0)
    pick3 = jnp.concatenate([pick.astype(BF16)] * 3, axis=1)

    def pair_transpose(x):
        return _dot_nt(pick3, _split3_exact_rows(by_head(x)))

    chains = [(bb, p) for bb in range(bb_n) for p in range(RW_PAIRS)]
    n_ch = len(chains)
    tile = lambda name, c: rows[chains[c][0]][name][:, chains[c][1] * LANE:(chains[c][1] + 1) * LANE]

    ar_rows, gb, gk_hk_kc3, bc3 = [], [], [], []
    for c in range(n_ch):
        ar = jnp.concatenate([by_head(tile("at", c)), by_head(tile("rt", c))], axis=0)
        bk = jnp.concatenate([by_head(tile("bh", c)), by_head(tile("kh", c))], axis=0)
        ar_rows.append(_split3_rhs_rows(ar))
        gram = _dot_nt(_split3_lhs(bk), ar_rows[c])
        gb.append(jnp.where(strict, gram[0:LANE, 0:LANE], 0.0))
        hb = jnp.where(incl, gram[0:LANE, LANE:2 * LANE], 0.0)
        gk = jnp.where(strict, gram[LANE:2 * LANE, 0:LANE], 0.0)
        hk = jnp.where(incl, gram[LANE:2 * LANE, LANE:2 * LANE], 0.0)
        gk_hk_kc3.append(_split3_rhs(jnp.concatenate([gk, hk, by_head(tile("kc", c))], axis=1)))
        bc3.append(_split3_rhs(jnp.concatenate([hb, by_head(tile("bc", c))], axis=1)))
    vt = [pair_transpose(tile("v", c)) for c in range(n_ch)]
    v_terms = [_dot_nn(_split3_lhs(vt[c]), gk_hk_kc3[c]) for c in range(n_ch)]
    powers = [[_split3_rhs(g) for g in gb]]
    cur = gb
    for _ in range(5):
        cur = [_dot_nn(_split3_lhs(cur[c]), powers[-1][c]) for c in range(n_ch)]
        powers.append([_split3_rhs(g) for g in cur])

    s_old = [s_scr[bb, p] for bb, p in chains]
    uy = [_dot_nt(_split3_lhs(s_old[c]), ar_rows[c]) for c in range(n_ch)]
    x = [uy[c][:, 0:LANE] + v_terms[c][:, 0:LANE] for c in range(n_ch)]
    for lvl in range(6):
        x = [x[c] + _dot_nn(_split3_lhs(x[c]), powers[lvl][c]) for c in range(n_ch)]
    out2 = [_dot_nn(_split3_lhs(x[c]), bc3[c]) for c in range(n_ch)]
    y_tiles = []
    for c, (bb, p) in enumerate(chains):
        y_col = uy[c][:, LANE:2 * LANE] + out2[c][:, 0:LANE] + v_terms[c][:, LANE:2 * LANE]
        y_tiles.append(pair_transpose(y_col))
        s_scr[bb, p] = (s_old[c] * tile("wt", c) + out2[c][:, LANE:2 * LANE]
                        + v_terms[c][:, 2 * LANE:3 * LANE])

    inv = 1.0 / RW_HEAD_DIM
    for bb in range(bb_n):
        y = jnp.concatenate(y_tiles[bb * RW_PAIRS:(bb + 1) * RW_PAIRS], axis=1)
        mu = _head_sum(y, ones2) * inv
        d = y - mu
        var = _head_sum(d * d, ones2) * inv
        yn = d * lax.rsqrt(var + RW_GN_EPS) * lng_ref[...] + lnb_ref[...]
        y_ref[bb] = (yn + rows[bb]["bonus"]) * _silu(gate_ref[bb])

    @pl.when(j == pl.num_programs(1) - 1)
    def _():
        s_out[...] = s_scr[...]


def _rwkv(p3, sh0_rkv, sh0_wa, s0, lp, *, bb_n):
    bsz, seq_rows, _ = p3.shape
    nblk = seq_rows // BLK
    blk = lambda w, off: pl.BlockSpec((bb_n, BLK, w), lambda b, j: (b, j, off // w))
    par = lambda w: pl.BlockSpec((1, w), lambda b, j: (0, 0))
    mat = pl.BlockSpec((LANE, RW_WIDTH), lambda b, j: (0, 0))
    st = lambda w: pl.BlockSpec((bb_n, 1, w), lambda b, j: (b, 0, 0))
    s_spec = pl.BlockSpec((bb_n, RW_PAIRS, RW_HEAD_DIM, LANE), lambda b, j: (b, 0, 0, 0))
    return pl.pallas_call(
        functools.partial(_rw_kernel, bb_n=bb_n),
        out_shape=(jax.ShapeDtypeStruct((bsz, seq_rows, RW_WIDTH), F32),
                   jax.ShapeDtypeStruct((bsz, RW_PAIRS, RW_HEAD_DIM, LANE), F32),
                   jax.ShapeDtypeStruct((bsz, 1, RW_RKV), F32),
                   jax.ShapeDtypeStruct((bsz, 1, LANE), F32)),
        grid=(bsz // bb_n, nblk),
        in_specs=[blk(W_RKV, OFF_RKV), blk(W_WA, OFF_WA), blk(W_RWG, OFF_RWG),
                  st(RW_RKV), st(LANE), s_spec,
                  par(RW_RKV), par(LANE), par(RW_WIDTH), mat, par(RW_WIDTH), mat,
                  par(RW_WIDTH), par(RW_WIDTH), par(RW_WIDTH), par(RW_WIDTH), par(RW_WIDTH),
                  pl.BlockSpec((2 * LANE, LANE), lambda b, j: (0, 0))],
        out_specs=(pl.BlockSpec((bb_n, BLK, RW_WIDTH), lambda b, j: (b, j, 0)), s_spec,
                   st(RW_RKV), st(LANE)),
        scratch_shapes=[pltpu.VMEM((bb_n, RW_PAIRS, RW_HEAD_DIM, LANE), F32),
                        pltpu.VMEM((bb_n, 1, RW_RKV), F32), pltpu.VMEM((bb_n, 1, LANE), F32)],
        compiler_params=_cparams(("parallel", "arbitrary")),
        name="rwkv",
    )(p3, p3, p3, sh0_rkv, sh0_wa, s0, lp["mu_rkv"], lp["mu_wa"], lp["rw_w0"], lp["rw_w2p"],
      lp["rw_a0"], lp["rw_a2p"], lp["rw_kk"], lp["rw_ka"], lp["rw_rk"], lp["rw_ln_g"],
      lp["rw_ln_b"], lp["ones2"])


def _cmul(ar, ai, br, bi):
    return ar * br - ai * bi, ar * bi + ai * br


def _s5_prep_kernel(are_ref, aim_ref, ldt_ref, bre_ref, bim_ref,
                    bbre_out, bbim_out, lvre_out, lvim_out, pwre_out, pwim_out):
    ar, ai = are_ref[...], aim_ref[...]
    dt = jnp.exp(ldt_ref[...])
    mag = jnp.exp(ar * dt)
    lr, li = mag * jnp.cos(ai * dt), mag * jnp.sin(ai * dt)
    nr, ni = lr - 1.0, li
    den = ar * ar + ai * ai
    qr, qi = (nr * ar + ni * ai) / den, (ni * ar - nr * ai) / den
    br, bi = bre_ref[...], bim_ref[...]
    bbr, bbi = _cmul(qr[:, None, :], qi[:, None, :], br, bi)
    bbre_out[...] = bbr
    bbim_out[...] = bbi
    sq_r, sq_i = lr, li
    pows = [(lr, li)]
    for lvl in range(SCAN_LEVELS):
        lvre_out[lvl] = sq_r
        lvim_out[lvl] = sq_i
        pows = pows + [_cmul(pr, pi, sq_r, sq_i) for pr, pi in pows]
        sq_r, sq_i = _cmul(sq_r, sq_i, sq_r, sq_i)
    for t in range(SUBLANES):
        pwre_out[t] = pows[t][0]
        pwim_out[t] = pows[t][1]


def _s5_prep(a_re, a_im, log_dt, b_re_t, b_im_t):
    ghp = jax.ShapeDtypeStruct((S5_GROUPS, S5_GROUP_CH, S5_STATE), F32)
    lv = jax.ShapeDtypeStruct((SCAN_LEVELS, S5_GROUPS, S5_STATE), F32)
    pw = jax.ShapeDtypeStruct((SUBLANES, S5_GROUPS, S5_STATE), F32)
    return pl.pallas_call(
        _s5_prep_kernel,
        out_shape=(ghp, ghp, lv, lv, pw, pw),
        name="s5_prep",
    )(a_re, a_im, log_dt, b_re_t, b_im_t)


def _gelu_tanh(x):
    return 0.5 * x * (1.0 + jnp.tanh(math.sqrt(2.0 / math.pi) * (x + 0.044715 * (x * x * x))))


def _s5_kernel(p_ref, h0_ref, wb_ref, wc_ref, lv_ref, pw_ref, d_ref, wglu_ref, bglu_ref,
               y_ref, h_out, h_scr, hs_scr, *, tb):
    j = pl.program_id(1)

    @pl.when(j == 0)
    def _():
        h_scr[...] = h0_ref[...]

    u = p_ref[:, 0:S5_WIDTH]
    gate = p_ref[:, S5_WIDTH:2 * S5_WIDTH]
    y_parts = []
    for jb in range(S5_NBLK):
        bu = _bdot(u[:, jb * LANE:(jb + 1) * LANE], wb_ref[jb])
        for s in range(tb // BLK):
            xr = bu[s * BLK:(s + 1) * BLK, 0:S5_BLK_STATE]
            xi = bu[s * BLK:(s + 1) * BLK, S5_BLK_STATE:2 * S5_BLK_STATE]
            xr = xr.reshape(BLK // SUBLANES, SUBLANES, S5_BLK_STATE)
            xi = xi.reshape(BLK // SUBLANES, SUBLANES, S5_BLK_STATE)
            for lvl in range(SCAN_LEVELS):
                sr = pltpu.roll(xr, 1 << lvl, 1)
                si = pltpu.roll(xi, 1 << lvl, 1)
                lr, li = lv_ref[lvl, 2 * jb], lv_ref[lvl, 2 * jb + 1]
                xr, xi = xr + (lr * sr - li * si), xi + (lr * si + li * sr)
            xr = xr.reshape(BLK, S5_BLK_STATE)
            xi = xi.reshape(BLK, S5_BLK_STATE)
            c_r = h_scr[2 * jb:2 * jb + 1, :]
            c_i = h_scr[2 * jb + 1:2 * jb + 2, :]
            pr, pi = pw_ref[2 * jb], pw_ref[2 * jb + 1]
            for grp in range(BLK // SUBLANES):
                rows = slice(grp * SUBLANES, (grp + 1) * SUBLANES)
                hr = xr[rows, :] + (pr * c_r - pi * c_i)
                hi = xi[rows, :] + (pr * c_i + pi * c_r)
                c_r, c_i = hr[SUBLANES - 1:SUBLANES, :], hi[SUBLANES - 1:SUBLANES, :]
                out_rows = slice(s * BLK + grp * SUBLANES, s * BLK + (grp + 1) * SUBLANES)
                hs_scr[out_rows, 0:S5_BLK_STATE] = hr
                hs_scr[out_rows, S5_BLK_STATE:2 * S5_BLK_STATE] = hi
            h_scr[2 * jb:2 * jb + 1, :] = c_r
            h_scr[2 * jb + 1:2 * jb + 2, :] = c_i
        y_parts.append(_bdot(hs_scr[...], wc_ref[jb]))
    y = jnp.concatenate(y_parts, axis=-1) + d_ref[...] * u
    y = _gelu_tanh(y)
    y = y * _sigmoid(_bdot(y, wglu_ref[...]) + bglu_ref[...])
    y_ref[...] = y * _silu(gate)

    @pl.when(j == pl.num_programs(1) - 1)
    def _():
        h_out[...] = h_scr[...]


def _s5(p3, h0, lp, *, tb):
    bsz, seq_rows, _ = p3.shape
    st_spec = pl.BlockSpec((None, 2 * S5_NBLK, S5_BLK_STATE), lambda b, j: (b, 0, 0))
    full = lambda shape: pl.BlockSpec(shape, lambda b, j: (0,) * len(shape))
    return pl.pallas_call(
        functools.partial(_s5_kernel, tb=tb),
        out_shape=(jax.ShapeDtypeStruct((bsz, seq_rows, S5_WIDTH), F32),
                   jax.ShapeDtypeStruct((bsz, 2 * S5_NBLK, S5_BLK_STATE), F32)),
        grid=(bsz, seq_rows // tb),
        in_specs=[pl.BlockSpec((None, tb, W_S5), lambda b, j: (b, j, OFF_S5 // W_S5)),
                  st_spec,
                  full((S5_NBLK, LANE, 2 * S5_BLK_STATE)),
                  full((S5_NBLK, 2 * S5_BLK_STATE, LANE)),
                  full((SCAN_LEVELS, 2 * S5_NBLK, SUBLANES, S5_BLK_STATE)),
                  full((2 * S5_NBLK, SUBLANES, S5_BLK_STATE)),
                  full((1, S5_WIDTH)), full((S5_WIDTH, S5_WIDTH)), full((1, S5_WIDTH))],
        out_specs=(pl.BlockSpec((None, tb, S5_WIDTH), lambda b, j: (b, j, 0)), st_spec),
        scratch_shapes=[pltpu.VMEM((2 * S5_NBLK, S5_BLK_STATE), F32),
                        pltpu.VMEM((tb, 2 * S5_BLK_STATE), F32)],
        compiler_params=_cparams(("parallel", "arbitrary")),
        name="s5",
    )(p3, h0, lp["s5_wb"], lp["s5_wc"], lp["s5_lv"], lp["s5_pw"], lp["s5_d"],
      lp["s5_w_glu"], lp["s5_b_glu"])


def _mlstm_kernel(qk_ref, oz_ref, v_ref, gi_ref, gf_ref, conv0_ref, c0_ref, n0_ref, m0_ref,
                  cw_ref, cb_ref, bi_ref, bf_ref, lng_ref,
                  y_ref, conv_out, c_out, n_out, m_out,
                  xp_scr, c_scr, n_scr, m_scr, *, pad, bb_n):
    j = pl.program_id(1)
    halo = 8

    @pl.when(j == 0)
    def _():
        xp_scr[:, 0:halo, :] = jnp.zeros((bb_n, halo, W_QK), F32)
        xp_scr[:, halo - (ML_CONV - 1):halo, :] = conv0_ref[...]
        c_scr[...] = jnp.zeros(c_scr.shape, F32)
        c_scr[:, :, 0:ML_HEAD_DIM, 0:ML_HEAD_DIM] = c0_ref[...]
        n_scr[...] = jnp.zeros(n_scr.shape, F32)
        n_scr[:, :, :, 0:ML_HEAD_DIM] = n0_ref[...]
        m_scr[...] = m0_ref[...]

    t0 = jnp.where(j == 0, pad, 0) if pad else 0
    row1 = lax.broadcasted_iota(jnp.int32, (BLK, 1), 0)
    row_ok = row1 >= t0
    ti = lax.broadcasted_iota(jnp.int32, (BLK, BLK), 0)
    si = lax.broadcasted_iota(jnp.int32, (BLK, BLK), 1)
    pair_ok = (si <= ti) & (si >= t0)
    tril = jnp.where(si <= ti, 1.0, 0.0)
    eye = jnp.where(si == ti, 1.0, 0.0)
    ones_sq = jnp.ones((BLK, BLK), F32)
    lane_ok = lax.broadcasted_iota(jnp.int32, (1, ML_HEAD_PAD), 1) < ML_HEAD_DIM
    head = lambda a, h: a[:, h * ML_HEAD_PAD:(h + 1) * ML_HEAD_PAD]
    chains = [(bb, h) for bb in range(bb_n) for h in range(ML_HEADS)]

    log_i, log_f, b = [], [], []
    for bb in range(bb_n):
        log_i.append(gi_ref[bb] + bi_ref[...])
        log_f.append(jnp.where(row_ok, -_softplus(-(gf_ref[bb] + bf_ref[...])), 0.0))
        b.append(_hdot(tril, log_f[bb]))
    d_row = {}
    for bb, h in chains:
        x_col = log_i[bb][:, h:h + 1] - b[bb][:, h:h + 1]
        d_row[bb, h] = _hdot(ones_sq, eye * x_col)

    q, k, v = [], [], []
    for bb in range(bb_n):
        xp_scr[bb, halo:halo + BLK, :] = qk_ref[bb]
        conv = cb_ref[...] + xp_scr[bb, halo - 3:halo - 3 + BLK, :] * cw_ref[0:1, :]
        for tap in range(1, ML_CONV):
            conv = conv + xp_scr[bb, halo - 3 + tap:halo - 3 + tap + BLK, :] * cw_ref[tap:tap + 1, :]
        tail = xp_scr[bb, halo + BLK - (ML_CONV - 1):halo + BLK, :]
        xp_scr[bb, halo - (ML_CONV - 1):halo, :] = tail
        conv_out[bb] = tail
        act = _silu(conv)
        q.append(act[:, 0:ML_WIDTH_PAD])
        k.append(act[:, ML_WIDTH_PAD:2 * ML_WIDTH_PAD] * (1.0 / math.sqrt(ML_HEAD_DIM)))
        v.append(v_ref[bb])

    qk, q_c = {}, {}
    for bb, h in chains:
        qh = head(q[bb], h).astype(BF16)
        qk[bb, h] = lax.dot_general(qh, head(k[bb], h).astype(BF16), (((1,), (1,)), ((), ())),
                                    preferred_element_type=F32)
        q_c[bb, h] = jnp.dot(qh, c_scr[bb, h].astype(BF16), preferred_element_type=F32)

    g, we, keep = [], [], []
    for bb in range(bb_n):
        m_prev = m_scr[bb]
        g.append(b[bb] + m_prev)
        b_end = b[bb][BLK - 1:BLK, :]
        e_log = jnp.where(row_ok, b_end - b[bb] + log_i[bb], -jnp.inf)
        m_new = jnp.maximum(b_end + m_prev, jnp.max(e_log, axis=0, keepdims=True))
        we.append(jnp.exp(e_log - m_new))
        keep.append(jnp.exp(b_end + m_prev - m_new))
        m_scr[bb] = m_new

    s_mat, m_row, w_inter = {}, {}, {}
    for bb, h in chains:
        d = b[bb][:, h:h + 1] + d_row[bb, h]
        d = jnp.where(pair_ok, d, -jnp.inf)
        g_col = g[bb][:, h:h + 1]
        m_row[bb, h] = jnp.maximum(g_col, jnp.max(d, axis=1, keepdims=True))
        s_mat[bb, h] = qk[bb, h] * jnp.exp(d - m_row[bb, h])
        w_inter[bb, h] = jnp.exp(g_col - m_row[bb, h])

    s_v, k_v = {}, {}
    for bb, h in chains:
        vh = head(v[bb], h)
        s_v[bb, h] = _bdot(s_mat[bb, h], vh)
        k_v[bb, h] = lax.dot_general(head(k[bb], h).astype(BF16),
                                     (we[bb][:, h:h + 1] * vh).astype(BF16),
                                     (((0,), (0,)), ((), ())), preferred_element_type=F32)

    for bb, h in chains:
        hs = slice(h * ML_HEAD_PAD, (h + 1) * ML_HEAD_PAD)
        kh = head(k[bb], h)
        n_h = n_scr[bb, h]
        num = s_v[bb, h] + w_inter[bb, h] * q_c[bb, h]
        qn = jnp.sum(head(q[bb], h) * n_h, axis=1, keepdims=True)
        den = jnp.sum(s_mat[bb, h], axis=1, keepdims=True) + w_inter[bb, h] * qn
        hh = num / jnp.maximum(jnp.abs(den), jnp.exp(-m_row[bb, h]))
        mu = jnp.sum(hh, axis=1, keepdims=True) * (1.0 / ML_HEAD_DIM)
        dv = jnp.where(lane_ok, hh - mu, 0.0)
        var = jnp.sum(dv * dv, axis=1, keepdims=True) * (1.0 / ML_HEAD_DIM)
        hn = dv * lax.rsqrt(var + LN_EPS) * lng_ref[:, hs]
        y_ref[bb, :, hs] = (_sigmoid(oz_ref[bb, :, hs]) * hn
                            * _silu(oz_ref[bb, :, ML_WIDTH_PAD + h * ML_HEAD_PAD:
                                           ML_WIDTH_PAD + (h + 1) * ML_HEAD_PAD]))
        keep_h = keep[bb][:, h:h + 1]
        c_scr[bb, h] = keep_h * c_scr[bb, h] + k_v[bb, h]
        n_scr[bb, h] = keep_h * n_h + jnp.sum(we[bb][:, h:h + 1] * kh, axis=0, keepdims=True)

    @pl.when(j == pl.num_programs(1) - 1)
    def _():
        c_out[...] = c_scr[:, :, 0:ML_HEAD_DIM, 0:ML_HEAD_DIM]
        n_out[...] = n_scr[:, :, :, 0:ML_HEAD_DIM]
        m_out[...] = m_scr[...]


def _mlstm(p3, conv0, c0, n0, m0, lp, *, pad, bb_n):
    bsz, seq_rows, _ = p3.shape
    nblk = seq_rows // BLK
    blk = lambda w, off: pl.BlockSpec((bb_n, BLK, w), lambda b, j: (b, j, off // w))
    par = lambda r, w: pl.BlockSpec((r, w), lambda b, j: (0, 0))
    conv_spec = pl.BlockSpec((bb_n, ML_CONV - 1, W_QK), lambda b, j: (b, 0, 0))
    c_spec = pl.BlockSpec((bb_n, ML_HEADS, ML_HEAD_DIM, ML_HEAD_DIM), lambda b, j: (b, 0, 0, 0))
    n_spec = pl.BlockSpec((bb_n, ML_HEADS, 1, ML_HEAD_DIM), lambda b, j: (b, 0, 0, 0))
    m_spec = pl.BlockSpec((bb_n, 1, LANE), lambda b, j: (b, 0, 0))
    return pl.pallas_call(
        functools.partial(_mlstm_kernel, pad=pad, bb_n=bb_n),
        out_shape=(jax.ShapeDtypeStruct((bsz, seq_rows, ML_WIDTH_PAD), F32),
                   jax.ShapeDtypeStruct((bsz, ML_CONV - 1, W_QK), F32),
                   jax.ShapeDtypeStruct((bsz, ML_HEADS, ML_HEAD_DIM, ML_HEAD_DIM), F32),
                   jax.ShapeDtypeStruct((bsz, ML_HEADS, 1, ML_HEAD_DIM), F32),
                   jax.ShapeDtypeStruct((bsz, 1, LANE), F32)),
        grid=(bsz // bb_n, nblk),
        in_specs=[blk(W_QK, OFF_QK), blk(W_OZ, OFF_OZ), blk(W_MV, OFF_MV),
                  blk(W_MI, OFF_MI), blk(W_MF, OFF_MF),
                  conv_spec, c_spec, n_spec, m_spec,
                  par(ML_CONV, W_QK), par(1, W_QK), par(1, LANE), par(1, LANE),
                  par(1, ML_WIDTH_PAD)],
        out_specs=(pl.BlockSpec((bb_n, BLK, ML_WIDTH_PAD), lambda b, j: (b, j, 0)),
                   conv_spec, c_spec, n_spec, m_spec),
        scratch_shapes=[pltpu.VMEM((bb_n, 8 + BLK, W_QK), F32),
                        pltpu.VMEM((bb_n, ML_HEADS, ML_HEAD_PAD, ML_HEAD_PAD), F32),
                        pltpu.VMEM((bb_n, ML_HEADS, 1, ML_HEAD_PAD), F32),
                        pltpu.VMEM((bb_n, 1, LANE), F32)],
        compiler_params=_cparams(("parallel", "arbitrary")),
        name="mlstm",
    )(p3, p3, p3, p3, p3, conv0, c0, n0, m0, lp["ml_cw"], lp["ml_cb"], lp["ml_bi"], lp["ml_bf"],
      lp["ml_ln_g"])


def _merge_kernel(x_ref, mg_ref, yrw_ref, ys5_ref, yml_ref, bmg_ref, wrw_ref, ws5_ref, wml_ref,
                  wout_ref, g_ref, b_ref, o_ref, *, tm, seq_rows, pad):
    gates = _sigmoid(mg_ref[...] + bmg_ref[...])
    merged = (gates[:, 0:D_MODEL] * _bdot(yrw_ref[...], wrw_ref[...])
              + gates[:, D_MODEL:2 * D_MODEL] * _bdot(ys5_ref[...], ws5_ref[...])
              + gates[:, 2 * D_MODEL:3 * D_MODEL] * _bdot(yml_ref[...], wml_ref[...]))
    out = _bdot(merged, wout_ref[...])
    y = _layer_norm_rows(DN_ALPHA * x_ref[...] + out, g_ref[...], b_ref[...])
    if pad:
        y = jnp.where(_pad_row_mask(pl.program_id(0), tm, seq_rows, pad), 0.0, y)
    o_ref[...] = y


def _merge(x, p, y_rw, y_s5, y_ml, lp, *, tm, seq_rows, pad):
    n = x.shape[0]
    rows = lambda w: pl.BlockSpec((tm, w), lambda i: (i, 0))
    full = lambda r, w: pl.BlockSpec((r, w), lambda i: (0, 0))
    return pl.pallas_call(
        functools.partial(_merge_kernel, tm=tm, seq_rows=seq_rows, pad=pad),
        out_shape=jax.ShapeDtypeStruct((n, D_MODEL), F32),
        grid=(n // tm,),
        in_specs=[rows(D_MODEL), pl.BlockSpec((tm, W_MG), lambda i: (i, OFF_MG // W_MG)),
                  rows(RW_WIDTH), rows(S5_WIDTH), rows(ML_WIDTH_PAD),
                  full(1, W_MG), full(RW_WIDTH, D_MODEL), full(S5_WIDTH, D_MODEL),
                  full(ML_WIDTH_PAD, D_MODEL), full(D_MODEL, D_MODEL),
                  full(1, D_MODEL), full(1, D_MODEL)],
        out_specs=rows(D_MODEL),
        compiler_params=_cparams(("parallel",)),
        name="merge",
    )(x, p, y_rw, y_s5, y_ml, lp["b_merge"], lp["w_br_rw"], lp["w_br_s5"], lp["w_br_ml"],
      lp["w_out"], lp["ln_g"], lp["ln_b"])


def _pad_heads(w):
    lead = w.shape[:-1]
    w = w.reshape(lead + (ML_HEADS, ML_HEAD_DIM))
    w = jnp.pad(w, [(0, 0)] * len(lead) + [(0, 0), (0, ML_HEAD_PAD - ML_HEAD_DIM)])
    return w.reshape(lead + (ML_WIDTH_PAD,))


def _unpad_heads(w):
    lead = w.shape[:-1]
    return w.reshape(lead + (ML_HEADS, ML_HEAD_PAD))[..., :ML_HEAD_DIM].reshape(lead + (ML_WIDTH,))


def _pad_lanes(w, width=LANE):
    return jnp.pad(w, [(0, 0)] * (w.ndim - 1) + [(0, width - w.shape[-1])])


def _qk_pad(w):
    return jnp.concatenate([_pad_heads(w[..., :ML_WIDTH]), _pad_heads(w[..., ML_WIDTH:])], axis=-1)


def _layer_params(l, w_in, rw_mu, rw_w0, rw_w2, rw_a0, rw_a2, rw_kk, rw_ka, rw_rk, rw_ln_g, rw_ln_b,
                  s5_a_re, s5_a_im, s5_b_re, s5_b_im, s5_c_re, s5_c_im, s5_d, s5_log_dt, s5_w_glu,
                  s5_b_glu, ml_conv_w, ml_conv_b, ml_b_if, ml_ln_g, b_merge, w_br_rw, w_br_s5,
                  w_br_ml, w_out, ln_g, ln_b):
    w = w_in[l]
    o = 0
    cols = {}
    for name, size in (("rwc", RW_SHIFT), ("rwg", RW_WIDTH), ("s5", 2 * S5_WIDTH),
                       ("qk", 2 * ML_WIDTH), ("mv", ML_WIDTH), ("mi", ML_HEADS), ("mf", ML_HEADS),
                       ("mo", ML_WIDTH), ("mz", ML_WIDTH), ("mg", 3 * D_MODEL)):
        cols[name] = w[:, o:o + size]
        o += size
    w_all = jnp.concatenate([
        cols["mg"], cols["s5"], _qk_pad(cols["qk"]),
        _pad_heads(cols["mo"]), _pad_heads(cols["mz"]), _pad_heads(cols["mv"]),
        cols["rwc"][:, :RW_RKV], cols["rwg"], cols["rwc"][:, RW_RKV:],
        _pad_lanes(cols["mi"]), _pad_lanes(cols["mf"])], axis=1).astype(BF16)
    row = lambda a: a.reshape(1, -1)
    zeros_lora = jnp.zeros((RW_LORA, RW_WIDTH), F32)
    lane = jnp.arange(LANE)
    ones_blk = (lane[:, None] // RW_HEAD_DIM == lane[None, :] // RW_HEAD_DIM).astype(F32)

    bb_re, bb_im, lv_re, lv_im, pw_re, pw_im = _s5_prep(
        s5_a_re[l], s5_a_im[l], s5_log_dt[l].reshape(S5_GROUPS, 1),
        jnp.swapaxes(s5_b_re[l], 1, 2), jnp.swapaxes(s5_b_im[l], 1, 2))
    eye8 = jnp.eye(8, dtype=F32)
    blocked = lambda a: a.reshape((S5_NBLK, 8) + a.shape[1:])
    bb = jnp.stack([blocked(bb_re), blocked(bb_im)])
    wb = jnp.einsum("cjghp,gk->jghckp", bb, eye8).reshape(S5_NBLK, LANE, 2 * S5_BLK_STATE)
    cc = jnp.stack([blocked(s5_c_re[l]), -blocked(s5_c_im[l])])
    wc = jnp.einsum("cjghp,gk->jcgpkh", cc, eye8).reshape(S5_NBLK, 2 * S5_BLK_STATE, LANE)
    state_rows = lambda re, im: jnp.stack(
        [re.reshape(re.shape[:-2] + (S5_NBLK, S5_BLK_STATE)),
         im.reshape(im.shape[:-2] + (S5_NBLK, S5_BLK_STATE))], axis=-2)
    lv = state_rows(lv_re, lv_im).reshape(SCAN_LEVELS, 2 * S5_NBLK, 1, S5_BLK_STATE)
    row_in_group = jnp.arange(SUBLANES)[None, None, :, None]
    lv = jnp.where(row_in_group >= (1 << jnp.arange(SCAN_LEVELS))[:, None, None, None], lv, 0.0)
    pw = jnp.moveaxis(state_rows(pw_re, pw_im).reshape(SUBLANES, 2 * S5_NBLK, S5_BLK_STATE), 0, 1)

    return dict(
        w_all=w_all,
        mu_rkv=row(rw_mu[l][:RW_RKV]), mu_wa=row(rw_mu[l][RW_RKV:]),
        rw_w0=row(rw_w0[l]), rw_a0=row(rw_a0[l]),
        rw_w2p=jnp.concatenate([rw_w2[l], zeros_lora], axis=0),
        rw_a2p=jnp.concatenate([zeros_lora, rw_a2[l]], axis=0),
        rw_kk=row(rw_kk[l]), rw_ka=row(rw_ka[l]), rw_rk=row(rw_rk[l]),
        rw_ln_g=row(rw_ln_g[l]), rw_ln_b=row(rw_ln_b[l]),
        ones2=jnp.concatenate([ones_blk, ones_blk], axis=0).astype(BF16),
        s5_wb=wb.astype(BF16), s5_wc=wc.astype(BF16), s5_lv=lv, s5_pw=pw,
        s5_d=row(s5_d[l]), s5_w_glu=s5_w_glu[l].astype(BF16), s5_b_glu=row(s5_b_glu[l]),
        ml_cw=_qk_pad(ml_conv_w[l]), ml_cb=row(_qk_pad(ml_conv_b[l])),
        ml_bi=row(_pad_lanes(ml_b_if[l][:ML_HEADS])), ml_bf=row(_pad_lanes(ml_b_if[l][ML_HEADS:])),
        ml_ln_g=row(_pad_heads(ml_ln_g[l])),
        b_merge=row(b_merge[l]), w_br_rw=w_br_rw[l].astype(BF16), w_br_s5=w_br_s5[l].astype(BF16),
        w_br_ml=jnp.pad(w_br_ml[l].reshape(ML_HEADS, ML_HEAD_DIM, D_MODEL),
                        ((0, 0), (0, ML_HEAD_PAD - ML_HEAD_DIM), (0, 0))
                        ).reshape(ML_WIDTH_PAD, D_MODEL).astype(BF16),
        w_out=w_out[l].astype(BF16), ln_g=row(ln_g[l]), ln_b=row(ln_b[l]))


def _pack_wkv(s):
    b = s.shape[0]
    s = s.reshape(b, RW_PAIRS, 2, RW_HEAD_DIM, RW_HEAD_DIM)
    return jnp.swapaxes(s, 2, 3).reshape(b, RW_PAIRS, RW_HEAD_DIM, LANE)


def _unpack_wkv(s):
    b = s.shape[0]
    s = s.reshape(b, RW_PAIRS, RW_HEAD_DIM, 2, RW_HEAD_DIM)
    return jnp.swapaxes(s, 2, 3).reshape(b, RW_HEADS, RW_HEAD_DIM, RW_HEAD_DIM)


def _pack_s5(re, im):
    b = re.shape[0]
    return jnp.stack([re.reshape(b, S5_NBLK, S5_BLK_STATE), im.reshape(b, S5_NBLK, S5_BLK_STATE)],
                     axis=2).reshape(b, 2 * S5_NBLK, S5_BLK_STATE)


def _unpack_s5(h):
    b = h.shape[0]
    h = h.reshape(b, S5_NBLK, 2, S5_BLK_STATE)
    return (h[:, :, 0].reshape(b, S5_GROUPS, S5_STATE), h[:, :, 1].reshape(b, S5_GROUPS, S5_STATE))


def _row_tile(n, seq_rows, target):
    best = 8
    for t in range(8, min(n, target) + 1, 8):
        if seq_rows % t == 0 or (t % seq_rows == 0 and n % t == 0):
            best = t
    return best


def _trunk_layer(x, st, lp, *, bsz, seq_rows, pad, bb_n, s5_tb, tm_proj, tm_merge):
    rw_shift0, rw_wkv0, s5_re0, s5_im0, ml_conv0, ml_c0, ml_n0, ml_m0 = st
    p = _proj(x, lp["w_all"], tm=tm_proj)
    p3 = p.reshape(bsz, seq_rows, P_COLS)
    n = bsz * seq_rows

    y_rw, wkv1, sh_rkv1, sh_wa1 = _rwkv(p3, rw_shift0[:, None, :RW_RKV], rw_shift0[:, None, RW_RKV:],
                                        _pack_wkv(rw_wkv0), lp, bb_n=bb_n)
    rw_shift1 = jnp.concatenate([sh_rkv1[:, 0], sh_wa1[:, 0]], axis=-1)

    y_s5, h1 = _s5(p3, _pack_s5(s5_re0, s5_im0), lp, tb=s5_tb)
    s5_re1, s5_im1 = _unpack_s5(h1)

    y_ml, conv1, c1, n1, m1 = _mlstm(
        p3, _qk_pad(ml_conv0), ml_c0, ml_n0[:, :, None, :], _pad_lanes(ml_m0)[:, None, :], lp, pad=pad,
        bb_n=bb_n)
    ml_conv1 = jnp.concatenate([_unpad_heads(conv1[..., :ML_WIDTH_PAD]),
                                _unpad_heads(conv1[..., ML_WIDTH_PAD:])], axis=-1)

    x_new = _merge(x, p, y_rw.reshape(n, RW_WIDTH), y_s5.reshape(n, S5_WIDTH),
                   y_ml.reshape(n, ML_WIDTH_PAD), lp, tm=tm_merge, seq_rows=seq_rows, pad=pad)
    return x_new, (rw_shift1, _unpack_wkv(wkv1), s5_re1, s5_im1, ml_conv1, c1, n1[:, :, 0, :],
                   m1[:, 0, :ML_HEADS])


def _run_group(x_rows, states, lps, *, bsz, seq_rows, pad, in_ln_g, in_ln_b):
    n = bsz * seq_rows
    bb_n = 2 if bsz % 2 == 0 else 1
    s5_tb = max(t for t in (BLK, 5 * BLK) if seq_rows % t == 0)
    tm_ln = _row_tile(n, seq_rows, 1024)
    tm_proj = _row_tile(n, seq_rows, 2080)
    tm_merge = _row_tile(n, seq_rows, 320)
    x = _ln_in(x_rows, in_ln_g, in_ln_b, tm=tm_ln, seq_rows=seq_rows, pad=pad)
    outs = []
    for l in range(DEPTH):
        x, st = _trunk_layer(x, states[l], lps[l], bsz=bsz, seq_rows=seq_rows, pad=pad, bb_n=bb_n,
                             s5_tb=s5_tb, tm_proj=tm_proj, tm_merge=tm_merge)
        outs.append(st)
    return x, outs


def kernel(x_prompt, x_sample, state_rwkv_shift, state_rwkv_wkv, state_s5_re, state_s5_im, state_mlstm_conv, state_mlstm_c, state_mlstm_n, state_mlstm_m, meta, in_ln_g, in_ln_b, w_in, rw_mu, rw_w0, rw_w2, rw_a0, rw_a2, rw_kk, rw_ka, rw_rk, rw_ln_g, rw_ln_b, s5_a_re, s5_a_im, s5_b_re, s5_b_im, s5_c_re, s5_c_im, s5_d, s5_log_dt, s5_w_glu, s5_b_glu, ml_conv_w, ml_conv_b, ml_b_if, ml_ln_g, b_merge, w_br_rw, w_br_s5, w_br_ml, w_out, ln_g, ln_b):
    lps = [_layer_params(l, w_in, rw_mu, rw_w0, rw_w2, rw_a0, rw_a2, rw_kk, rw_ka, rw_rk, rw_ln_g,
                         rw_ln_b, s5_a_re, s5_a_im, s5_b_re, s5_b_im, s5_c_re, s5_c_im, s5_d,
                         s5_log_dt, s5_w_glu, s5_b_glu, ml_conv_w, ml_conv_b, ml_b_if, ml_ln_g,
                         b_merge, w_br_rw, w_br_s5, w_br_ml, w_out, ln_g, ln_b)
           for l in range(DEPTH)]
    g_in, b_in = in_ln_g.reshape(1, D_MODEL), in_ln_b.reshape(1, D_MODEL)

    bp, sp = x_prompt.shape[0], x_prompt.shape[1]
    lp_rows = PAD + N_META + sp
    xp = jnp.concatenate([jnp.zeros((bp, PAD, D_MODEL), F32),
                          jnp.broadcast_to(meta[None], (bp, N_META, D_MODEL)), x_prompt], axis=1)
    z = lambda *shape: jnp.zeros((bp,) + shape, F32)
    zero_state = (z(RW_SHIFT), z(RW_HEADS, RW_HEAD_DIM, RW_HEAD_DIM), z(S5_GROUPS, S5_STATE),
                  z(S5_GROUPS, S5_STATE), z(ML_CONV - 1, 2 * ML_WIDTH),
                  z(ML_HEADS, ML_HEAD_DIM, ML_HEAD_DIM), z(ML_HEADS, ML_HEAD_DIM), z(ML_HEADS))
    yp, p_states = _run_group(xp.reshape(bp * lp_rows, D_MODEL), [zero_state] * DEPTH, lps,
                              bsz=bp, seq_rows=lp_rows, pad=PAD, in_ln_g=g_in, in_ln_b=b_in)
    y_prompt = yp.reshape(bp, lp_rows, D_MODEL)[:, PAD + N_META:]

    bs, ds = x_sample.shape[0], x_sample.shape[1]
    s_in = [(state_rwkv_shift[l], state_rwkv_wkv[l], state_s5_re[l], state_s5_im[l],
             state_mlstm_conv[l], state_mlstm_c[l], state_mlstm_n[l], state_mlstm_m[l])
            for l in range(DEPTH)]
    ys, s_states = _run_group(x_sample.reshape(bs * ds, D_MODEL), s_in, lps,
                              bsz=bs, seq_rows=ds, pad=0, in_ln_g=g_in, in_ln_b=b_in)
    y_sample = ys.reshape(bs, ds, D_MODEL)

    stack = lambda sts: tuple(jnp.stack(s, 0) for s in zip(*sts))
    return (y_prompt, y_sample) + stack(p_states) + stack(s_states)
```

```python
import functools
import math

import jax
import jax.numpy as jnp
from jax import lax
from jax.experimental import pallas as pl
from jax.experimental.pallas import tpu as pltpu

F32 = jnp.float32
BF16 = jnp.bfloat16

D_MODEL = 1024
DEPTH = 2
N_META = 16
RW_HEADS = 12
RW_HEAD_DIM = 64
RW_WIDTH = RW_HEADS * RW_HEAD_DIM
RW_PAIRS = RW_HEADS // 2
RW_LORA = 64
RW_RKV = 3 * RW_WIDTH
RW_SHIFT = RW_RKV + 2 * RW_LORA
S5_GROUPS = 32
S5_GROUP_CH = 16
S5_WIDTH = S5_GROUPS * S5_GROUP_CH
S5_STATE = 64
S5_NBLK = 4
S5_BLK_STATE = 512
ML_HEADS = 4
ML_HEAD_DIM = 192
ML_HEAD_PAD = 256
ML_WIDTH = ML_HEADS * ML_HEAD_DIM
ML_WIDTH_PAD = ML_HEADS * ML_HEAD_PAD
ML_CONV = 4
DN_ALPHA = (2 * DEPTH) ** 0.25
LN_EPS = 1e-5
RW_GN_EPS = 64e-5

LANE = 128
SUBLANES = 8
BLK = 64
PAD = BLK - N_META
SCAN_LEVELS = 3

OFF_MG, W_MG = 0, 3 * D_MODEL
OFF_S5, W_S5 = 3072, 2 * S5_WIDTH
OFF_QK, W_QK = 4096, 2 * ML_WIDTH_PAD
OFF_OZ, W_OZ = 6144, 2 * ML_WIDTH_PAD
OFF_MV, W_MV = 8192, ML_WIDTH_PAD
OFF_RKV, W_RKV = 9216, RW_RKV
OFF_RWG, W_RWG = 11520, RW_WIDTH
OFF_WA, W_WA = 12288, LANE
OFF_MI, W_MI = 12416, LANE
OFF_MF, W_MF = 12544, LANE
P_COLS = 12672
P_TN = 1152

VMEM_LIMIT = 56 * 1024 * 1024


def _cparams(sem):
    return pltpu.CompilerParams(dimension_semantics=sem, vmem_limit_bytes=VMEM_LIMIT)


def _bdot(a, b):
    return jnp.dot(a.astype(BF16), b.astype(BF16), preferred_element_type=F32)


def _hdot(a, b):
    return jnp.dot(a, b, precision=lax.Precision.HIGHEST, preferred_element_type=F32)


def _sigmoid(x):
    return 1.0 / (1.0 + jnp.exp(-x))


def _silu(x):
    return x * _sigmoid(x)


def _softplus(x):
    return jnp.maximum(x, 0.0) + jnp.log1p(jnp.exp(-jnp.abs(x)))


def _pad_row_mask(tile_idx, tm, seq_rows, pad):
    pos0 = lax.rem(tile_idx * tm, seq_rows)
    row = lax.broadcasted_iota(jnp.int32, (tm, 1), 0) + pos0
    return row < pad


def _layer_norm_rows(x, g, b):
    mu = jnp.mean(x, axis=-1, keepdims=True)
    d = x - mu
    var = jnp.mean(d * d, axis=-1, keepdims=True)
    return d * lax.rsqrt(var + LN_EPS) * g + b


def _ln_in_kernel(x_ref, g_ref, b_ref, o_ref, *, tm, seq_rows, pad):
    y = _layer_norm_rows(x_ref[...], g_ref[...], b_ref[...])
    if pad:
        y = jnp.where(_pad_row_mask(pl.program_id(0), tm, seq_rows, pad), 0.0, y)
    o_ref[...] = y


def _ln_in(x, g, b, *, tm, seq_rows, pad):
    n = x.shape[0]
    return pl.pallas_call(
        functools.partial(_ln_in_kernel, tm=tm, seq_rows=seq_rows, pad=pad),
        out_shape=jax.ShapeDtypeStruct((n, D_MODEL), F32),
        grid=(n // tm,),
        in_specs=[pl.BlockSpec((tm, D_MODEL), lambda i: (i, 0)),
                  pl.BlockSpec((1, D_MODEL), lambda i: (0, 0)),
                  pl.BlockSpec((1, D_MODEL), lambda i: (0, 0))],
        out_specs=pl.BlockSpec((tm, D_MODEL), lambda i: (i, 0)),
        compiler_params=_cparams(("parallel",)),
        name="ln_in",
    )(x, g, b)


def _proj_kernel(x_ref, w_ref, o_ref):
    o_ref[...] = jnp.dot(x_ref[...].astype(BF16), w_ref[...], preferred_element_type=F32)


def _proj(x, w_all, *, tm):
    n = x.shape[0]
    return pl.pallas_call(
        _proj_kernel,
        out_shape=jax.ShapeDtypeStruct((n, P_COLS), F32),
        grid=(n // tm, P_COLS // P_TN),
        in_specs=[pl.BlockSpec((tm, D_MODEL), lambda i, j: (i, 0)),
                  pl.BlockSpec((D_MODEL, P_TN), lambda i, j: (0, j))],
        out_specs=pl.BlockSpec((tm, P_TN), lambda i, j: (i, j)),
        compiler_params=_cparams(("parallel", "arbitrary")),
        name="proj",
    )(x, w_all)


def _split_lhs(hi_f32, lo_f32):
    return jnp.concatenate([hi_f32.astype(BF16), lo_f32.astype(BF16)], axis=1)


def _seg_lhs(x):
    hi = x.astype(BF16).astype(F32)
    return _split_lhs(hi, x - hi)


def _head_sum(x, ones2):
    parts = [jnp.dot(_seg_lhs(x[:, i * LANE:(i + 1) * LANE]), ones2, preferred_element_type=F32)
             for i in range(RW_WIDTH // LANE)]
    return jnp.concatenate(parts, axis=-1)


def _shift_rows(u, carry):
    rolled = pltpu.roll(u, 1, 0)
    row = lax.broadcasted_iota(jnp.int32, u.shape, 0)
    return jnp.where(row == 0, carry, rolled)


def _hi_lo(a):
    hi = a.astype(BF16)
    return hi, (a - hi.astype(F32)).astype(BF16)


def _lhs3(hl):
    return jnp.concatenate([hl[0], hl[1], hl[0]], axis=1)


def _rhs3_rows(hl):
    return jnp.concatenate([hl[0], hl[0], hl[1]], axis=1)


def _rhs3(hl):
    return jnp.concatenate([hl[0], hl[0], hl[1]], axis=0)


def _dot_nt(lhs3, rhs3_rows):
    return lax.dot_general(lhs3, rhs3_rows, (((1,), (1,)), ((), ())), preferred_element_type=F32)


def _dot_nn(lhs3, rhs3):
    return jnp.dot(lhs3, rhs3, preferred_element_type=F32)


def _rw_prepare(u, uw, carry_rkv, carry_wa, mu_rkv, mu_wa, w0, w2p, a0, a2p, kk_gain, ka_gain,
                rk_gain, ones2):
    xs = u + (_shift_rows(u, carry_rkv) - u) * mu_rkv
    xwa = uw + (_shift_rows(uw, carry_wa) - uw) * mu_wa
    r = xs[:, 0:RW_WIDTH]
    k = xs[:, RW_WIDTH:2 * RW_WIDTH]
    v = xs[:, 2 * RW_WIDTH:3 * RW_WIDTH]
    log_decay = -math.exp(-0.5) * _sigmoid(w0 + _bdot(jnp.tanh(xwa), w2p))
    a = _sigmoid(a0 + _bdot(xwa, a2p))
    kk = k * kk_gain
    kk = kk * lax.rsqrt(_head_sum(kk * kk, ones2) + 1e-12)
    k = k * (1.0 + (a - 1.0) * ka_gain)
    kka = kk * a
    ti = lax.broadcasted_iota(jnp.int32, (BLK, BLK), 0)
    si = lax.broadcasted_iota(jnp.int32, (BLK, BLK), 1)
    cum = _hdot(jnp.where(si <= ti, 1.0, 0.0), log_decay)
    cum_end = cum[BLK - 1:BLK, :]
    inv_w = jnp.exp(-cum)
    tail_w = jnp.exp(cum_end - cum)
    return dict(at=-kk * jnp.exp(cum - log_decay), rt=r * jnp.exp(cum), bh=kka * inv_w, kh=k * inv_w,
                bc=kka * tail_w, kc=k * tail_w, v=v, wt=jnp.exp(cum_end),
                bonus=_head_sum(r * k * rk_gain, ones2) * v)


def _rw_kernel(rkv_ref, wa_ref, gate_ref, sh0_rkv_ref, sh0_wa_ref, s0_ref,
               mu_rkv_ref, mu_wa_ref, w0_ref, w2_ref, a0_ref, a2_ref, kk_ref, ka_ref, rk_ref,
               lng_ref, lnb_ref, ones_ref,
               y_ref, s_out, sh_rkv_out, sh_wa_out, s_scr, c_rkv, c_wa, *, bb_n):
    j = pl.program_id(1)

    @pl.when(j == 0)
    def _():
        s_scr[...] = s0_ref[...]
        c_rkv[...] = sh0_rkv_ref[...]
        c_wa[...] = sh0_wa_ref[...]

    ones2 = ones_ref[...]
    rows = []
    for bb in range(bb_n):
        u, uw = rkv_ref[bb], wa_ref[bb]
        rows.append(_rw_prepare(u, uw, c_rkv[bb], c_wa[bb], mu_rkv_ref[...], mu_wa_ref[...],
                                w0_ref[...], w2_ref[...], a0_ref[...], a2_ref[...], kk_ref[...],
                                ka_ref[...], rk_ref[...], ones2))
        for carry, out, src in ((c_rkv, sh_rkv_out, u), (c_wa, sh_wa_out, uw)):
            carry[bb] = src[BLK - 1:BLK, :]
            out[bb] = src[BLK - 1:BLK, :]

    lo_lane = lax.broadcasted_iota(jnp.int32, (1, LANE), 1) < RW_HEAD_DIM
    row_j = lax.broadcasted_iota(jnp.int32, (RW_HEAD_DIM, LANE), 0)
    col_t = lax.broadcasted_iota(jnp.int32, (RW_HEAD_DIM, LANE), 1) & (RW_HEAD_DIM - 1)
    strict = row_j < col_t
    incl = row_j <= col_t

    def by_head(x):
        return jnp.concatenate([jnp.where(lo_lane, x, 0.0), jnp.where(lo_lane, 0.0, x)], axis=0)

    pick = jnp.where(row_j == col_t, 1.0, 0.0).astype(BF16)
    pick2 = jnp.concatenate([pick, pick], axis=1)

    bd = lambda hl: (by_head(hl[0]), by_head(hl[1]))
    stack_rows = lambda parts: tuple(jnp.concatenate(x, axis=0) for x in zip(*parts))
    stack_lanes = lambda parts: tuple(jnp.concatenate(x, axis=1) for x in zip(*parts))

    def pair_transpose(x):
        return _dot_nt(pick2, jnp.concatenate(bd(_hi_lo(x)), axis=1))

    chains = [(bb, p) for bb in range(bb_n) for p in range(RW_PAIRS)]
    n_ch = len(chains)
    tile = lambda name, c: rows[chains[c][0]][name][:, chains[c][1] * LANE:(chains[c][1] + 1) * LANE]
    tile_hl = lambda name, c: _hi_lo(tile(name, c))

    ar_rows, gb, gk_hk_kc3, bc3 = [], [], [], []
    for c in range(n_ch):
        ar_rows.append(_rhs3_rows(stack_rows([bd(tile_hl("at", c)), bd(tile_hl("rt", c))])))
        bk = stack_rows([tile_hl("bh", c), tile_hl("kh", c)])
        gram = _dot_nt(_lhs3(bk), ar_rows[c])
        gb.append(_hi_lo(jnp.where(strict, gram[0:RW_HEAD_DIM, 0:LANE], 0.0)))
        hb = jnp.where(incl, gram[0:RW_HEAD_DIM, LANE:2 * LANE], 0.0)
        gk = jnp.where(strict, gram[RW_HEAD_DIM:LANE, 0:LANE], 0.0)
        hk = jnp.where(incl, gram[RW_HEAD_DIM:LANE, LANE:2 * LANE], 0.0)
        gk_hk_kc3.append(_rhs3(stack_lanes([bd(_hi_lo(gk)), bd(_hi_lo(hk)), bd(tile_hl("kc", c))])))
        bc3.append(_rhs3(stack_lanes([bd(_hi_lo(hb)), bd(tile_hl("bc", c))])))
    vt = [pair_transpose(tile("v", c)) for c in range(n_ch)]
    v_terms = [_dot_nn(_lhs3(_hi_lo(vt[c])), gk_hk_kc3[c]) for c in range(n_ch)]
    powers = [[_rhs3(bd(g)) for g in gb]]
    cur = gb
    for _ in range(5):
        cur = [_hi_lo(_dot_nn(_lhs3(cur[c]), powers[-1][c])) for c in range(n_ch)]
        powers.append([_rhs3(bd(g)) for g in cur])

    s_old = [s_scr[bb, p] for bb, p in chains]
    uy = [_dot_nt(_lhs3(_hi_lo(s_old[c])), ar_rows[c]) for c in range(n_ch)]
    x = [uy[c][:, 0:LANE] + v_terms[c][:, 0:LANE] for c in range(n_ch)]
    for lvl in range(6):
        x = [x[c] + _dot_nn(_lhs3(_hi_lo(x[c])), powers[lvl][c]) for c in range(n_ch)]
    out2 = [_dot_nn(_lhs3(_hi_lo(x[c])), bc3[c]) for c in range(n_ch)]
    y_tiles = []
    for c, (bb, p) in enumerate(chains):
        y_col = uy[c][:, LANE:2 * LANE] + out2[c][:, 0:LANE] + v_terms[c][:, LANE:2 * LANE]
        y_tiles.append(pair_transpose(y_col))
        s_scr[bb, p] = (s_old[c] * tile("wt", c) + out2[c][:, LANE:2 * LANE]
                        + v_terms[c][:, 2 * LANE:3 * LANE])

    inv = 1.0 / RW_HEAD_DIM
    for bb in range(bb_n):
        y = jnp.concatenate(y_tiles[bb * RW_PAIRS:(bb + 1) * RW_PAIRS], axis=1)
        mu = _head_sum(y, ones2) * inv
        d = y - mu
        var = _head_sum(d * d, ones2) * inv
        yn = d * lax.rsqrt(var + RW_GN_EPS) * lng_ref[...] + lnb_ref[...]
        y_ref[bb] = (yn + rows[bb]["bonus"]) * _silu(gate_ref[bb])

    @pl.when(j == pl.num_programs(1) - 1)
    def _():
        s_out[...] = s_scr[...]


def _rwkv(p3, sh0_rkv, sh0_wa, s0, lp, *, bb_n):
    bsz, seq_rows, _ = p3.shape
    nblk = seq_rows // BLK
    blk = lambda w, off: pl.BlockSpec((bb_n, BLK, w), lambda b, j: (b, j, off // w))
    par = lambda w: pl.BlockSpec((1, w), lambda b, j: (0, 0))
    mat = pl.BlockSpec((LANE, RW_WIDTH), lambda b, j: (0, 0))
    st = lambda w: pl.BlockSpec((bb_n, 1, w), lambda b, j: (b, 0, 0))
    s_spec = pl.BlockSpec((bb_n, RW_PAIRS, RW_HEAD_DIM, LANE), lambda b, j: (b, 0, 0, 0))
    return pl.pallas_call(
        functools.partial(_rw_kernel, bb_n=bb_n),
        out_shape=(jax.ShapeDtypeStruct((bsz, seq_rows, RW_WIDTH), F32),
                   jax.ShapeDtypeStruct((bsz, RW_PAIRS, RW_HEAD_DIM, LANE), F32),
                   jax.ShapeDtypeStruct((bsz, 1, RW_RKV), F32),
                   jax.ShapeDtypeStruct((bsz, 1, LANE), F32)),
        grid=(bsz // bb_n, nblk),
        in_specs=[blk(W_RKV, OFF_RKV), blk(W_WA, OFF_WA), blk(W_RWG, OFF_RWG),
                  st(RW_RKV), st(LANE), s_spec,
                  par(RW_RKV), par(LANE), par(RW_WIDTH), mat, par(RW_WIDTH), mat,
                  par(RW_WIDTH), par(RW_WIDTH), par(RW_WIDTH), par(RW_WIDTH), par(RW_WIDTH),
                  pl.BlockSpec((2 * LANE, LANE), lambda b, j: (0, 0))],
        out_specs=(pl.BlockSpec((bb_n, BLK, RW_WIDTH), lambda b, j: (b, j, 0)), s_spec,
                   st(RW_RKV), st(LANE)),
        scratch_shapes=[pltpu.VMEM((bb_n, RW_PAIRS, RW_HEAD_DIM, LANE), F32),
                        pltpu.VMEM((bb_n, 1, RW_RKV), F32), pltpu.VMEM((bb_n, 1, LANE), F32)],
        compiler_params=_cparams(("parallel", "arbitrary")),
        name="rwkv",
    )(p3, p3, p3, sh0_rkv, sh0_wa, s0, lp["mu_rkv"], lp["mu_wa"], lp["rw_w0"], lp["rw_w2p"],
      lp["rw_a0"], lp["rw_a2p"], lp["rw_kk"], lp["rw_ka"], lp["rw_rk"], lp["rw_ln_g"],
      lp["rw_ln_b"], lp["ones2"])


def _cmul(ar, ai, br, bi):
    return ar * br - ai * bi, ar * bi + ai * br


def _s5_prep_kernel(are_ref, aim_ref, ldt_ref, bre_ref, bim_ref,
                    bbre_out, bbim_out, lvre_out, lvim_out, pwre_out, pwim_out):
    ar, ai = are_ref[...], aim_ref[...]
    dt = jnp.exp(ldt_ref[...])
    mag = jnp.exp(ar * dt)
    lr, li = mag * jnp.cos(ai * dt), mag * jnp.sin(ai * dt)
    nr, ni = lr - 1.0, li
    den = ar * ar + ai * ai
    qr, qi = (nr * ar + ni * ai) / den, (ni * ar - nr * ai) / den
    br, bi = bre_ref[...], bim_ref[...]
    bbr, bbi = _cmul(qr[:, None, :], qi[:, None, :], br, bi)
    bbre_out[...] = bbr
    bbim_out[...] = bbi
    sq_r, sq_i = lr, li
    pows = [(lr, li)]
    for lvl in range(SCAN_LEVELS):
        lvre_out[lvl] = sq_r
        lvim_out[lvl] = sq_i
        pows = pows + [_cmul(pr, pi, sq_r, sq_i) for pr, pi in pows]
        sq_r, sq_i = _cmul(sq_r, sq_i, sq_r, sq_i)
    for t in range(SUBLANES):
        pwre_out[t] = pows[t][0]
        pwim_out[t] = pows[t][1]


def _s5_prep(a_re, a_im, log_dt, b_re_t, b_im_t):
    ghp = jax.ShapeDtypeStruct((S5_GROUPS, S5_GROUP_CH, S5_STATE), F32)
    lv = jax.ShapeDtypeStruct((SCAN_LEVELS, S5_GROUPS, S5_STATE), F32)
    pw = jax.ShapeDtypeStruct((SUBLANES, S5_GROUPS, S5_STATE), F32)
    return pl.pallas_call(
        _s5_prep_kernel,
        out_shape=(ghp, ghp, lv, lv, pw, pw),
        name="s5_prep",
    )(a_re, a_im, log_dt, b_re_t, b_im_t)


def _gelu_tanh(x):
    return 0.5 * x * (1.0 + jnp.tanh(math.sqrt(2.0 / math.pi) * (x + 0.044715 * (x * x * x))))


def _s5_kernel(p_ref, h0_ref, wb_ref, wc_ref, lv_ref, pw_ref, d_ref, wglu_ref, bglu_ref,
               y_ref, h_out, h_scr, hs_scr, *, tb):
    j = pl.program_id(1)

    @pl.when(j == 0)
    def _():
        h_scr[...] = h0_ref[...]

    u = p_ref[:, 0:S5_WIDTH]
    gate = p_ref[:, S5_WIDTH:2 * S5_WIDTH]
    y_parts = []
    for jb in range(S5_NBLK):
        bu = _bdot(u[:, jb * LANE:(jb + 1) * LANE], wb_ref[jb])
        for s in range(tb // BLK):
            xr = bu[s * BLK:(s + 1) * BLK, 0:S5_BLK_STATE]
            xi = bu[s * BLK:(s + 1) * BLK, S5_BLK_STATE:2 * S5_BLK_STATE]
            xr = xr.reshape(BLK // SUBLANES, SUBLANES, S5_BLK_STATE)
            xi = xi.reshape(BLK // SUBLANES, SUBLANES, S5_BLK_STATE)
            for lvl in range(SCAN_LEVELS):
                sr = pltpu.roll(xr, 1 << lvl, 1)
                si = pltpu.roll(xi, 1 << lvl, 1)
                lr, li = lv_ref[lvl, 2 * jb], lv_ref[lvl, 2 * jb + 1]
                xr, xi = xr + (lr * sr - li * si), xi + (lr * si + li * sr)
            xr = xr.reshape(BLK, S5_BLK_STATE)
            xi = xi.reshape(BLK, S5_BLK_STATE)
            c_r = h_scr[2 * jb:2 * jb + 1, :]
            c_i = h_scr[2 * jb + 1:2 * jb + 2, :]
            pr, pi = pw_ref[2 * jb], pw_ref[2 * jb + 1]
            for grp in range(BLK // SUBLANES):
                rows = slice(grp * SUBLANES, (grp + 1) * SUBLANES)
                hr = xr[rows, :] + (pr * c_r - pi * c_i)
                hi = xi[rows, :] + (pr * c_i + pi * c_r)
                c_r, c_i = hr[SUBLANES - 1:SUBLANES, :], hi[SUBLANES - 1:SUBLANES, :]
                out_rows = slice(s * BLK + grp * SUBLANES, s * BLK + (grp + 1) * SUBLANES)
                hs_scr[out_rows, 0:S5_BLK_STATE] = hr
                hs_scr[out_rows, S5_BLK_STATE:2 * S5_BLK_STATE] = hi
            h_scr[2 * jb:2 * jb + 1, :] = c_r
            h_scr[2 * jb + 1:2 * jb + 2, :] = c_i
        y_parts.append(_bdot(hs_scr[...], wc_ref[jb]))
    y = jnp.concatenate(y_parts, axis=-1) + d_ref[...] * u
    y = _gelu_tanh(y)
    y = y * _sigmoid(_bdot(y, wglu_ref[...]) + bglu_ref[...])
    y_ref[...] = y * _silu(gate)

    @pl.when(j == pl.num_programs(1) - 1)
    def _():
        h_out[...] = h_scr[...]


def _s5(p3, h0, lp, *, tb):
    bsz, seq_rows, _ = p3.shape
    st_spec = pl.BlockSpec((None, 2 * S5_NBLK, S5_BLK_STATE), lambda b, j: (b, 0, 0))
    full = lambda shape: pl.BlockSpec(shape, lambda b, j: (0,) * len(shape))
    return pl.pallas_call(
        functools.partial(_s5_kernel, tb=tb),
        out_shape=(jax.ShapeDtypeStruct((bsz, seq_rows, S5_WIDTH), F32),
                   jax.ShapeDtypeStruct((bsz, 2 * S5_NBLK, S5_BLK_STATE), F32)),
        grid=(bsz, seq_rows // tb),
        in_specs=[pl.BlockSpec((None, tb, W_S5), lambda b, j: (b, j, OFF_S5 // W_S5)),
                  st_spec,
                  full((S5_NBLK, LANE, 2 * S5_BLK_STATE)),
                  full((S5_NBLK, 2 * S5_BLK_STATE, LANE)),
                  full((SCAN_LEVELS, 2 * S5_NBLK, SUBLANES, S5_BLK_STATE)),
                  full((2 * S5_NBLK, SUBLANES, S5_BLK_STATE)),
                  full((1, S5_WIDTH)), full((S5_WIDTH, S5_WIDTH)), full((1, S5_WIDTH))],
        out_specs=(pl.BlockSpec((None, tb, S5_WIDTH), lambda b, j: (b, j, 0)), st_spec),
        scratch_shapes=[pltpu.VMEM((2 * S5_NBLK, S5_BLK_STATE), F32),
                        pltpu.VMEM((tb, 2 * S5_BLK_STATE), F32)],
        compiler_params=_cparams(("parallel", "arbitrary")),
        name="s5",
    )(p3, h0, lp["s5_wb"], lp["s5_wc"], lp["s5_lv"], lp["s5_pw"], lp["s5_d"],
      lp["s5_w_glu"], lp["s5_b_glu"])


def _mlstm_kernel(qk_ref, oz_ref, v_ref, gi_ref, gf_ref, conv0_ref, c0_ref, n0_ref, m0_ref,
                  cw_ref, cb_ref, bi_ref, bf_ref, lng_ref,
                  y_ref, conv_out, c_out, n_out, m_out,
                  xp_scr, c_scr, n_scr, m_scr, *, pad, bb_n):
    j = pl.program_id(1)
    halo = 8

    @pl.when(j == 0)
    def _():
        xp_scr[:, 0:halo, :] = jnp.zeros((bb_n, halo, W_QK), F32)
        xp_scr[:, halo - (ML_CONV - 1):halo, :] = conv0_ref[...]
        c_scr[...] = jnp.zeros(c_scr.shape, F32)
        c_scr[:, :, 0:ML_HEAD_DIM, 0:ML_HEAD_DIM] = c0_ref[...]
        n_scr[...] = jnp.zeros(n_scr.shape, F32)
        n_scr[:, :, :, 0:ML_HEAD_DIM] = n0_ref[...]
        m_scr[...] = m0_ref[...]

    t0 = jnp.where(j == 0, pad, 0) if pad else 0
    row1 = lax.broadcasted_iota(jnp.int32, (BLK, 1), 0)
    row_ok = row1 >= t0
    ti = lax.broadcasted_iota(jnp.int32, (BLK, BLK), 0)
    si = lax.broadcasted_iota(jnp.int32, (BLK, BLK), 1)
    pair_ok = (si <= ti) & (si >= t0)
    tril = jnp.where(si <= ti, 1.0, 0.0)
    eye = jnp.where(si == ti, 1.0, 0.0)
    ones_sq = jnp.ones((BLK, BLK), F32)
    lane_ok = lax.broadcasted_iota(jnp.int32, (1, ML_HEAD_PAD), 1) < ML_HEAD_DIM
    head = lambda a, h: a[:, h * ML_HEAD_PAD:(h + 1) * ML_HEAD_PAD]
    chains = [(bb, h) for bb in range(bb_n) for h in range(ML_HEADS)]

    log_i, log_f, b = [], [], []
    for bb in range(bb_n):
        log_i.append(gi_ref[bb] + bi_ref[...])
        log_f.append(jnp.where(row_ok, -_softplus(-(gf_ref[bb] + bf_ref[...])), 0.0))
        b.append(_hdot(tril, log_f[bb]))
    d_row = {}
    for bb, h in chains:
        x_col = log_i[bb][:, h:h + 1] - b[bb][:, h:h + 1]
        d_row[bb, h] = _hdot(ones_sq, eye * x_col)

    q, k, v = [], [], []
    for bb in range(bb_n):
        xp_scr[bb, halo:halo + BLK, :] = qk_ref[bb]
        conv = cb_ref[...] + xp_scr[bb, halo - 3:halo - 3 + BLK, :] * cw_ref[0:1, :]
        for tap in range(1, ML_CONV):
            conv = conv + xp_scr[bb, halo - 3 + tap:halo - 3 + tap + BLK, :] * cw_ref[tap:tap + 1, :]
        tail = xp_scr[bb, halo + BLK - (ML_CONV - 1):halo + BLK, :]
        xp_scr[bb, halo - (ML_CONV - 1):halo, :] = tail
        conv_out[bb] = tail
        act = _silu(conv)
        q.append(act[:, 0:ML_WIDTH_PAD])
        k.append(act[:, ML_WIDTH_PAD:2 * ML_WIDTH_PAD] * (1.0 / math.sqrt(ML_HEAD_DIM)))
        v.append(v_ref[bb])

    qk, q_c = {}, {}
    for bb, h in chains:
        qh = head(q[bb], h).astype(BF16)
        qk[bb, h] = lax.dot_general(qh, head(k[bb], h).astype(BF16), (((1,), (1,)), ((), ())),
                                    preferred_element_type=F32)
        q_c[bb, h] = jnp.dot(qh, c_scr[bb, h].astype(BF16), preferred_element_type=F32)

    g, we, keep = [], [], []
    for bb in range(bb_n):
        m_prev = m_scr[bb]
        g.append(b[bb] + m_prev)
        b_end = b[bb][BLK - 1:BLK, :]
        e_log = jnp.where(row_ok, b_end - b[bb] + log_i[bb], -jnp.inf)
        m_new = jnp.maximum(b_end + m_prev, jnp.max(e_log, axis=0, keepdims=True))
        we.append(jnp.exp(e_log - m_new))
        keep.append(jnp.exp(b_end + m_prev - m_new))
        m_scr[bb] = m_new

    s_mat, m_row, w_inter = {}, {}, {}
    for bb, h in chains:
        d = b[bb][:, h:h + 1] + d_row[bb, h]
        d = jnp.where(pair_ok, d, -jnp.inf)
        g_col = g[bb][:, h:h + 1]
        m_row[bb, h] = jnp.maximum(g_col, jnp.max(d, axis=1, keepdims=True))
        s_mat[bb, h] = qk[bb, h] * jnp.exp(d - m_row[bb, h])
        w_inter[bb, h] = jnp.exp(g_col - m_row[bb, h])

    s_v, k_v = {}, {}
    for bb, h in chains:
        vh = head(v[bb], h)
        s_v[bb, h] = _bdot(s_mat[bb, h], vh)
        k_v[bb, h] = lax.dot_general(head(k[bb], h).astype(BF16),
                                     (we[bb][:, h:h + 1] * vh).astype(BF16),
                                     (((0,), (0,)), ((), ())), preferred_element_type=F32)

    for bb, h in chains:
        hs = slice(h * ML_HEAD_PAD, (h + 1) * ML_HEAD_PAD)
        kh = head(k[bb], h)
        n_h = n_scr[bb, h]
        num = s_v[bb, h] + w_inter[bb, h] * q_c[bb, h]
        qn = jnp.sum(head(q[bb], h) * n_h, axis=1, keepdims=True)
        den = jnp.sum(s_mat[bb, h], axis=1, keepdims=True) + w_inter[bb, h] * qn
        hh = num / jnp.maximum(jnp.abs(den), jnp.exp(-m_row[bb, h]))
        mu = jnp.sum(hh, axis=1, keepdims=True) * (1.0 / ML_HEAD_DIM)
        dv = jnp.where(lane_ok, hh - mu, 0.0)
        var = jnp.sum(dv * dv, axis=1, keepdims=True) * (1.0 / ML_HEAD_DIM)
        hn = dv * lax.rsqrt(var + LN_EPS) * lng_ref[:, hs]
        y_ref[bb, :, hs] = (_sigmoid(oz_ref[bb, :, hs]) * hn
                            * _silu(oz_ref[bb, :, ML_WIDTH_PAD + h * ML_HEAD_PAD:
                                           ML_WIDTH_PAD + (h + 1) * ML_HEAD_PAD]))
        keep_h = keep[bb][:, h:h + 1]
        c_scr[bb, h] = keep_h * c_scr[bb, h] + k_v[bb, h]
        n_scr[bb, h] = keep_h * n_h + jnp.sum(we[bb][:, h:h + 1] * kh, axis=0, keepdims=True)

    @pl.when(j == pl.num_programs(1) - 1)
    def _():
        c_out[...] = c_scr[:, :, 0:ML_HEAD_DIM, 0:ML_HEAD_DIM]
        n_out[...] = n_scr[:, :, :, 0:ML_HEAD_DIM]
        m_out[...] = m_scr[...]


def _mlstm(p3, conv0, c0, n0, m0, lp, *, pad, bb_n):
    bsz, seq_rows, _ = p3.shape
    nblk = seq_rows // BLK
    blk = lambda w, off: pl.BlockSpec((bb_n, BLK, w), lambda b, j: (b, j, off // w))
    par = lambda r, w: pl.BlockSpec((r, w), lambda b, j: (0, 0))
    conv_spec = pl.BlockSpec((bb_n, ML_CONV - 1, W_QK), lambda b, j: (b, 0, 0))
    c_spec = pl.BlockSpec((bb_n, ML_HEADS, ML_HEAD_DIM, ML_HEAD_DIM), lambda b, j: (b, 0, 0, 0))
    n_spec = pl.BlockSpec((bb_n, ML_HEADS, 1, ML_HEAD_DIM), lambda b, j: (b, 0, 0, 0))
    m_spec = pl.BlockSpec((bb_n, 1, LANE), lambda b, j: (b, 0, 0))
    return pl.pallas_call(
        functools.partial(_mlstm_kernel, pad=pad, bb_n=bb_n),
        out_shape=(jax.ShapeDtypeStruct((bsz, seq_rows, ML_WIDTH_PAD), F32),
                   jax.ShapeDtypeStruct((bsz, ML_CONV - 1, W_QK), F32),
                   jax.ShapeDtypeStruct((bsz, ML_HEADS, ML_HEAD_DIM, ML_HEAD_DIM), F32),
                   jax.ShapeDtypeStruct((bsz, ML_HEADS, 1, ML_HEAD_DIM), F32),
                   jax.ShapeDtypeStruct((bsz, 1, LANE), F32)),
        grid=(bsz // bb_n, nblk),
        in_specs=[blk(W_QK, OFF_QK), blk(W_OZ, OFF_OZ), blk(W_MV, OFF_MV),
                  blk(W_MI, OFF_MI), blk(W_MF, OFF_MF),
                  conv_spec, c_spec, n_spec, m_spec,
                  par(ML_CONV, W_QK), par(1, W_QK), par(1, LANE), par(1, LANE),
                  par(1, ML_WIDTH_PAD)],
        out_specs=(pl.BlockSpec((bb_n, BLK, ML_WIDTH_PAD), lambda b, j: (b, j, 0)),
                   conv_spec, c_spec, n_spec, m_spec),
        scratch_shapes=[pltpu.VMEM((bb_n, 8 + BLK, W_QK), F32),
                        pltpu.VMEM((bb_n, ML_HEADS, ML_HEAD_PAD, ML_HEAD_PAD), F32),
                        pltpu.VMEM((bb_n, ML_HEADS, 1, ML_HEAD_PAD), F32),
                        pltpu.VMEM((bb_n, 1, LANE), F32)],
        compiler_params=_cparams(("parallel", "arbitrary")),
        name="mlstm",
    )(p3, p3, p3, p3, p3, conv0, c0, n0, m0, lp["ml_cw"], lp["ml_cb"], lp["ml_bi"], lp["ml_bf"],
      lp["ml_ln_g"])


def _merge_kernel(x_ref, mg_ref, yrw_ref, ys5_ref, yml_ref, bmg_ref, wrw_ref, ws5_ref, wml_ref,
                  wout_ref, g_ref, b_ref, o_ref, *, tm, seq_rows, pad):
    gates = _sigmoid(mg_ref[...] + bmg_ref[...])
    merged = (gates[:, 0:D_MODEL] * _bdot(yrw_ref[...], wrw_ref[...])
              + gates[:, D_MODEL:2 * D_MODEL] * _bdot(ys5_ref[...], ws5_ref[...])
              + gates[:, 2 * D_MODEL:3 * D_MODEL] * _bdot(yml_ref[...], wml_ref[...]))
    out = _bdot(merged, wout_ref[...])
    y = _layer_norm_rows(DN_ALPHA * x_ref[...] + out, g_ref[...], b_ref[...])
    if pad:
        y = jnp.where(_pad_row_mask(pl.program_id(0), tm, seq_rows, pad), 0.0, y)
    o_ref[...] = y


def _merge(x, p, y_rw, y_s5, y_ml, lp, *, tm, seq_rows, pad):
    n = x.shape[0]
    rows = lambda w: pl.BlockSpec((tm, w), lambda i: (i, 0))
    full = lambda r, w: pl.BlockSpec((r, w), lambda i: (0, 0))
    return pl.pallas_call(
        functools.partial(_merge_kernel, tm=tm, seq_rows=seq_rows, pad=pad),
        out_shape=jax.ShapeDtypeStruct((n, D_MODEL), F32),
        grid=(n // tm,),
        in_specs=[rows(D_MODEL), pl.BlockSpec((tm, W_MG), lambda i: (i, OFF_MG // W_MG)),
                  rows(RW_WIDTH), rows(S5_WIDTH), rows(ML_WIDTH_PAD),
                  full(1, W_MG), full(RW_WIDTH, D_MODEL), full(S5_WIDTH, D_MODEL),
                  full(ML_WIDTH_PAD, D_MODEL), full(D_MODEL, D_MODEL),
                  full(1, D_MODEL), full(1, D_MODEL)],
        out_specs=rows(D_MODEL),
        compiler_params=_cparams(("parallel",)),
        name="merge",
    )(x, p, y_rw, y_s5, y_ml, lp["b_merge"], lp["w_br_rw"], lp["w_br_s5"], lp["w_br_ml"],
      lp["w_out"], lp["ln_g"], lp["ln_b"])


def _pad_heads(w):
    lead = w.shape[:-1]
    w = w.reshape(lead + (ML_HEADS, ML_HEAD_DIM))
    w = jnp.pad(w, [(0, 0)] * len(lead) + [(0, 0), (0, ML_HEAD_PAD - ML_HEAD_DIM)])
    return w.reshape(lead + (ML_WIDTH_PAD,))


def _unpad_heads(w):
    lead = w.shape[:-1]
    return w.reshape(lead + (ML_HEADS, ML_HEAD_PAD))[..., :ML_HEAD_DIM].reshape(lead + (ML_WIDTH,))


def _pad_lanes(w, width=LANE):
    return jnp.pad(w, [(0, 0)] * (w.ndim - 1) + [(0, width - w.shape[-1])])


def _qk_pad(w):
    return jnp.concatenate([_pad_heads(w[..., :ML_WIDTH]), _pad_heads(w[..., ML_WIDTH:])], axis=-1)


def _layer_params(l, w_in, rw_mu, rw_w0, rw_w2, rw_a0, rw_a2, rw_kk, rw_ka, rw_rk, rw_ln_g, rw_ln_b,
                  s5_a_re, s5_a_im, s5_b_re, s5_b_im, s5_c_re, s5_c_im, s5_d, s5_log_dt, s5_w_glu,
                  s5_b_glu, ml_conv_w, ml_conv_b, ml_b_if, ml_ln_g, b_merge, w_br_rw, w_br_s5,
                  w_br_ml, w_out, ln_g, ln_b):
    w = w_in[l]
    o = 0
    cols = {}
    for name, size in (("rwc", RW_SHIFT), ("rwg", RW_WIDTH), ("s5", 2 * S5_WIDTH),
                       ("qk", 2 * ML_WIDTH), ("mv", ML_WIDTH), ("mi", ML_HEADS), ("mf", ML_HEADS),
                       ("mo", ML_WIDTH), ("mz", ML_WIDTH), ("mg", 3 * D_MODEL)):
        cols[name] = w[:, o:o + size]
        o += size
    w_all = jnp.concatenate([
        cols["mg"], cols["s5"], _qk_pad(cols["qk"]),
        _pad_heads(cols["mo"]), _pad_heads(cols["mz"]), _pad_heads(cols["mv"]),
        cols["rwc"][:, :RW_RKV], cols["rwg"], cols["rwc"][:, RW_RKV:],
        _pad_lanes(cols["mi"]), _pad_lanes(cols["mf"])], axis=1).astype(BF16)
    row = lambda a: a.reshape(1, -1)
    zeros_lora = jnp.zeros((RW_LORA, RW_WIDTH), F32)
    lane = jnp.arange(LANE)
    ones_blk = (lane[:, None] // RW_HEAD_DIM == lane[None, :] // RW_HEAD_DIM).astype(F32)

    bb_re, bb_im, lv_re, lv_im, pw_re, pw_im = _s5_prep(
        s5_a_re[l], s5_a_im[l], s5_log_dt[l].reshape(S5_GROUPS, 1),
        jnp.swapaxes(s5_b_re[l], 1, 2), jnp.swapaxes(s5_b_im[l], 1, 2))
    eye8 = jnp.eye(8, dtype=F32)
    blocked = lambda a: a.reshape((S5_NBLK, 8) + a.shape[1:])
    bb = jnp.stack([blocked(bb_re), blocked(bb_im)])
    wb = jnp.einsum("cjghp,gk->jghckp", bb, eye8).reshape(S5_NBLK, LANE, 2 * S5_BLK_STATE)
    cc = jnp.stack([blocked(s5_c_re[l]), -blocked(s5_c_im[l])])
    wc = jnp.einsum("cjghp,gk->jcgpkh", cc, eye8).reshape(S5_NBLK, 2 * S5_BLK_STATE, LANE)
    state_rows = lambda re, im: jnp.stack(
        [re.reshape(re.shape[:-2] + (S5_NBLK, S5_BLK_STATE)),
         im.reshape(im.shape[:-2] + (S5_NBLK, S5_BLK_STATE))], axis=-2)
    lv = state_rows(lv_re, lv_im).reshape(SCAN_LEVELS, 2 * S5_NBLK, 1, S5_BLK_STATE)
    row_in_group = jnp.arange(SUBLANES)[None, None, :, None]
    lv = jnp.where(row_in_group >= (1 << jnp.arange(SCAN_LEVELS))[:, None, None, None], lv, 0.0)
    pw = jnp.moveaxis(state_rows(pw_re, pw_im).reshape(SUBLANES, 2 * S5_NBLK, S5_BLK_STATE), 0, 1)

    return dict(
        w_all=w_all,
        mu_rkv=row(rw_mu[l][:RW_RKV]), mu_wa=row(rw_mu[l][RW_RKV:]),
        rw_w0=row(rw_w0[l]), rw_a0=row(rw_a0[l]),
        rw_w2p=jnp.concatenate([rw_w2[l], zeros_lora], axis=0),
        rw_a2p=jnp.concatenate([zeros_lora, rw_a2[l]], axis=0),
        rw_kk=row(rw_kk[l]), rw_ka=row(rw_ka[l]), rw_rk=row(rw_rk[l]),
        rw_ln_g=row(rw_ln_g[l]), rw_ln_b=row(rw_ln_b[l]),
        ones2=jnp.concatenate([ones_blk, ones_blk], axis=0).astype(BF16),
        s5_wb=wb.astype(BF16), s5_wc=wc.astype(BF16), s5_lv=lv, s5_pw=pw,
        s5_d=row(s5_d[l]), s5_w_glu=s5_w_glu[l].astype(BF16), s5_b_glu=row(s5_b_glu[l]),
        ml_cw=_qk_pad(ml_conv_w[l]), ml_cb=row(_qk_pad(ml_conv_b[l])),
        ml_bi=row(_pad_lanes(ml_b_if[l][:ML_HEADS])), ml_bf=row(_pad_lanes(ml_b_if[l][ML_HEADS:])),
        ml_ln_g=row(_pad_heads(ml_ln_g[l])),
        b_merge=row(b_merge[l]), w_br_rw=w_br_rw[l].astype(BF16), w_br_s5=w_br_s5[l].astype(BF16),
        w_br_ml=jnp.pad(w_br_ml[l].reshape(ML_HEADS, ML_HEAD_DIM, D_MODEL),
                        ((0, 0), (0, ML_HEAD_PAD - ML_HEAD_DIM), (0, 0))
                        ).reshape(ML_WIDTH_PAD, D_MODEL).astype(BF16),
        w_out=w_out[l].astype(BF16), ln_g=row(ln_g[l]), ln_b=row(ln_b[l]))


def _pack_wkv(s):
    b = s.shape[0]
    s = s.reshape(b, RW_PAIRS, 2, RW_HEAD_DIM, RW_HEAD_DIM)
    return jnp.swapaxes(s, 2, 3).reshape(b, RW_PAIRS, RW_HEAD_DIM, LANE)


def _unpack_wkv(s):
    b = s.shape[0]
    s = s.reshape(b, RW_PAIRS, RW_HEAD_DIM, 2, RW_HEAD_DIM)
    return jnp.swapaxes(s, 2, 3).reshape(b, RW_HEADS, RW_HEAD_DIM, RW_HEAD_DIM)


def _pack_s5(re, im):
    b = re.shape[0]
    return jnp.stack([re.reshape(b, S5_NBLK, S5_BLK_STATE), im.reshape(b, S5_NBLK, S5_BLK_STATE)],
                     axis=2).reshape(b, 2 * S5_NBLK, S5_BLK_STATE)


def _unpack_s5(h):
    b = h.shape[0]
    h = h.reshape(b, S5_NBLK, 2, S5_BLK_STATE)
    return (h[:, :, 0].reshape(b, S5_GROUPS, S5_STATE), h[:, :, 1].reshape(b, S5_GROUPS, S5_STATE))


def _row_tile(n, seq_rows, target):
    best = 8
    for t in range(8, min(n, target) + 1, 8):
        if seq_rows % t == 0 or (t % seq_rows == 0 and n % t == 0):
            best = t
    return best


def _trunk_layer(x, st, lp, *, bsz, seq_rows, pad, bb_n, s5_tb, tm_proj, tm_merge):
    rw_shift0, rw_wkv0, s5_re0, s5_im0, ml_conv0, ml_c0, ml_n0, ml_m0 = st
    p = _proj(x, lp["w_all"], tm=tm_proj)
    p3 = p.reshape(bsz, seq_rows, P_COLS)
    n = bsz * seq_rows

    y_rw, wkv1, sh_rkv1, sh_wa1 = _rwkv(p3, rw_shift0[:, None, :RW_RKV], rw_shift0[:, None, RW_RKV:],
                                        _pack_wkv(rw_wkv0), lp, bb_n=4 if bsz % 4 == 0 else bb_n)
    rw_shift1 = jnp.concatenate([sh_rkv1[:, 0], sh_wa1[:, 0]], axis=-1)

    y_s5, h1 = _s5(p3, _pack_s5(s5_re0, s5_im0), lp, tb=s5_tb)
    s5_re1, s5_im1 = _unpack_s5(h1)

    y_ml, conv1, c1, n1, m1 = _mlstm(
        p3, _qk_pad(ml_conv0), ml_c0, ml_n0[:, :, None, :], _pad_lanes(ml_m0)[:, None, :], lp, pad=pad,
        bb_n=bb_n)
    ml_conv1 = jnp.concatenate([_unpad_heads(conv1[..., :ML_WIDTH_PAD]),
                                _unpad_heads(conv1[..., ML_WIDTH_PAD:])], axis=-1)

    x_new = _merge(x, p, y_rw.reshape(n, RW_WIDTH), y_s5.reshape(n, S5_WIDTH),
                   y_ml.reshape(n, ML_WIDTH_PAD), lp, tm=tm_merge, seq_rows=seq_rows, pad=pad)
    return x_new, (rw_shift1, _unpack_wkv(wkv1), s5_re1, s5_im1, ml_conv1, c1, n1[:, :, 0, :],
                   m1[:, 0, :ML_HEADS])


def _run_group(x_rows, states, lps, *, bsz, seq_rows, pad, in_ln_g, in_ln_b):
    n = bsz * seq_rows
    bb_n = 2 if bsz % 2 == 0 else 1
    s5_tb = max(t for t in (BLK, 5 * BLK) if seq_rows % t == 0)
    tm_ln = _row_tile(n, seq_rows, 1024)
    tm_proj = _row_tile(n, seq_rows, 2080)
    tm_merge = _row_tile(n, seq_rows, 320)
    x = _ln_in(x_rows, in_ln_g, in_ln_b, tm=tm_ln, seq_rows=seq_rows, pad=pad)
    outs = []
    for l in range(DEPTH):
        x, st = _trunk_layer(x, states[l], lps[l], bsz=bsz, seq_rows=seq_rows, pad=pad, bb_n=bb_n,
                             s5_tb=s5_tb, tm_proj=tm_proj, tm_merge=tm_merge)
        outs.append(st)
    return x, outs


def kernel(x_prompt, x_sample, state_rwkv_shift, state_rwkv_wkv, state_s5_re, state_s5_im, state_mlstm_conv, state_mlstm_c, state_mlstm_n, state_mlstm_m, meta, in_ln_g, in_ln_b, w_in, rw_mu, rw_w0, rw_w2, rw_a0, rw_a2, rw_kk, rw_ka, rw_rk, rw_ln_g, rw_ln_b, s5_a_re, s5_a_im, s5_b_re, s5_b_im, s5_c_re, s5_c_im, s5_d, s5_log_dt, s5_w_glu, s5_b_glu, ml_conv_w, ml_conv_b, ml_b_if, ml_ln_g, b_merge, w_br_rw, w_br_s5, w_br_ml, w_out, ln_g, ln_b):
    lps = [_layer_params(l, w_in, rw_mu, rw_w0, rw_w2, rw_a0, rw_a2, rw_kk, rw_ka, rw_rk, rw_ln_g,
                         rw_ln_b, s5_a_re, s5_a_im, s5_b_re, s5_b_im, s5_c_re, s5_c_im, s5_d,
                         s5_log_dt, s5_w_glu, s5_b_glu, ml_conv_w, ml_conv_b, ml_b_if, ml_ln_g,
                         b_merge, w_br_rw, w_br_s5, w_br_ml, w_out, ln_g, ln_b)
           for l in range(DEPTH)]
    g_in, b_in = in_ln_g.reshape(1, D_MODEL), in_ln_b.reshape(1, D_MODEL)

    bp, sp = x_prompt.shape[0], x_prompt.shape[1]
    lp_rows = PAD + N_META + sp
    xp = jnp.concatenate([jnp.zeros((bp, PAD, D_MODEL), F32),
                          jnp.broadcast_to(meta[None], (bp, N_META, D_MODEL)), x_prompt], axis=1)
    z = lambda *shape: jnp.zeros((bp,) + shape, F32)
    zero_state = (z(RW_SHIFT), z(RW_HEADS, RW_HEAD_DIM, RW_HEAD_DIM), z(S5_GROUPS, S5_STATE),
                  z(S5_GROUPS, S5_STATE), z(ML_CONV - 1, 2 * ML_WIDTH),
                  z(ML_HEADS, ML_HEAD_DIM, ML_HEAD_DIM), z(ML_HEADS, ML_HEAD_DIM), z(ML_HEADS))
    yp, p_states = _run_group(xp.reshape(bp * lp_rows, D_MODEL), [zero_state] * DEPTH, lps,
                              bsz=bp, seq_rows=lp_rows, pad=PAD, in_ln_g=g_in, in_ln_b=b_in)
    y_prompt = yp.reshape(bp, lp_rows, D_MODEL)[:, PAD + N_META:]

    bs, ds = x_sample.shape[0], x_sample.shape[1]
    s_in = [(state_rwkv_shift[l], state_rwkv_wkv[l], state_s5_re[l], state_s5_im[l],
             state_mlstm_conv[l], state_mlstm_c[l], state_mlstm_n[l], state_mlstm_m[l])
            for l in range(DEPTH)]
    ys, s_states = _run_group(x_sample.reshape(bs * ds, D_MODEL), s_in, lps,
                              bsz=bs, seq_rows=ds, pad=0, in_ln_g=g_in, in_ln_b=b_in)
    y_sample = ys.reshape(bs, ds, D_MODEL)

    stack = lambda sts: tuple(jnp.stack(s, 0) for s in zip(*sts))
    return (y_prompt, y_sample) + stack(p_states) + stack(s_states)
```

```python
import functools
import math

import jax
import jax.numpy as jnp
from jax import lax
from jax.experimental import pallas as pl
from jax.experimental.pallas import tpu as pltpu

F32 = jnp.float32
BF16 = jnp.bfloat16

D_MODEL = 1024
DEPTH = 2
N_META = 16
RW_HEADS = 12
RW_HEAD_DIM = 64
RW_WIDTH = RW_HEADS * RW_HEAD_DIM
RW_PAIRS = RW_HEADS // 2
RW_LORA = 64
RW_RKV = 3 * RW_WIDTH
RW_SHIFT = RW_RKV + 2 * RW_LORA
S5_GROUPS = 32
S5_GROUP_CH = 16
S5_WIDTH = S5_GROUPS * S5_GROUP_CH
S5_STATE = 64
S5_NBLK = 4
S5_BLK_STATE = 512
ML_HEADS = 4
ML_HEAD_DIM = 192
ML_HEAD_PAD = 256
ML_WIDTH = ML_HEADS * ML_HEAD_DIM
ML_WIDTH_PAD = ML_HEADS * ML_HEAD_PAD
ML_CONV = 4
DN_ALPHA = (2 * DEPTH) ** 0.25
LN_EPS = 1e-5
RW_GN_EPS = 64e-5

LANE = 128
SUBLANES = 8
BLK = 64
PAD = BLK - N_META
SCAN_LEVELS = 3

OFF_MG, W_MG = 0, 3 * D_MODEL
OFF_S5, W_S5 = 3072, 2 * S5_WIDTH
OFF_QK, W_QK = 4096, 2 * ML_WIDTH_PAD
OFF_OZ, W_OZ = 6144, 2 * ML_WIDTH_PAD
OFF_MV, W_MV = 8192, ML_WIDTH_PAD
OFF_RKV, W_RKV = 9216, RW_RKV
OFF_RWG, W_RWG = 11520, RW_WIDTH
OFF_WA, W_WA = 12288, LANE
OFF_MI, W_MI = 12416, LANE
OFF_MF, W_MF = 12544, LANE
P_USED = 12672
P_COLS = 12800
P_TN = 1280

VMEM_LIMIT = 56 * 1024 * 1024


def _cparams(sem):
    return pltpu.CompilerParams(dimension_semantics=sem, vmem_limit_bytes=VMEM_LIMIT)


def _bdot(a, b):
    return jnp.dot(a.astype(BF16), b.astype(BF16), preferred_element_type=F32)


def _hdot(a, b):
    return jnp.dot(a, b, precision=lax.Precision.HIGHEST, preferred_element_type=F32)


def _sigmoid(x):
    return 1.0 / (1.0 + jnp.exp(-x))


def _silu(x):
    return x * _sigmoid(x)


def _softplus(x):
    return jnp.maximum(x, 0.0) + jnp.log1p(jnp.exp(-jnp.abs(x)))


def _pad_row_mask(tile_idx, tm, seq_rows, pad):
    pos0 = lax.rem(tile_idx * tm, seq_rows)
    row = lax.broadcasted_iota(jnp.int32, (tm, 1), 0) + pos0
    return row < pad


def _layer_norm_rows(x, g, b):
    mu = jnp.mean(x, axis=-1, keepdims=True)
    d = x - mu
    var = jnp.mean(d * d, axis=-1, keepdims=True)
    return d * lax.rsqrt(var + LN_EPS) * g + b


def _ln_in_kernel(x_ref, g_ref, b_ref, o_ref, *, tm, seq_rows, pad):
    y = _layer_norm_rows(x_ref[...], g_ref[...], b_ref[...])
    if pad:
        y = jnp.where(_pad_row_mask(pl.program_id(0), tm, seq_rows, pad), 0.0, y)
    o_ref[...] = y


def _ln_in(x, g, b, *, tm, seq_rows, pad):
    n = x.shape[0]
    return pl.pallas_call(
        functools.partial(_ln_in_kernel, tm=tm, seq_rows=seq_rows, pad=pad),
        out_shape=jax.ShapeDtypeStruct((n, D_MODEL), F32),
        grid=(n // tm,),
        in_specs=[pl.BlockSpec((tm, D_MODEL), lambda i: (i, 0)),
                  pl.BlockSpec((1, D_MODEL), lambda i: (0, 0)),
                  pl.BlockSpec((1, D_MODEL), lambda i: (0, 0))],
        out_specs=pl.BlockSpec((tm, D_MODEL), lambda i: (i, 0)),
        compiler_params=_cparams(("parallel",)),
        name="ln_in",
    )(x, g, b)


def _proj_kernel(x_ref, w_ref, o_ref):
    o_ref[...] = jnp.dot(x_ref[...].astype(BF16), w_ref[...], preferred_element_type=F32)


def _proj(x, w_all, *, tm):
    n = x.shape[0]
    return pl.pallas_call(
        _proj_kernel,
        out_shape=jax.ShapeDtypeStruct((n, P_COLS), F32),
        grid=(n // tm, P_COLS // P_TN),
        in_specs=[pl.BlockSpec((tm, D_MODEL), lambda i, j: (i, 0)),
                  pl.BlockSpec((D_MODEL, P_TN), lambda i, j: (0, j))],
        out_specs=pl.BlockSpec((tm, P_TN), lambda i, j: (i, j)),
        compiler_params=_cparams(("parallel", "arbitrary")),
        name="proj",
    )(x, w_all)


def _split_lhs(hi_f32, lo_f32):
    return jnp.concatenate([hi_f32.astype(BF16), lo_f32.astype(BF16)], axis=1)


def _seg_lhs(x):
    hi = x.astype(BF16).astype(F32)
    return _split_lhs(hi, x - hi)


def _head_sum(x, ones2):
    parts = [jnp.dot(_seg_lhs(x[:, i * LANE:(i + 1) * LANE]), ones2, preferred_element_type=F32)
             for i in range(RW_WIDTH // LANE)]
    return jnp.concatenate(parts, axis=-1)


def _shift_rows(u, carry):
    rolled = pltpu.roll(u, 1, 0)
    row = lax.broadcasted_iota(jnp.int32, u.shape, 0)
    return jnp.where(row == 0, carry, rolled)


def _hi_lo(a):
    hi = a.astype(BF16)
    return hi, (a - hi.astype(F32)).astype(BF16)


def _lhs3(hl):
    return jnp.concatenate([hl[0], hl[1], hl[0]], axis=1)


def _rhs3_rows(hl):
    return jnp.concatenate([hl[0], hl[0], hl[1]], axis=1)


def _rhs3(hl):
    return jnp.concatenate([hl[0], hl[0], hl[1]], axis=0)


def _dot_nt(lhs3, rhs3_rows):
    return lax.dot_general(lhs3, rhs3_rows, (((1,), (1,)), ((), ())), preferred_element_type=F32)


def _dot_nn(lhs3, rhs3):
    return jnp.dot(lhs3, rhs3, preferred_element_type=F32)


def _rw_prepare(u, uw, carry_rkv, carry_wa, mu_rkv, mu_wa, w0, w2p, a0, a2p, kk_gain, ka_gain,
                rk_gain, ones2):
    xs = u + (_shift_rows(u, carry_rkv) - u) * mu_rkv
    xwa = uw + (_shift_rows(uw, carry_wa) - uw) * mu_wa
    r = xs[:, 0:RW_WIDTH]
    k = xs[:, RW_WIDTH:2 * RW_WIDTH]
    v = xs[:, 2 * RW_WIDTH:3 * RW_WIDTH]
    log_decay = -math.exp(-0.5) * _sigmoid(w0 + _bdot(jnp.tanh(xwa), w2p))
    a = _sigmoid(a0 + _bdot(xwa, a2p))
    kk = k * kk_gain
    kk = kk * lax.rsqrt(_head_sum(kk * kk, ones2) + 1e-12)
    k = k * (1.0 + (a - 1.0) * ka_gain)
    kka = kk * a
    ti = lax.broadcasted_iota(jnp.int32, (BLK, BLK), 0)
    si = lax.broadcasted_iota(jnp.int32, (BLK, BLK), 1)
    cum = _hdot(jnp.where(si <= ti, 1.0, 0.0), log_decay)
    cum_end = cum[BLK - 1:BLK, :]
    inv_w = jnp.exp(-cum)
    tail_w = jnp.exp(cum_end - cum)
    return dict(at=-kk * jnp.exp(cum - log_decay), rt=r * jnp.exp(cum), bh=kka * inv_w, kh=k * inv_w,
                bc=kka * tail_w, kc=k * tail_w, v=v, wt=jnp.exp(cum_end),
                bonus=_head_sum(r * k * rk_gain, ones2) * v)


def _rw_kernel(rkv_ref, wa_ref, gate_ref, sh0_rkv_ref, sh0_wa_ref, s0_ref,
               mu_rkv_ref, mu_wa_ref, w0_ref, w2_ref, a0_ref, a2_ref, kk_ref, ka_ref, rk_ref,
               lng_ref, lnb_ref, ones_ref,
               y_ref, s_out, sh_rkv_out, sh_wa_out, s_scr, c_rkv, c_wa, *, bb_n):
    j = pl.program_id(1)

    @pl.when(j == 0)
    def _():
        for bb in range(bb_n):
            for p in range(RW_PAIRS):
                s_scr[bb, p] = jnp.concatenate([s0_ref[bb, 2 * p], s0_ref[bb, 2 * p + 1]], axis=1)
        c_rkv[...] = sh0_rkv_ref[...]
        c_wa[...] = sh0_wa_ref[...]

    ones2 = ones_ref[...]
    rows = []
    for bb in range(bb_n):
        u, uw = rkv_ref[bb], wa_ref[bb]
        rows.append(_rw_prepare(u, uw, c_rkv[bb], c_wa[bb], mu_rkv_ref[...], mu_wa_ref[...],
                                w0_ref[...], w2_ref[...], a0_ref[...], a2_ref[...], kk_ref[...],
                                ka_ref[...], rk_ref[...], ones2))
        for carry, out, src in ((c_rkv, sh_rkv_out, u), (c_wa, sh_wa_out, uw)):
            carry[bb] = src[BLK - 1:BLK, :]
            out[bb] = src[BLK - 1:BLK, :]

    lo_lane = lax.broadcasted_iota(jnp.int32, (1, LANE), 1) < RW_HEAD_DIM
    row_j = lax.broadcasted_iota(jnp.int32, (RW_HEAD_DIM, LANE), 0)
    col_t = lax.broadcasted_iota(jnp.int32, (RW_HEAD_DIM, LANE), 1) & (RW_HEAD_DIM - 1)
    strict = row_j < col_t
    incl = row_j <= col_t

    def by_head(x):
        return jnp.concatenate([jnp.where(lo_lane, x, 0.0), jnp.where(lo_lane, 0.0, x)], axis=0)

    pick = jnp.where(row_j == col_t, 1.0, 0.0).astype(BF16)
    pick2 = jnp.concatenate([pick, pick], axis=1)

    bd = lambda hl: (by_head(hl[0]), by_head(hl[1]))
    stack_rows = lambda parts: tuple(jnp.concatenate(x, axis=0) for x in zip(*parts))
    stack_lanes = lambda parts: tuple(jnp.concatenate(x, axis=1) for x in zip(*parts))

    def pair_transpose(x):
        return _dot_nt(pick2, jnp.concatenate(bd(_hi_lo(x)), axis=1))

    chains = [(bb, p) for bb in range(bb_n) for p in range(RW_PAIRS)]
    n_ch = len(chains)
    tile = lambda name, c: rows[chains[c][0]][name][:, chains[c][1] * LANE:(chains[c][1] + 1) * LANE]
    tile_hl = lambda name, c: _hi_lo(tile(name, c))

    ar_rows, gb, gk_hk_kc3, bc3 = [], [], [], []
    for c in range(n_ch):
        ar_rows.append(_rhs3_rows(stack_rows([bd(tile_hl("at", c)), bd(tile_hl("rt", c))])))
        bk = stack_rows([tile_hl("bh", c), tile_hl("kh", c)])
        gram = _dot_nt(_lhs3(bk), ar_rows[c])
        gb.append(_hi_lo(jnp.where(strict, gram[0:RW_HEAD_DIM, 0:LANE], 0.0)))
        hb = jnp.where(incl, gram[0:RW_HEAD_DIM, LANE:2 * LANE], 0.0)
        gk = jnp.where(strict, gram[RW_HEAD_DIM:LANE, 0:LANE], 0.0)
        hk = jnp.where(incl, gram[RW_HEAD_DIM:LANE, LANE:2 * LANE], 0.0)
        gk_hk_kc3.append(_rhs3(stack_lanes([bd(_hi_lo(gk)), bd(_hi_lo(hk)), bd(tile_hl("kc", c))])))
        bc3.append(_rhs3(stack_lanes([bd(_hi_lo(hb)), bd(tile_hl("bc", c))])))
    vt = [pair_transpose(tile("v", c)) for c in range(n_ch)]
    v_terms = [_dot_nn(_lhs3(_hi_lo(vt[c])), gk_hk_kc3[c]) for c in range(n_ch)]
    powers = [[_rhs3(bd(g)) for g in gb]]
    cur = gb
    for _ in range(5):
        cur = [_hi_lo(_dot_nn(_lhs3(cur[c]), powers[-1][c])) for c in range(n_ch)]
        powers.append([_rhs3(bd(g)) for g in cur])

    s_old = [s_scr[bb, p] for bb, p in chains]
    uy = [_dot_nt(_lhs3(_hi_lo(s_old[c])), ar_rows[c]) for c in range(n_ch)]
    x = [uy[c][:, 0:LANE] + v_terms[c][:, 0:LANE] for c in range(n_ch)]
    for lvl in range(6):
        x = [x[c] + _dot_nn(_lhs3(_hi_lo(x[c])), powers[lvl][c]) for c in range(n_ch)]
    out2 = [_dot_nn(_lhs3(_hi_lo(x[c])), bc3[c]) for c in range(n_ch)]
    y_tiles = []
    for c, (bb, p) in enumerate(chains):
        y_col = uy[c][:, LANE:2 * LANE] + out2[c][:, 0:LANE] + v_terms[c][:, LANE:2 * LANE]
        y_tiles.append(pair_transpose(y_col))
        s_scr[bb, p] = (s_old[c] * tile("wt", c) + out2[c][:, LANE:2 * LANE]
                        + v_terms[c][:, 2 * LANE:3 * LANE])

    inv = 1.0 / RW_HEAD_DIM
    for bb in range(bb_n):
        y = jnp.concatenate(y_tiles[bb * RW_PAIRS:(bb + 1) * RW_PAIRS], axis=1)
        mu = _head_sum(y, ones2) * inv
        d = y - mu
        var = _head_sum(d * d, ones2) * inv
        yn = d * lax.rsqrt(var + RW_GN_EPS) * lng_ref[...] + lnb_ref[...]
        y_ref[bb] = (yn + rows[bb]["bonus"]) * _silu(gate_ref[bb])

    @pl.when(j == pl.num_programs(1) - 1)
    def _():
        for bb in range(bb_n):
            for p in range(RW_PAIRS):
                s_out[bb, 2 * p] = s_scr[bb, p, :, 0:RW_HEAD_DIM]
                s_out[bb, 2 * p + 1] = s_scr[bb, p, :, RW_HEAD_DIM:LANE]


def _rwkv(p3, sh0_rkv, sh0_wa, s0, lp, *, bb_n):
    bsz, seq_rows, _ = p3.shape
    nblk = seq_rows // BLK
    blk = lambda w, off: pl.BlockSpec((bb_n, BLK, w), lambda b, j: (b, j, off // w))
    par = lambda w: pl.BlockSpec((1, w), lambda b, j: (0, 0))
    mat = pl.BlockSpec((LANE, RW_WIDTH), lambda b, j: (0, 0))
    st = lambda w: pl.BlockSpec((bb_n, 1, w), lambda b, j: (b, 0, 0))
    s_spec = pl.BlockSpec((bb_n, RW_HEADS, RW_HEAD_DIM, RW_HEAD_DIM), lambda b, j: (b, 0, 0, 0))
    return pl.pallas_call(
        functools.partial(_rw_kernel, bb_n=bb_n),
        out_shape=(jax.ShapeDtypeStruct((bsz, seq_rows, RW_WIDTH), F32),
                   jax.ShapeDtypeStruct((bsz, RW_HEADS, RW_HEAD_DIM, RW_HEAD_DIM), F32),
                   jax.ShapeDtypeStruct((bsz, 1, RW_RKV), F32),
                   jax.ShapeDtypeStruct((bsz, 1, LANE), F32)),
        grid=(bsz // bb_n, nblk),
        in_specs=[blk(W_RKV, OFF_RKV), blk(W_WA, OFF_WA), blk(W_RWG, OFF_RWG),
                  st(RW_RKV), st(LANE), s_spec,
                  par(RW_RKV), par(LANE), par(RW_WIDTH), mat, par(RW_WIDTH), mat,
                  par(RW_WIDTH), par(RW_WIDTH), par(RW_WIDTH), par(RW_WIDTH), par(RW_WIDTH),
                  pl.BlockSpec((2 * LANE, LANE), lambda b, j: (0, 0))],
        out_specs=(pl.BlockSpec((bb_n, BLK, RW_WIDTH), lambda b, j: (b, j, 0)), s_spec,
                   st(RW_RKV), st(LANE)),
        scratch_shapes=[pltpu.VMEM((bb_n, RW_PAIRS, RW_HEAD_DIM, LANE), F32),
                        pltpu.VMEM((bb_n, 1, RW_RKV), F32), pltpu.VMEM((bb_n, 1, LANE), F32)],
        compiler_params=_cparams(("parallel", "arbitrary")),
        name="rwkv",
    )(p3, p3, p3, sh0_rkv, sh0_wa, s0, lp["mu_rkv"], lp["mu_wa"], lp["rw_w0"], lp["rw_w2p"],
      lp["rw_a0"], lp["rw_a2p"], lp["rw_kk"], lp["rw_ka"], lp["rw_rk"], lp["rw_ln_g"],
      lp["rw_ln_b"], lp["ones2"])


def _cmul(ar, ai, br, bi):
    return ar * br - ai * bi, ar * bi + ai * br


def _s5_prep_kernel(are_ref, aim_ref, ldt_ref, bre_ref, bim_ref,
                    bbre_out, bbim_out, lvre_out, lvim_out, pwre_out, pwim_out):
    ar, ai = are_ref[...], aim_ref[...]
    dt = jnp.exp(ldt_ref[...])
    mag = jnp.exp(ar * dt)
    lr, li = mag * jnp.cos(ai * dt), mag * jnp.sin(ai * dt)
    nr, ni = lr - 1.0, li
    den = ar * ar + ai * ai
    qr, qi = (nr * ar + ni * ai) / den, (ni * ar - nr * ai) / den
    br, bi = bre_ref[...], bim_ref[...]
    bbr, bbi = _cmul(qr[:, None, :], qi[:, None, :], br, bi)
    bbre_out[...] = bbr
    bbim_out[...] = bbi
    sq_r, sq_i = lr, li
    pows = [(lr, li)]
    for lvl in range(SCAN_LEVELS):
        lvre_out[lvl] = sq_r
        lvim_out[lvl] = sq_i
        pows = pows + [_cmul(pr, pi, sq_r, sq_i) for pr, pi in pows]
        sq_r, sq_i = _cmul(sq_r, sq_i, sq_r, sq_i)
    for t in range(SUBLANES):
        pwre_out[t] = pows[t][0]
        pwim_out[t] = pows[t][1]


def _s5_prep(a_re, a_im, log_dt, b_re_t, b_im_t):
    ghp = jax.ShapeDtypeStruct((S5_GROUPS, S5_GROUP_CH, S5_STATE), F32)
    lv = jax.ShapeDtypeStruct((SCAN_LEVELS, S5_GROUPS, S5_STATE), F32)
    pw = jax.ShapeDtypeStruct((SUBLANES, S5_GROUPS, S5_STATE), F32)
    return pl.pallas_call(
        _s5_prep_kernel,
        out_shape=(ghp, ghp, lv, lv, pw, pw),
        name="s5_prep",
    )(a_re, a_im, log_dt, b_re_t, b_im_t)


def _gelu_tanh(x):
    return 0.5 * x * (1.0 + jnp.tanh(math.sqrt(2.0 / math.pi) * (x + 0.044715 * (x * x * x))))


def _s5_kernel(p_ref, h0_ref, wb_ref, wc_ref, lv_ref, pw_ref, d_ref, wglu_ref, bglu_ref,
               y_ref, h_out, h_scr, hs_scr, *, tb):
    j = pl.program_id(1)

    @pl.when(j == 0)
    def _():
        h_scr[...] = h0_ref[...]

    u = p_ref[:, 0:S5_WIDTH]
    gate = p_ref[:, S5_WIDTH:2 * S5_WIDTH]
    y_parts = []
    for jb in range(S5_NBLK):
        bu = _bdot(u[:, jb * LANE:(jb + 1) * LANE], wb_ref[jb])
        for s in range(tb // BLK):
            xr = bu[s * BLK:(s + 1) * BLK, 0:S5_BLK_STATE]
            xi = bu[s * BLK:(s + 1) * BLK, S5_BLK_STATE:2 * S5_BLK_STATE]
            xr = xr.reshape(BLK // SUBLANES, SUBLANES, S5_BLK_STATE)
            xi = xi.reshape(BLK // SUBLANES, SUBLANES, S5_BLK_STATE)
            for lvl in range(SCAN_LEVELS):
                sr = pltpu.roll(xr, 1 << lvl, 1)
                si = pltpu.roll(xi, 1 << lvl, 1)
                lr, li = lv_ref[lvl, 2 * jb], lv_ref[lvl, 2 * jb + 1]
                xr, xi = xr + (lr * sr - li * si), xi + (lr * si + li * sr)
            xr = xr.reshape(BLK, S5_BLK_STATE)
            xi = xi.reshape(BLK, S5_BLK_STATE)
            c_r = h_scr[2 * jb:2 * jb + 1, :]
            c_i = h_scr[2 * jb + 1:2 * jb + 2, :]
            pr, pi = pw_ref[2 * jb], pw_ref[2 * jb + 1]
            for grp in range(BLK // SUBLANES):
                rows = slice(grp * SUBLANES, (grp + 1) * SUBLANES)
                hr = xr[rows, :] + (pr * c_r - pi * c_i)
                hi = xi[rows, :] + (pr * c_i + pi * c_r)
                c_r, c_i = hr[SUBLANES - 1:SUBLANES, :], hi[SUBLANES - 1:SUBLANES, :]
                out_rows = slice(s * BLK + grp * SUBLANES, s * BLK + (grp + 1) * SUBLANES)
                hs_scr[out_rows, 0:S5_BLK_STATE] = hr
                hs_scr[out_rows, S5_BLK_STATE:2 * S5_BLK_STATE] = hi
            h_scr[2 * jb:2 * jb + 1, :] = c_r
            h_scr[2 * jb + 1:2 * jb + 2, :] = c_i
        y_parts.append(_bdot(hs_scr[...], wc_ref[jb]))
    y = jnp.concatenate(y_parts, axis=-1) + d_ref[...] * u
    y = _gelu_tanh(y)
    y = y * _sigmoid(_bdot(y, wglu_ref[...]) + bglu_ref[...])
    y_ref[...] = y * _silu(gate)

    @pl.when(j == pl.num_programs(1) - 1)
    def _():
        h_out[...] = h_scr[...]


def _s5(p3, h0, lp, *, tb):
    bsz, seq_rows, _ = p3.shape
    st_spec = pl.BlockSpec((None, 2 * S5_NBLK, S5_BLK_STATE), lambda b, j: (b, 0, 0))
    full = lambda shape: pl.BlockSpec(shape, lambda b, j: (0,) * len(shape))
    return pl.pallas_call(
        functools.partial(_s5_kernel, tb=tb),
        out_shape=(jax.ShapeDtypeStruct((bsz, seq_rows, S5_WIDTH), F32),
                   jax.ShapeDtypeStruct((bsz, 2 * S5_NBLK, S5_BLK_STATE), F32)),
        grid=(bsz, seq_rows // tb),
        in_specs=[pl.BlockSpec((None, tb, W_S5), lambda b, j: (b, j, OFF_S5 // W_S5)),
                  st_spec,
                  full((S5_NBLK, LANE, 2 * S5_BLK_STATE)),
                  full((S5_NBLK, 2 * S5_BLK_STATE, LANE)),
                  full((SCAN_LEVELS, 2 * S5_NBLK, SUBLANES, S5_BLK_STATE)),
                  full((2 * S5_NBLK, SUBLANES, S5_BLK_STATE)),
                  full((1, S5_WIDTH)), full((S5_WIDTH, S5_WIDTH)), full((1, S5_WIDTH))],
        out_specs=(pl.BlockSpec((None, tb, S5_WIDTH), lambda b, j: (b, j, 0)), st_spec),
        scratch_shapes=[pltpu.VMEM((2 * S5_NBLK, S5_BLK_STATE), F32),
                        pltpu.VMEM((tb, 2 * S5_BLK_STATE), F32)],
        compiler_params=_cparams(("parallel", "arbitrary")),
        name="s5",
    )(p3, h0, lp["s5_wb"], lp["s5_wc"], lp["s5_lv"], lp["s5_pw"], lp["s5_d"],
      lp["s5_w_glu"], lp["s5_b_glu"])


def _mlstm_kernel(qk_ref, oz_ref, v_ref, gi_ref, gf_ref, conv0_ref, c0_ref, n0_ref, m0_ref,
                  cw_ref, cb_ref, bi_ref, bf_ref, lng_ref,
                  y_ref, conv_out, c_out, n_out, m_out,
                  xp_scr, c_scr, n_scr, m_scr, *, pad, bb_n):
    j = pl.program_id(1)
    halo = 8

    @pl.when(j == 0)
    def _():
        xp_scr[:, 0:halo, :] = jnp.zeros((bb_n, halo, W_QK), F32)
        xp_scr[:, halo - (ML_CONV - 1):halo, :] = conv0_ref[...]
        c_scr[...] = jnp.zeros(c_scr.shape, F32)
        c_scr[:, :, 0:ML_HEAD_DIM, 0:ML_HEAD_DIM] = c0_ref[...]
        n_scr[...] = jnp.zeros(n_scr.shape, F32)
        n_scr[:, :, :, 0:ML_HEAD_DIM] = n0_ref[...]
        m_scr[...] = m0_ref[...]

    t0 = jnp.where(j == 0, pad, 0) if pad else 0
    row1 = lax.broadcasted_iota(jnp.int32, (BLK, 1), 0)
    row_ok = row1 >= t0
    ti = lax.broadcasted_iota(jnp.int32, (BLK, BLK), 0)
    si = lax.broadcasted_iota(jnp.int32, (BLK, BLK), 1)
    pair_ok = (si <= ti) & (si >= t0)
    tril = jnp.where(si <= ti, 1.0, 0.0)
    eye = jnp.where(si == ti, 1.0, 0.0)
    ones_sq = jnp.ones((BLK, BLK), F32)
    lane_ok = lax.broadcasted_iota(jnp.int32, (1, ML_HEAD_PAD), 1) < ML_HEAD_DIM
    head = lambda a, h: a[:, h * ML_HEAD_PAD:(h + 1) * ML_HEAD_PAD]
    chains = [(bb, h) for bb in range(bb_n) for h in range(ML_HEADS)]

    log_i, log_f, b = [], [], []
    for bb in range(bb_n):
        log_i.append(gi_ref[bb] + bi_ref[...])
        log_f.append(jnp.where(row_ok, -_softplus(-(gf_ref[bb] + bf_ref[...])), 0.0))
        b.append(_hdot(tril, log_f[bb]))
    d_row = {}
    for bb, h in chains:
        x_col = log_i[bb][:, h:h + 1] - b[bb][:, h:h + 1]
        d_row[bb, h] = _hdot(ones_sq, eye * x_col)

    q, k, v = [], [], []
    for bb in range(bb_n):
        xp_scr[bb, halo:halo + BLK, :] = qk_ref[bb]
        conv = cb_ref[...] + xp_scr[bb, halo - 3:halo - 3 + BLK, :] * cw_ref[0:1, :]
        for tap in range(1, ML_CONV):
            conv = conv + xp_scr[bb, halo - 3 + tap:halo - 3 + tap + BLK, :] * cw_ref[tap:tap + 1, :]
        tail = xp_scr[bb, halo + BLK - (ML_CONV - 1):halo + BLK, :]
        xp_scr[bb, halo - (ML_CONV - 1):halo, :] = tail
        conv_out[bb] = tail
        act = _silu(conv)
        q.append(act[:, 0:ML_WIDTH_PAD])
        k.append(act[:, ML_WIDTH_PAD:2 * ML_WIDTH_PAD] * (1.0 / math.sqrt(ML_HEAD_DIM)))
        v.append(v_ref[bb])

    qk, q_c = {}, {}
    for bb, h in chains:
        qh = head(q[bb], h).astype(BF16)
        qk[bb, h] = lax.dot_general(qh, head(k[bb], h).astype(BF16), (((1,), (1,)), ((), ())),
                                    preferred_element_type=F32)
        q_c[bb, h] = jnp.dot(qh, c_scr[bb, h].astype(BF16), preferred_element_type=F32)

    g, we, keep = [], [], []
    for bb in range(bb_n):
        m_prev = m_scr[bb]
        g.append(b[bb] + m_prev)
        b_end = b[bb][BLK - 1:BLK, :]
        e_log = jnp.where(row_ok, b_end - b[bb] + log_i[bb], -jnp.inf)
        m_new = jnp.maximum(b_end + m_prev, jnp.max(e_log, axis=0, keepdims=True))
        we.append(jnp.exp(e_log - m_new))
        keep.append(jnp.exp(b_end + m_prev - m_new))
        m_scr[bb] = m_new

    s_mat, m_row, w_inter = {}, {}, {}
    for bb, h in chains:
        d = b[bb][:, h:h + 1] + d_row[bb, h]
        d = jnp.where(pair_ok, d, -jnp.inf)
        g_col = g[bb][:, h:h + 1]
        m_row[bb, h] = jnp.maximum(g_col, jnp.max(d, axis=1, keepdims=True))
        s_mat[bb, h] = qk[bb, h] * jnp.exp(d - m_row[bb, h])
        w_inter[bb, h] = jnp.exp(g_col - m_row[bb, h])

    s_v, k_v = {}, {}
    for bb, h in chains:
        vh = head(v[bb], h)
        s_v[bb, h] = _bdot(s_mat[bb, h], vh)
        k_v[bb, h] = lax.dot_general(head(k[bb], h).astype(BF16),
                                     (we[bb][:, h:h + 1] * vh).astype(BF16),
                                     (((0,), (0,)), ((), ())), preferred_element_type=F32)

    for bb, h in chains:
        hs = slice(h * ML_HEAD_PAD, (h + 1) * ML_HEAD_PAD)
        kh = head(k[bb], h)
        n_h = n_scr[bb, h]
        num = s_v[bb, h] + w_inter[bb, h] * q_c[bb, h]
        qn = jnp.sum(head(q[bb], h) * n_h, axis=1, keepdims=True)
        den = jnp.sum(s_mat[bb, h], axis=1, keepdims=True) + w_inter[bb, h] * qn
        hh = num / jnp.maximum(jnp.abs(den), jnp.exp(-m_row[bb, h]))
        mu = jnp.sum(hh, axis=1, keepdims=True) * (1.0 / ML_HEAD_DIM)
        dv = jnp.where(lane_ok, hh - mu, 0.0)
        var = jnp.sum(dv * dv, axis=1, keepdims=True) * (1.0 / ML_HEAD_DIM)
        hn = dv * lax.rsqrt(var + LN_EPS) * lng_ref[:, hs]
        y_ref[bb, :, hs] = (_sigmoid(oz_ref[bb, :, hs]) * hn
                            * _silu(oz_ref[bb, :, ML_WIDTH_PAD + h * ML_HEAD_PAD:
                                           ML_WIDTH_PAD + (h + 1) * ML_HEAD_PAD]))
        keep_h = keep[bb][:, h:h + 1]
        c_scr[bb, h] = keep_h * c_scr[bb, h] + k_v[bb, h]
        n_scr[bb, h] = keep_h * n_h + jnp.sum(we[bb][:, h:h + 1] * kh, axis=0, keepdims=True)

    @pl.when(j == pl.num_programs(1) - 1)
    def _():
        c_out[...] = c_scr[:, :, 0:ML_HEAD_DIM, 0:ML_HEAD_DIM]
        n_out[...] = n_scr[:, :, :, 0:ML_HEAD_DIM]
        m_out[...] = m_scr[...]


def _mlstm(p3, conv0, c0_layers, n0, m0, lp, *, pad, bb_n):
    c0, c0_layer = c0_layers
    bsz, seq_rows, _ = p3.shape
    nblk = seq_rows // BLK
    blk = lambda w, off: pl.BlockSpec((bb_n, BLK, w), lambda b, j: (b, j, off // w))
    par = lambda r, w: pl.BlockSpec((r, w), lambda b, j: (0, 0))
    conv_spec = pl.BlockSpec((bb_n, ML_CONV - 1, W_QK), lambda b, j: (b, 0, 0))
    c_spec = pl.BlockSpec((bb_n, ML_HEADS, ML_HEAD_DIM, ML_HEAD_DIM), lambda b, j: (b, 0, 0, 0))
    n_spec = pl.BlockSpec((bb_n, ML_HEADS, 1, ML_HEAD_DIM), lambda b, j: (b, 0, 0, 0))
    m_spec = pl.BlockSpec((bb_n, 1, LANE), lambda b, j: (b, 0, 0))
    return pl.pallas_call(
        functools.partial(_mlstm_kernel, pad=pad, bb_n=bb_n),
        out_shape=(jax.ShapeDtypeStruct((bsz, seq_rows, ML_WIDTH_PAD), F32),
                   jax.ShapeDtypeStruct((bsz, ML_CONV - 1, W_QK), F32),
                   jax.ShapeDtypeStruct((bsz, ML_HEADS, ML_HEAD_DIM, ML_HEAD_DIM), F32),
                   jax.ShapeDtypeStruct((bsz, ML_HEADS, 1, ML_HEAD_DIM), F32),
                   jax.ShapeDtypeStruct((bsz, 1, LANE), F32)),
        grid=(bsz // bb_n, nblk),
        in_specs=[blk(W_QK, OFF_QK), blk(W_OZ, OFF_OZ), blk(W_MV, OFF_MV),
                  blk(W_MI, OFF_MI), blk(W_MF, OFF_MF),
                  conv_spec,
                  pl.BlockSpec((None, bb_n, ML_HEADS, ML_HEAD_DIM, ML_HEAD_DIM),
                               lambda b, j: (c0_layer, b, 0, 0, 0)),
                  n_spec, m_spec,
                  par(ML_CONV, W_QK), par(1, W_QK), par(1, LANE), par(1, LANE),
                  par(1, ML_WIDTH_PAD)],
        out_specs=(pl.BlockSpec((bb_n, BLK, ML_WIDTH_PAD), lambda b, j: (b, j, 0)),
                   conv_spec, c_spec, n_spec, m_spec),
        scratch_shapes=[pltpu.VMEM((bb_n, 8 + BLK, W_QK), F32),
                        pltpu.VMEM((bb_n, ML_HEADS, ML_HEAD_PAD, ML_HEAD_PAD), F32),
                        pltpu.VMEM((bb_n, ML_HEADS, 1, ML_HEAD_PAD), F32),
                        pltpu.VMEM((bb_n, 1, LANE), F32)],
        compiler_params=_cparams(("parallel", "arbitrary")),
        name="mlstm",
    )(p3, p3, p3, p3, p3, conv0, c0, n0, m0, lp["ml_cw"], lp["ml_cb"], lp["ml_bi"], lp["ml_bf"],
      lp["ml_ln_g"])


def _merge_kernel(x_ref, mg_ref, yrw_ref, ys5_ref, yml_ref, bmg_ref, wrw_ref, ws5_ref, wml_ref,
                  wout_ref, g_ref, b_ref, o_ref, *, tm, seq_rows, pad):
    gates = _sigmoid(mg_ref[...] + bmg_ref[...])
    merged = (gates[:, 0:D_MODEL] * _bdot(yrw_ref[...], wrw_ref[...])
              + gates[:, D_MODEL:2 * D_MODEL] * _bdot(ys5_ref[...], ws5_ref[...])
              + gates[:, 2 * D_MODEL:3 * D_MODEL] * _bdot(yml_ref[...], wml_ref[...]))
    out = _bdot(merged, wout_ref[...])
    y = _layer_norm_rows(DN_ALPHA * x_ref[...] + out, g_ref[...], b_ref[...])
    if pad:
        y = jnp.where(_pad_row_mask(pl.program_id(0), tm, seq_rows, pad), 0.0, y)
    o_ref[...] = y


def _merge(x, p, y_rw, y_s5, y_ml, lp, *, tm, seq_rows, pad):
    n = x.shape[0]
    rows = lambda w: pl.BlockSpec((tm, w), lambda i: (i, 0))
    full = lambda r, w: pl.BlockSpec((r, w), lambda i: (0, 0))
    return pl.pallas_call(
        functools.partial(_merge_kernel, tm=tm, seq_rows=seq_rows, pad=pad),
        out_shape=jax.ShapeDtypeStruct((n, D_MODEL), F32),
        grid=(n // tm,),
        in_specs=[rows(D_MODEL), pl.BlockSpec((tm, W_MG), lambda i: (i, OFF_MG // W_MG)),
                  rows(RW_WIDTH), rows(S5_WIDTH), rows(ML_WIDTH_PAD),
                  full(1, W_MG), full(RW_WIDTH, D_MODEL), full(S5_WIDTH, D_MODEL),
                  full(ML_WIDTH_PAD, D_MODEL), full(D_MODEL, D_MODEL),
                  full(1, D_MODEL), full(1, D_MODEL)],
        out_specs=rows(D_MODEL),
        compiler_params=_cparams(("parallel",)),
        name="merge",
    )(x, p, y_rw, y_s5, y_ml, lp["b_merge"], lp["w_br_rw"], lp["w_br_s5"], lp["w_br_ml"],
      lp["w_out"], lp["ln_g"], lp["ln_b"])


def _pad_heads(w):
    lead = w.shape[:-1]
    w = w.reshape(lead + (ML_HEADS, ML_HEAD_DIM))
    w = jnp.pad(w, [(0, 0)] * len(lead) + [(0, 0), (0, ML_HEAD_PAD - ML_HEAD_DIM)])
    return w.reshape(lead + (ML_WIDTH_PAD,))


def _unpad_heads(w):
    lead = w.shape[:-1]
    return w.reshape(lead + (ML_HEADS, ML_HEAD_PAD))[..., :ML_HEAD_DIM].reshape(lead + (ML_WIDTH,))


def _pad_lanes(w, width=LANE):
    return jnp.pad(w, [(0, 0)] * (w.ndim - 1) + [(0, width - w.shape[-1])])


def _qk_pad(w):
    return jnp.concatenate([_pad_heads(w[..., :ML_WIDTH]), _pad_heads(w[..., ML_WIDTH:])], axis=-1)


def _layer_params(l, w_in, rw_mu, rw_w0, rw_w2, rw_a0, rw_a2, rw_kk, rw_ka, rw_rk, rw_ln_g, rw_ln_b,
                  s5_a_re, s5_a_im, s5_b_re, s5_b_im, s5_c_re, s5_c_im, s5_d, s5_log_dt, s5_w_glu,
                  s5_b_glu, ml_conv_w, ml_conv_b, ml_b_if, ml_ln_g, b_merge, w_br_rw, w_br_s5,
                  w_br_ml, w_out, ln_g, ln_b):
    w = w_in[l]
    o = 0
    cols = {}
    for name, size in (("rwc", RW_SHIFT), ("rwg", RW_WIDTH), ("s5", 2 * S5_WIDTH),
                       ("qk", 2 * ML_WIDTH), ("mv", ML_WIDTH), ("mi", ML_HEADS), ("mf", ML_HEADS),
                       ("mo", ML_WIDTH), ("mz", ML_WIDTH), ("mg", 3 * D_MODEL)):
        cols[name] = w[:, o:o + size]
        o += size
    w_all = jnp.concatenate([
        cols["mg"], cols["s5"], _qk_pad(cols["qk"]),
        _pad_heads(cols["mo"]), _pad_heads(cols["mz"]), _pad_heads(cols["mv"]),
        cols["rwc"][:, :RW_RKV], cols["rwg"], cols["rwc"][:, RW_RKV:],
        _pad_lanes(cols["mi"]), _pad_lanes(cols["mf"]),
        jnp.zeros((D_MODEL, P_COLS - P_USED), F32)], axis=1).astype(BF16)
    row = lambda a: a.reshape(1, -1)
    zeros_lora = jnp.zeros((RW_LORA, RW_WIDTH), F32)
    lane = jnp.arange(LANE)
    ones_blk = (lane[:, None] // RW_HEAD_DIM == lane[None, :] // RW_HEAD_DIM).astype(F32)

    bb_re, bb_im, lv_re, lv_im, pw_re, pw_im = _s5_prep(
        s5_a_re[l], s5_a_im[l], s5_log_dt[l].reshape(S5_GROUPS, 1),
        jnp.swapaxes(s5_b_re[l], 1, 2), jnp.swapaxes(s5_b_im[l], 1, 2))
    eye8 = jnp.eye(8, dtype=F32)
    blocked = lambda a: a.reshape((S5_NBLK, 8) + a.shape[1:])
    bb = jnp.stack([blocked(bb_re), blocked(bb_im)])
    wb = jnp.einsum("cjghp,gk->jghckp", bb, eye8).reshape(S5_NBLK, LANE, 2 * S5_BLK_STATE)
    cc = jnp.stack([blocked(s5_c_re[l]), -blocked(s5_c_im[l])])
    wc = jnp.einsum("cjghp,gk->jcgpkh", cc, eye8).reshape(S5_NBLK, 2 * S5_BLK_STATE, LANE)
    state_rows = lambda re, im: jnp.stack(
        [re.reshape(re.shape[:-2] + (S5_NBLK, S5_BLK_STATE)),
         im.reshape(im.shape[:-2] + (S5_NBLK, S5_BLK_STATE))], axis=-2)
    lv = state_rows(lv_re, lv_im).reshape(SCAN_LEVELS, 2 * S5_NBLK, 1, S5_BLK_STATE)
    row_in_group = jnp.arange(SUBLANES)[None, None, :, None]
    lv = jnp.where(row_in_group >= (1 << jnp.arange(SCAN_LEVELS))[:, None, None, None], lv, 0.0)
    pw = jnp.moveaxis(state_rows(pw_re, pw_im).reshape(SUBLANES, 2 * S5_NBLK, S5_BLK_STATE), 0, 1)

    return dict(
        w_all=w_all,
        mu_rkv=row(rw_mu[l][:RW_RKV]), mu_wa=row(rw_mu[l][RW_RKV:]),
        rw_w0=row(rw_w0[l]), rw_a0=row(rw_a0[l]),
        rw_w2p=jnp.concatenate([rw_w2[l], zeros_lora], axis=0),
        rw_a2p=jnp.concatenate([zeros_lora, rw_a2[l]], axis=0),
        rw_kk=row(rw_kk[l]), rw_ka=row(rw_ka[l]), rw_rk=row(rw_rk[l]),
        rw_ln_g=row(rw_ln_g[l]), rw_ln_b=row(rw_ln_b[l]),
        ones2=jnp.concatenate([ones_blk, ones_blk], axis=0).astype(BF16),
        s5_wb=wb.astype(BF16), s5_wc=wc.astype(BF16), s5_lv=lv, s5_pw=pw,
        s5_d=row(s5_d[l]), s5_w_glu=s5_w_glu[l].astype(BF16), s5_b_glu=row(s5_b_glu[l]),
        ml_cw=_qk_pad(ml_conv_w[l]), ml_cb=row(_qk_pad(ml_conv_b[l])),
        ml_bi=row(_pad_lanes(ml_b_if[l][:ML_HEADS])), ml_bf=row(_pad_lanes(ml_b_if[l][ML_HEADS:])),
        ml_ln_g=row(_pad_heads(ml_ln_g[l])),
        b_merge=row(b_merge[l]), w_br_rw=w_br_rw[l].astype(BF16), w_br_s5=w_br_s5[l].astype(BF16),
        w_br_ml=jnp.pad(w_br_ml[l].reshape(ML_HEADS, ML_HEAD_DIM, D_MODEL),
                        ((0, 0), (0, ML_HEAD_PAD - ML_HEAD_DIM), (0, 0))
                        ).reshape(ML_WIDTH_PAD, D_MODEL).astype(BF16),
        w_out=w_out[l].astype(BF16), ln_g=row(ln_g[l]), ln_b=row(ln_b[l]))


def _pack_s5(re, im):
    b = re.shape[0]
    return jnp.stack([re.reshape(b, S5_NBLK, S5_BLK_STATE), im.reshape(b, S5_NBLK, S5_BLK_STATE)],
                     axis=2).reshape(b, 2 * S5_NBLK, S5_BLK_STATE)


def _unpack_s5(h):
    b = h.shape[0]
    h = h.reshape(b, S5_NBLK, 2, S5_BLK_STATE)
    return (h[:, :, 0].reshape(b, S5_GROUPS, S5_STATE), h[:, :, 1].reshape(b, S5_GROUPS, S5_STATE))


def _row_tile(n, seq_rows, target):
    best = 8
    for t in range(8, min(n, target) + 1, 8):
        if seq_rows % t == 0 or (t % seq_rows == 0 and n % t == 0):
            best = t
    return best


def _trunk_layer(x, st, lp, *, bsz, seq_rows, pad, bb_n, s5_tb, tm_proj, tm_merge):
    rw_shift0, rw_wkv0, s5_re0, s5_im0, ml_conv0, ml_c0, ml_n0, ml_m0 = st
    p = _proj(x, lp["w_all"], tm=tm_proj)
    p3 = p.reshape(bsz, seq_rows, P_COLS)
    n = bsz * seq_rows

    y_rw, wkv1, sh_rkv1, sh_wa1 = _rwkv(p3, rw_shift0[:, None, :RW_RKV], rw_shift0[:, None, RW_RKV:],
                                        rw_wkv0, lp, bb_n=4 if bsz % 4 == 0 else bb_n)
    rw_shift1 = jnp.concatenate([sh_rkv1[:, 0], sh_wa1[:, 0]], axis=-1)

    y_s5, h1 = _s5(p3, _pack_s5(s5_re0, s5_im0), lp, tb=s5_tb)
    s5_re1, s5_im1 = _unpack_s5(h1)

    y_ml, conv1, c1, n1, m1 = _mlstm(
        p3, _qk_pad(ml_conv0), ml_c0, ml_n0[:, :, None, :], _pad_lanes(ml_m0)[:, None, :], lp, pad=pad,
        bb_n=bb_n)
    ml_conv1 = jnp.concatenate([_unpad_heads(conv1[..., :ML_WIDTH_PAD]),
                                _unpad_heads(conv1[..., ML_WIDTH_PAD:])], axis=-1)

    x_new = _merge(x, p, y_rw.reshape(n, RW_WIDTH), y_s5.reshape(n, S5_WIDTH),
                   y_ml.reshape(n, ML_WIDTH_PAD), lp, tm=tm_merge, seq_rows=seq_rows, pad=pad)
    return x_new, (rw_shift1, wkv1, s5_re1, s5_im1, ml_conv1, c1, n1[:, :, 0, :],
                   m1[:, 0, :ML_HEADS])


def _run_group(x_rows, states, lps, *, bsz, seq_rows, pad, in_ln_g, in_ln_b):
    n = bsz * seq_rows
    bb_n = 2 if bsz % 2 == 0 else 1
    s5_tb = max(t for t in (BLK, 5 * BLK) if seq_rows % t == 0)
    tm_ln = _row_tile(n, seq_rows, 1024)
    tm_proj = _row_tile(n, seq_rows, 2080)
    tm_merge = _row_tile(n, seq_rows, 320)
    x = _ln_in(x_rows, in_ln_g, in_ln_b, tm=tm_ln, seq_rows=seq_rows, pad=pad)
    outs = []
    for l in range(DEPTH):
        x, st = _trunk_layer(x, states[l], lps[l], bsz=bsz, seq_rows=seq_rows, pad=pad, bb_n=bb_n,
                             s5_tb=s5_tb, tm_proj=tm_proj, tm_merge=tm_merge)
        outs.append(st)
    return x, outs


def kernel(x_prompt, x_sample, state_rwkv_shift, state_rwkv_wkv, state_s5_re, state_s5_im, state_mlstm_conv, state_mlstm_c, state_mlstm_n, state_mlstm_m, meta, in_ln_g, in_ln_b, w_in, rw_mu, rw_w0, rw_w2, rw_a0, rw_a2, rw_kk, rw_ka, rw_rk, rw_ln_g, rw_ln_b, s5_a_re, s5_a_im, s5_b_re, s5_b_im, s5_c_re, s5_c_im, s5_d, s5_log_dt, s5_w_glu, s5_b_glu, ml_conv_w, ml_conv_b, ml_b_if, ml_ln_g, b_merge, w_br_rw, w_br_s5, w_br_ml, w_out, ln_g, ln_b):
    lps = [_layer_params(l, w_in, rw_mu, rw_w0, rw_w2, rw_a0, rw_a2, rw_kk, rw_ka, rw_rk, rw_ln_g,
                         rw_ln_b, s5_a_re, s5_a_im, s5_b_re, s5_b_im, s5_c_re, s5_c_im, s5_d,
                         s5_log_dt, s5_w_glu, s5_b_glu, ml_conv_w, ml_conv_b, ml_b_if, ml_ln_g,
                         b_merge, w_br_rw, w_br_s5, w_br_ml, w_out, ln_g, ln_b)
           for l in range(DEPTH)]
    g_in, b_in = in_ln_g.reshape(1, D_MODEL), in_ln_b.reshape(1, D_MODEL)

    bp, sp = x_prompt.shape[0], x_prompt.shape[1]
    lp_rows = PAD + N_META + sp
    xp = jnp.concatenate([jnp.zeros((bp, PAD, D_MODEL), F32),
                          jnp.broadcast_to(meta[None], (bp, N_META, D_MODEL)), x_prompt], axis=1)
    z = lambda *shape: jnp.zeros((bp,) + shape, F32)
    zero_state = (z(RW_SHIFT), z(RW_HEADS, RW_HEAD_DIM, RW_HEAD_DIM), z(S5_GROUPS, S5_STATE),
                  z(S5_GROUPS, S5_STATE), z(ML_CONV - 1, 2 * ML_WIDTH),
                  (jnp.zeros((1, bp, ML_HEADS, ML_HEAD_DIM, ML_HEAD_DIM), F32), 0),
                  z(ML_HEADS, ML_HEAD_DIM), z(ML_HEADS))
    yp, p_states = _run_group(xp.reshape(bp * lp_rows, D_MODEL), [zero_state] * DEPTH, lps,
                              bsz=bp, seq_rows=lp_rows, pad=PAD, in_ln_g=g_in, in_ln_b=b_in)
    y_prompt = yp.reshape(bp, lp_rows, D_MODEL)[:, PAD + N_META:]

    bs, ds = x_sample.shape[0], x_sample.shape[1]
    s_in = [(state_rwkv_shift[l], state_rwkv_wkv[l], state_s5_re[l], state_s5_im[l],
             state_mlstm_conv[l], (state_mlstm_c, l), state_mlstm_n[l], state_mlstm_m[l])
            for l in range(DEPTH)]
    ys, s_states = _run_group(x_sample.reshape(bs * ds, D_MODEL), s_in, lps,
                              bsz=bs, seq_rows=ds, pad=0, in_ln_g=g_in, in_ln_b=b_in)
    y_sample = ys.reshape(bs, ds, D_MODEL)

    stack = lambda sts: tuple(jnp.stack(s, 0) for s in zip(*sts))
    return (y_prompt, y_sample) + stack(p_states) + stack(s_states)
```

```python
import functools
import math

import jax
import jax.numpy as jnp
from jax import lax
from jax.experimental import pallas as pl
from jax.experimental.pallas import tpu as pltpu

F32 = jnp.float32
BF16 = jnp.bfloat16

D_MODEL = 1024
DEPTH = 2
N_META = 16
RW_HEADS = 12
RW_HEAD_DIM = 64
RW_WIDTH = RW_HEADS * RW_HEAD_DIM
RW_PAIRS = RW_HEADS // 2
RW_LORA = 64
RW_RKV = 3 * RW_WIDTH
RW_SHIFT = RW_RKV + 2 * RW_LORA
S5_GROUPS = 32
S5_GROUP_CH = 16
S5_WIDTH = S5_GROUPS * S5_GROUP_CH
S5_STATE = 64
S5_NBLK = 4
S5_BLK_STATE = 512
ML_HEADS = 4
ML_HEAD_DIM = 192
ML_HEAD_PAD = 256
ML_WIDTH = ML_HEADS * ML_HEAD_DIM
ML_WIDTH_PAD = ML_HEADS * ML_HEAD_PAD
ML_CONV = 4
DN_ALPHA = (2 * DEPTH) ** 0.25
LN_EPS = 1e-5
RW_GN_EPS = 64e-5

LANE = 128
SUBLANES = 8
BLK = 64
PAD = BLK - N_META
SCAN_LEVELS = 3
SOLVE_LEVELS = 6
S5_BLK_GROUPS = LANE // S5_GROUP_CH

W_MG, W_S5, W_QK, W_OZ, W_MV = 3 * D_MODEL, 2 * S5_WIDTH, 2 * ML_WIDTH_PAD, 2 * ML_WIDTH_PAD, ML_WIDTH_PAD
W_RKV, W_RWG, W_WA, W_MI, W_MF = RW_RKV, RW_WIDTH, LANE, LANE, LANE
OFF_MG = 0
OFF_S5 = OFF_MG + W_MG
OFF_QK = OFF_S5 + W_S5
OFF_OZ = OFF_QK + W_QK
OFF_MV = OFF_OZ + W_OZ
OFF_RKV = OFF_MV + W_MV
OFF_RWG = OFF_RKV + W_RKV
OFF_WA = OFF_RWG + W_RWG
OFF_MI = OFF_WA + W_WA
OFF_MF = OFF_MI + W_MI
P_USED = OFF_MF + W_MF
MXU_COLS = 256
P_TN = 5 * MXU_COLS
P_COLS = -(-P_USED // P_TN) * P_TN
assert all(off % w == 0 for off, w in (
    (OFF_MG, W_MG), (OFF_S5, W_S5), (OFF_QK, W_QK), (OFF_OZ, W_OZ), (OFF_MV, W_MV),
    (OFF_RKV, W_RKV), (OFF_RWG, W_RWG), (OFF_WA, W_WA), (OFF_MI, W_MI), (OFF_MF, W_MF)))

VMEM_LIMIT = 56 * 1024 * 1024


def _cparams(sem):
    return pltpu.CompilerParams(dimension_semantics=sem, vmem_limit_bytes=VMEM_LIMIT)


def _bdot(a, b):
    return jnp.dot(a.astype(BF16), b.astype(BF16), preferred_element_type=F32)


def _hdot(a, b):
    return jnp.dot(a, b, precision=lax.Precision.HIGHEST, preferred_element_type=F32)


def _sigmoid(x):
    return 1.0 / (1.0 + jnp.exp(-x))


def _silu(x):
    return x * _sigmoid(x)


def _softplus(x):
    return jnp.maximum(x, 0.0) + jnp.log1p(jnp.exp(-jnp.abs(x)))


def _pad_row_mask(tile_idx, tm, seq_rows, pad):
    pos0 = lax.rem(tile_idx * tm, seq_rows)
    row = lax.broadcasted_iota(jnp.int32, (tm, 1), 0) + pos0
    return row < pad


def _layer_norm_rows(x, g, b):
    mu = jnp.mean(x, axis=-1, keepdims=True)
    d = x - mu
    var = jnp.mean(d * d, axis=-1, keepdims=True)
    return d * lax.rsqrt(var + LN_EPS) * g + b


def _ln_in_kernel(x_ref, g_ref, b_ref, o_ref, *, tm, seq_rows, pad):
    y = _layer_norm_rows(x_ref[...], g_ref[...], b_ref[...])
    if pad:
        y = jnp.where(_pad_row_mask(pl.program_id(0), tm, seq_rows, pad), 0.0, y)
    o_ref[...] = y


def _ln_in(x, g, b, *, tm, seq_rows, pad):
    n = x.shape[0]
    return pl.pallas_call(
        functools.partial(_ln_in_kernel, tm=tm, seq_rows=seq_rows, pad=pad),
        out_shape=jax.ShapeDtypeStruct((n, D_MODEL), F32),
        grid=(n // tm,),
        in_specs=[pl.BlockSpec((tm, D_MODEL), lambda i: (i, 0)),
                  pl.BlockSpec((1, D_MODEL), lambda i: (0, 0)),
                  pl.BlockSpec((1, D_MODEL), lambda i: (0, 0))],
        out_specs=pl.BlockSpec((tm, D_MODEL), lambda i: (i, 0)),
        compiler_params=_cparams(("parallel",)),
        name="ln_in",
    )(x, g, b)


def _proj_kernel(x_ref, w_ref, o_ref):
    o_ref[...] = jnp.dot(x_ref[...].astype(BF16), w_ref[...], preferred_element_type=F32)


def _proj(x, w_all, *, tm):
    n = x.shape[0]
    return pl.pallas_call(
        _proj_kernel,
        out_shape=jax.ShapeDtypeStruct((n, P_COLS), F32),
        grid=(n // tm, P_COLS // P_TN),
        in_specs=[pl.BlockSpec((tm, D_MODEL), lambda i, j: (i, 0)),
                  pl.BlockSpec((D_MODEL, P_TN), lambda i, j: (0, j))],
        out_specs=pl.BlockSpec((tm, P_TN), lambda i, j: (i, j)),
        compiler_params=_cparams(("parallel", "arbitrary")),
        name="proj",
    )(x, w_all)


def _split_lhs(hi_f32, lo_f32):
    return jnp.concatenate([hi_f32.astype(BF16), lo_f32.astype(BF16)], axis=1)


def _seg_lhs(x):
    hi = x.astype(BF16).astype(F32)
    return _split_lhs(hi, x - hi)


def _head_sum(x, ones2):
    parts = [jnp.dot(_seg_lhs(x[:, i * LANE:(i + 1) * LANE]), ones2, preferred_element_type=F32)
             for i in range(RW_WIDTH // LANE)]
    return jnp.concatenate(parts, axis=-1)


def _shift_rows(u, carry):
    rolled = pltpu.roll(u, 1, 0)
    row = lax.broadcasted_iota(jnp.int32, u.shape, 0)
    return jnp.where(row == 0, carry, rolled)


def _hi_lo(a):
    hi = a.astype(BF16)
    return hi, (a - hi.astype(F32)).astype(BF16)


def _lhs2(hl):
    return jnp.concatenate([hl[0], hl[1]], axis=1)


def _rhs2_rows(hi):
    return jnp.concatenate([hi, hi], axis=1)


def _rhs2(hi):
    return jnp.concatenate([hi, hi], axis=0)


def _dot_nt(lhs, rhs_rows):
    return lax.dot_general(lhs, rhs_rows, (((1,), (1,)), ((), ())), preferred_element_type=F32)


def _dot_nn(lhs, rhs):
    return jnp.dot(lhs, rhs, preferred_element_type=F32)


def _rw_prepare(u, uw, carry_rkv, carry_wa, mu_rkv, mu_wa, w0, w2p, a0, a2p, kk_gain, ka_gain,
                rk_gain, ones2):
    xs = u + (_shift_rows(u, carry_rkv) - u) * mu_rkv
    xwa = uw + (_shift_rows(uw, carry_wa) - uw) * mu_wa
    r = xs[:, 0:RW_WIDTH]
    k = xs[:, RW_WIDTH:2 * RW_WIDTH]
    v = xs[:, 2 * RW_WIDTH:3 * RW_WIDTH]
    log_decay = -math.exp(-0.5) * _sigmoid(w0 + _bdot(jnp.tanh(xwa), w2p))
    a = _sigmoid(a0 + _bdot(xwa, a2p))
    kk = k * kk_gain
    kk = kk * lax.rsqrt(_head_sum(kk * kk, ones2) + 1e-12)
    k = k * (1.0 + (a - 1.0) * ka_gain)
    kka = kk * a
    ti = lax.broadcasted_iota(jnp.int32, (BLK, BLK), 0)
    si = lax.broadcasted_iota(jnp.int32, (BLK, BLK), 1)
    cum = _hdot(jnp.where(si <= ti, 1.0, 0.0), log_decay)
    cum_end = cum[BLK - 1:BLK, :]
    inv_w = jnp.exp(-cum)
    tail_w = jnp.exp(cum_end - cum)
    return dict(at=-kk * jnp.exp(cum - log_decay), rt=r * jnp.exp(cum), bh=kka * inv_w, kh=k * inv_w,
                bc=kka * tail_w, kc=k * tail_w, v=v, wt=jnp.exp(cum_end),
                bonus=_head_sum(r * k * rk_gain, ones2) * v)


def _rw_kernel(rkv_ref, wa_ref, gate_ref, sh0_rkv_ref, sh0_wa_ref, s0_ref,
               mu_rkv_ref, mu_wa_ref, w0_ref, w2_ref, a0_ref, a2_ref, kk_ref, ka_ref, rk_ref,
               lng_ref, lnb_ref, ones_ref,
               y_ref, s_out, sh_rkv_out, sh_wa_out, s_scr, c_rkv, c_wa, *, bb_n):
    j = pl.program_id(1)

    @pl.when(j == 0)
    def _():
        for bb in range(bb_n):
            for p in range(RW_PAIRS):
                s_scr[bb, p] = jnp.concatenate([s0_ref[bb, 2 * p], s0_ref[bb, 2 * p + 1]], axis=1)
        c_rkv[...] = sh0_rkv_ref[...]
        c_wa[...] = sh0_wa_ref[...]

    ones2 = ones_ref[...]
    rows = []
    for bb in range(bb_n):
        u, uw = rkv_ref[bb], wa_ref[bb]
        rows.append(_rw_prepare(u, uw, c_rkv[bb], c_wa[bb], mu_rkv_ref[...], mu_wa_ref[...],
                                w0_ref[...], w2_ref[...], a0_ref[...], a2_ref[...], kk_ref[...],
                                ka_ref[...], rk_ref[...], ones2))
        for carry, out, src in ((c_rkv, sh_rkv_out, u), (c_wa, sh_wa_out, uw)):
            carry[bb] = src[BLK - 1:BLK, :]
            out[bb] = src[BLK - 1:BLK, :]

    lo_lane = lax.broadcasted_iota(jnp.int32, (1, LANE), 1) < RW_HEAD_DIM
    row_j = lax.broadcasted_iota(jnp.int32, (RW_HEAD_DIM, LANE), 0)
    col_t = lax.broadcasted_iota(jnp.int32, (RW_HEAD_DIM, LANE), 1) & (RW_HEAD_DIM - 1)
    strict = row_j < col_t
    incl = row_j <= col_t

    def by_head(x):
        return jnp.concatenate([jnp.where(lo_lane, x, 0.0), jnp.where(lo_lane, 0.0, x)], axis=0)

    pick = jnp.where(row_j == col_t, 1.0, 0.0).astype(BF16)
    pick2 = jnp.concatenate([pick, pick], axis=1)

    bd = lambda hl: (by_head(hl[0]), by_head(hl[1]))
    stack_rows = lambda parts: tuple(jnp.concatenate(x, axis=0) for x in zip(*parts))
    bf = lambda x: x.astype(BF16)

    def pair_transpose(x):
        return _dot_nt(pick2, jnp.concatenate(bd(_hi_lo(x)), axis=1))

    chains = [(bb, p) for bb in range(bb_n) for p in range(RW_PAIRS)]
    n_ch = len(chains)
    tile = lambda name, c: rows[chains[c][0]][name][:, chains[c][1] * LANE:(chains[c][1] + 1) * LANE]
    tile_hl = lambda name, c: _hi_lo(tile(name, c))

    ar_rows, gb, gk_hk_kc, hb_bc = [], [], [], []
    for c in range(n_ch):
        ar_rows.append(_rhs2_rows(jnp.concatenate(
            [by_head(bf(tile("at", c))), by_head(bf(tile("rt", c)))], axis=0)))
        bk = stack_rows([tile_hl("bh", c), tile_hl("kh", c)])
        gram = _dot_nt(_lhs2(bk), ar_rows[c])
        gb.append(_hi_lo(jnp.where(strict, gram[0:RW_HEAD_DIM, 0:LANE], 0.0)))
        hb = jnp.where(incl, gram[0:RW_HEAD_DIM, LANE:2 * LANE], 0.0)
        gk = jnp.where(strict, gram[RW_HEAD_DIM:LANE, 0:LANE], 0.0)
        hk = jnp.where(incl, gram[RW_HEAD_DIM:LANE, LANE:2 * LANE], 0.0)
        gk_hk_kc.append(_rhs2(jnp.concatenate(
            [by_head(bf(gk)), by_head(bf(hk)), by_head(bf(tile("kc", c)))], axis=1)))
        hb_bc.append(_rhs2(jnp.concatenate([by_head(bf(hb)), by_head(bf(tile("bc", c)))], axis=1)))
    vt = [pair_transpose(tile("v", c)) for c in range(n_ch)]
    v_terms = [_dot_nn(_lhs2(_hi_lo(vt[c])), gk_hk_kc[c]) for c in range(n_ch)]
    powers = [[_rhs2(by_head(g[0])) for g in gb]]
    cur = gb
    for _ in range(SOLVE_LEVELS - 1):
        cur = [_hi_lo(_dot_nn(_lhs2(cur[c]), powers[-1][c])) for c in range(n_ch)]
        powers.append([_rhs2(by_head(g[0])) for g in cur])

    s_old = [s_scr[bb, p] for bb, p in chains]
    uy = [_dot_nt(_lhs2(_hi_lo(s_old[c])), ar_rows[c]) for c in range(n_ch)]
    x = [uy[c][:, 0:LANE] + v_terms[c][:, 0:LANE] for c in range(n_ch)]
    for lvl in range(SOLVE_LEVELS):
        x = [x[c] + _dot_nn(_lhs2(_hi_lo(x[c])), powers[lvl][c]) for c in range(n_ch)]
    out2 = [_dot_nn(_lhs2(_hi_lo(x[c])), hb_bc[c]) for c in range(n_ch)]
    y_tiles = []
    for c, (bb, p) in enumerate(chains):
        y_col = uy[c][:, LANE:2 * LANE] + out2[c][:, 0:LANE] + v_terms[c][:, LANE:2 * LANE]
        y_tiles.append(pair_transpose(y_col))
        s_scr[bb, p] = (s_old[c] * tile("wt", c) + out2[c][:, LANE:2 * LANE]
                        + v_terms[c][:, 2 * LANE:3 * LANE])

    inv = 1.0 / RW_HEAD_DIM
    for bb in range(bb_n):
        y = jnp.concatenate(y_tiles[bb * RW_PAIRS:(bb + 1) * RW_PAIRS], axis=1)
        mu = _head_sum(y, ones2) * inv
        d = y - mu
        var = _head_sum(d * d, ones2) * inv
        yn = d * lax.rsqrt(var + RW_GN_EPS) * lng_ref[...] + lnb_ref[...]
        y_ref[bb] = (yn + rows[bb]["bonus"]) * _silu(gate_ref[bb])

    @pl.when(j == pl.num_programs(1) - 1)
    def _():
        for bb in range(bb_n):
            for p in range(RW_PAIRS):
                s_out[bb, 2 * p] = s_scr[bb, p, :, 0:RW_HEAD_DIM]
                s_out[bb, 2 * p + 1] = s_scr[bb, p, :, RW_HEAD_DIM:LANE]


def _rwkv(p3, sh0_rkv, sh0_wa, s0, lp, *, bb_n):
    bsz, seq_rows, _ = p3.shape
    nblk = seq_rows // BLK
    blk = lambda w, off: pl.BlockSpec((bb_n, BLK, w), lambda b, j: (b, j, off // w))
    par = lambda w: pl.BlockSpec((1, w), lambda b, j: (0, 0))
    mat = pl.BlockSpec((LANE, RW_WIDTH), lambda b, j: (0, 0))
    st = lambda w: pl.BlockSpec((bb_n, 1, w), lambda b, j: (b, 0, 0))
    s_spec = pl.BlockSpec((bb_n, RW_HEADS, RW_HEAD_DIM, RW_HEAD_DIM), lambda b, j: (b, 0, 0, 0))
    return pl.pallas_call(
        functools.partial(_rw_kernel, bb_n=bb_n),
        out_shape=(jax.ShapeDtypeStruct((bsz, seq_rows, RW_WIDTH), F32),
                   jax.ShapeDtypeStruct((bsz, RW_HEADS, RW_HEAD_DIM, RW_HEAD_DIM), F32),
                   jax.ShapeDtypeStruct((bsz, 1, RW_RKV), F32),
                   jax.ShapeDtypeStruct((bsz, 1, LANE), F32)),
        grid=(bsz // bb_n, nblk),
        in_specs=[blk(W_RKV, OFF_RKV), blk(W_WA, OFF_WA), blk(W_RWG, OFF_RWG),
                  st(RW_RKV), st(LANE), s_spec,
                  par(RW_RKV), par(LANE), par(RW_WIDTH), mat, par(RW_WIDTH), mat,
                  par(RW_WIDTH), par(RW_WIDTH), par(RW_WIDTH), par(RW_WIDTH), par(RW_WIDTH),
                  pl.BlockSpec((2 * LANE, LANE), lambda b, j: (0, 0))],
        out_specs=(pl.BlockSpec((bb_n, BLK, RW_WIDTH), lambda b, j: (b, j, 0)), s_spec,
                   st(RW_RKV), st(LANE)),
        scratch_shapes=[pltpu.VMEM((bb_n, RW_PAIRS, RW_HEAD_DIM, LANE), F32),
                        pltpu.VMEM((bb_n, 1, RW_RKV), F32), pltpu.VMEM((bb_n, 1, LANE), F32)],
        compiler_params=_cparams(("parallel", "arbitrary")),
        name="rwkv",
    )(p3, p3, p3, sh0_rkv, sh0_wa, s0, lp["mu_rkv"], lp["mu_wa"], lp["rw_w0"], lp["rw_w2p"],
      lp["rw_a0"], lp["rw_a2p"], lp["rw_kk"], lp["rw_ka"], lp["rw_rk"], lp["rw_ln_g"],
      lp["rw_ln_b"], lp["ones2"])


def _cmul(ar, ai, br, bi):
    return ar * br - ai * bi, ar * bi + ai * br


def _s5_prep_kernel(are_ref, aim_ref, ldt_ref, bre_ref, bim_ref,
                    bbre_out, bbim_out, lvre_out, lvim_out, pwre_out, pwim_out):
    ar, ai = are_ref[...], aim_ref[...]
    dt = jnp.exp(ldt_ref[...])
    mag = jnp.exp(ar * dt)
    lr, li = mag * jnp.cos(ai * dt), mag * jnp.sin(ai * dt)
    nr, ni = lr - 1.0, li
    den = ar * ar + ai * ai
    qr, qi = (nr * ar + ni * ai) / den, (ni * ar - nr * ai) / den
    br, bi = bre_ref[...], bim_ref[...]
    bbr, bbi = _cmul(qr[:, None, :], qi[:, None, :], br, bi)
    bbre_out[...] = bbr
    bbim_out[...] = bbi
    sq_r, sq_i = lr, li
    pows = [(lr, li)]
    for lvl in range(SCAN_LEVELS):
        lvre_out[lvl] = sq_r
        lvim_out[lvl] = sq_i
        pows = pows + [_cmul(pr, pi, sq_r, sq_i) for pr, pi in pows]
        sq_r, sq_i = _cmul(sq_r, sq_i, sq_r, sq_i)
    for t in range(SUBLANES):
        pwre_out[t] = pows[t][0]
        pwim_out[t] = pows[t][1]


def _s5_prep(a_re, a_im, log_dt, b_re_t, b_im_t):
    ghp = jax.ShapeDtypeStruct((S5_GROUPS, S5_GROUP_CH, S5_STATE), F32)
    lv = jax.ShapeDtypeStruct((SCAN_LEVELS, S5_GROUPS, S5_STATE), F32)
    pw = jax.ShapeDtypeStruct((SUBLANES, S5_GROUPS, S5_STATE), F32)
    return pl.pallas_call(
        _s5_prep_kernel,
        out_shape=(ghp, ghp, lv, lv, pw, pw),
        name="s5_prep",
    )(a_re, a_im, log_dt, b_re_t, b_im_t)


def _gelu_tanh(x):
    return 0.5 * x * (1.0 + jnp.tanh(math.sqrt(2.0 / math.pi) * (x + 0.044715 * (x * x * x))))


def _s5_kernel(p_ref, h0_ref, wb_ref, wc_ref, lv_ref, pw_ref, d_ref, wglu_ref, bglu_ref,
               y_ref, h_out, h_scr, hs_scr, *, tb):
    j = pl.program_id(1)

    @pl.when(j == 0)
    def _():
        h_scr[...] = h0_ref[...]

    u = p_ref[:, 0:S5_WIDTH]
    gate = p_ref[:, S5_WIDTH:2 * S5_WIDTH]
    y_parts = []
    for jb in range(S5_NBLK):
        bu = _bdot(u[:, jb * LANE:(jb + 1) * LANE], wb_ref[jb])
        for s in range(tb // BLK):
            xr = bu[s * BLK:(s + 1) * BLK, 0:S5_BLK_STATE]
            xi = bu[s * BLK:(s + 1) * BLK, S5_BLK_STATE:2 * S5_BLK_STATE]
            xr = xr.reshape(BLK // SUBLANES, SUBLANES, S5_BLK_STATE)
            xi = xi.reshape(BLK // SUBLANES, SUBLANES, S5_BLK_STATE)
            for lvl in range(SCAN_LEVELS):
                sr = pltpu.roll(xr, 1 << lvl, 1)
                si = pltpu.roll(xi, 1 << lvl, 1)
                lr, li = lv_ref[lvl, 2 * jb], lv_ref[lvl, 2 * jb + 1]
                xr, xi = xr + (lr * sr - li * si), xi + (lr * si + li * sr)
            xr = xr.reshape(BLK, S5_BLK_STATE)
            xi = xi.reshape(BLK, S5_BLK_STATE)
            c_r = h_scr[2 * jb:2 * jb + 1, :]
            c_i = h_scr[2 * jb + 1:2 * jb + 2, :]
            pr, pi = pw_ref[2 * jb], pw_ref[2 * jb + 1]
            for grp in range(BLK // SUBLANES):
                rows = slice(grp * SUBLANES, (grp + 1) * SUBLANES)
                hr = xr[rows, :] + (pr * c_r - pi * c_i)
                hi = xi[rows, :] + (pr * c_i + pi * c_r)
                c_r, c_i = hr[SUBLANES - 1:SUBLANES, :], hi[SUBLANES - 1:SUBLANES, :]
                out_rows = slice(s * BLK + grp * SUBLANES, s * BLK + (grp + 1) * SUBLANES)
                hs_scr[out_rows, 0:S5_BLK_STATE] = hr
                hs_scr[out_rows, S5_BLK_STATE:2 * S5_BLK_STATE] = hi
            h_scr[2 * jb:2 * jb + 1, :] = c_r
            h_scr[2 * jb + 1:2 * jb + 2, :] = c_i
        y_parts.append(_bdot(hs_scr[...], wc_ref[jb]))
    y = jnp.concatenate(y_parts, axis=-1) + d_ref[...] * u
    y = _gelu_tanh(y)
    y = y * _sigmoid(_bdot(y, wglu_ref[...]) + bglu_ref[...])
    y_ref[...] = y * _silu(gate)

    @pl.when(j == pl.num_programs(1) - 1)
    def _():
        h_out[...] = h_scr[...]


def _s5(p3, h0, lp, *, tb):
    bsz, seq_rows, _ = p3.shape
    st_spec = pl.BlockSpec((None, 2 * S5_NBLK, S5_BLK_STATE), lambda b, j: (b, 0, 0))
    full = lambda shape: pl.BlockSpec(shape, lambda b, j: (0,) * len(shape))
    return pl.pallas_call(
        functools.partial(_s5_kernel, tb=tb),
        out_shape=(jax.ShapeDtypeStruct((bsz, seq_rows, S5_WIDTH), F32),
                   jax.ShapeDtypeStruct((bsz, 2 * S5_NBLK, S5_BLK_STATE), F32)),
        grid=(bsz, seq_rows // tb),
        in_specs=[pl.BlockSpec((None, tb, W_S5), lambda b, j: (b, j, OFF_S5 // W_S5)),
                  st_spec,
                  full((S5_NBLK, LANE, 2 * S5_BLK_STATE)),
                  full((S5_NBLK, 2 * S5_BLK_STATE, LANE)),
                  full((SCAN_LEVELS, 2 * S5_NBLK, SUBLANES, S5_BLK_STATE)),
                  full((2 * S5_NBLK, SUBLANES, S5_BLK_STATE)),
                  full((1, S5_WIDTH)), full((S5_WIDTH, S5_WIDTH)), full((1, S5_WIDTH))],
        out_specs=(pl.BlockSpec((None, tb, S5_WIDTH), lambda b, j: (b, j, 0)), st_spec),
        scratch_shapes=[pltpu.VMEM((2 * S5_NBLK, S5_BLK_STATE), F32),
                        pltpu.VMEM((tb, 2 * S5_BLK_STATE), F32)],
        compiler_params=_cparams(("parallel", "arbitrary")),
        name="s5",
    )(p3, h0, lp["s5_wb"], lp["s5_wc"], lp["s5_lv"], lp["s5_pw"], lp["s5_d"],
      lp["s5_w_glu"], lp["s5_b_glu"])


def _mlstm_kernel(qk_ref, oz_ref, v_ref, gi_ref, gf_ref, conv0_ref, c0_ref, n0_ref, m0_ref,
                  cw_ref, cb_ref, bi_ref, bf_ref, lng_ref,
                  y_ref, conv_out, c_out, n_out, m_out,
                  xp_scr, c_scr, n_scr, m_scr, *, pad, bb_n):
    j = pl.program_id(1)
    halo = SUBLANES

    @pl.when(j == 0)
    def _():
        xp_scr[:, 0:halo, :] = jnp.zeros((bb_n, halo, W_QK), F32)
        xp_scr[:, halo - (ML_CONV - 1):halo, :] = conv0_ref[...]
        c_scr[...] = jnp.zeros(c_scr.shape, F32)
        c_scr[:, :, 0:ML_HEAD_DIM, 0:ML_HEAD_DIM] = c0_ref[...]
        n_scr[...] = jnp.zeros(n_scr.shape, F32)
        n_scr[:, :, :, 0:ML_HEAD_DIM] = n0_ref[...]
        m_scr[...] = m0_ref[...]

    t0 = jnp.where(j == 0, pad, 0) if pad else 0
    row1 = lax.broadcasted_iota(jnp.int32, (BLK, 1), 0)
    row_ok = row1 >= t0
    ti = lax.broadcasted_iota(jnp.int32, (BLK, BLK), 0)
    si = lax.broadcasted_iota(jnp.int32, (BLK, BLK), 1)
    pair_ok = (si <= ti) & (si >= t0)
    tril = jnp.where(si <= ti, 1.0, 0.0)
    eye = jnp.where(si == ti, 1.0, 0.0)
    ones_sq = jnp.ones((BLK, BLK), F32)
    lane_ok = lax.broadcasted_iota(jnp.int32, (1, ML_HEAD_PAD), 1) < ML_HEAD_DIM
    head = lambda a, h: a[:, h * ML_HEAD_PAD:(h + 1) * ML_HEAD_PAD]
    chains = [(bb, h) for bb in range(bb_n) for h in range(ML_HEADS)]

    log_i, log_f, b = [], [], []
    for bb in range(bb_n):
        log_i.append(gi_ref[bb] + bi_ref[...])
        log_f.append(jnp.where(row_ok, -_softplus(-(gf_ref[bb] + bf_ref[...])), 0.0))
        b.append(_hdot(tril, log_f[bb]))
    d_row = {}
    for bb, h in chains:
        x_col = log_i[bb][:, h:h + 1] - b[bb][:, h:h + 1]
        d_row[bb, h] = _hdot(ones_sq, eye * x_col)

    q, k, v = [], [], []
    for bb in range(bb_n):
        xp_scr[bb, halo:halo + BLK, :] = qk_ref[bb]
        conv = cb_ref[...] + xp_scr[bb, halo - 3:halo - 3 + BLK, :] * cw_ref[0:1, :]
        for tap in range(1, ML_CONV):
            conv = conv + xp_scr[bb, halo - 3 + tap:halo - 3 + tap + BLK, :] * cw_ref[tap:tap + 1, :]
        tail = xp_scr[bb, halo + BLK - (ML_CONV - 1):halo + BLK, :]
        xp_scr[bb, halo - (ML_CONV - 1):halo, :] = tail
        conv_out[bb] = tail
        act = _silu(conv)
        q.append(act[:, 0:ML_WIDTH_PAD])
        k.append(act[:, ML_WIDTH_PAD:2 * ML_WIDTH_PAD] * (1.0 / math.sqrt(ML_HEAD_DIM)))
        v.append(v_ref[bb])

    qk, q_c = {}, {}
    for bb, h in chains:
        qh = head(q[bb], h).astype(BF16)
        qk[bb, h] = lax.dot_general(qh, head(k[bb], h).astype(BF16), (((1,), (1,)), ((), ())),
                                    preferred_element_type=F32)
        q_c[bb, h] = jnp.dot(qh, c_scr[bb, h].astype(BF16), preferred_element_type=F32)

    g, we, keep = [], [], []
    for bb in range(bb_n):
        m_prev = m_scr[bb]
        g.append(b[bb] + m_prev)
        b_end = b[bb][BLK - 1:BLK, :]
        e_log = jnp.where(row_ok, b_end - b[bb] + log_i[bb], -jnp.inf)
        m_new = jnp.maximum(b_end + m_prev, jnp.max(e_log, axis=0, keepdims=True))
        we.append(jnp.exp(e_log - m_new))
        keep.append(jnp.exp(b_end + m_prev - m_new))
        m_scr[bb] = m_new

    s_mat, m_row, w_inter = {}, {}, {}
    for bb, h in chains:
        d = b[bb][:, h:h + 1] + d_row[bb, h]
        d = jnp.where(pair_ok, d, -jnp.inf)
        g_col = g[bb][:, h:h + 1]
        m_row[bb, h] = jnp.maximum(g_col, jnp.max(d, axis=1, keepdims=True))
        s_mat[bb, h] = qk[bb, h] * jnp.exp(d - m_row[bb, h])
        w_inter[bb, h] = jnp.exp(g_col - m_row[bb, h])

    s_v, k_v = {}, {}
    for bb, h in chains:
        vh = head(v[bb], h)
        s_v[bb, h] = _bdot(s_mat[bb, h], vh)
        k_v[bb, h] = lax.dot_general(head(k[bb], h).astype(BF16),
                                     (we[bb][:, h:h + 1] * vh).astype(BF16),
                                     (((0,), (0,)), ((), ())), preferred_element_type=F32)

    for bb, h in chains:
        hs = slice(h * ML_HEAD_PAD, (h + 1) * ML_HEAD_PAD)
        kh = head(k[bb], h)
        n_h = n_scr[bb, h]
        num = s_v[bb, h] + w_inter[bb, h] * q_c[bb, h]
        qn = jnp.sum(head(q[bb], h) * n_h, axis=1, keepdims=True)
        den = jnp.sum(s_mat[bb, h], axis=1, keepdims=True) + w_inter[bb, h] * qn
        hh = num / jnp.maximum(jnp.abs(den), jnp.exp(-m_row[bb, h]))
        mu = jnp.sum(hh, axis=1, keepdims=True) * (1.0 / ML_HEAD_DIM)
        dv = jnp.where(lane_ok, hh - mu, 0.0)
        var = jnp.sum(dv * dv, axis=1, keepdims=True) * (1.0 / ML_HEAD_DIM)
        hn = dv * lax.rsqrt(var + LN_EPS) * lng_ref[:, hs]
        y_ref[bb, :, hs] = (_sigmoid(oz_ref[bb, :, hs]) * hn
                            * _silu(oz_ref[bb, :, ML_WIDTH_PAD + h * ML_HEAD_PAD:
                                           ML_WIDTH_PAD + (h + 1) * ML_HEAD_PAD]))
        keep_h = keep[bb][:, h:h + 1]
        c_scr[bb, h] = keep_h * c_scr[bb, h] + k_v[bb, h]
        n_scr[bb, h] = keep_h * n_h + jnp.sum(we[bb][:, h:h + 1] * kh, axis=0, keepdims=True)

    @pl.when(j == pl.num_programs(1) - 1)
    def _():
        c_out[...] = c_scr[:, :, 0:ML_HEAD_DIM, 0:ML_HEAD_DIM]
        n_out[...] = n_scr[:, :, :, 0:ML_HEAD_DIM]
        m_out[...] = m_scr[...]


def _mlstm(p3, conv0, c0_layers, n0, m0, lp, *, pad, bb_n):
    c0, c0_layer = c0_layers
    bsz, seq_rows, _ = p3.shape
    nblk = seq_rows // BLK
    blk = lambda w, off: pl.BlockSpec((bb_n, BLK, w), lambda b, j: (b, j, off // w))
    par = lambda r, w: pl.BlockSpec((r, w), lambda b, j: (0, 0))
    conv_spec = pl.BlockSpec((bb_n, ML_CONV - 1, W_QK), lambda b, j: (b, 0, 0))
    c_spec = pl.BlockSpec((bb_n, ML_HEADS, ML_HEAD_DIM, ML_HEAD_DIM), lambda b, j: (b, 0, 0, 0))
    n_spec = pl.BlockSpec((bb_n, ML_HEADS, 1, ML_HEAD_DIM), lambda b, j: (b, 0, 0, 0))
    m_spec = pl.BlockSpec((bb_n, 1, LANE), lambda b, j: (b, 0, 0))
    return pl.pallas_call(
        functools.partial(_mlstm_kernel, pad=pad, bb_n=bb_n),
        out_shape=(jax.ShapeDtypeStruct((bsz, seq_rows, ML_WIDTH_PAD), F32),
                   jax.ShapeDtypeStruct((bsz, ML_CONV - 1, W_QK), F32),
                   jax.ShapeDtypeStruct((bsz, ML_HEADS, ML_HEAD_DIM, ML_HEAD_DIM), F32),
                   jax.ShapeDtypeStruct((bsz, ML_HEADS, 1, ML_HEAD_DIM), F32),
                   jax.ShapeDtypeStruct((bsz, 1, LANE), F32)),
        grid=(bsz // bb_n, nblk),
        in_specs=[blk(W_QK, OFF_QK), blk(W_OZ, OFF_OZ), blk(W_MV, OFF_MV),
                  blk(W_MI, OFF_MI), blk(W_MF, OFF_MF),
                  conv_spec,
                  pl.BlockSpec((None, bb_n, ML_HEADS, ML_HEAD_DIM, ML_HEAD_DIM),
                               lambda b, j: (c0_layer, b, 0, 0, 0)),
                  n_spec, m_spec,
                  par(ML_CONV, W_QK), par(1, W_QK), par(1, LANE), par(1, LANE),
                  par(1, ML_WIDTH_PAD)],
        out_specs=(pl.BlockSpec((bb_n, BLK, ML_WIDTH_PAD), lambda b, j: (b, j, 0)),
                   conv_spec, c_spec, n_spec, m_spec),
        scratch_shapes=[pltpu.VMEM((bb_n, SUBLANES + BLK, W_QK), F32),
                        pltpu.VMEM((bb_n, ML_HEADS, ML_HEAD_PAD, ML_HEAD_PAD), F32),
                        pltpu.VMEM((bb_n, ML_HEADS, 1, ML_HEAD_PAD), F32),
                        pltpu.VMEM((bb_n, 1, LANE), F32)],
        compiler_params=_cparams(("parallel", "arbitrary")),
        name="mlstm",
    )(p3, p3, p3, p3, p3, conv0, c0, n0, m0, lp["ml_cw"], lp["ml_cb"], lp["ml_bi"], lp["ml_bf"],
      lp["ml_ln_g"])


def _merge_kernel(x_ref, mg_ref, yrw_ref, ys5_ref, yml_ref, bmg_ref, wrw_ref, ws5_ref, wml_ref,
                  wout_ref, g_ref, b_ref, o_ref, *, tm, seq_rows, pad):
    gates = _sigmoid(mg_ref[...] + bmg_ref[...])
    merged = (gates[:, 0:D_MODEL] * _bdot(yrw_ref[...], wrw_ref[...])
              + gates[:, D_MODEL:2 * D_MODEL] * _bdot(ys5_ref[...], ws5_ref[...])
              + gates[:, 2 * D_MODEL:3 * D_MODEL] * _bdot(yml_ref[...], wml_ref[...]))
    out = _bdot(merged, wout_ref[...])
    y = _layer_norm_rows(DN_ALPHA * x_ref[...] + out, g_ref[...], b_ref[...])
    if pad:
        y = jnp.where(_pad_row_mask(pl.program_id(0), tm, seq_rows, pad), 0.0, y)
    o_ref[...] = y


def _merge(x, p, y_rw, y_s5, y_ml, lp, *, tm, seq_rows, pad):
    n = x.shape[0]
    rows = lambda w: pl.BlockSpec((tm, w), lambda i: (i, 0))
    full = lambda r, w: pl.BlockSpec((r, w), lambda i: (0, 0))
    return pl.pallas_call(
        functools.partial(_merge_kernel, tm=tm, seq_rows=seq_rows, pad=pad),
        out_shape=jax.ShapeDtypeStruct((n, D_MODEL), F32),
        grid=(n // tm,),
        in_specs=[rows(D_MODEL), pl.BlockSpec((tm, W_MG), lambda i: (i, OFF_MG // W_MG)),
                  rows(RW_WIDTH), rows(S5_WIDTH), rows(ML_WIDTH_PAD),
                  full(1, W_MG), full(RW_WIDTH, D_MODEL), full(S5_WIDTH, D_MODEL),
                  full(ML_WIDTH_PAD, D_MODEL), full(D_MODEL, D_MODEL),
                  full(1, D_MODEL), full(1, D_MODEL)],
        out_specs=rows(D_MODEL),
        compiler_params=_cparams(("parallel",)),
        name="merge",
    )(x, p, y_rw, y_s5, y_ml, lp["b_merge"], lp["w_br_rw"], lp["w_br_s5"], lp["w_br_ml"],
      lp["w_out"], lp["ln_g"], lp["ln_b"])


def _pad_heads(w):
    lead = w.shape[:-1]
    w = w.reshape(lead + (ML_HEADS, ML_HEAD_DIM))
    w = jnp.pad(w, [(0, 0)] * len(lead) + [(0, 0), (0, ML_HEAD_PAD - ML_HEAD_DIM)])
    return w.reshape(lead + (ML_WIDTH_PAD,))


def _unpad_heads(w):
    lead = w.shape[:-1]
    return w.reshape(lead + (ML_HEADS, ML_HEAD_PAD))[..., :ML_HEAD_DIM].reshape(lead + (ML_WIDTH,))


def _pad_lanes(w, width=LANE):
    return jnp.pad(w, [(0, 0)] * (w.ndim - 1) + [(0, width - w.shape[-1])])


def _qk_pad(w):
    return jnp.concatenate([_pad_heads(w[..., :ML_WIDTH]), _pad_heads(w[..., ML_WIDTH:])], axis=-1)


def _layer_params(l, w_in, rw_mu, rw_w0, rw_w2, rw_a0, rw_a2, rw_kk, rw_ka, rw_rk, rw_ln_g, rw_ln_b,
                  s5_a_re, s5_a_im, s5_b_re, s5_b_im, s5_c_re, s5_c_im, s5_d, s5_log_dt, s5_w_glu,
                  s5_b_glu, ml_conv_w, ml_conv_b, ml_b_if, ml_ln_g, b_merge, w_br_rw, w_br_s5,
                  w_br_ml, w_out, ln_g, ln_b):
    w = w_in[l]
    o = 0
    cols = {}
    for name, size in (("rwc", RW_SHIFT), ("rwg", RW_WIDTH), ("s5", 2 * S5_WIDTH),
                       ("qk", 2 * ML_WIDTH), ("mv", ML_WIDTH), ("mi", ML_HEADS), ("mf", ML_HEADS),
                       ("mo", ML_WIDTH), ("mz", ML_WIDTH), ("mg", 3 * D_MODEL)):
        cols[name] = w[:, o:o + size]
        o += size
    w_all = jnp.concatenate([
        cols["mg"], cols["s5"], _qk_pad(cols["qk"]),
        _pad_heads(cols["mo"]), _pad_heads(cols["mz"]), _pad_heads(cols["mv"]),
        cols["rwc"][:, :RW_RKV], cols["rwg"], cols["rwc"][:, RW_RKV:],
        _pad_lanes(cols["mi"]), _pad_lanes(cols["mf"]),
        jnp.zeros((D_MODEL, P_COLS - P_USED), F32)], axis=1).astype(BF16)
    row = lambda a: a.reshape(1, -1)
    zeros_lora = jnp.zeros((RW_LORA, RW_WIDTH), F32)
    lane = jnp.arange(LANE)
    ones_blk = (lane[:, None] // RW_HEAD_DIM == lane[None, :] // RW_HEAD_DIM).astype(F32)

    bb_re, bb_im, lv_re, lv_im, pw_re, pw_im = _s5_prep(
        s5_a_re[l], s5_a_im[l], s5_log_dt[l].reshape(S5_GROUPS, 1),
        jnp.swapaxes(s5_b_re[l], 1, 2), jnp.swapaxes(s5_b_im[l], 1, 2))
    eye8 = jnp.eye(S5_BLK_GROUPS, dtype=F32)
    blocked = lambda a: a.reshape((S5_NBLK, S5_BLK_GROUPS) + a.shape[1:])
    bb = jnp.stack([blocked(bb_re), blocked(bb_im)])
    wb = jnp.einsum("cjghp,gk->jghckp", bb, eye8).reshape(S5_NBLK, LANE, 2 * S5_BLK_STATE)
    cc = jnp.stack([blocked(s5_c_re[l]), -blocked(s5_c_im[l])])
    wc = jnp.einsum("cjghp,gk->jcgpkh", cc, eye8).reshape(S5_NBLK, 2 * S5_BLK_STATE, LANE)
    state_rows = lambda re, im: jnp.stack(
        [re.reshape(re.shape[:-2] + (S5_NBLK, S5_BLK_STATE)),
         im.reshape(im.shape[:-2] + (S5_NBLK, S5_BLK_STATE))], axis=-2)
    lv = state_rows(lv_re, lv_im).reshape(SCAN_LEVELS, 2 * S5_NBLK, 1, S5_BLK_STATE)
    row_in_group = jnp.arange(SUBLANES)[None, None, :, None]
    lv = jnp.where(row_in_group >= (1 << jnp.arange(SCAN_LEVELS))[:, None, None, None], lv, 0.0)
    pw = jnp.moveaxis(state_rows(pw_re, pw_im).reshape(SUBLANES, 2 * S5_NBLK, S5_BLK_STATE), 0, 1)

    return dict(
        w_all=w_all,
        mu_rkv=row(rw_mu[l][:RW_RKV]), mu_wa=row(rw_mu[l][RW_RKV:]),
        rw_w0=row(rw_w0[l]), rw_a0=row(rw_a0[l]),
        rw_w2p=jnp.concatenate([rw_w2[l], zeros_lora], axis=0),
        rw_a2p=jnp.concatenate([zeros_lora, rw_a2[l]], axis=0),
        rw_kk=row(rw_kk[l]), rw_ka=row(rw_ka[l]), rw_rk=row(rw_rk[l]),
        rw_ln_g=row(rw_ln_g[l]), rw_ln_b=row(rw_ln_b[l]),
        ones2=jnp.concatenate([ones_blk, ones_blk], axis=0).astype(BF16),
        s5_wb=wb.astype(BF16), s5_wc=wc.astype(BF16), s5_lv=lv, s5_pw=pw,
        s5_d=row(s5_d[l]), s5_w_glu=s5_w_glu[l].astype(BF16), s5_b_glu=row(s5_b_glu[l]),
        ml_cw=_qk_pad(ml_conv_w[l]), ml_cb=row(_qk_pad(ml_conv_b[l])),
        ml_bi=row(_pad_lanes(ml_b_if[l][:ML_HEADS])), ml_bf=row(_pad_lanes(ml_b_if[l][ML_HEADS:])),
        ml_ln_g=row(_pad_heads(ml_ln_g[l])),
        b_merge=row(b_merge[l]), w_br_rw=w_br_rw[l].astype(BF16), w_br_s5=w_br_s5[l].astype(BF16),
        w_br_ml=jnp.pad(w_br_ml[l].reshape(ML_HEADS, ML_HEAD_DIM, D_MODEL),
                        ((0, 0), (0, ML_HEAD_PAD - ML_HEAD_DIM), (0, 0))
                        ).reshape(ML_WIDTH_PAD, D_MODEL).astype(BF16),
        w_out=w_out[l].astype(BF16), ln_g=row(ln_g[l]), ln_b=row(ln_b[l]))


def _pack_s5(re, im):
    b = re.shape[0]
    return jnp.stack([re.reshape(b, S5_NBLK, S5_BLK_STATE), im.reshape(b, S5_NBLK, S5_BLK_STATE)],
                     axis=2).reshape(b, 2 * S5_NBLK, S5_BLK_STATE)


def _unpack_s5(h):
    b = h.shape[0]
    h = h.reshape(b, S5_NBLK, 2, S5_BLK_STATE)
    return (h[:, :, 0].reshape(b, S5_GROUPS, S5_STATE), h[:, :, 1].reshape(b, S5_GROUPS, S5_STATE))


def _row_tile(n, seq_rows, target):
    best = 8
    for t in range(8, min(n, target) + 1, 8):
        if seq_rows % t == 0 or (t % seq_rows == 0 and n % t == 0):
            best = t
    return best


def _trunk_layer(x, st, lp, *, bsz, seq_rows, pad, bb_n, s5_tb, tm_proj, tm_merge):
    rw_shift0, rw_wkv0, s5_re0, s5_im0, ml_conv0, ml_c0, ml_n0, ml_m0 = st
    p = _proj(x, lp["w_all"], tm=tm_proj)
    p3 = p.reshape(bsz, seq_rows, P_COLS)
    n = bsz * seq_rows

    y_rw, wkv1, sh_rkv1, sh_wa1 = _rwkv(p3, rw_shift0[:, None, :RW_RKV], rw_shift0[:, None, RW_RKV:],
                                        rw_wkv0, lp, bb_n=4 if bsz % 4 == 0 else bb_n)
    rw_shift1 = jnp.concatenate([sh_rkv1[:, 0], sh_wa1[:, 0]], axis=-1)

    y_s5, h1 = _s5(p3, _pack_s5(s5_re0, s5_im0), lp, tb=s5_tb)
    s5_re1, s5_im1 = _unpack_s5(h1)

    y_ml, conv1, c1, n1, m1 = _mlstm(
        p3, _qk_pad(ml_conv0), ml_c0, ml_n0[:, :, None, :], _pad_lanes(ml_m0)[:, None, :], lp, pad=pad,
        bb_n=bb_n)
    ml_conv1 = jnp.concatenate([_unpad_heads(conv1[..., :ML_WIDTH_PAD]),
                                _unpad_heads(conv1[..., ML_WIDTH_PAD:])], axis=-1)

    x_new = _merge(x, p, y_rw.reshape(n, RW_WIDTH), y_s5.reshape(n, S5_WIDTH),
                   y_ml.reshape(n, ML_WIDTH_PAD), lp, tm=tm_merge, seq_rows=seq_rows, pad=pad)
    return x_new, (rw_shift1, wkv1, s5_re1, s5_im1, ml_conv1, c1, n1[:, :, 0, :],
                   m1[:, 0, :ML_HEADS])


def _run_group(x_rows, states, lps, *, bsz, seq_rows, pad, in_ln_g, in_ln_b):
    n = bsz * seq_rows
    bb_n = 2 if bsz % 2 == 0 else 1
    s5_tb = max(t for t in (BLK, 5 * BLK) if seq_rows % t == 0)
    tm_ln = _row_tile(n, seq_rows, 1024)
    tm_proj = _row_tile(n, seq_rows, 2080)
    tm_merge = _row_tile(n, seq_rows, 320)
    x = _ln_in(x_rows, in_ln_g, in_ln_b, tm=tm_ln, seq_rows=seq_rows, pad=pad)
    outs = []
    for l in range(DEPTH):
        x, st = _trunk_layer(x, states[l], lps[l], bsz=bsz, seq_rows=seq_rows, pad=pad, bb_n=bb_n,
                             s5_tb=s5_tb, tm_proj=tm_proj, tm_merge=tm_merge)
        outs.append(st)
    return x, outs


def kernel(x_prompt, x_sample, state_rwkv_shift, state_rwkv_wkv, state_s5_re, state_s5_im, state_mlstm_conv, state_mlstm_c, state_mlstm_n, state_mlstm_m, meta, in_ln_g, in_ln_b, w_in, rw_mu, rw_w0, rw_w2, rw_a0, rw_a2, rw_kk, rw_ka, rw_rk, rw_ln_g, rw_ln_b, s5_a_re, s5_a_im, s5_b_re, s5_b_im, s5_c_re, s5_c_im, s5_d, s5_log_dt, s5_w_glu, s5_b_glu, ml_conv_w, ml_conv_b, ml_b_if, ml_ln_g, b_merge, w_br_rw, w_br_s5, w_br_ml, w_out, ln_g, ln_b):
    lps = [_layer_params(l, w_in, rw_mu, rw_w0, rw_w2, rw_a0, rw_a2, rw_kk, rw_ka, rw_rk, rw_ln_g,
                         rw_ln_b, s5_a_re, s5_a_im, s5_b_re, s5_b_im, s5_c_re, s5_c_im, s5_d,
                         s5_log_dt, s5_w_glu, s5_b_glu, ml_conv_w, ml_conv_b, ml_b_if, ml_ln_g,
                         b_merge, w_br_rw, w_br_s5, w_br_ml, w_out, ln_g, ln_b)
           for l in range(DEPTH)]
    g_in, b_in = in_ln_g.reshape(1, D_MODEL), in_ln_b.reshape(1, D_MODEL)

    bp, sp = x_prompt.shape[0], x_prompt.shape[1]
    lp_rows = PAD + N_META + sp
    xp = jnp.concatenate([jnp.zeros((bp, PAD, D_MODEL), F32),
                          jnp.broadcast_to(meta[None], (bp, N_META, D_MODEL)), x_prompt], axis=1)
    z = lambda *shape: jnp.zeros((bp,) + shape, F32)
    zero_state = (z(RW_SHIFT), z(RW_HEADS, RW_HEAD_DIM, RW_HEAD_DIM), z(S5_GROUPS, S5_STATE),
                  z(S5_GROUPS, S5_STATE), z(ML_CONV - 1, 2 * ML_WIDTH),
                  (jnp.zeros((1, bp, ML_HEADS, ML_HEAD_DIM, ML_HEAD_DIM), F32), 0),
                  z(ML_HEADS, ML_HEAD_DIM), z(ML_HEADS))
    yp, p_states = _run_group(xp.reshape(bp * lp_rows, D_MODEL), [zero_state] * DEPTH, lps,
                              bsz=bp, seq_rows=lp_rows, pad=PAD, in_ln_g=g_in, in_ln_b=b_in)
    y_prompt = yp.reshape(bp, lp_rows, D_MODEL)[:, PAD + N_META:]

    bs, ds = x_sample.shape[0], x_sample.shape[1]
    s_in = [(state_rwkv_shift[l], state_rwkv_wkv[l], state_s5_re[l], state_s5_im[l],
             state_mlstm_conv[l], (state_mlstm_c, l), state_mlstm_n[l], state_mlstm_m[l])
            for l in range(DEPTH)]
    ys, s_states = _run_group(x_sample.reshape(bs * ds, D_MODEL), s_in, lps,
                              bsz=bs, seq_rows=ds, pad=0, in_ln_g=g_in, in_ln_b=b_in)
    y_sample = ys.reshape(bs, ds, D_MODEL)

    stack = lambda sts: tuple(jnp.stack(s, 0) for s in zip(*sts))
    return (y_prompt, y_sample) + stack(p_states) + stack(s_states)
```

```python
import functools
import math

import jax
import jax.numpy as jnp
from jax import lax
from jax.experimental import pallas as pl
from jax.experimental.pallas import tpu as pltpu

F32 = jnp.float32
BF16 = jnp.bfloat16

D_MODEL = 1024
DEPTH = 2
N_META = 16
RW_HEADS = 12
RW_HEAD_DIM = 64
RW_WIDTH = RW_HEADS * RW_HEAD_DIM
RW_PAIRS = RW_HEADS // 2
RW_LORA = 64
RW_RKV = 3 * RW_WIDTH
RW_SHIFT = RW_RKV + 2 * RW_LORA
S5_GROUPS = 32
S5_GROUP_CH = 16
S5_WIDTH = S5_GROUPS * S5_GROUP_CH
S5_STATE = 64
S5_NBLK = 4
S5_BLK_STATE = 512
ML_HEADS = 4
ML_HEAD_DIM = 192
ML_HEAD_PAD = 256
ML_WIDTH = ML_HEADS * ML_HEAD_DIM
ML_WIDTH_PAD = ML_HEADS * ML_HEAD_PAD
ML_CONV = 4
DN_ALPHA = (2 * DEPTH) ** 0.25
LN_EPS = 1e-5
RW_GN_EPS = 64e-5

LANE = 128
SUBLANES = 8
BLK = 64
PAD = BLK - N_META
SCAN_LEVELS = 3
SOLVE_LEVELS = 6
S5_BLK_GROUPS = LANE // S5_GROUP_CH

W_MG, W_S5, W_QK, W_OZ, W_MV = 3 * D_MODEL, 2 * S5_WIDTH, 2 * ML_WIDTH_PAD, 2 * ML_WIDTH_PAD, ML_WIDTH_PAD
W_RKV, W_RWG, W_WA, W_MI, W_MF = RW_RKV, RW_WIDTH, LANE, LANE, LANE
OFF_MG = 0
OFF_S5 = OFF_MG + W_MG
OFF_QK = OFF_S5 + W_S5
OFF_OZ = OFF_QK + W_QK
OFF_MV = OFF_OZ + W_OZ
OFF_RKV = OFF_MV + W_MV
OFF_RWG = OFF_RKV + W_RKV
OFF_WA = OFF_RWG + W_RWG
OFF_MI = OFF_WA + W_WA
OFF_MF = OFF_MI + W_MI
P_USED = OFF_MF + W_MF
MXU_COLS = 256
P_TN = 5 * MXU_COLS
P_COLS = -(-P_USED // P_TN) * P_TN
assert all(off % w == 0 for off, w in (
    (OFF_MG, W_MG), (OFF_S5, W_S5), (OFF_QK, W_QK), (OFF_OZ, W_OZ), (OFF_MV, W_MV),
    (OFF_RKV, W_RKV), (OFF_RWG, W_RWG), (OFF_WA, W_WA), (OFF_MI, W_MI), (OFF_MF, W_MF)))

VMEM_LIMIT = 56 * 1024 * 1024


def _cparams(sem):
    return pltpu.CompilerParams(dimension_semantics=sem, vmem_limit_bytes=VMEM_LIMIT)


def _bdot(a, b):
    return jnp.dot(a.astype(BF16), b.astype(BF16), preferred_element_type=F32)


def _hdot(a, b):
    return jnp.dot(a, b, precision=lax.Precision.HIGHEST, preferred_element_type=F32)


def _dot01(a01, b):
    hi = b.astype(BF16)
    rest = b - hi.astype(F32)
    mid = rest.astype(BF16)
    lo = (rest - mid.astype(F32)).astype(BF16)
    a = a01.astype(BF16)
    return jnp.dot(jnp.concatenate([a, a, a], axis=1), jnp.concatenate([hi, mid, lo], axis=0),
                   preferred_element_type=F32)


def _sigmoid(x):
    return 1.0 / (1.0 + jnp.exp(-x))


def _silu(x):
    return x * _sigmoid(x)


def _softplus(x):
    return jnp.maximum(x, 0.0) + jnp.log1p(jnp.exp(-jnp.abs(x)))


def _pad_row_mask(tile_idx, tm, seq_rows, pad):
    pos0 = lax.rem(tile_idx * tm, seq_rows)
    row = lax.broadcasted_iota(jnp.int32, (tm, 1), 0) + pos0
    return row < pad


def _layer_norm_rows(x, g, b):
    mu = jnp.mean(x, axis=-1, keepdims=True)
    d = x - mu
    var = jnp.mean(d * d, axis=-1, keepdims=True)
    return d * lax.rsqrt(var + LN_EPS) * g + b


def _ln_in_kernel(x_ref, g_ref, b_ref, o_ref, *, tm, seq_rows, pad):
    y = _layer_norm_rows(x_ref[...], g_ref[...], b_ref[...])
    if pad:
        y = jnp.where(_pad_row_mask(pl.program_id(0), tm, seq_rows, pad), 0.0, y)
    o_ref[...] = y


def _ln_in(x, g, b, *, tm, seq_rows, pad):
    n = x.shape[0]
    return pl.pallas_call(
        functools.partial(_ln_in_kernel, tm=tm, seq_rows=seq_rows, pad=pad),
        out_shape=jax.ShapeDtypeStruct((n, D_MODEL), F32),
        grid=(n // tm,),
        in_specs=[pl.BlockSpec((tm, D_MODEL), lambda i: (i, 0)),
                  pl.BlockSpec((1, D_MODEL), lambda i: (0, 0)),
                  pl.BlockSpec((1, D_MODEL), lambda i: (0, 0))],
        out_specs=pl.BlockSpec((tm, D_MODEL), lambda i: (i, 0)),
        compiler_params=_cparams(("parallel",)),
        name="ln_in",
    )(x, g, b)


def _proj_kernel(x_ref, w_ref, o_ref):
    o_ref[...] = jnp.dot(x_ref[...].astype(BF16), w_ref[...], preferred_element_type=F32)


def _proj(x, w_all, *, tm):
    n = x.shape[0]
    return pl.pallas_call(
        _proj_kernel,
        out_shape=jax.ShapeDtypeStruct((n, P_COLS), F32),
        grid=(n // tm, P_COLS // P_TN),
        in_specs=[pl.BlockSpec((tm, D_MODEL), lambda i, j: (i, 0)),
                  pl.BlockSpec((D_MODEL, P_TN), lambda i, j: (0, j))],
        out_specs=pl.BlockSpec((tm, P_TN), lambda i, j: (i, j)),
        compiler_params=_cparams(("parallel", "arbitrary")),
        name="proj",
    )(x, w_all)


def _split_lhs(hi_f32, lo_f32):
    return jnp.concatenate([hi_f32.astype(BF16), lo_f32.astype(BF16)], axis=1)


def _seg_lhs(x):
    hi = x.astype(BF16).astype(F32)
    return _split_lhs(hi, x - hi)


def _head_sum(x, ones2):
    parts = [jnp.dot(_seg_lhs(x[:, i * LANE:(i + 1) * LANE]), ones2, preferred_element_type=F32)
             for i in range(RW_WIDTH // LANE)]
    return jnp.concatenate(parts, axis=-1)


def _shift_rows(u, carry):
    rolled = pltpu.roll(u, 1, 0)
    row = lax.broadcasted_iota(jnp.int32, u.shape, 0)
    return jnp.where(row == 0, carry, rolled)


def _hi_lo(a):
    hi = a.astype(BF16)
    return hi, (a - hi.astype(F32)).astype(BF16)


def _lhs2(hl):
    return jnp.concatenate([hl[0], hl[1]], axis=1)


def _rhs2_rows(hi):
    return jnp.concatenate([hi, hi], axis=1)


def _rhs2(hi):
    return jnp.concatenate([hi, hi], axis=0)


def _dot_nt(lhs, rhs_rows):
    return lax.dot_general(lhs, rhs_rows, (((1,), (1,)), ((), ())), preferred_element_type=F32)


def _dot_nn(lhs, rhs):
    return jnp.dot(lhs, rhs, preferred_element_type=F32)


def _rw_prepare(u, uw, carry_rkv, carry_wa, mu_rkv, mu_wa, w0, w2p, a0, a2p, kk_gain, ka_gain,
                rk_gain, ones2):
    xs = u + (_shift_rows(u, carry_rkv) - u) * mu_rkv
    xwa = uw + (_shift_rows(uw, carry_wa) - uw) * mu_wa
    r = xs[:, 0:RW_WIDTH]
    k = xs[:, RW_WIDTH:2 * RW_WIDTH]
    v = xs[:, 2 * RW_WIDTH:3 * RW_WIDTH]
    log_decay = -math.exp(-0.5) * _sigmoid(w0 + _bdot(jnp.tanh(xwa), w2p))
    a = _sigmoid(a0 + _bdot(xwa, a2p))
    kk = k * kk_gain
    kk = kk * lax.rsqrt(_head_sum(kk * kk, ones2) + 1e-12)
    k = k * (1.0 + (a - 1.0) * ka_gain)
    kka = kk * a
    ti = lax.broadcasted_iota(jnp.int32, (BLK, BLK), 0)
    si = lax.broadcasted_iota(jnp.int32, (BLK, BLK), 1)
    cum = _dot01(jnp.where(si <= ti, 1.0, 0.0), log_decay)
    cum_end = cum[BLK - 1:BLK, :]
    inv_w = jnp.exp(-cum)
    tail_w = jnp.exp(cum_end - cum)
    return dict(at=-kk * jnp.exp(cum - log_decay), rt=r * jnp.exp(cum), bh=kka * inv_w, kh=k * inv_w,
                bc=kka * tail_w, kc=k * tail_w, v=v, wt=jnp.exp(cum_end),
                bonus=_head_sum(r * k * rk_gain, ones2) * v)


def _rw_kernel(rkv_ref, wa_ref, gate_ref, sh0_rkv_ref, sh0_wa_ref, s0_ref,
               mu_rkv_ref, mu_wa_ref, w0_ref, w2_ref, a0_ref, a2_ref, kk_ref, ka_ref, rk_ref,
               lng_ref, lnb_ref, ones_ref,
               y_ref, s_out, sh_rkv_out, sh_wa_out, s_scr, c_rkv, c_wa, *, bb_n):
    j = pl.program_id(1)

    @pl.when(j == 0)
    def _():
        for bb in range(bb_n):
            for p in range(RW_PAIRS):
                s_scr[bb, p] = jnp.concatenate([s0_ref[bb, 2 * p], s0_ref[bb, 2 * p + 1]], axis=1)
        c_rkv[...] = sh0_rkv_ref[...]
        c_wa[...] = sh0_wa_ref[...]

    ones2 = ones_ref[...]
    rows = []
    for bb in range(bb_n):
        u, uw = rkv_ref[bb], wa_ref[bb]
        rows.append(_rw_prepare(u, uw, c_rkv[bb], c_wa[bb], mu_rkv_ref[...], mu_wa_ref[...],
                                w0_ref[...], w2_ref[...], a0_ref[...], a2_ref[...], kk_ref[...],
                                ka_ref[...], rk_ref[...], ones2))
        for carry, out, src in ((c_rkv, sh_rkv_out, u), (c_wa, sh_wa_out, uw)):
            carry[bb] = src[BLK - 1:BLK, :]
            out[bb] = src[BLK - 1:BLK, :]

    lo_lane = lax.broadcasted_iota(jnp.int32, (1, LANE), 1) < RW_HEAD_DIM
    row_j = lax.broadcasted_iota(jnp.int32, (RW_HEAD_DIM, LANE), 0)
    col_t = lax.broadcasted_iota(jnp.int32, (RW_HEAD_DIM, LANE), 1) & (RW_HEAD_DIM - 1)
    strict = row_j < col_t
    incl = row_j <= col_t

    def by_head(x):
        return jnp.concatenate([jnp.where(lo_lane, x, 0.0), jnp.where(lo_lane, 0.0, x)], axis=0)

    pick = jnp.where(row_j == col_t, 1.0, 0.0).astype(BF16)
    pick2 = jnp.concatenate([pick, pick], axis=1)

    bd = lambda hl: (by_head(hl[0]), by_head(hl[1]))
    stack_rows = lambda parts: tuple(jnp.concatenate(x, axis=0) for x in zip(*parts))
    bf = lambda x: x.astype(BF16)

    def pair_transpose(x):
        return _dot_nt(pick2, jnp.concatenate(bd(_hi_lo(x)), axis=1))

    chains = [(bb, p) for bb in range(bb_n) for p in range(RW_PAIRS)]
    n_ch = len(chains)
    tile = lambda name, c: rows[chains[c][0]][name][:, chains[c][1] * LANE:(chains[c][1] + 1) * LANE]
    tile_hl = lambda name, c: _hi_lo(tile(name, c))

    ar_rows, gb, gk_hk_kc, hb_bc = [], [], [], []
    for c in range(n_ch):
        ar_rows.append(_rhs2_rows(jnp.concatenate(
            [by_head(bf(tile("at", c))), by_head(bf(tile("rt", c)))], axis=0)))
        bk = stack_rows([tile_hl("bh", c), tile_hl("kh", c)])
        gram = _dot_nt(_lhs2(bk), ar_rows[c])
        gb.append(_hi_lo(jnp.where(strict, gram[0:RW_HEAD_DIM, 0:LANE], 0.0)))
        hb = jnp.where(incl, gram[0:RW_HEAD_DIM, LANE:2 * LANE], 0.0)
        gk = jnp.where(strict, gram[RW_HEAD_DIM:LANE, 0:LANE], 0.0)
        hk = jnp.where(incl, gram[RW_HEAD_DIM:LANE, LANE:2 * LANE], 0.0)
        gk_hk_kc.append(_rhs2(jnp.concatenate(
            [by_head(bf(gk)), by_head(bf(hk)), by_head(bf(tile("kc", c)))], axis=1)))
        hb_bc.append(_rhs2(jnp.concatenate([by_head(bf(hb)), by_head(bf(tile("bc", c)))], axis=1)))
    vt = [pair_transpose(tile("v", c)) for c in range(n_ch)]
    v_terms = [_dot_nn(_lhs2(_hi_lo(vt[c])), gk_hk_kc[c]) for c in range(n_ch)]
    powers = [[_rhs2(by_head(g[0])) for g in gb]]
    cur = gb
    for _ in range(SOLVE_LEVELS - 1):
        cur = [_hi_lo(_dot_nn(_lhs2(cur[c]), powers[-1][c])) for c in range(n_ch)]
        powers.append([_rhs2(by_head(g[0])) for g in cur])

    s_old = [s_scr[bb, p] for bb, p in chains]
    uy = [_dot_nt(_lhs2(_hi_lo(s_old[c])), ar_rows[c]) for c in range(n_ch)]
    x = [uy[c][:, 0:LANE] + v_terms[c][:, 0:LANE] for c in range(n_ch)]
    for lvl in range(SOLVE_LEVELS):
        x = [x[c] + _dot_nn(_lhs2(_hi_lo(x[c])), powers[lvl][c]) for c in range(n_ch)]
    out2 = [_dot_nn(_lhs2(_hi_lo(x[c])), hb_bc[c]) for c in range(n_ch)]
    y_tiles = []
    for c, (bb, p) in enumerate(chains):
        y_col = uy[c][:, LANE:2 * LANE] + out2[c][:, 0:LANE] + v_terms[c][:, LANE:2 * LANE]
        y_tiles.append(pair_transpose(y_col))
        s_scr[bb, p] = (s_old[c] * tile("wt", c) + out2[c][:, LANE:2 * LANE]
                        + v_terms[c][:, 2 * LANE:3 * LANE])

    inv = 1.0 / RW_HEAD_DIM
    for bb in range(bb_n):
        y = jnp.concatenate(y_tiles[bb * RW_PAIRS:(bb + 1) * RW_PAIRS], axis=1)
        mu = _head_sum(y, ones2) * inv
        d = y - mu
        var = _head_sum(d * d, ones2) * inv
        yn = d * lax.rsqrt(var + RW_GN_EPS) * lng_ref[...] + lnb_ref[...]
        y_ref[bb] = (yn + rows[bb]["bonus"]) * _silu(gate_ref[bb])

    @pl.when(j == pl.num_programs(1) - 1)
    def _():
        for bb in range(bb_n):
            for p in range(RW_PAIRS):
                s_out[bb, 2 * p] = s_scr[bb, p, :, 0:RW_HEAD_DIM]
                s_out[bb, 2 * p + 1] = s_scr[bb, p, :, RW_HEAD_DIM:LANE]


def _rwkv(p3, sh0_rkv, sh0_wa, s0, lp, *, bb_n):
    bsz, seq_rows, _ = p3.shape
    nblk = seq_rows // BLK
    blk = lambda w, off: pl.BlockSpec((bb_n, BLK, w), lambda b, j: (b, j, off // w))
    par = lambda w: pl.BlockSpec((1, w), lambda b, j: (0, 0))
    mat = pl.BlockSpec((LANE, RW_WIDTH), lambda b, j: (0, 0))
    st = lambda w: pl.BlockSpec((bb_n, 1, w), lambda b, j: (b, 0, 0))
    s_spec = pl.BlockSpec((bb_n, RW_HEADS, RW_HEAD_DIM, RW_HEAD_DIM), lambda b, j: (b, 0, 0, 0))
    return pl.pallas_call(
        functools.partial(_rw_kernel, bb_n=bb_n),
        out_shape=(jax.ShapeDtypeStruct((bsz, seq_rows, RW_WIDTH), F32),
                   jax.ShapeDtypeStruct((bsz, RW_HEADS, RW_HEAD_DIM, RW_HEAD_DIM), F32),
                   jax.ShapeDtypeStruct((bsz, 1, RW_RKV), F32),
                   jax.ShapeDtypeStruct((bsz, 1, LANE), F32)),
        grid=(bsz // bb_n, nblk),
        in_specs=[blk(W_RKV, OFF_RKV), blk(W_WA, OFF_WA), blk(W_RWG, OFF_RWG),
                  st(RW_RKV), st(LANE), s_spec,
                  par(RW_RKV), par(LANE), par(RW_WIDTH), mat, par(RW_WIDTH), mat,
                  par(RW_WIDTH), par(RW_WIDTH), par(RW_WIDTH), par(RW_WIDTH), par(RW_WIDTH),
                  pl.BlockSpec((2 * LANE, LANE), lambda b, j: (0, 0))],
        out_specs=(pl.BlockSpec((bb_n, BLK, RW_WIDTH), lambda b, j: (b, j, 0)), s_spec,
                   st(RW_RKV), st(LANE)),
        scratch_shapes=[pltpu.VMEM((bb_n, RW_PAIRS, RW_HEAD_DIM, LANE), F32),
                        pltpu.VMEM((bb_n, 1, RW_RKV), F32), pltpu.VMEM((bb_n, 1, LANE), F32)],
        compiler_params=_cparams(("parallel", "arbitrary")),
        name="rwkv",
    )(p3, p3, p3, sh0_rkv, sh0_wa, s0, lp["mu_rkv"], lp["mu_wa"], lp["rw_w0"], lp["rw_w2p"],
      lp["rw_a0"], lp["rw_a2p"], lp["rw_kk"], lp["rw_ka"], lp["rw_rk"], lp["rw_ln_g"],
      lp["rw_ln_b"], lp["ones2"])


def _cmul(ar, ai, br, bi):
    return ar * br - ai * bi, ar * bi + ai * br


def _s5_prep_kernel(are_ref, aim_ref, ldt_ref, bre_ref, bim_ref,
                    bbre_out, bbim_out, lvre_out, lvim_out, pwre_out, pwim_out):
    ar, ai = are_ref[...], aim_ref[...]
    dt = jnp.exp(ldt_ref[...])
    mag = jnp.exp(ar * dt)
    lr, li = mag * jnp.cos(ai * dt), mag * jnp.sin(ai * dt)
    nr, ni = lr - 1.0, li
    den = ar * ar + ai * ai
    qr, qi = (nr * ar + ni * ai) / den, (ni * ar - nr * ai) / den
    br, bi = bre_ref[...], bim_ref[...]
    bbr, bbi = _cmul(qr[:, None, :], qi[:, None, :], br, bi)
    bbre_out[...] = bbr
    bbim_out[...] = bbi
    sq_r, sq_i = lr, li
    pows = [(lr, li)]
    for lvl in range(SCAN_LEVELS):
        lvre_out[lvl] = sq_r
        lvim_out[lvl] = sq_i
        pows = pows + [_cmul(pr, pi, sq_r, sq_i) for pr, pi in pows]
        sq_r, sq_i = _cmul(sq_r, sq_i, sq_r, sq_i)
    for t in range(SUBLANES):
        pwre_out[t] = pows[t][0]
        pwim_out[t] = pows[t][1]


def _s5_prep(a_re, a_im, log_dt, b_re_t, b_im_t):
    ghp = jax.ShapeDtypeStruct((S5_GROUPS, S5_GROUP_CH, S5_STATE), F32)
    lv = jax.ShapeDtypeStruct((SCAN_LEVELS, S5_GROUPS, S5_STATE), F32)
    pw = jax.ShapeDtypeStruct((SUBLANES, S5_GROUPS, S5_STATE), F32)
    return pl.pallas_call(
        _s5_prep_kernel,
        out_shape=(ghp, ghp, lv, lv, pw, pw),
        name="s5_prep",
    )(a_re, a_im, log_dt, b_re_t, b_im_t)


def _gelu_tanh(x):
    return 0.5 * x * (1.0 + jnp.tanh(math.sqrt(2.0 / math.pi) * (x + 0.044715 * (x * x * x))))


def _s5_kernel(p_ref, h0_ref, wb_ref, wc_ref, lv_ref, pw_ref, d_ref, wglu_ref, bglu_ref,
               y_ref, h_out, h_scr, hs_scr, *, tb):
    j = pl.program_id(1)

    @pl.when(j == 0)
    def _():
        h_scr[...] = h0_ref[...]

    u = p_ref[:, 0:S5_WIDTH]
    gate = p_ref[:, S5_WIDTH:2 * S5_WIDTH]
    y_parts = []
    for jb in range(S5_NBLK):
        bu = _bdot(u[:, jb * LANE:(jb + 1) * LANE], wb_ref[jb])
        for s in range(tb // BLK):
            xr = bu[s * BLK:(s + 1) * BLK, 0:S5_BLK_STATE]
            xi = bu[s * BLK:(s + 1) * BLK, S5_BLK_STATE:2 * S5_BLK_STATE]
            xr = xr.reshape(BLK // SUBLANES, SUBLANES, S5_BLK_STATE)
            xi = xi.reshape(BLK // SUBLANES, SUBLANES, S5_BLK_STATE)
            for lvl in range(SCAN_LEVELS):
                sr = pltpu.roll(xr, 1 << lvl, 1)
                si = pltpu.roll(xi, 1 << lvl, 1)
                lr, li = lv_ref[lvl, 2 * jb], lv_ref[lvl, 2 * jb + 1]
                xr, xi = xr + (lr * sr - li * si), xi + (lr * si + li * sr)
            xr = xr.reshape(BLK, S5_BLK_STATE)
            xi = xi.reshape(BLK, S5_BLK_STATE)
            c_r = h_scr[2 * jb:2 * jb + 1, :]
            c_i = h_scr[2 * jb + 1:2 * jb + 2, :]
            pr, pi = pw_ref[2 * jb], pw_ref[2 * jb + 1]
            for grp in range(BLK // SUBLANES):
                rows = slice(grp * SUBLANES, (grp + 1) * SUBLANES)
                hr = xr[rows, :] + (pr * c_r - pi * c_i)
                hi = xi[rows, :] + (pr * c_i + pi * c_r)
                c_r, c_i = hr[SUBLANES - 1:SUBLANES, :], hi[SUBLANES - 1:SUBLANES, :]
                out_rows = slice(s * BLK + grp * SUBLANES, s * BLK + (grp + 1) * SUBLANES)
                hs_scr[out_rows, 0:S5_BLK_STATE] = hr
                hs_scr[out_rows, S5_BLK_STATE:2 * S5_BLK_STATE] = hi
            h_scr[2 * jb:2 * jb + 1, :] = c_r
            h_scr[2 * jb + 1:2 * jb + 2, :] = c_i
        y_parts.append(_bdot(hs_scr[...], wc_ref[jb]))
    y = jnp.concatenate(y_parts, axis=-1) + d_ref[...] * u
    y = _gelu_tanh(y)
    y = y * _sigmoid(_bdot(y, wglu_ref[...]) + bglu_ref[...])
    y_ref[...] = y * _silu(gate)

    @pl.when(j == pl.num_programs(1) - 1)
    def _():
        h_out[...] = h_scr[...]


def _s5(p3, h0, lp, *, tb):
    bsz, seq_rows, _ = p3.shape
    st_spec = pl.BlockSpec((None, 2 * S5_NBLK, S5_BLK_STATE), lambda b, j: (b, 0, 0))
    full = lambda shape: pl.BlockSpec(shape, lambda b, j: (0,) * len(shape))
    return pl.pallas_call(
        functools.partial(_s5_kernel, tb=tb),
        out_shape=(jax.ShapeDtypeStruct((bsz, seq_rows, S5_WIDTH), F32),
                   jax.ShapeDtypeStruct((bsz, 2 * S5_NBLK, S5_BLK_STATE), F32)),
        grid=(bsz, seq_rows // tb),
        in_specs=[pl.BlockSpec((None, tb, W_S5), lambda b, j: (b, j, OFF_S5 // W_S5)),
                  st_spec,
                  full((S5_NBLK, LANE, 2 * S5_BLK_STATE)),
                  full((S5_NBLK, 2 * S5_BLK_STATE, LANE)),
                  full((SCAN_LEVELS, 2 * S5_NBLK, SUBLANES, S5_BLK_STATE)),
                  full((2 * S5_NBLK, SUBLANES, S5_BLK_STATE)),
                  full((1, S5_WIDTH)), full((S5_WIDTH, S5_WIDTH)), full((1, S5_WIDTH))],
        out_specs=(pl.BlockSpec((None, tb, S5_WIDTH), lambda b, j: (b, j, 0)), st_spec),
        scratch_shapes=[pltpu.VMEM((2 * S5_NBLK, S5_BLK_STATE), F32),
                        pltpu.VMEM((tb, 2 * S5_BLK_STATE), F32)],
        compiler_params=_cparams(("parallel", "arbitrary")),
        name="s5",
    )(p3, h0, lp["s5_wb"], lp["s5_wc"], lp["s5_lv"], lp["s5_pw"], lp["s5_d"],
      lp["s5_w_glu"], lp["s5_b_glu"])


def _mlstm_kernel(qk_ref, oz_ref, v_ref, gi_ref, gf_ref, conv0_ref, c0_ref, n0_ref, m0_ref,
                  cw_ref, cb_ref, bi_ref, bf_ref, lng_ref,
                  y_ref, conv_out, c_out, n_out, m_out,
                  xp_scr, c_scr, n_scr, m_scr, *, pad, bb_n):
    j = pl.program_id(1)
    halo = SUBLANES

    @pl.when(j == 0)
    def _():
        xp_scr[:, 0:halo, :] = jnp.zeros((bb_n, halo, W_QK), F32)
        xp_scr[:, halo - (ML_CONV - 1):halo, :] = conv0_ref[...]
        c_scr[...] = jnp.zeros(c_scr.shape, F32)
        c_scr[:, :, 0:ML_HEAD_DIM, 0:ML_HEAD_DIM] = c0_ref[...]
        n_scr[...] = jnp.zeros(n_scr.shape, F32)
        n_scr[:, :, :, 0:ML_HEAD_DIM] = n0_ref[...]
        m_scr[...] = m0_ref[...]

    t0 = jnp.where(j == 0, pad, 0) if pad else 0
    row1 = lax.broadcasted_iota(jnp.int32, (BLK, 1), 0)
    row_ok = row1 >= t0
    ti = lax.broadcasted_iota(jnp.int32, (BLK, BLK), 0)
    si = lax.broadcasted_iota(jnp.int32, (BLK, BLK), 1)
    pair_ok = (si <= ti) & (si >= t0)
    tril = jnp.where(si <= ti, 1.0, 0.0)
    eye = jnp.where(si == ti, 1.0, 0.0)
    ones_sq = jnp.ones((BLK, BLK), F32)
    lane_ok = lax.broadcasted_iota(jnp.int32, (1, ML_HEAD_PAD), 1) < ML_HEAD_DIM
    head = lambda a, h: a[:, h * ML_HEAD_PAD:(h + 1) * ML_HEAD_PAD]
    chains = [(bb, h) for bb in range(bb_n) for h in range(ML_HEADS)]

    log_i, log_f, b = [], [], []
    for bb in range(bb_n):
        log_i.append(gi_ref[bb] + bi_ref[...])
        log_f.append(jnp.where(row_ok, -_softplus(-(gf_ref[bb] + bf_ref[...])), 0.0))
        b.append(_hdot(tril, log_f[bb]))
    d_row = {}
    for bb, h in chains:
        x_col = log_i[bb][:, h:h + 1] - b[bb][:, h:h + 1]
        d_row[bb, h] = _hdot(ones_sq, eye * x_col)

    q, k, v = [], [], []
    for bb in range(bb_n):
        xp_scr[bb, halo:halo + BLK, :] = qk_ref[bb]
        conv = cb_ref[...] + xp_scr[bb, halo - 3:halo - 3 + BLK, :] * cw_ref[0:1, :]
        for tap in range(1, ML_CONV):
            conv = conv + xp_scr[bb, halo - 3 + tap:halo - 3 + tap + BLK, :] * cw_ref[tap:tap + 1, :]
        tail = xp_scr[bb, halo + BLK - (ML_CONV - 1):halo + BLK, :]
        xp_scr[bb, halo - (ML_CONV - 1):halo, :] = tail
        conv_out[bb] = tail
        act = _silu(conv)
        q.append(act[:, 0:ML_WIDTH_PAD])
        k.append(act[:, ML_WIDTH_PAD:2 * ML_WIDTH_PAD] * (1.0 / math.sqrt(ML_HEAD_DIM)))
        v.append(v_ref[bb])

    qk, q_c = {}, {}
    for bb, h in chains:
        qh = head(q[bb], h).astype(BF16)
        qk[bb, h] = lax.dot_general(qh, head(k[bb], h).astype(BF16), (((1,), (1,)), ((), ())),
                                    preferred_element_type=F32)
        q_c[bb, h] = jnp.dot(qh, c_scr[bb, h].astype(BF16), preferred_element_type=F32)

    g, we, keep = [], [], []
    for bb in range(bb_n):
        m_prev = m_scr[bb]
        g.append(b[bb] + m_prev)
        b_end = b[bb][BLK - 1:BLK, :]
        e_log = jnp.where(row_ok, b_end - b[bb] + log_i[bb], -jnp.inf)
        m_new = jnp.maximum(b_end + m_prev, jnp.max(e_log, axis=0, keepdims=True))
        we.append(jnp.exp(e_log - m_new))
        keep.append(jnp.exp(b_end + m_prev - m_new))
        m_scr[bb] = m_new

    s_mat, m_row, w_inter = {}, {}, {}
    for bb, h in chains:
        d = b[bb][:, h:h + 1] + d_row[bb, h]
        d = jnp.where(pair_ok, d, -jnp.inf)
        g_col = g[bb][:, h:h + 1]
        m_row[bb, h] = jnp.maximum(g_col, jnp.max(d, axis=1, keepdims=True))
        s_mat[bb, h] = qk[bb, h] * jnp.exp(d - m_row[bb, h])
        w_inter[bb, h] = jnp.exp(g_col - m_row[bb, h])

    s_v, k_v = {}, {}
    for bb, h in chains:
        vh = head(v[bb], h)
        s_v[bb, h] = _bdot(s_mat[bb, h], vh)
        k_v[bb, h] = lax.dot_general(head(k[bb], h).astype(BF16),
                                     (we[bb][:, h:h + 1] * vh).astype(BF16),
                                     (((0,), (0,)), ((), ())), preferred_element_type=F32)

    for bb, h in chains:
        hs = slice(h * ML_HEAD_PAD, (h + 1) * ML_HEAD_PAD)
        kh = head(k[bb], h)
        n_h = n_scr[bb, h]
        num = s_v[bb, h] + w_inter[bb, h] * q_c[bb, h]
        qn = jnp.sum(head(q[bb], h) * n_h, axis=1, keepdims=True)
        den = jnp.sum(s_mat[bb, h], axis=1, keepdims=True) + w_inter[bb, h] * qn
        hh = num / jnp.maximum(jnp.abs(den), jnp.exp(-m_row[bb, h]))
        mu = jnp.sum(hh, axis=1, keepdims=True) * (1.0 / ML_HEAD_DIM)
        dv = jnp.where(lane_ok, hh - mu, 0.0)
        var = jnp.sum(dv * dv, axis=1, keepdims=True) * (1.0 / ML_HEAD_DIM)
        hn = dv * lax.rsqrt(var + LN_EPS) * lng_ref[:, hs]
        y_ref[bb, :, hs] = (_sigmoid(oz_ref[bb, :, hs]) * hn
                            * _silu(oz_ref[bb, :, ML_WIDTH_PAD + h * ML_HEAD_PAD:
                                           ML_WIDTH_PAD + (h + 1) * ML_HEAD_PAD]))
        keep_h = keep[bb][:, h:h + 1]
        c_scr[bb, h] = keep_h * c_scr[bb, h] + k_v[bb, h]
        n_scr[bb, h] = keep_h * n_h + jnp.sum(we[bb][:, h:h + 1] * kh, axis=0, keepdims=True)

    @pl.when(j == pl.num_programs(1) - 1)
    def _():
        c_out[...] = c_scr[:, :, 0:ML_HEAD_DIM, 0:ML_HEAD_DIM]
        n_out[...] = n_scr[:, :, :, 0:ML_HEAD_DIM]
        m_out[...] = m_scr[...]


def _mlstm(p3, conv0, c0_layers, n0, m0, lp, *, pad, bb_n):
    c0, c0_layer = c0_layers
    bsz, seq_rows, _ = p3.shape
    nblk = seq_rows // BLK
    blk = lambda w, off: pl.BlockSpec((bb_n, BLK, w), lambda b, j: (b, j, off // w))
    par = lambda r, w: pl.BlockSpec((r, w), lambda b, j: (0, 0))
    conv_spec = pl.BlockSpec((bb_n, ML_CONV - 1, W_QK), lambda b, j: (b, 0, 0))
    c_spec = pl.BlockSpec((bb_n, ML_HEADS, ML_HEAD_DIM, ML_HEAD_DIM), lambda b, j: (b, 0, 0, 0))
    n_spec = pl.BlockSpec((bb_n, ML_HEADS, 1, ML_HEAD_DIM), lambda b, j: (b, 0, 0, 0))
    m_spec = pl.BlockSpec((bb_n, 1, LANE), lambda b, j: (b, 0, 0))
    return pl.pallas_call(
        functools.partial(_mlstm_kernel, pad=pad, bb_n=bb_n),
        out_shape=(jax.ShapeDtypeStruct((bsz, seq_rows, ML_WIDTH_PAD), F32),
                   jax.ShapeDtypeStruct((bsz, ML_CONV - 1, W_QK), F32),
                   jax.ShapeDtypeStruct((bsz, ML_HEADS, ML_HEAD_DIM, ML_HEAD_DIM), F32),
                   jax.ShapeDtypeStruct((bsz, ML_HEADS, 1, ML_HEAD_DIM), F32),
                   jax.ShapeDtypeStruct((bsz, 1, LANE), F32)),
        grid=(bsz // bb_n, nblk),
        in_specs=[blk(W_QK, OFF_QK), blk(W_OZ, OFF_OZ), blk(W_MV, OFF_MV),
                  blk(W_MI, OFF_MI), blk(W_MF, OFF_MF),
                  conv_spec,
                  pl.BlockSpec((None, bb_n, ML_HEADS, ML_HEAD_DIM, ML_HEAD_DIM),
                               lambda b, j: (c0_layer, b, 0, 0, 0)),
                  n_spec, m_spec,
                  par(ML_CONV, W_QK), par(1, W_QK), par(1, LANE), par(1, LANE),
                  par(1, ML_WIDTH_PAD)],
        out_specs=(pl.BlockSpec((bb_n, BLK, ML_WIDTH_PAD), lambda b, j: (b, j, 0)),
                   conv_spec, c_spec, n_spec, m_spec),
        scratch_shapes=[pltpu.VMEM((bb_n, SUBLANES + BLK, W_QK), F32),
                        pltpu.VMEM((bb_n, ML_HEADS, ML_HEAD_PAD, ML_HEAD_PAD), F32),
                        pltpu.VMEM((bb_n, ML_HEADS, 1, ML_HEAD_PAD), F32),
                        pltpu.VMEM((bb_n, 1, LANE), F32)],
        compiler_params=_cparams(("parallel", "arbitrary")),
        name="mlstm",
    )(p3, p3, p3, p3, p3, conv0, c0, n0, m0, lp["ml_cw"], lp["ml_cb"], lp["ml_bi"], lp["ml_bf"],
      lp["ml_ln_g"])


def _merge_kernel(x_ref, mg_ref, yrw_ref, ys5_ref, yml_ref, bmg_ref, wrw_ref, ws5_ref, wml_ref,
                  wout_ref, g_ref, b_ref, o_ref, *, tm, seq_rows, pad):
    gates = _sigmoid(mg_ref[...] + bmg_ref[...])
    merged = (gates[:, 0:D_MODEL] * _bdot(yrw_ref[...], wrw_ref[...])
              + gates[:, D_MODEL:2 * D_MODEL] * _bdot(ys5_ref[...], ws5_ref[...])
              + gates[:, 2 * D_MODEL:3 * D_MODEL] * _bdot(yml_ref[...], wml_ref[...]))
    out = _bdot(merged, wout_ref[...])
    y = _layer_norm_rows(DN_ALPHA * x_ref[...] + out, g_ref[...], b_ref[...])
    if pad:
        y = jnp.where(_pad_row_mask(pl.program_id(0), tm, seq_rows, pad), 0.0, y)
    o_ref[...] = y


def _merge(x, p, y_rw, y_s5, y_ml, lp, *, tm, seq_rows, pad):
    n = x.shape[0]
    rows = lambda w: pl.BlockSpec((tm, w), lambda i: (i, 0))
    full = lambda r, w: pl.BlockSpec((r, w), lambda i: (0, 0))
    return pl.pallas_call(
        functools.partial(_merge_kernel, tm=tm, seq_rows=seq_rows, pad=pad),
        out_shape=jax.ShapeDtypeStruct((n, D_MODEL), F32),
        grid=(n // tm,),
        in_specs=[rows(D_MODEL), pl.BlockSpec((tm, W_MG), lambda i: (i, OFF_MG // W_MG)),
                  rows(RW_WIDTH), rows(S5_WIDTH), rows(ML_WIDTH_PAD),
                  full(1, W_MG), full(RW_WIDTH, D_MODEL), full(S5_WIDTH, D_MODEL),
                  full(ML_WIDTH_PAD, D_MODEL), full(D_MODEL, D_MODEL),
                  full(1, D_MODEL), full(1, D_MODEL)],
        out_specs=rows(D_MODEL),
        compiler_params=_cparams(("parallel",)),
        name="merge",
    )(x, p, y_rw, y_s5, y_ml, lp["b_merge"], lp["w_br_rw"], lp["w_br_s5"], lp["w_br_ml"],
      lp["w_out"], lp["ln_g"], lp["ln_b"])


def _pad_heads(w):
    lead = w.shape[:-1]
    w = w.reshape(lead + (ML_HEADS, ML_HEAD_DIM))
    w = jnp.pad(w, [(0, 0)] * len(lead) + [(0, 0), (0, ML_HEAD_PAD - ML_HEAD_DIM)])
    return w.reshape(lead + (ML_WIDTH_PAD,))


def _unpad_heads(w):
    lead = w.shape[:-1]
    return w.reshape(lead + (ML_HEADS, ML_HEAD_PAD))[..., :ML_HEAD_DIM].reshape(lead + (ML_WIDTH,))


def _pad_lanes(w, width=LANE):
    return jnp.pad(w, [(0, 0)] * (w.ndim - 1) + [(0, width - w.shape[-1])])


def _qk_pad(w):
    return jnp.concatenate([_pad_heads(w[..., :ML_WIDTH]), _pad_heads(w[..., ML_WIDTH:])], axis=-1)


def _layer_params(l, w_in, rw_mu, rw_w0, rw_w2, rw_a0, rw_a2, rw_kk, rw_ka, rw_rk, rw_ln_g, rw_ln_b,
                  s5_a_re, s5_a_im, s5_b_re, s5_b_im, s5_c_re, s5_c_im, s5_d, s5_log_dt, s5_w_glu,
                  s5_b_glu, ml_conv_w, ml_conv_b, ml_b_if, ml_ln_g, b_merge, w_br_rw, w_br_s5,
                  w_br_ml, w_out, ln_g, ln_b):
    w = w_in[l]
    o = 0
    cols = {}
    for name, size in (("rwc", RW_SHIFT), ("rwg", RW_WIDTH), ("s5", 2 * S5_WIDTH),
                       ("qk", 2 * ML_WIDTH), ("mv", ML_WIDTH), ("mi", ML_HEADS), ("mf", ML_HEADS),
                       ("mo", ML_WIDTH), ("mz", ML_WIDTH), ("mg", 3 * D_MODEL)):
        cols[name] = w[:, o:o + size]
        o += size
    w_all = jnp.concatenate([
        cols["mg"], cols["s5"], _qk_pad(cols["qk"]),
        _pad_heads(cols["mo"]), _pad_heads(cols["mz"]), _pad_heads(cols["mv"]),
        cols["rwc"][:, :RW_RKV], cols["rwg"], cols["rwc"][:, RW_RKV:],
        _pad_lanes(cols["mi"]), _pad_lanes(cols["mf"]),
        jnp.zeros((D_MODEL, P_COLS - P_USED), F32)], axis=1).astype(BF16)
    row = lambda a: a.reshape(1, -1)
    zeros_lora = jnp.zeros((RW_LORA, RW_WIDTH), F32)
    lane = jnp.arange(LANE)
    ones_blk = (lane[:, None] // RW_HEAD_DIM == lane[None, :] // RW_HEAD_DIM).astype(F32)

    bb_re, bb_im, lv_re, lv_im, pw_re, pw_im = _s5_prep(
        s5_a_re[l], s5_a_im[l], s5_log_dt[l].reshape(S5_GROUPS, 1),
        jnp.swapaxes(s5_b_re[l], 1, 2), jnp.swapaxes(s5_b_im[l], 1, 2))
    eye8 = jnp.eye(S5_BLK_GROUPS, dtype=F32)
    blocked = lambda a: a.reshape((S5_NBLK, S5_BLK_GROUPS) + a.shape[1:])
    bb = jnp.stack([blocked(bb_re), blocked(bb_im)])
    wb = jnp.einsum("cjghp,gk->jghckp", bb, eye8).reshape(S5_NBLK, LANE, 2 * S5_BLK_STATE)
    cc = jnp.stack([blocked(s5_c_re[l]), -blocked(s5_c_im[l])])
    wc = jnp.einsum("cjghp,gk->jcgpkh", cc, eye8).reshape(S5_NBLK, 2 * S5_BLK_STATE, LANE)
    state_rows = lambda re, im: jnp.stack(
        [re.reshape(re.shape[:-2] + (S5_NBLK, S5_BLK_STATE)),
         im.reshape(im.shape[:-2] + (S5_NBLK, S5_BLK_STATE))], axis=-2)
    lv = state_rows(lv_re, lv_im).reshape(SCAN_LEVELS, 2 * S5_NBLK, 1, S5_BLK_STATE)
    row_in_group = jnp.arange(SUBLANES)[None, None, :, None]
    lv = jnp.where(row_in_group >= (1 << jnp.arange(SCAN_LEVELS))[:, None, None, None], lv, 0.0)
    pw = jnp.moveaxis(state_rows(pw_re, pw_im).reshape(SUBLANES, 2 * S5_NBLK, S5_BLK_STATE), 0, 1)

    return dict(
        w_all=w_all,
        mu_rkv=row(rw_mu[l][:RW_RKV]), mu_wa=row(rw_mu[l][RW_RKV:]),
        rw_w0=row(rw_w0[l]), rw_a0=row(rw_a0[l]),
        rw_w2p=jnp.concatenate([rw_w2[l], zeros_lora], axis=0),
        rw_a2p=jnp.concatenate([zeros_lora, rw_a2[l]], axis=0),
        rw_kk=row(rw_kk[l]), rw_ka=row(rw_ka[l]), rw_rk=row(rw_rk[l]),
        rw_ln_g=row(rw_ln_g[l]), rw_ln_b=row(rw_ln_b[l]),
        ones2=jnp.concatenate([ones_blk, ones_blk], axis=0).astype(BF16),
        s5_wb=wb.astype(BF16), s5_wc=wc.astype(BF16), s5_lv=lv, s5_pw=pw,
        s5_d=row(s5_d[l]), s5_w_glu=s5_w_glu[l].astype(BF16), s5_b_glu=row(s5_b_glu[l]),
        ml_cw=_qk_pad(ml_conv_w[l]), ml_cb=row(_qk_pad(ml_conv_b[l])),
        ml_bi=row(_pad_lanes(ml_b_if[l][:ML_HEADS])), ml_bf=row(_pad_lanes(ml_b_if[l][ML_HEADS:])),
        ml_ln_g=row(_pad_heads(ml_ln_g[l])),
        b_merge=row(b_merge[l]), w_br_rw=w_br_rw[l].astype(BF16), w_br_s5=w_br_s5[l].astype(BF16),
        w_br_ml=jnp.pad(w_br_ml[l].reshape(ML_HEADS, ML_HEAD_DIM, D_MODEL),
                        ((0, 0), (0, ML_HEAD_PAD - ML_HEAD_DIM), (0, 0))
                        ).reshape(ML_WIDTH_PAD, D_MODEL).astype(BF16),
        w_out=w_out[l].astype(BF16), ln_g=row(ln_g[l]), ln_b=row(ln_b[l]))


def _pack_s5(re, im):
    b = re.shape[0]
    return jnp.stack([re.reshape(b, S5_NBLK, S5_BLK_STATE), im.reshape(b, S5_NBLK, S5_BLK_STATE)],
                     axis=2).reshape(b, 2 * S5_NBLK, S5_BLK_STATE)


def _unpack_s5(h):
    b = h.shape[0]
    h = h.reshape(b, S5_NBLK, 2, S5_BLK_STATE)
    return (h[:, :, 0].reshape(b, S5_GROUPS, S5_STATE), h[:, :, 1].reshape(b, S5_GROUPS, S5_STATE))


def _row_tile(n, seq_rows, target):
    best = 8
    for t in range(8, min(n, target) + 1, 8):
        if seq_rows % t == 0 or (t % seq_rows == 0 and n % t == 0):
            best = t
    return best


def _trunk_layer(x, st, lp, *, bsz, seq_rows, pad, bb_n, s5_tb, tm_proj, tm_merge):
    rw_shift0, rw_wkv0, s5_re0, s5_im0, ml_conv0, ml_c0, ml_n0, ml_m0 = st
    p = _proj(x, lp["w_all"], tm=tm_proj)
    p3 = p.reshape(bsz, seq_rows, P_COLS)
    n = bsz * seq_rows

    y_rw, wkv1, sh_rkv1, sh_wa1 = _rwkv(p3, rw_shift0[:, None, :RW_RKV], rw_shift0[:, None, RW_RKV:],
                                        rw_wkv0, lp, bb_n=4 if bsz % 4 == 0 else bb_n)
    rw_shift1 = jnp.concatenate([sh_rkv1[:, 0], sh_wa1[:, 0]], axis=-1)

    y_s5, h1 = _s5(p3, _pack_s5(s5_re0, s5_im0), lp, tb=s5_tb)
    s5_re1, s5_im1 = _unpack_s5(h1)

    y_ml, conv1, c1, n1, m1 = _mlstm(
        p3, _qk_pad(ml_conv0), ml_c0, ml_n0[:, :, None, :], _pad_lanes(ml_m0)[:, None, :], lp, pad=pad,
        bb_n=bb_n)
    ml_conv1 = jnp.concatenate([_unpad_heads(conv1[..., :ML_WIDTH_PAD]),
                                _unpad_heads(conv1[..., ML_WIDTH_PAD:])], axis=-1)

    x_new = _merge(x, p, y_rw.reshape(n, RW_WIDTH), y_s5.reshape(n, S5_WIDTH),
                   y_ml.reshape(n, ML_WIDTH_PAD), lp, tm=tm_merge, seq_rows=seq_rows, pad=pad)
    return x_new, (rw_shift1, wkv1, s5_re1, s5_im1, ml_conv1, c1, n1[:, :, 0, :],
                   m1[:, 0, :ML_HEADS])


def _run_group(x_rows, states, lps, *, bsz, seq_rows, pad, in_ln_g, in_ln_b):
    n = bsz * seq_rows
    bb_n = 2 if bsz % 2 == 0 else 1
    s5_tb = max(t for t in (BLK, 5 * BLK) if seq_rows % t == 0)
    tm_ln = _row_tile(n, seq_rows, 1024)
    tm_proj = _row_tile(n, seq_rows, 2080)
    tm_merge = _row_tile(n, seq_rows, 320)
    x = _ln_in(x_rows, in_ln_g, in_ln_b, tm=tm_ln, seq_rows=seq_rows, pad=pad)
    outs = []
    for l in range(DEPTH):
        x, st = _trunk_layer(x, states[l], lps[l], bsz=bsz, seq_rows=seq_rows, pad=pad, bb_n=bb_n,
                             s5_tb=s5_tb, tm_proj=tm_proj, tm_merge=tm_merge)
        outs.append(st)
    return x, outs


def kernel(x_prompt, x_sample, state_rwkv_shift, state_rwkv_wkv, state_s5_re, state_s5_im, state_mlstm_conv, state_mlstm_c, state_mlstm_n, state_mlstm_m, meta, in_ln_g, in_ln_b, w_in, rw_mu, rw_w0, rw_w2, rw_a0, rw_a2, rw_kk, rw_ka, rw_rk, rw_ln_g, rw_ln_b, s5_a_re, s5_a_im, s5_b_re, s5_b_im, s5_c_re, s5_c_im, s5_d, s5_log_dt, s5_w_glu, s5_b_glu, ml_conv_w, ml_conv_b, ml_b_if, ml_ln_g, b_merge, w_br_rw, w_br_s5, w_br_ml, w_out, ln_g, ln_b):
    lps = [_layer_params(l, w_in, rw_mu, rw_w0, rw_w2, rw_a0, rw_a2, rw_kk, rw_ka, rw_rk, rw_ln_g,
                         rw_ln_b, s5_a_re, s5_a_im, s5_b_re, s5_b_im, s5_c_re, s5_c_im, s5_d,
                         s5_log_dt, s5_w_glu, s5_b_glu, ml_conv_w, ml_conv_b, ml_b_if, ml_ln_g,
                         b_merge, w_br_rw, w_br_s5, w_br_ml, w_out, ln_g, ln_b)
           for l in range(DEPTH)]
    g_in, b_in = in_ln_g.reshape(1, D_MODEL), in_ln_b.reshape(1, D_MODEL)

    bp, sp = x_prompt.shape[0], x_prompt.shape[1]
    lp_rows = PAD + N_META + sp
    xp = jnp.concatenate([jnp.zeros((bp, PAD, D_MODEL), F32),
                          jnp.broadcast_to(meta[None], (bp, N_META, D_MODEL)), x_prompt], axis=1)
    z = lambda *shape: jnp.zeros((bp,) + shape, F32)
    zero_state = (z(RW_SHIFT), z(RW_HEADS, RW_HEAD_DIM, RW_HEAD_DIM), z(S5_GROUPS, S5_STATE),
                  z(S5_GROUPS, S5_STATE), z(ML_CONV - 1, 2 * ML_WIDTH),
                  (jnp.zeros((1, bp, ML_HEADS, ML_HEAD_DIM, ML_HEAD_DIM), F32), 0),
                  z(ML_HEADS, ML_HEAD_DIM), z(ML_HEADS))
    yp, p_states = _run_group(xp.reshape(bp * lp_rows, D_MODEL), [zero_state] * DEPTH, lps,
                              bsz=bp, seq_rows=lp_rows, pad=PAD, in_ln_g=g_in, in_ln_b=b_in)
    y_prompt = yp.reshape(bp, lp_rows, D_MODEL)[:, PAD + N_META:]

    bs, ds = x_sample.shape[0], x_sample.shape[1]
    s_in = [(state_rwkv_shift[l], state_rwkv_wkv[l], state_s5_re[l], state_s5_im[l],
             state_mlstm_conv[l], (state_mlstm_c, l), state_mlstm_n[l], state_mlstm_m[l])
            for l in range(DEPTH)]
    ys, s_states = _run_group(x_sample.reshape(bs * ds, D_MODEL), s_in, lps,
                              bsz=bs, seq_rows=ds, pad=0, in_ln_g=g_in, in_ln_b=b_in)
    y_sample = ys.reshape(bs, ds, D_MODEL)

    stack = lambda sts: tuple(jnp.stack(s, 0) for s in zip(*sts))
    return (y_prompt, y_sample) + stack(p_states) + stack(s_states)
```

```python
import functools
import math

import jax
import jax.numpy as jnp
from jax import lax
from jax.experimental import pallas as pl
from jax.experimental.pallas import tpu as pltpu

F32 = jnp.float32
BF16 = jnp.bfloat16

D_MODEL = 1024
DEPTH = 2
N_META = 16
RW_HEADS = 12
RW_HEAD_DIM = 64
RW_WIDTH = RW_HEADS * RW_HEAD_DIM
RW_PAIRS = RW_HEADS // 2
RW_LORA = 64
RW_RKV = 3 * RW_WIDTH
RW_SHIFT = RW_RKV + 2 * RW_LORA
S5_GROUPS = 32
S5_GROUP_CH = 16
S5_WIDTH = S5_GROUPS * S5_GROUP_CH
S5_STATE = 64
S5_NBLK = 4
S5_BLK_STATE = 512
ML_HEADS = 4
ML_HEAD_DIM = 192
ML_HEAD_PAD = 256
ML_WIDTH = ML_HEADS * ML_HEAD_DIM
ML_WIDTH_PAD = ML_HEADS * ML_HEAD_PAD
ML_CONV = 4
DN_ALPHA = (2 * DEPTH) ** 0.25
LN_EPS = 1e-5
RW_GN_EPS = 64e-5

LANE = 128
SUBLANES = 8
BLK = 64
PAD = BLK - N_META
SCAN_LEVELS = 3
SOLVE_LEVELS = 6
S5_BLK_GROUPS = LANE // S5_GROUP_CH

W_MG, W_S5, W_QK, W_OZ, W_MV = 3 * D_MODEL, 2 * S5_WIDTH, 2 * ML_WIDTH_PAD, 2 * ML_WIDTH_PAD, ML_WIDTH_PAD
W_RKV, W_RWG, W_WA, W_MI, W_MF = RW_RKV, RW_WIDTH, LANE, LANE, LANE
OFF_MG = 0
OFF_S5 = OFF_MG + W_MG
OFF_QK = OFF_S5 + W_S5
OFF_OZ = OFF_QK + W_QK
OFF_MV = OFF_OZ + W_OZ
OFF_RKV = OFF_MV + W_MV
OFF_RWG = OFF_RKV + W_RKV
OFF_WA = OFF_RWG + W_RWG
OFF_MI = OFF_WA + W_WA
OFF_MF = OFF_MI + W_MI
P_USED = OFF_MF + W_MF
MXU_COLS = 256
P_TN = 5 * MXU_COLS
P_COLS = -(-P_USED // P_TN) * P_TN
assert all(off % w == 0 for off, w in (
    (OFF_MG, W_MG), (OFF_S5, W_S5), (OFF_QK, W_QK), (OFF_OZ, W_OZ), (OFF_MV, W_MV),
    (OFF_RKV, W_RKV), (OFF_RWG, W_RWG), (OFF_WA, W_WA), (OFF_MI, W_MI), (OFF_MF, W_MF)))

VMEM_LIMIT = 56 * 1024 * 1024


def _cparams(sem):
    return pltpu.CompilerParams(dimension_semantics=sem, vmem_limit_bytes=VMEM_LIMIT)


def _bdot(a, b):
    return jnp.dot(a.astype(BF16), b.astype(BF16), preferred_element_type=F32)


def _hdot(a, b):
    return jnp.dot(a, b, precision=lax.Precision.HIGHEST, preferred_element_type=F32)


def _dot01(a01, b):
    hi = b.astype(BF16)
    rest = b - hi.astype(F32)
    mid = rest.astype(BF16)
    lo = (rest - mid.astype(F32)).astype(BF16)
    a = a01.astype(BF16)
    return jnp.dot(jnp.concatenate([a, a, a], axis=1), jnp.concatenate([hi, mid, lo], axis=0),
                   preferred_element_type=F32)


def _sigmoid(x):
    return 1.0 / (1.0 + jnp.exp(-x))


def _silu(x):
    return x * _sigmoid(x)


def _softplus(x):
    return jnp.maximum(x, 0.0) + jnp.log1p(jnp.exp(-jnp.abs(x)))


def _pad_row_mask(tile_idx, tm, seq_rows, pad):
    pos0 = lax.rem(tile_idx * tm, seq_rows)
    row = lax.broadcasted_iota(jnp.int32, (tm, 1), 0) + pos0
    return row < pad


def _layer_norm_rows(x, g, b):
    mu = jnp.mean(x, axis=-1, keepdims=True)
    d = x - mu
    var = jnp.mean(d * d, axis=-1, keepdims=True)
    return d * lax.rsqrt(var + LN_EPS) * g + b


def _ln_in_kernel(x_ref, g_ref, b_ref, o_ref, *, tm, seq_rows, pad):
    y = _layer_norm_rows(x_ref[...], g_ref[...], b_ref[...])
    if pad:
        y = jnp.where(_pad_row_mask(pl.program_id(0), tm, seq_rows, pad), 0.0, y)
    o_ref[...] = y


def _ln_in(x, g, b, *, tm, seq_rows, pad):
    n = x.shape[0]
    return pl.pallas_call(
        functools.partial(_ln_in_kernel, tm=tm, seq_rows=seq_rows, pad=pad),
        out_shape=jax.ShapeDtypeStruct((n, D_MODEL), F32),
        grid=(n // tm,),
        in_specs=[pl.BlockSpec((tm, D_MODEL), lambda i: (i, 0)),
                  pl.BlockSpec((1, D_MODEL), lambda i: (0, 0)),
                  pl.BlockSpec((1, D_MODEL), lambda i: (0, 0))],
        out_specs=pl.BlockSpec((tm, D_MODEL), lambda i: (i, 0)),
        compiler_params=_cparams(("parallel",)),
        name="ln_in",
    )(x, g, b)


def _proj_kernel(x_ref, w_ref, o_ref):
    o_ref[...] = jnp.dot(x_ref[...].astype(BF16), w_ref[...], preferred_element_type=F32)


def _proj(x, w_all, *, tm):
    n = x.shape[0]
    return pl.pallas_call(
        _proj_kernel,
        out_shape=jax.ShapeDtypeStruct((n, P_COLS), F32),
        grid=(n // tm, P_COLS // P_TN),
        in_specs=[pl.BlockSpec((tm, D_MODEL), lambda i, j: (i, 0)),
                  pl.BlockSpec((D_MODEL, P_TN), lambda i, j: (0, j))],
        out_specs=pl.BlockSpec((tm, P_TN), lambda i, j: (i, j)),
        compiler_params=_cparams(("parallel", "arbitrary")),
        name="proj",
    )(x, w_all)


def _split_lhs(hi_f32, lo_f32):
    return jnp.concatenate([hi_f32.astype(BF16), lo_f32.astype(BF16)], axis=1)


def _seg_lhs(x):
    hi = x.astype(BF16).astype(F32)
    return _split_lhs(hi, x - hi)


def _head_sum(x, ones2):
    parts = [jnp.dot(_seg_lhs(x[:, i * LANE:(i + 1) * LANE]), ones2, preferred_element_type=F32)
             for i in range(RW_WIDTH // LANE)]
    return jnp.concatenate(parts, axis=-1)


def _shift_rows(u, carry):
    rolled = pltpu.roll(u, 1, 0)
    row = lax.broadcasted_iota(jnp.int32, u.shape, 0)
    return jnp.where(row == 0, carry, rolled)


def _hi_lo(a):
    hi = a.astype(BF16)
    return hi, (a - hi.astype(F32)).astype(BF16)


def _lhs2(hl):
    return jnp.concatenate([hl[0], hl[1]], axis=1)


def _rhs2_rows(hi):
    return jnp.concatenate([hi, hi], axis=1)


def _rhs2(hi):
    return jnp.concatenate([hi, hi], axis=0)


def _dot_nt(lhs, rhs_rows):
    return lax.dot_general(lhs, rhs_rows, (((1,), (1,)), ((), ())), preferred_element_type=F32)


def _dot_nn(lhs, rhs):
    return jnp.dot(lhs, rhs, preferred_element_type=F32)


def _rw_prepare(u, uw, carry_rkv, carry_wa, mu_rkv, mu_wa, w0, w2p, a0, a2p, kk_gain, ka_gain,
                rk_gain, ones2):
    xs = u + (_shift_rows(u, carry_rkv) - u) * mu_rkv
    xwa = uw + (_shift_rows(uw, carry_wa) - uw) * mu_wa
    r = xs[:, 0:RW_WIDTH]
    k = xs[:, RW_WIDTH:2 * RW_WIDTH]
    v = xs[:, 2 * RW_WIDTH:3 * RW_WIDTH]
    log_decay = -math.exp(-0.5) * _sigmoid(w0 + _bdot(jnp.tanh(xwa), w2p))
    a = _sigmoid(a0 + _bdot(xwa, a2p))
    kk = k * kk_gain
    kk = kk * lax.rsqrt(_head_sum(kk * kk, ones2) + 1e-12)
    k = k * (1.0 + (a - 1.0) * ka_gain)
    kka = kk * a
    ti = lax.broadcasted_iota(jnp.int32, (BLK, BLK), 0)
    si = lax.broadcasted_iota(jnp.int32, (BLK, BLK), 1)
    cum = _dot01(jnp.where(si <= ti, 1.0, 0.0), log_decay)
    cum_end = cum[BLK - 1:BLK, :]
    inv_w = jnp.exp(-cum)
    tail_w = jnp.exp(cum_end - cum)
    return dict(at=-kk * jnp.exp(cum - log_decay), rt=r * jnp.exp(cum), bh=kka * inv_w, kh=k * inv_w,
                bc=kka * tail_w, kc=k * tail_w, v=v, wt=jnp.exp(cum_end),
                bonus=_head_sum(r * k * rk_gain, ones2) * v)


def _rw_kernel(rkv_ref, wa_ref, gate_ref, sh0_rkv_ref, sh0_wa_ref, s0_ref,
               mu_rkv_ref, mu_wa_ref, w0_ref, w2_ref, a0_ref, a2_ref, kk_ref, ka_ref, rk_ref,
               lng_ref, lnb_ref, ones_ref,
               y_ref, s_out, sh_rkv_out, sh_wa_out, s_scr, c_rkv, c_wa, *, bb_n):
    j = pl.program_id(1)

    @pl.when(j == 0)
    def _():
        for bb in range(bb_n):
            for p in range(RW_PAIRS):
                s_scr[bb, p] = jnp.concatenate([s0_ref[bb, 2 * p], s0_ref[bb, 2 * p + 1]], axis=1)
        c_rkv[...] = sh0_rkv_ref[...]
        c_wa[...] = sh0_wa_ref[...]

    ones2 = ones_ref[...]
    rows = []
    for bb in range(bb_n):
        u, uw = rkv_ref[bb], wa_ref[bb]
        rows.append(_rw_prepare(u, uw, c_rkv[bb], c_wa[bb], mu_rkv_ref[...], mu_wa_ref[...],
                                w0_ref[...], w2_ref[...], a0_ref[...], a2_ref[...], kk_ref[...],
                                ka_ref[...], rk_ref[...], ones2))
        for carry, out, src in ((c_rkv, sh_rkv_out, u), (c_wa, sh_wa_out, uw)):
            carry[bb] = src[BLK - 1:BLK, :]
            out[bb] = src[BLK - 1:BLK, :]

    lo_lane = lax.broadcasted_iota(jnp.int32, (1, LANE), 1) < RW_HEAD_DIM
    row_j = lax.broadcasted_iota(jnp.int32, (RW_HEAD_DIM, LANE), 0)
    col_t = lax.broadcasted_iota(jnp.int32, (RW_HEAD_DIM, LANE), 1) & (RW_HEAD_DIM - 1)
    strict = row_j < col_t
    incl = row_j <= col_t

    def by_head(x):
        return jnp.concatenate([jnp.where(lo_lane, x, 0.0), jnp.where(lo_lane, 0.0, x)], axis=0)

    pick = jnp.where(row_j == col_t, 1.0, 0.0).astype(BF16)
    pick2 = jnp.concatenate([pick, pick], axis=1)

    bd = lambda hl: (by_head(hl[0]), by_head(hl[1]))
    stack_rows = lambda parts: tuple(jnp.concatenate(x, axis=0) for x in zip(*parts))
    bf = lambda x: x.astype(BF16)

    def pair_transpose(x):
        return _dot_nt(pick2, jnp.concatenate(bd(_hi_lo(x)), axis=1))

    chains = [(bb, p) for bb in range(bb_n) for p in range(RW_PAIRS)]
    n_ch = len(chains)
    tile = lambda name, c: rows[chains[c][0]][name][:, chains[c][1] * LANE:(chains[c][1] + 1) * LANE]
    tile_hl = lambda name, c: _hi_lo(tile(name, c))

    ar_rows, gb, gk_hk_kc, hb_bc = [], [], [], []
    for c in range(n_ch):
        ar_rows.append(_rhs2_rows(jnp.concatenate(
            [by_head(bf(tile("at", c))), by_head(bf(tile("rt", c)))], axis=0)))
        bk = stack_rows([tile_hl("bh", c), tile_hl("kh", c)])
        gram = _dot_nt(_lhs2(bk), ar_rows[c])
        gb.append(_hi_lo(jnp.where(strict, gram[0:RW_HEAD_DIM, 0:LANE], 0.0)))
        hb = jnp.where(incl, gram[0:RW_HEAD_DIM, LANE:2 * LANE], 0.0)
        gk = jnp.where(strict, gram[RW_HEAD_DIM:LANE, 0:LANE], 0.0)
        hk = jnp.where(incl, gram[RW_HEAD_DIM:LANE, LANE:2 * LANE], 0.0)
        gk_hk_kc.append(_rhs2(jnp.concatenate(
            [by_head(bf(gk)), by_head(bf(hk)), by_head(bf(tile("kc", c)))], axis=1)))
        hb_bc.append(_rhs2(jnp.concatenate([by_head(bf(hb)), by_head(bf(tile("bc", c)))], axis=1)))
    vt = [pair_transpose(tile("v", c)) for c in range(n_ch)]
    v_terms = [_dot_nn(_lhs2(_hi_lo(vt[c])), gk_hk_kc[c]) for c in range(n_ch)]
    powers = [[_rhs2(by_head(g[0])) for g in gb]]
    cur = gb
    for _ in range(SOLVE_LEVELS - 1):
        cur = [_hi_lo(_dot_nn(_lhs2(cur[c]), powers[-1][c])) for c in range(n_ch)]
        powers.append([_rhs2(by_head(g[0])) for g in cur])

    s_old = [s_scr[bb, p] for bb, p in chains]
    uy = [_dot_nt(_lhs2(_hi_lo(s_old[c])), ar_rows[c]) for c in range(n_ch)]
    x = [uy[c][:, 0:LANE] + v_terms[c][:, 0:LANE] for c in range(n_ch)]
    for lvl in range(SOLVE_LEVELS):
        x = [x[c] + _dot_nn(_lhs2(_hi_lo(x[c])), powers[lvl][c]) for c in range(n_ch)]
    out2 = [_dot_nn(_lhs2(_hi_lo(x[c])), hb_bc[c]) for c in range(n_ch)]
    y_tiles = []
    for c, (bb, p) in enumerate(chains):
        y_col = uy[c][:, LANE:2 * LANE] + out2[c][:, 0:LANE] + v_terms[c][:, LANE:2 * LANE]
        y_tiles.append(pair_transpose(y_col))
        s_scr[bb, p] = (s_old[c] * tile("wt", c) + out2[c][:, LANE:2 * LANE]
                        + v_terms[c][:, 2 * LANE:3 * LANE])

    inv = 1.0 / RW_HEAD_DIM
    for bb in range(bb_n):
        y = jnp.concatenate(y_tiles[bb * RW_PAIRS:(bb + 1) * RW_PAIRS], axis=1)
        mu = _head_sum(y, ones2) * inv
        d = y - mu
        var = _head_sum(d * d, ones2) * inv
        yn = d * lax.rsqrt(var + RW_GN_EPS) * lng_ref[...] + lnb_ref[...]
        y_ref[bb] = (yn + rows[bb]["bonus"]) * _silu(gate_ref[bb])

    @pl.when(j == pl.num_programs(1) - 1)
    def _():
        for bb in range(bb_n):
            for p in range(RW_PAIRS):
                s_out[bb, 2 * p] = s_scr[bb, p, :, 0:RW_HEAD_DIM]
                s_out[bb, 2 * p + 1] = s_scr[bb, p, :, RW_HEAD_DIM:LANE]


def _rwkv(p3, sh0_rkv, sh0_wa, s0, lp, *, bb_n):
    bsz, seq_rows, _ = p3.shape
    nblk = seq_rows // BLK
    blk = lambda w, off: pl.BlockSpec((bb_n, BLK, w), lambda b, j: (b, j, off // w))
    par = lambda w: pl.BlockSpec((1, w), lambda b, j: (0, 0))
    mat = pl.BlockSpec((LANE, RW_WIDTH), lambda b, j: (0, 0))
    st = lambda w: pl.BlockSpec((bb_n, 1, w), lambda b, j: (b, 0, 0))
    s_spec = pl.BlockSpec((bb_n, RW_HEADS, RW_HEAD_DIM, RW_HEAD_DIM), lambda b, j: (b, 0, 0, 0))
    return pl.pallas_call(
        functools.partial(_rw_kernel, bb_n=bb_n),
        out_shape=(jax.ShapeDtypeStruct((bsz, seq_rows, RW_WIDTH), F32),
                   jax.ShapeDtypeStruct((bsz, RW_HEADS, RW_HEAD_DIM, RW_HEAD_DIM), F32),
                   jax.ShapeDtypeStruct((bsz, 1, RW_RKV), F32),
                   jax.ShapeDtypeStruct((bsz, 1, LANE), F32)),
        grid=(bsz // bb_n, nblk),
        in_specs=[blk(W_RKV, OFF_RKV), blk(W_WA, OFF_WA), blk(W_RWG, OFF_RWG),
                  st(RW_RKV), st(LANE), s_spec,
                  par(RW_RKV), par(LANE), par(RW_WIDTH), mat, par(RW_WIDTH), mat,
                  par(RW_WIDTH), par(RW_WIDTH), par(RW_WIDTH), par(RW_WIDTH), par(RW_WIDTH),
                  pl.BlockSpec((2 * LANE, LANE), lambda b, j: (0, 0))],
        out_specs=(pl.BlockSpec((bb_n, BLK, RW_WIDTH), lambda b, j: (b, j, 0)), s_spec,
                   st(RW_RKV), st(LANE)),
        scratch_shapes=[pltpu.VMEM((bb_n, RW_PAIRS, RW_HEAD_DIM, LANE), F32),
                        pltpu.VMEM((bb_n, 1, RW_RKV), F32), pltpu.VMEM((bb_n, 1, LANE), F32)],
        compiler_params=_cparams(("parallel", "arbitrary")),
        name="rwkv",
    )(p3, p3, p3, sh0_rkv, sh0_wa, s0, lp["mu_rkv"], lp["mu_wa"], lp["rw_w0"], lp["rw_w2p"],
      lp["rw_a0"], lp["rw_a2p"], lp["rw_kk"], lp["rw_ka"], lp["rw_rk"], lp["rw_ln_g"],
      lp["rw_ln_b"], lp["ones2"])


def _cmul(ar, ai, br, bi):
    return ar * br - ai * bi, ar * bi + ai * br


def _s5_prep_kernel(are_ref, aim_ref, ldt_ref, bre_ref, bim_ref,
                    bbre_out, bbim_out, lvre_out, lvim_out, pwre_out, pwim_out):
    ar, ai = are_ref[...], aim_ref[...]
    dt = jnp.exp(ldt_ref[...])
    mag = jnp.exp(ar * dt)
    lr, li = mag * jnp.cos(ai * dt), mag * jnp.sin(ai * dt)
    nr, ni = lr - 1.0, li
    den = ar * ar + ai * ai
    qr, qi = (nr * ar + ni * ai) / den, (ni * ar - nr * ai) / den
    br, bi = bre_ref[...], bim_ref[...]
    bbr, bbi = _cmul(qr[:, None, :], qi[:, None, :], br, bi)
    bbre_out[...] = bbr
    bbim_out[...] = bbi
    sq_r, sq_i = lr, li
    pows = [(lr, li)]
    for lvl in range(SCAN_LEVELS):
        lvre_out[lvl] = sq_r
        lvim_out[lvl] = sq_i
        pows = pows + [_cmul(pr, pi, sq_r, sq_i) for pr, pi in pows]
        sq_r, sq_i = _cmul(sq_r, sq_i, sq_r, sq_i)
    for t in range(SUBLANES):
        pwre_out[t] = pows[t][0]
        pwim_out[t] = pows[t][1]


def _s5_prep(a_re, a_im, log_dt, b_re_t, b_im_t):
    ghp = jax.ShapeDtypeStruct((S5_GROUPS, S5_GROUP_CH, S5_STATE), F32)
    lv = jax.ShapeDtypeStruct((SCAN_LEVELS, S5_GROUPS, S5_STATE), F32)
    pw = jax.ShapeDtypeStruct((SUBLANES, S5_GROUPS, S5_STATE), F32)
    return pl.pallas_call(
        _s5_prep_kernel,
        out_shape=(ghp, ghp, lv, lv, pw, pw),
        name="s5_prep",
    )(a_re, a_im, log_dt, b_re_t, b_im_t)


def _gelu_tanh(x):
    return 0.5 * x * (1.0 + jnp.tanh(math.sqrt(2.0 / math.pi) * (x + 0.044715 * (x * x * x))))


def _s5_kernel(p_ref, h0_ref, wb_ref, wc_ref, lv_ref, pw_ref, d_ref, wglu_ref, bglu_ref,
               y_ref, h_out, h_scr, hs_scr, *, tb):
    j = pl.program_id(1)

    @pl.when(j == 0)
    def _():
        h_scr[...] = h0_ref[...]

    u = p_ref[:, 0:S5_WIDTH]
    gate = p_ref[:, S5_WIDTH:2 * S5_WIDTH]
    y_parts = []
    for jb in range(S5_NBLK):
        bu = _bdot(u[:, jb * LANE:(jb + 1) * LANE], wb_ref[jb])
        for s in range(tb // BLK):
            xr = bu[s * BLK:(s + 1) * BLK, 0:S5_BLK_STATE]
            xi = bu[s * BLK:(s + 1) * BLK, S5_BLK_STATE:2 * S5_BLK_STATE]
            xr = xr.reshape(BLK // SUBLANES, SUBLANES, S5_BLK_STATE)
            xi = xi.reshape(BLK // SUBLANES, SUBLANES, S5_BLK_STATE)
            for lvl in range(SCAN_LEVELS):
                sr = pltpu.roll(xr, 1 << lvl, 1)
                si = pltpu.roll(xi, 1 << lvl, 1)
                lr, li = lv_ref[lvl, 2 * jb], lv_ref[lvl, 2 * jb + 1]
                xr, xi = xr + (lr * sr - li * si), xi + (lr * si + li * sr)
            xr = xr.reshape(BLK, S5_BLK_STATE)
            xi = xi.reshape(BLK, S5_BLK_STATE)
            c_r = h_scr[2 * jb:2 * jb + 1, :]
            c_i = h_scr[2 * jb + 1:2 * jb + 2, :]
            pr, pi = pw_ref[2 * jb], pw_ref[2 * jb + 1]
            for grp in range(BLK // SUBLANES):
                rows = slice(grp * SUBLANES, (grp + 1) * SUBLANES)
                hr = xr[rows, :] + (pr * c_r - pi * c_i)
                hi = xi[rows, :] + (pr * c_i + pi * c_r)
                c_r, c_i = hr[SUBLANES - 1:SUBLANES, :], hi[SUBLANES - 1:SUBLANES, :]
                out_rows = slice(s * BLK + grp * SUBLANES, s * BLK + (grp + 1) * SUBLANES)
                hs_scr[out_rows, 0:S5_BLK_STATE] = hr
                hs_scr[out_rows, S5_BLK_STATE:2 * S5_BLK_STATE] = hi
            h_scr[2 * jb:2 * jb + 1, :] = c_r
            h_scr[2 * jb + 1:2 * jb + 2, :] = c_i
        y_parts.append(_bdot(hs_scr[...], wc_ref[jb]))
    y = jnp.concatenate(y_parts, axis=-1) + d_ref[...] * u
    y = _gelu_tanh(y)
    y = y * _sigmoid(_bdot(y, wglu_ref[...]) + bglu_ref[...])
    y_ref[...] = y * _silu(gate)

    @pl.when(j == pl.num_programs(1) - 1)
    def _():
        h_out[...] = h_scr[...]


def _s5(p3, h0, lp, *, tb):
    bsz, seq_rows, _ = p3.shape
    st_spec = pl.BlockSpec((None, 2 * S5_NBLK, S5_BLK_STATE), lambda b, j: (b, 0, 0))
    full = lambda shape: pl.BlockSpec(shape, lambda b, j: (0,) * len(shape))
    return pl.pallas_call(
        functools.partial(_s5_kernel, tb=tb),
        out_shape=(jax.ShapeDtypeStruct((bsz, seq_rows, S5_WIDTH), F32),
                   jax.ShapeDtypeStruct((bsz, 2 * S5_NBLK, S5_BLK_STATE), F32)),
        grid=(bsz, seq_rows // tb),
        in_specs=[pl.BlockSpec((None, tb, W_S5), lambda b, j: (b, j, OFF_S5 // W_S5)),
                  st_spec,
                  full((S5_NBLK, LANE, 2 * S5_BLK_STATE)),
                  full((S5_NBLK, 2 * S5_BLK_STATE, LANE)),
                  full((SCAN_LEVELS, 2 * S5_NBLK, SUBLANES, S5_BLK_STATE)),
                  full((2 * S5_NBLK, SUBLANES, S5_BLK_STATE)),
                  full((1, S5_WIDTH)), full((S5_WIDTH, S5_WIDTH)), full((1, S5_WIDTH))],
        out_specs=(pl.BlockSpec((None, tb, S5_WIDTH), lambda b, j: (b, j, 0)), st_spec),
        scratch_shapes=[pltpu.VMEM((2 * S5_NBLK, S5_BLK_STATE), F32),
                        pltpu.VMEM((tb, 2 * S5_BLK_STATE), F32)],
        compiler_params=_cparams(("parallel", "arbitrary")),
        name="s5",
    )(p3, h0, lp["s5_wb"], lp["s5_wc"], lp["s5_lv"], lp["s5_pw"], lp["s5_d"],
      lp["s5_w_glu"], lp["s5_b_glu"])


def _mlstm_kernel(qk_ref, oz_ref, v_ref, gi_ref, gf_ref, conv0_ref, c0_ref, n0_ref, m0_ref,
                  cw_ref, cb_ref, bi_ref, bf_ref, lng_ref,
                  y_ref, conv_out, c_out, n_out, m_out,
                  xp_scr, c_scr, n_scr, m_scr, *, pad, bb_n):
    j = pl.program_id(1)
    halo = SUBLANES

    @pl.when(j == 0)
    def _():
        xp_scr[:, 0:halo, :] = jnp.zeros((bb_n, halo, W_QK), F32)
        xp_scr[:, halo - (ML_CONV - 1):halo, :] = conv0_ref[...]
        c_scr[...] = jnp.zeros(c_scr.shape, F32)
        c_scr[:, :, 0:ML_HEAD_DIM, 0:ML_HEAD_DIM] = c0_ref[...]
        n_scr[...] = jnp.zeros(n_scr.shape, F32)
        n_scr[:, :, :, 0:ML_HEAD_DIM] = n0_ref[...]
        m_scr[...] = m0_ref[...]

    t0 = jnp.where(j == 0, pad, 0) if pad else 0
    row1 = lax.broadcasted_iota(jnp.int32, (BLK, 1), 0)
    row_ok = row1 >= t0
    ti = lax.broadcasted_iota(jnp.int32, (BLK, BLK), 0)
    si = lax.broadcasted_iota(jnp.int32, (BLK, BLK), 1)
    pair_ok = (si <= ti) & (si >= t0)
    tril = jnp.where(si <= ti, 1.0, 0.0)
    eye = jnp.where(si == ti, 1.0, 0.0)
    ones_sq = jnp.ones((BLK, BLK), F32)
    lane_ok = lax.broadcasted_iota(jnp.int32, (1, ML_HEAD_PAD), 1) < ML_HEAD_DIM
    head = lambda a, h: a[:, h * ML_HEAD_PAD:(h + 1) * ML_HEAD_PAD]
    chains = [(bb, h) for bb in range(bb_n) for h in range(ML_HEADS)]

    log_i, log_f, b = [], [], []
    for bb in range(bb_n):
        log_i.append(gi_ref[bb] + bi_ref[...])
        log_f.append(jnp.where(row_ok, -_softplus(-(gf_ref[bb] + bf_ref[...])), 0.0))
        b.append(_dot01(tril, log_f[bb]))
    d_row = {}
    for bb, h in chains:
        x_col = log_i[bb][:, h:h + 1] - b[bb][:, h:h + 1]
        d_row[bb, h] = _dot01(ones_sq, eye * x_col)

    q, k, v = [], [], []
    for bb in range(bb_n):
        xp_scr[bb, halo:halo + BLK, :] = qk_ref[bb]
        conv = cb_ref[...] + xp_scr[bb, halo - 3:halo - 3 + BLK, :] * cw_ref[0:1, :]
        for tap in range(1, ML_CONV):
            conv = conv + xp_scr[bb, halo - 3 + tap:halo - 3 + tap + BLK, :] * cw_ref[tap:tap + 1, :]
        tail = xp_scr[bb, halo + BLK - (ML_CONV - 1):halo + BLK, :]
        xp_scr[bb, halo - (ML_CONV - 1):halo, :] = tail
        conv_out[bb] = tail
        act = _silu(conv)
        q.append(act[:, 0:ML_WIDTH_PAD])
        k.append(act[:, ML_WIDTH_PAD:2 * ML_WIDTH_PAD] * (1.0 / math.sqrt(ML_HEAD_DIM)))
        v.append(v_ref[bb])

    qk, q_c = {}, {}
    for bb, h in chains:
        qh = head(q[bb], h).astype(BF16)
        qk[bb, h] = lax.dot_general(qh, head(k[bb], h).astype(BF16), (((1,), (1,)), ((), ())),
                                    preferred_element_type=F32)
        q_c[bb, h] = jnp.dot(qh, c_scr[bb, h].astype(BF16), preferred_element_type=F32)

    g, we, keep = [], [], []
    for bb in range(bb_n):
        m_prev = m_scr[bb]
        g.append(b[bb] + m_prev)
        b_end = b[bb][BLK - 1:BLK, :]
        e_log = jnp.where(row_ok, b_end - b[bb] + log_i[bb], -jnp.inf)
        m_new = jnp.maximum(b_end + m_prev, jnp.max(e_log, axis=0, keepdims=True))
        we.append(jnp.exp(e_log - m_new))
        keep.append(jnp.exp(b_end + m_prev - m_new))
        m_scr[bb] = m_new

    s_mat, m_row, w_inter = {}, {}, {}
    for bb, h in chains:
        d = b[bb][:, h:h + 1] + d_row[bb, h]
        d = jnp.where(pair_ok, d, -jnp.inf)
        g_col = g[bb][:, h:h + 1]
        m_row[bb, h] = jnp.maximum(g_col, jnp.max(d, axis=1, keepdims=True))
        s_mat[bb, h] = qk[bb, h] * jnp.exp(d - m_row[bb, h])
        w_inter[bb, h] = jnp.exp(g_col - m_row[bb, h])

    s_v, k_v = {}, {}
    for bb, h in chains:
        vh = head(v[bb], h)
        s_v[bb, h] = _bdot(s_mat[bb, h], vh)
        k_v[bb, h] = lax.dot_general(head(k[bb], h).astype(BF16),
                                     (we[bb][:, h:h + 1] * vh).astype(BF16),
                                     (((0,), (0,)), ((), ())), preferred_element_type=F32)

    for bb, h in chains:
        hs = slice(h * ML_HEAD_PAD, (h + 1) * ML_HEAD_PAD)
        kh = head(k[bb], h)
        n_h = n_scr[bb, h]
        num = s_v[bb, h] + w_inter[bb, h] * q_c[bb, h]
        qn = jnp.sum(head(q[bb], h) * n_h, axis=1, keepdims=True)
        den = jnp.sum(s_mat[bb, h], axis=1, keepdims=True) + w_inter[bb, h] * qn
        hh = num / jnp.maximum(jnp.abs(den), jnp.exp(-m_row[bb, h]))
        mu = jnp.sum(hh, axis=1, keepdims=True) * (1.0 / ML_HEAD_DIM)
        dv = jnp.where(lane_ok, hh - mu, 0.0)
        var = jnp.sum(dv * dv, axis=1, keepdims=True) * (1.0 / ML_HEAD_DIM)
        hn = dv * lax.rsqrt(var + LN_EPS) * lng_ref[:, hs]
        y_ref[bb, :, hs] = (_sigmoid(oz_ref[bb, :, hs]) * hn
                            * _silu(oz_ref[bb, :, ML_WIDTH_PAD + h * ML_HEAD_PAD:
                                           ML_WIDTH_PAD + (h + 1) * ML_HEAD_PAD]))
        keep_h = keep[bb][:, h:h + 1]
        c_scr[bb, h] = keep_h * c_scr[bb, h] + k_v[bb, h]
        n_scr[bb, h] = keep_h * n_h + jnp.sum(we[bb][:, h:h + 1] * kh, axis=0, keepdims=True)

    @pl.when(j == pl.num_programs(1) - 1)
    def _():
        c_out[...] = c_scr[:, :, 0:ML_HEAD_DIM, 0:ML_HEAD_DIM]
        n_out[...] = n_scr[:, :, :, 0:ML_HEAD_DIM]
        m_out[...] = m_scr[...]


def _mlstm(p3, conv0, c0_layers, n0, m0, lp, *, pad, bb_n):
    c0, c0_layer = c0_layers
    bsz, seq_rows, _ = p3.shape
    nblk = seq_rows // BLK
    blk = lambda w, off: pl.BlockSpec((bb_n, BLK, w), lambda b, j: (b, j, off // w))
    par = lambda r, w: pl.BlockSpec((r, w), lambda b, j: (0, 0))
    conv_spec = pl.BlockSpec((bb_n, ML_CONV - 1, W_QK), lambda b, j: (b, 0, 0))
    c_spec = pl.BlockSpec((bb_n, ML_HEADS, ML_HEAD_DIM, ML_HEAD_DIM), lambda b, j: (b, 0, 0, 0))
    n_spec = pl.BlockSpec((bb_n, ML_HEADS, 1, ML_HEAD_DIM), lambda b, j: (b, 0, 0, 0))
    m_spec = pl.BlockSpec((bb_n, 1, LANE), lambda b, j: (b, 0, 0))
    return pl.pallas_call(
        functools.partial(_mlstm_kernel, pad=pad, bb_n=bb_n),
        out_shape=(jax.ShapeDtypeStruct((bsz, seq_rows, ML_WIDTH_PAD), F32),
                   jax.ShapeDtypeStruct((bsz, ML_CONV - 1, W_QK), F32),
                   jax.ShapeDtypeStruct((bsz, ML_HEADS, ML_HEAD_DIM, ML_HEAD_DIM), F32),
                   jax.ShapeDtypeStruct((bsz, ML_HEADS, 1, ML_HEAD_DIM), F32),
                   jax.ShapeDtypeStruct((bsz, 1, LANE), F32)),
        grid=(bsz // bb_n, nblk),
        in_specs=[blk(W_QK, OFF_QK), blk(W_OZ, OFF_OZ), blk(W_MV, OFF_MV),
                  blk(W_MI, OFF_MI), blk(W_MF, OFF_MF),
                  conv_spec,
                  pl.BlockSpec((None, bb_n, ML_HEADS, ML_HEAD_DIM, ML_HEAD_DIM),
                               lambda b, j: (c0_layer, b, 0, 0, 0)),
                  n_spec, m_spec,
                  par(ML_CONV, W_QK), par(1, W_QK), par(1, LANE), par(1, LANE),
                  par(1, ML_WIDTH_PAD)],
        out_specs=(pl.BlockSpec((bb_n, BLK, ML_WIDTH_PAD), lambda b, j: (b, j, 0)),
                   conv_spec, c_spec, n_spec, m_spec),
        scratch_shapes=[pltpu.VMEM((bb_n, SUBLANES + BLK, W_QK), F32),
                        pltpu.VMEM((bb_n, ML_HEADS, ML_HEAD_PAD, ML_HEAD_PAD), F32),
                        pltpu.VMEM((bb_n, ML_HEADS, 1, ML_HEAD_PAD), F32),
                        pltpu.VMEM((bb_n, 1, LANE), F32)],
        compiler_params=_cparams(("parallel", "arbitrary")),
        name="mlstm",
    )(p3, p3, p3, p3, p3, conv0, c0, n0, m0, lp["ml_cw"], lp["ml_cb"], lp["ml_bi"], lp["ml_bf"],
      lp["ml_ln_g"])


def _merge_kernel(x_ref, mg_ref, yrw_ref, ys5_ref, yml_ref, bmg_ref, wrw_ref, ws5_ref, wml_ref,
                  wout_ref, g_ref, b_ref, o_ref, *, tm, seq_rows, pad):
    gates = _sigmoid(mg_ref[...] + bmg_ref[...])
    merged = (gates[:, 0:D_MODEL] * _bdot(yrw_ref[...], wrw_ref[...])
              + gates[:, D_MODEL:2 * D_MODEL] * _bdot(ys5_ref[...], ws5_ref[...])
              + gates[:, 2 * D_MODEL:3 * D_MODEL] * _bdot(yml_ref[...], wml_ref[...]))
    out = _bdot(merged, wout_ref[...])
    y = _layer_norm_rows(DN_ALPHA * x_ref[...] + out, g_ref[...], b_ref[...])
    if pad:
        y = jnp.where(_pad_row_mask(pl.program_id(0), tm, seq_rows, pad), 0.0, y)
    o_ref[...] = y


def _merge(x, p, y_rw, y_s5, y_ml, lp, *, tm, seq_rows, pad):
    n = x.shape[0]
    rows = lambda w: pl.BlockSpec((tm, w), lambda i: (i, 0))
    full = lambda r, w: pl.BlockSpec((r, w), lambda i: (0, 0))
    return pl.pallas_call(
        functools.partial(_merge_kernel, tm=tm, seq_rows=seq_rows, pad=pad),
        out_shape=jax.ShapeDtypeStruct((n, D_MODEL), F32),
        grid=(n // tm,),
        in_specs=[rows(D_MODEL), pl.BlockSpec((tm, W_MG), lambda i: (i, OFF_MG // W_MG)),
                  rows(RW_WIDTH), rows(S5_WIDTH), rows(ML_WIDTH_PAD),
                  full(1, W_MG), full(RW_WIDTH, D_MODEL), full(S5_WIDTH, D_MODEL),
                  full(ML_WIDTH_PAD, D_MODEL), full(D_MODEL, D_MODEL),
                  full(1, D_MODEL), full(1, D_MODEL)],
        out_specs=rows(D_MODEL),
        compiler_params=_cparams(("parallel",)),
        name="merge",
    )(x, p, y_rw, y_s5, y_ml, lp["b_merge"], lp["w_br_rw"], lp["w_br_s5"], lp["w_br_ml"],
      lp["w_out"], lp["ln_g"], lp["ln_b"])


def _pad_heads(w):
    lead = w.shape[:-1]
    w = w.reshape(lead + (ML_HEADS, ML_HEAD_DIM))
    w = jnp.pad(w, [(0, 0)] * len(lead) + [(0, 0), (0, ML_HEAD_PAD - ML_HEAD_DIM)])
    return w.reshape(lead + (ML_WIDTH_PAD,))


def _unpad_heads(w):
    lead = w.shape[:-1]
    return w.reshape(lead + (ML_HEADS, ML_HEAD_PAD))[..., :ML_HEAD_DIM].reshape(lead + (ML_WIDTH,))


def _pad_lanes(w, width=LANE):
    return jnp.pad(w, [(0, 0)] * (w.ndim - 1) + [(0, width - w.shape[-1])])


def _qk_pad(w):
    return jnp.concatenate([_pad_heads(w[..., :ML_WIDTH]), _pad_heads(w[..., ML_WIDTH:])], axis=-1)


def _layer_params(l, w_in, rw_mu, rw_w0, rw_w2, rw_a0, rw_a2, rw_kk, rw_ka, rw_rk, rw_ln_g, rw_ln_b,
                  s5_a_re, s5_a_im, s5_b_re, s5_b_im, s5_c_re, s5_c_im, s5_d, s5_log_dt, s5_w_glu,
                  s5_b_glu, ml_conv_w, ml_conv_b, ml_b_if, ml_ln_g, b_merge, w_br_rw, w_br_s5,
                  w_br_ml, w_out, ln_g, ln_b):
    w = w_in[l]
    o = 0
    cols = {}
    for name, size in (("rwc", RW_SHIFT), ("rwg", RW_WIDTH), ("s5", 2 * S5_WIDTH),
                       ("qk", 2 * ML_WIDTH), ("mv", ML_WIDTH), ("mi", ML_HEADS), ("mf", ML_HEADS),
                       ("mo", ML_WIDTH), ("mz", ML_WIDTH), ("mg", 3 * D_MODEL)):
        cols[name] = w[:, o:o + size]
        o += size
    w_all = jnp.concatenate([
        cols["mg"], cols["s5"], _qk_pad(cols["qk"]),
        _pad_heads(cols["mo"]), _pad_heads(cols["mz"]), _pad_heads(cols["mv"]),
        cols["rwc"][:, :RW_RKV], cols["rwg"], cols["rwc"][:, RW_RKV:],
        _pad_lanes(cols["mi"]), _pad_lanes(cols["mf"]),
        jnp.zeros((D_MODEL, P_COLS - P_USED), F32)], axis=1).astype(BF16)
    row = lambda a: a.reshape(1, -1)
    zeros_lora = jnp.zeros((RW_LORA, RW_WIDTH), F32)
    lane = jnp.arange(LANE)
    ones_blk = (lane[:, None] // RW_HEAD_DIM == lane[None, :] // RW_HEAD_DIM).astype(F32)

    bb_re, bb_im, lv_re, lv_im, pw_re, pw_im = _s5_prep(
        s5_a_re[l], s5_a_im[l], s5_log_dt[l].reshape(S5_GROUPS, 1),
        jnp.swapaxes(s5_b_re[l], 1, 2), jnp.swapaxes(s5_b_im[l], 1, 2))
    eye8 = jnp.eye(S5_BLK_GROUPS, dtype=F32)
    blocked = lambda a: a.reshape((S5_NBLK, S5_BLK_GROUPS) + a.shape[1:])
    bb = jnp.stack([blocked(bb_re), blocked(bb_im)])
    wb = jnp.einsum("cjghp,gk->jghckp", bb, eye8).reshape(S5_NBLK, LANE, 2 * S5_BLK_STATE)
    cc = jnp.stack([blocked(s5_c_re[l]), -blocked(s5_c_im[l])])
    wc = jnp.einsum("cjghp,gk->jcgpkh", cc, eye8).reshape(S5_NBLK, 2 * S5_BLK_STATE, LANE)
    state_rows = lambda re, im: jnp.stack(
        [re.reshape(re.shape[:-2] + (S5_NBLK, S5_BLK_STATE)),
         im.reshape(im.shape[:-2] + (S5_NBLK, S5_BLK_STATE))], axis=-2)
    lv = state_rows(lv_re, lv_im).reshape(SCAN_LEVELS, 2 * S5_NBLK, 1, S5_BLK_STATE)
    row_in_group = jnp.arange(SUBLANES)[None, None, :, None]
    lv = jnp.where(row_in_group >= (1 << jnp.arange(SCAN_LEVELS))[:, None, None, None], lv, 0.0)
    pw = jnp.moveaxis(state_rows(pw_re, pw_im).reshape(SUBLANES, 2 * S5_NBLK, S5_BLK_STATE), 0, 1)

    return dict(
        w_all=w_all,
        mu_rkv=row(rw_mu[l][:RW_RKV]), mu_wa=row(rw_mu[l][RW_RKV:]),
        rw_w0=row(rw_w0[l]), rw_a0=row(rw_a0[l]),
        rw_w2p=jnp.concatenate([rw_w2[l], zeros_lora], axis=0),
        rw_a2p=jnp.concatenate([zeros_lora, rw_a2[l]], axis=0),
        rw_kk=row(rw_kk[l]), rw_ka=row(rw_ka[l]), rw_rk=row(rw_rk[l]),
        rw_ln_g=row(rw_ln_g[l]), rw_ln_b=row(rw_ln_b[l]),
        ones2=jnp.concatenate([ones_blk, ones_blk], axis=0).astype(BF16),
        s5_wb=wb.astype(BF16), s5_wc=wc.astype(BF16), s5_lv=lv, s5_pw=pw,
        s5_d=row(s5_d[l]), s5_w_glu=s5_w_glu[l].astype(BF16), s5_b_glu=row(s5_b_glu[l]),
        ml_cw=_qk_pad(ml_conv_w[l]), ml_cb=row(_qk_pad(ml_conv_b[l])),
        ml_bi=row(_pad_lanes(ml_b_if[l][:ML_HEADS])), ml_bf=row(_pad_lanes(ml_b_if[l][ML_HEADS:])),
        ml_ln_g=row(_pad_heads(ml_ln_g[l])),
        b_merge=row(b_merge[l]), w_br_rw=w_br_rw[l].astype(BF16), w_br_s5=w_br_s5[l].astype(BF16),
        w_br_ml=jnp.pad(w_br_ml[l].reshape(ML_HEADS, ML_HEAD_DIM, D_MODEL),
                        ((0, 0), (0, ML_HEAD_PAD - ML_HEAD_DIM), (0, 0))
                        ).reshape(ML_WIDTH_PAD, D_MODEL).astype(BF16),
        w_out=w_out[l].astype(BF16), ln_g=row(ln_g[l]), ln_b=row(ln_b[l]))


def _pack_s5(re, im):
    b = re.shape[0]
    return jnp.stack([re.reshape(b, S5_NBLK, S5_BLK_STATE), im.reshape(b, S5_NBLK, S5_BLK_STATE)],
                     axis=2).reshape(b, 2 * S5_NBLK, S5_BLK_STATE)


def _unpack_s5(h):
    b = h.shape[0]
    h = h.reshape(b, S5_NBLK, 2, S5_BLK_STATE)
    return (h[:, :, 0].reshape(b, S5_GROUPS, S5_STATE), h[:, :, 1].reshape(b, S5_GROUPS, S5_STATE))


def _row_tile(n, seq_rows, target):
    best = 8
    for t in range(8, min(n, target) + 1, 8):
        if seq_rows % t == 0 or (t % seq_rows == 0 and n % t == 0):
            best = t
    return best


def _trunk_layer(x, st, lp, *, bsz, seq_rows, pad, bb_n, s5_tb, tm_proj, tm_merge):
    rw_shift0, rw_wkv0, s5_re0, s5_im0, ml_conv0, ml_c0, ml_n0, ml_m0 = st
    p = _proj(x, lp["w_all"], tm=tm_proj)
    p3 = p.reshape(bsz, seq_rows, P_COLS)
    n = bsz * seq_rows

    y_rw, wkv1, sh_rkv1, sh_wa1 = _rwkv(p3, rw_shift0[:, None, :RW_RKV], rw_shift0[:, None, RW_RKV:],
                                        rw_wkv0, lp, bb_n=4 if bsz % 4 == 0 else bb_n)
    rw_shift1 = jnp.concatenate([sh_rkv1[:, 0], sh_wa1[:, 0]], axis=-1)

    y_s5, h1 = _s5(p3, _pack_s5(s5_re0, s5_im0), lp, tb=s5_tb)
    s5_re1, s5_im1 = _unpack_s5(h1)

    y_ml, conv1, c1, n1, m1 = _mlstm(
        p3, _qk_pad(ml_conv0), ml_c0, ml_n0[:, :, None, :], _pad_lanes(ml_m0)[:, None, :], lp, pad=pad,
        bb_n=bb_n)
    ml_conv1 = jnp.concatenate([_unpad_heads(conv1[..., :ML_WIDTH_PAD]),
                                _unpad_heads(conv1[..., ML_WIDTH_PAD:])], axis=-1)

    x_new = _merge(x, p, y_rw.reshape(n, RW_WIDTH), y_s5.reshape(n, S5_WIDTH),
                   y_ml.reshape(n, ML_WIDTH_PAD), lp, tm=tm_merge, seq_rows=seq_rows, pad=pad)
    return x_new, (rw_shift1, wkv1, s5_re1, s5_im1, ml_conv1, c1, n1[:, :, 0, :],
                   m1[:, 0, :ML_HEADS])


def _run_group(x_rows, states, lps, *, bsz, seq_rows, pad, in_ln_g, in_ln_b):
    n = bsz * seq_rows
    bb_n = 2 if bsz % 2 == 0 else 1
    s5_tb = max(t for t in (BLK, 5 * BLK) if seq_rows % t == 0)
    tm_ln = _row_tile(n, seq_rows, 1024)
    tm_proj = _row_tile(n, seq_rows, 2080)
    tm_merge = _row_tile(n, seq_rows, 320)
    x = _ln_in(x_rows, in_ln_g, in_ln_b, tm=tm_ln, seq_rows=seq_rows, pad=pad)
    outs = []
    for l in range(DEPTH):
        x, st = _trunk_layer(x, states[l], lps[l], bsz=bsz, seq_rows=seq_rows, pad=pad, bb_n=bb_n,
                             s5_tb=s5_tb, tm_proj=tm_proj, tm_merge=tm_merge)
        outs.append(st)
    return x, outs


def kernel(x_prompt, x_sample, state_rwkv_shift, state_rwkv_wkv, state_s5_re, state_s5_im, state_mlstm_conv, state_mlstm_c, state_mlstm_n, state_mlstm_m, meta, in_ln_g, in_ln_b, w_in, rw_mu, rw_w0, rw_w2, rw_a0, rw_a2, rw_kk, rw_ka, rw_rk, rw_ln_g, rw_ln_b, s5_a_re, s5_a_im, s5_b_re, s5_b_im, s5_c_re, s5_c_im, s5_d, s5_log_dt, s5_w_glu, s5_b_glu, ml_conv_w, ml_conv_b, ml_b_if, ml_ln_g, b_merge, w_br_rw, w_br_s5, w_br_ml, w_out, ln_g, ln_b):
    lps = [_layer_params(l, w_in, rw_mu, rw_w0, rw_w2, rw_a0, rw_a2, rw_kk, rw_ka, rw_rk, rw_ln_g,
                         rw_ln_b, s5_a_re, s5_a_im, s5_b_re, s5_b_im, s5_c_re, s5_c_im, s5_d,
                         s5_log_dt, s5_w_glu, s5_b_glu, ml_conv_w, ml_conv_b, ml_b_if, ml_ln_g,
                         b_merge, w_br_rw, w_br_s5, w_br_ml, w_out, ln_g, ln_b)
           for l in range(DEPTH)]
    g_in, b_in = in_ln_g.reshape(1, D_MODEL), in_ln_b.reshape(1, D_MODEL)

    bp, sp = x_prompt.shape[0], x_prompt.shape[1]
    lp_rows = PAD + N_META + sp
    xp = jnp.concatenate([jnp.zeros((bp, PAD, D_MODEL), F32),
                          jnp.broadcast_to(meta[None], (bp, N_META, D_MODEL)), x_prompt], axis=1)
    z = lambda *shape: jnp.zeros((bp,) + shape, F32)
    zero_state = (z(RW_SHIFT), z(RW_HEADS, RW_HEAD_DIM, RW_HEAD_DIM), z(S5_GROUPS, S5_STATE),
                  z(S5_GROUPS, S5_STATE), z(ML_CONV - 1, 2 * ML_WIDTH),
                  (jnp.zeros((1, bp, ML_HEADS, ML_HEAD_DIM, ML_HEAD_DIM), F32), 0),
                  z(ML_HEADS, ML_HEAD_DIM), z(ML_HEADS))
    yp, p_states = _run_group(xp.reshape(bp * lp_rows, D_MODEL), [zero_state] * DEPTH, lps,
                              bsz=bp, seq_rows=lp_rows, pad=PAD, in_ln_g=g_in, in_ln_b=b_in)
    y_prompt = yp.reshape(bp, lp_rows, D_MODEL)[:, PAD + N_META:]

    bs, ds = x_sample.shape[0], x_sample.shape[1]
    s_in = [(state_rwkv_shift[l], state_rwkv_wkv[l], state_s5_re[l], state_s5_im[l],
             state_mlstm_conv[l], (state_mlstm_c, l), state_mlstm_n[l], state_mlstm_m[l])
            for l in range(DEPTH)]
    ys, s_states = _run_group(x_sample.reshape(bs * ds, D_MODEL), s_in, lps,
                              bsz=bs, seq_rows=ds, pad=0, in_ln_g=g_in, in_ln_b=b_in)
    y_sample = ys.reshape(bs, ds, D_MODEL)

    stack = lambda sts: tuple(jnp.stack(s, 0) for s in zip(*sts))
    return (y_prompt, y_sample) + stack(p_states) + stack(s_states)
```

```python
import functools
import math

import jax
import jax.numpy as jnp
from jax import lax
from jax.experimental import pallas as pl
from jax.experimental.pallas import tpu as pltpu

F32 = jnp.float32
BF16 = jnp.bfloat16

D_MODEL = 1024
DEPTH = 2
N_META = 16
RW_HEADS = 12
RW_HEAD_DIM = 64
RW_WIDTH = RW_HEADS * RW_HEAD_DIM
RW_PAIRS = RW_HEADS // 2
RW_LORA = 64
RW_RKV = 3 * RW_WIDTH
RW_SHIFT = RW_RKV + 2 * RW_LORA
S5_GROUPS = 32
S5_GROUP_CH = 16
S5_WIDTH = S5_GROUPS * S5_GROUP_CH
S5_STATE = 64
S5_NBLK = 4
S5_BLK_STATE = 512
ML_HEADS = 4
ML_HEAD_DIM = 192
ML_HEAD_PAD = 256
ML_WIDTH = ML_HEADS * ML_HEAD_DIM
ML_WIDTH_PAD = ML_HEADS * ML_HEAD_PAD
ML_CONV = 4
DN_ALPHA = (2 * DEPTH) ** 0.25
LN_EPS = 1e-5
RW_GN_EPS = 64e-5

LANE = 128
SUBLANES = 8
BLK = 64
PAD = BLK - N_META
SCAN_LEVELS = 3
SOLVE_BLK = 8
SOLVE_LEVELS = 3
S5_BLK_GROUPS = LANE // S5_GROUP_CH

W_MG, W_S5, W_QK, W_OZ, W_MV = 3 * D_MODEL, 2 * S5_WIDTH, 2 * ML_WIDTH_PAD, 2 * ML_WIDTH_PAD, ML_WIDTH_PAD
W_RKV, W_RWG, W_WA, W_MI, W_MF = RW_RKV, RW_WIDTH, LANE, LANE, LANE
OFF_MG = 0
OFF_S5 = OFF_MG + W_MG
OFF_QK = OFF_S5 + W_S5
OFF_OZ = OFF_QK + W_QK
OFF_MV = OFF_OZ + W_OZ
OFF_RKV = OFF_MV + W_MV
OFF_RWG = OFF_RKV + W_RKV
OFF_WA = OFF_RWG + W_RWG
OFF_MI = OFF_WA + W_WA
OFF_MF = OFF_MI + W_MI
P_USED = OFF_MF + W_MF
MXU_COLS = 256
P_TN = 5 * MXU_COLS
P_COLS = -(-P_USED // P_TN) * P_TN
assert all(off % w == 0 for off, w in (
    (OFF_MG, W_MG), (OFF_S5, W_S5), (OFF_QK, W_QK), (OFF_OZ, W_OZ), (OFF_MV, W_MV),
    (OFF_RKV, W_RKV), (OFF_RWG, W_RWG), (OFF_WA, W_WA), (OFF_MI, W_MI), (OFF_MF, W_MF)))

VMEM_LIMIT = 56 * 1024 * 1024


def _cparams(sem):
    return pltpu.CompilerParams(dimension_semantics=sem, vmem_limit_bytes=VMEM_LIMIT)


def _bdot(a, b):
    return jnp.dot(a.astype(BF16), b.astype(BF16), preferred_element_type=F32)


def _hdot(a, b):
    return jnp.dot(a, b, precision=lax.Precision.HIGHEST, preferred_element_type=F32)


def _dot01(a01, b):
    hi = b.astype(BF16)
    rest = b - hi.astype(F32)
    mid = rest.astype(BF16)
    lo = (rest - mid.astype(F32)).astype(BF16)
    a = a01.astype(BF16)
    return jnp.dot(jnp.concatenate([a, a, a], axis=1), jnp.concatenate([hi, mid, lo], axis=0),
                   preferred_element_type=F32)


def _sigmoid(x):
    return 1.0 / (1.0 + jnp.exp(-x))


def _silu(x):
    return x * _sigmoid(x)


def _softplus(x):
    return jnp.maximum(x, 0.0) + jnp.log1p(jnp.exp(-jnp.abs(x)))


def _pad_row_mask(tile_idx, tm, seq_rows, pad):
    pos0 = lax.rem(tile_idx * tm, seq_rows)
    row = lax.broadcasted_iota(jnp.int32, (tm, 1), 0) + pos0
    return row < pad


def _layer_norm_rows(x, g, b):
    mu = jnp.mean(x, axis=-1, keepdims=True)
    d = x - mu
    var = jnp.mean(d * d, axis=-1, keepdims=True)
    return d * lax.rsqrt(var + LN_EPS) * g + b


def _ln_in_kernel(x_ref, g_ref, b_ref, o_ref, *, tm, seq_rows, pad):
    y = _layer_norm_rows(x_ref[...], g_ref[...], b_ref[...])
    if pad:
        y = jnp.where(_pad_row_mask(pl.program_id(0), tm, seq_rows, pad), 0.0, y)
    o_ref[...] = y


def _ln_in(x, g, b, *, tm, seq_rows, pad):
    n = x.shape[0]
    return pl.pallas_call(
        functools.partial(_ln_in_kernel, tm=tm, seq_rows=seq_rows, pad=pad),
        out_shape=jax.ShapeDtypeStruct((n, D_MODEL), F32),
        grid=(n // tm,),
        in_specs=[pl.BlockSpec((tm, D_MODEL), lambda i: (i, 0)),
                  pl.BlockSpec((1, D_MODEL), lambda i: (0, 0)),
                  pl.BlockSpec((1, D_MODEL), lambda i: (0, 0))],
        out_specs=pl.BlockSpec((tm, D_MODEL), lambda i: (i, 0)),
        compiler_params=_cparams(("parallel",)),
        name="ln_in",
    )(x, g, b)


def _proj_kernel(x_ref, w_ref, o_ref):
    o_ref[...] = jnp.dot(x_ref[...].astype(BF16), w_ref[...], preferred_element_type=F32)


def _proj(x, w_all, *, tm):
    n = x.shape[0]
    return pl.pallas_call(
        _proj_kernel,
        out_shape=jax.ShapeDtypeStruct((n, P_COLS), F32),
        grid=(n // tm, P_COLS // P_TN),
        in_specs=[pl.BlockSpec((tm, D_MODEL), lambda i, j: (i, 0)),
                  pl.BlockSpec((D_MODEL, P_TN), lambda i, j: (0, j))],
        out_specs=pl.BlockSpec((tm, P_TN), lambda i, j: (i, j)),
        compiler_params=_cparams(("parallel", "arbitrary")),
        name="proj",
    )(x, w_all)


def _split_lhs(hi_f32, lo_f32):
    return jnp.concatenate([hi_f32.astype(BF16), lo_f32.astype(BF16)], axis=1)


def _seg_lhs(x):
    hi = x.astype(BF16).astype(F32)
    return _split_lhs(hi, x - hi)


def _head_sum(x, ones2):
    parts = [jnp.dot(_seg_lhs(x[:, i * LANE:(i + 1) * LANE]), ones2, preferred_element_type=F32)
             for i in range(RW_WIDTH // LANE)]
    return jnp.concatenate(parts, axis=-1)


def _shift_rows(u, carry):
    rolled = pltpu.roll(u, 1, 0)
    row = lax.broadcasted_iota(jnp.int32, u.shape, 0)
    return jnp.where(row == 0, carry, rolled)


def _hi_lo(a):
    hi = a.astype(BF16)
    return hi, (a - hi.astype(F32)).astype(BF16)


def _lhs3(hl):
    return jnp.concatenate([hl[0], hl[1], hl[0]], axis=1)


def _rhs3_rows(hl):
    return jnp.concatenate([hl[0], hl[0], hl[1]], axis=1)


def _rhs3(hl):
    return jnp.concatenate([hl[0], hl[0], hl[1]], axis=0)


def _dot_nt(lhs, rhs_rows):
    return lax.dot_general(lhs, rhs_rows, (((1,), (1,)), ((), ())), preferred_element_type=F32)


def _dot_nn(lhs, rhs):
    return jnp.dot(lhs, rhs, preferred_element_type=F32)


def _rw_prepare(u, uw, carry_rkv, carry_wa, mu_rkv, mu_wa, w0, w2p, a0, a2p, kk_gain, ka_gain,
                rk_gain, ones2):
    xs = u + (_shift_rows(u, carry_rkv) - u) * mu_rkv
    xwa = uw + (_shift_rows(uw, carry_wa) - uw) * mu_wa
    r = xs[:, 0:RW_WIDTH]
    k = xs[:, RW_WIDTH:2 * RW_WIDTH]
    v = xs[:, 2 * RW_WIDTH:3 * RW_WIDTH]
    log_decay = -math.exp(-0.5) * _sigmoid(w0 + _bdot(jnp.tanh(xwa), w2p))
    a = _sigmoid(a0 + _bdot(xwa, a2p))
    kk = k * kk_gain
    kk = kk * lax.rsqrt(_head_sum(kk * kk, ones2) + 1e-12)
    k = k * (1.0 + (a - 1.0) * ka_gain)
    kka = kk * a
    ti = lax.broadcasted_iota(jnp.int32, (BLK, BLK), 0)
    si = lax.broadcasted_iota(jnp.int32, (BLK, BLK), 1)
    cum = _dot01(jnp.where(si <= ti, 1.0, 0.0), log_decay)
    cum_end = cum[BLK - 1:BLK, :]
    inv_w = jnp.exp(-cum)
    tail_w = jnp.exp(cum_end - cum)
    return dict(at=-kk * jnp.exp(cum - log_decay), rt=r * jnp.exp(cum), bh=kka * inv_w, kh=k * inv_w,
                bc=kka * tail_w, kc=k * tail_w, v=v, wt=jnp.exp(cum_end),
                bonus=_head_sum(r * k * rk_gain, ones2) * v)


def _rw_kernel(rkv_ref, wa_ref, gate_ref, sh0_rkv_ref, sh0_wa_ref, s0_ref,
               mu_rkv_ref, mu_wa_ref, w0_ref, w2_ref, a0_ref, a2_ref, kk_ref, ka_ref, rk_ref,
               lng_ref, lnb_ref, ones_ref,
               y_ref, s_out, sh_rkv_out, sh_wa_out, s_scr, c_rkv, c_wa, *, bb_n):
    j = pl.program_id(1)

    @pl.when(j == 0)
    def _():
        for bb in range(bb_n):
            for p in range(RW_PAIRS):
                s_scr[bb, p] = jnp.concatenate([s0_ref[bb, 2 * p], s0_ref[bb, 2 * p + 1]], axis=1)
        c_rkv[...] = sh0_rkv_ref[...]
        c_wa[...] = sh0_wa_ref[...]

    ones2 = ones_ref[...]
    rows = []
    for bb in range(bb_n):
        u, uw = rkv_ref[bb], wa_ref[bb]
        rows.append(_rw_prepare(u, uw, c_rkv[bb], c_wa[bb], mu_rkv_ref[...], mu_wa_ref[...],
                                w0_ref[...], w2_ref[...], a0_ref[...], a2_ref[...], kk_ref[...],
                                ka_ref[...], rk_ref[...], ones2))
        for carry, out, src in ((c_rkv, sh_rkv_out, u), (c_wa, sh_wa_out, uw)):
            carry[bb] = src[BLK - 1:BLK, :]
            out[bb] = src[BLK - 1:BLK, :]

    lo_lane = lax.broadcasted_iota(jnp.int32, (1, LANE), 1) < RW_HEAD_DIM
    row_j = lax.broadcasted_iota(jnp.int32, (RW_HEAD_DIM, LANE), 0)
    col_t = lax.broadcasted_iota(jnp.int32, (RW_HEAD_DIM, LANE), 1) & (RW_HEAD_DIM - 1)
    strict = row_j < col_t
    incl = row_j <= col_t
    same_blk = (row_j // SOLVE_BLK) == (col_t // SOLVE_BLK)
    ident = jnp.where(row_j == col_t, 1.0, 0.0)
    col_blk = (lax.broadcasted_iota(jnp.int32, (1, LANE), 1) & (RW_HEAD_DIM - 1)) // SOLVE_BLK

    def by_head(x):
        return jnp.concatenate([jnp.where(lo_lane, x, 0.0), jnp.where(lo_lane, 0.0, x)], axis=0)

    pick = jnp.where(row_j == col_t, 1.0, 0.0).astype(BF16)
    pick2 = jnp.concatenate([pick, pick], axis=1)

    bd = lambda hl: (by_head(hl[0]), by_head(hl[1]))
    stack_rows = lambda parts: tuple(jnp.concatenate(x, axis=0) for x in zip(*parts))
    stack_lanes = lambda parts: tuple(jnp.concatenate(x, axis=1) for x in zip(*parts))
    wide_rhs = lambda w: _rhs3(bd(_hi_lo(w)))
    wide_dot = lambda a, rhs3: _dot_nn(_lhs3(_hi_lo(a)), rhs3)

    def pair_transpose(x):
        return _dot_nt(pick2, jnp.concatenate(bd(_hi_lo(x)), axis=1))

    chains = [(bb, p) for bb in range(bb_n) for p in range(RW_PAIRS)]
    n_ch = len(chains)
    tile = lambda name, c: rows[chains[c][0]][name][:, chains[c][1] * LANE:(chains[c][1] + 1) * LANE]
    tile_hl = lambda name, c: _hi_lo(tile(name, c))

    ar_rows, gb3, gd, gk_hk_kc3, bc3 = [], [], [], [], []
    for c in range(n_ch):
        ar_rows.append(_rhs3_rows(stack_rows([bd(tile_hl("at", c)), bd(tile_hl("rt", c))])))
        bk = stack_rows([tile_hl("bh", c), tile_hl("kh", c)])
        gram = _dot_nt(_lhs3(bk), ar_rows[c])
        gb = jnp.where(strict, gram[0:RW_HEAD_DIM, 0:LANE], 0.0)
        hb = jnp.where(incl, gram[0:RW_HEAD_DIM, LANE:2 * LANE], 0.0)
        gk = jnp.where(strict, gram[RW_HEAD_DIM:LANE, 0:LANE], 0.0)
        hk = jnp.where(incl, gram[RW_HEAD_DIM:LANE, LANE:2 * LANE], 0.0)
        gb3.append(wide_rhs(gb))
        gd.append(jnp.where(same_blk, gb, 0.0))
        gk_hk_kc3.append(_rhs3(stack_lanes([bd(_hi_lo(gk)), bd(_hi_lo(hk)), bd(tile_hl("kc", c))])))
        bc3.append(_rhs3(stack_lanes([bd(_hi_lo(hb)), bd(tile_hl("bc", c))])))
    vt = [pair_transpose(tile("v", c)) for c in range(n_ch)]
    v_terms = [_dot_nn(_lhs3(_hi_lo(vt[c])), gk_hk_kc3[c]) for c in range(n_ch)]
    t_inv = [ident + g for g in gd]
    pw = gd
    for _ in range(SOLVE_LEVELS - 1):
        pw = [wide_dot(pw[c], wide_rhs(pw[c])) for c in range(n_ch)]
        t_inv = [t_inv[c] + wide_dot(t_inv[c], wide_rhs(pw[c])) for c in range(n_ch)]
    td3 = [wide_rhs(t) for t in t_inv]

    s_old = [s_scr[bb, p] for bb, p in chains]
    uy = [_dot_nt(_lhs3(_hi_lo(s_old[c])), ar_rows[c]) for c in range(n_ch)]
    x0 = [uy[c][:, 0:LANE] + v_terms[c][:, 0:LANE] for c in range(n_ch)]
    x = [wide_dot(jnp.where(col_blk == 0, x0[c], 0.0), td3[c]) for c in range(n_ch)]
    for blk in range(1, BLK // SOLVE_BLK):
        rhs = [jnp.where(col_blk == blk, x0[c] + wide_dot(x[c], gb3[c]), 0.0) for c in range(n_ch)]
        x = [x[c] + wide_dot(rhs[c], td3[c]) for c in range(n_ch)]
    out2 = [wide_dot(x[c], bc3[c]) for c in range(n_ch)]
    y_tiles = []
    for c, (bb, p) in enumerate(chains):
        y_col = uy[c][:, LANE:2 * LANE] + out2[c][:, 0:LANE] + v_terms[c][:, LANE:2 * LANE]
        y_tiles.append(pair_transpose(y_col))
        s_scr[bb, p] = (s_old[c] * tile("wt", c) + out2[c][:, LANE:2 * LANE]
                        + v_terms[c][:, 2 * LANE:3 * LANE])

    inv = 1.0 / RW_HEAD_DIM
    for bb in range(bb_n):
        y = jnp.concatenate(y_tiles[bb * RW_PAIRS:(bb + 1) * RW_PAIRS], axis=1)
        mu = _head_sum(y, ones2) * inv
        d = y - mu
        var = _head_sum(d * d, ones2) * inv
        yn = d * lax.rsqrt(var + RW_GN_EPS) * lng_ref[...] + lnb_ref[...]
        y_ref[bb] = (yn + rows[bb]["bonus"]) * _silu(gate_ref[bb])

    @pl.when(j == pl.num_programs(1) - 1)
    def _():
        for bb in range(bb_n):
            for p in range(RW_PAIRS):
                s_out[bb, 2 * p] = s_scr[bb, p, :, 0:RW_HEAD_DIM]
                s_out[bb, 2 * p + 1] = s_scr[bb, p, :, RW_HEAD_DIM:LANE]


def _rwkv(p3, sh0_rkv, sh0_wa, s0, lp, *, bb_n):
    bsz, seq_rows, _ = p3.shape
    nblk = seq_rows // BLK
    blk = lambda w, off: pl.BlockSpec((bb_n, BLK, w), lambda b, j: (b, j, off // w))
    par = lambda w: pl.BlockSpec((1, w), lambda b, j: (0, 0))
    mat = pl.BlockSpec((LANE, RW_WIDTH), lambda b, j: (0, 0))
    st = lambda w: pl.BlockSpec((bb_n, 1, w), lambda b, j: (b, 0, 0))
    s_spec = pl.BlockSpec((bb_n, RW_HEADS, RW_HEAD_DIM, RW_HEAD_DIM), lambda b, j: (b, 0, 0, 0))
    return pl.pallas_call(
        functools.partial(_rw_kernel, bb_n=bb_n),
        out_shape=(jax.ShapeDtypeStruct((bsz, seq_rows, RW_WIDTH), F32),
                   jax.ShapeDtypeStruct((bsz, RW_HEADS, RW_HEAD_DIM, RW_HEAD_DIM), F32),
                   jax.ShapeDtypeStruct((bsz, 1, RW_RKV), F32),
                   jax.ShapeDtypeStruct((bsz, 1, LANE), F32)),
        grid=(bsz // bb_n, nblk),
        in_specs=[blk(W_RKV, OFF_RKV), blk(W_WA, OFF_WA), blk(W_RWG, OFF_RWG),
                  st(RW_RKV), st(LANE), s_spec,
                  par(RW_RKV), par(LANE), par(RW_WIDTH), mat, par(RW_WIDTH), mat,
                  par(RW_WIDTH), par(RW_WIDTH), par(RW_WIDTH), par(RW_WIDTH), par(RW_WIDTH),
                  pl.BlockSpec((2 * LANE, LANE), lambda b, j: (0, 0))],
        out_specs=(pl.BlockSpec((bb_n, BLK, RW_WIDTH), lambda b, j: (b, j, 0)), s_spec,
                   st(RW_RKV), st(LANE)),
        scratch_shapes=[pltpu.VMEM((bb_n, RW_PAIRS, RW_HEAD_DIM, LANE), F32),
                        pltpu.VMEM((bb_n, 1, RW_RKV), F32), pltpu.VMEM((bb_n, 1, LANE), F32)],
        compiler_params=_cparams(("parallel", "arbitrary")),
        name="rwkv",
    )(p3, p3, p3, sh0_rkv, sh0_wa, s0, lp["mu_rkv"], lp["mu_wa"], lp["rw_w0"], lp["rw_w2p"],
      lp["rw_a0"], lp["rw_a2p"], lp["rw_kk"], lp["rw_ka"], lp["rw_rk"], lp["rw_ln_g"],
      lp["rw_ln_b"], lp["ones2"])


def _cmul(ar, ai, br, bi):
    return ar * br - ai * bi, ar * bi + ai * br


def _s5_prep_kernel(are_ref, aim_ref, ldt_ref, bre_ref, bim_ref,
                    bbre_out, bbim_out, lvre_out, lvim_out, pwre_out, pwim_out):
    ar, ai = are_ref[...], aim_ref[...]
    dt = jnp.exp(ldt_ref[...])
    mag = jnp.exp(ar * dt)
    lr, li = mag * jnp.cos(ai * dt), mag * jnp.sin(ai * dt)
    nr, ni = lr - 1.0, li
    den = ar * ar + ai * ai
    qr, qi = (nr * ar + ni * ai) / den, (ni * ar - nr * ai) / den
    br, bi = bre_ref[...], bim_ref[...]
    bbr, bbi = _cmul(qr[:, None, :], qi[:, None, :], br, bi)
    bbre_out[...] = bbr
    bbim_out[...] = bbi
    sq_r, sq_i = lr, li
    pows = [(lr, li)]
    for lvl in range(SCAN_LEVELS):
        lvre_out[lvl] = sq_r
        lvim_out[lvl] = sq_i
        pows = pows + [_cmul(pr, pi, sq_r, sq_i) for pr, pi in pows]
        sq_r, sq_i = _cmul(sq_r, sq_i, sq_r, sq_i)
    for t in range(SUBLANES):
        pwre_out[t] = pows[t][0]
        pwim_out[t] = pows[t][1]


def _s5_prep(a_re, a_im, log_dt, b_re_t, b_im_t):
    ghp = jax.ShapeDtypeStruct((S5_GROUPS, S5_GROUP_CH, S5_STATE), F32)
    lv = jax.ShapeDtypeStruct((SCAN_LEVELS, S5_GROUPS, S5_STATE), F32)
    pw = jax.ShapeDtypeStruct((SUBLANES, S5_GROUPS, S5_STATE), F32)
    return pl.pallas_call(
        _s5_prep_kernel,
        out_shape=(ghp, ghp, lv, lv, pw, pw),
        name="s5_prep",
    )(a_re, a_im, log_dt, b_re_t, b_im_t)


def _gelu_tanh(x):
    return 0.5 * x * (1.0 + jnp.tanh(math.sqrt(2.0 / math.pi) * (x + 0.044715 * (x * x * x))))


def _s5_kernel(p_ref, h0_ref, wb_ref, wc_ref, lv_ref, pw_ref, d_ref, wglu_ref, bglu_ref,
               y_ref, h_out, h_scr, hs_scr, *, tb):
    j = pl.program_id(1)

    @pl.when(j == 0)
    def _():
        h_scr[...] = h0_ref[...]

    u = p_ref[:, 0:S5_WIDTH]
    gate = p_ref[:, S5_WIDTH:2 * S5_WIDTH]
    y_parts = []
    for jb in range(S5_NBLK):
        bu = _bdot(u[:, jb * LANE:(jb + 1) * LANE], wb_ref[jb])
        for s in range(tb // BLK):
            xr = bu[s * BLK:(s + 1) * BLK, 0:S5_BLK_STATE]
            xi = bu[s * BLK:(s + 1) * BLK, S5_BLK_STATE:2 * S5_BLK_STATE]
            xr = xr.reshape(BLK // SUBLANES, SUBLANES, S5_BLK_STATE)
            xi = xi.reshape(BLK // SUBLANES, SUBLANES, S5_BLK_STATE)
            for lvl in range(SCAN_LEVELS):
                sr = pltpu.roll(xr, 1 << lvl, 1)
                si = pltpu.roll(xi, 1 << lvl, 1)
                lr, li = lv_ref[lvl, 2 * jb], lv_ref[lvl, 2 * jb + 1]
                xr, xi = xr + (lr * sr - li * si), xi + (lr * si + li * sr)
            xr = xr.reshape(BLK, S5_BLK_STATE)
            xi = xi.reshape(BLK, S5_BLK_STATE)
            c_r = h_scr[2 * jb:2 * jb + 1, :]
            c_i = h_scr[2 * jb + 1:2 * jb + 2, :]
            pr, pi = pw_ref[2 * jb], pw_ref[2 * jb + 1]
            for grp in range(BLK // SUBLANES):
                rows = slice(grp * SUBLANES, (grp + 1) * SUBLANES)
                hr = xr[rows, :] + (pr * c_r - pi * c_i)
                hi = xi[rows, :] + (pr * c_i + pi * c_r)
                c_r, c_i = hr[SUBLANES - 1:SUBLANES, :], hi[SUBLANES - 1:SUBLANES, :]
                out_rows = slice(s * BLK + grp * SUBLANES, s * BLK + (grp + 1) * SUBLANES)
                hs_scr[out_rows, 0:S5_BLK_STATE] = hr
                hs_scr[out_rows, S5_BLK_STATE:2 * S5_BLK_STATE] = hi
            h_scr[2 * jb:2 * jb + 1, :] = c_r
            h_scr[2 * jb + 1:2 * jb + 2, :] = c_i
        y_parts.append(_bdot(hs_scr[...], wc_ref[jb]))
    y = jnp.concatenate(y_parts, axis=-1) + d_ref[...] * u
    y = _gelu_tanh(y)
    y = y * _sigmoid(_bdot(y, wglu_ref[...]) + bglu_ref[...])
    y_ref[...] = y * _silu(gate)

    @pl.when(j == pl.num_programs(1) - 1)
    def _():
        h_out[...] = h_scr[...]


def _s5(p3, h0, lp, *, tb):
    bsz, seq_rows, _ = p3.shape
    st_spec = pl.BlockSpec((None, 2 * S5_NBLK, S5_BLK_STATE), lambda b, j: (b, 0, 0))
    full = lambda shape: pl.BlockSpec(shape, lambda b, j: (0,) * len(shape))
    return pl.pallas_call(
        functools.partial(_s5_kernel, tb=tb),
        out_shape=(jax.ShapeDtypeStruct((bsz, seq_rows, S5_WIDTH), F32),
                   jax.ShapeDtypeStruct((bsz, 2 * S5_NBLK, S5_BLK_STATE), F32)),
        grid=(bsz, seq_rows // tb),
        in_specs=[pl.BlockSpec((None, tb, W_S5), lambda b, j: (b, j, OFF_S5 // W_S5)),
                  st_spec,
                  full((S5_NBLK, LANE, 2 * S5_BLK_STATE)),
                  full((S5_NBLK, 2 * S5_BLK_STATE, LANE)),
                  full((SCAN_LEVELS, 2 * S5_NBLK, SUBLANES, S5_BLK_STATE)),
                  full((2 * S5_NBLK, SUBLANES, S5_BLK_STATE)),
                  full((1, S5_WIDTH)), full((S5_WIDTH, S5_WIDTH)), full((1, S5_WIDTH))],
        out_specs=(pl.BlockSpec((None, tb, S5_WIDTH), lambda b, j: (b, j, 0)), st_spec),
        scratch_shapes=[pltpu.VMEM((2 * S5_NBLK, S5_BLK_STATE), F32),
                        pltpu.VMEM((tb, 2 * S5_BLK_STATE), F32)],
        compiler_params=_cparams(("parallel", "arbitrary")),
        name="s5",
    )(p3, h0, lp["s5_wb"], lp["s5_wc"], lp["s5_lv"], lp["s5_pw"], lp["s5_d"],
      lp["s5_w_glu"], lp["s5_b_glu"])


def _mlstm_kernel(qk_ref, oz_ref, v_ref, gi_ref, gf_ref, conv0_ref, c0_ref, n0_ref, m0_ref,
                  cw_ref, cb_ref, bi_ref, bf_ref, lng_ref,
                  y_ref, conv_out, c_out, n_out, m_out,
                  xp_scr, c_scr, n_scr, m_scr, *, pad, bb_n):
    j = pl.program_id(1)
    halo = SUBLANES

    @pl.when(j == 0)
    def _():
        xp_scr[:, 0:halo, :] = jnp.zeros((bb_n, halo, W_QK), F32)
        xp_scr[:, halo - (ML_CONV - 1):halo, :] = conv0_ref[...]
        c_scr[...] = jnp.zeros(c_scr.shape, F32)
        c_scr[:, :, 0:ML_HEAD_DIM, 0:ML_HEAD_DIM] = c0_ref[...]
        n_scr[...] = jnp.zeros(n_scr.shape, F32)
        n_scr[:, :, :, 0:ML_HEAD_DIM] = n0_ref[...]
        m_scr[...] = m0_ref[...]

    t0 = jnp.where(j == 0, pad, 0) if pad else 0
    row1 = lax.broadcasted_iota(jnp.int32, (BLK, 1), 0)
    row_ok = row1 >= t0
    ti = lax.broadcasted_iota(jnp.int32, (BLK, BLK), 0)
    si = lax.broadcasted_iota(jnp.int32, (BLK, BLK), 1)
    pair_ok = (si <= ti) & (si >= t0)
    tril = jnp.where(si <= ti, 1.0, 0.0)
    eye = jnp.where(si == ti, 1.0, 0.0)
    ones_sq = jnp.ones((BLK, BLK), F32)
    lane_ok = lax.broadcasted_iota(jnp.int32, (1, ML_HEAD_PAD), 1) < ML_HEAD_DIM
    head = lambda a, h: a[:, h * ML_HEAD_PAD:(h + 1) * ML_HEAD_PAD]
    chains = [(bb, h) for bb in range(bb_n) for h in range(ML_HEADS)]

    log_i, log_f, b = [], [], []
    for bb in range(bb_n):
        log_i.append(gi_ref[bb] + bi_ref[...])
        log_f.append(jnp.where(row_ok, -_softplus(-(gf_ref[bb] + bf_ref[...])), 0.0))
        b.append(_hdot(tril, log_f[bb]))
    d_row = {}
    for bb, h in chains:
        x_col = log_i[bb][:, h:h + 1] - b[bb][:, h:h + 1]
        d_row[bb, h] = _hdot(ones_sq, eye * x_col)

    q, k, v = [], [], []
    for bb in range(bb_n):
        xp_scr[bb, halo:halo + BLK, :] = qk_ref[bb]
        conv = cb_ref[...] + xp_scr[bb, halo - 3:halo - 3 + BLK, :] * cw_ref[0:1, :]
        for tap in range(1, ML_CONV):
            conv = conv + xp_scr[bb, halo - 3 + tap:halo - 3 + tap + BLK, :] * cw_ref[tap:tap + 1, :]
        tail = xp_scr[bb, halo + BLK - (ML_CONV - 1):halo + BLK, :]
        xp_scr[bb, halo - (ML_CONV - 1):halo, :] = tail
        conv_out[bb] = tail
        act = _silu(conv)
        q.append(act[:, 0:ML_WIDTH_PAD])
        k.append(act[:, ML_WIDTH_PAD:2 * ML_WIDTH_PAD] * (1.0 / math.sqrt(ML_HEAD_DIM)))
        v.append(v_ref[bb])

    qk, q_c = {}, {}
    for bb, h in chains:
        qh = head(q[bb], h).astype(BF16)
        qk[bb, h] = lax.dot_general(qh, head(k[bb], h).astype(BF16), (((1,), (1,)), ((), ())),
                                    preferred_element_type=F32)
        q_c[bb, h] = jnp.dot(qh, c_scr[bb, h].astype(BF16), preferred_element_type=F32)

    g, we, keep = [], [], []
    for bb in range(bb_n):
        m_prev = m_scr[bb]
        g.append(b[bb] + m_prev)
        b_end = b[bb][BLK - 1:BLK, :]
        e_log = jnp.where(row_ok, b_end - b[bb] + log_i[bb], -jnp.inf)
        m_new = jnp.maximum(b_end + m_prev, jnp.max(e_log, axis=0, keepdims=True))
        we.append(jnp.exp(e_log - m_new))
        keep.append(jnp.exp(b_end + m_prev - m_new))
        m_scr[bb] = m_new

    s_mat, m_row, w_inter = {}, {}, {}
    for bb, h in chains:
        d = b[bb][:, h:h + 1] + d_row[bb, h]
        d = jnp.where(pair_ok, d, -jnp.inf)
        g_col = g[bb][:, h:h + 1]
        m_row[bb, h] = jnp.maximum(g_col, jnp.max(d, axis=1, keepdims=True))
        s_mat[bb, h] = qk[bb, h] * jnp.exp(d - m_row[bb, h])
        w_inter[bb, h] = jnp.exp(g_col - m_row[bb, h])

    s_v, k_v = {}, {}
    for bb, h in chains:
        vh = head(v[bb], h)
        s_v[bb, h] = _bdot(s_mat[bb, h], vh)
        k_v[bb, h] = lax.dot_general(head(k[bb], h).astype(BF16),
                                     (we[bb][:, h:h + 1] * vh).astype(BF16),
                                     (((0,), (0,)), ((), ())), preferred_element_type=F32)

    for bb, h in chains:
        hs = slice(h * ML_HEAD_PAD, (h + 1) * ML_HEAD_PAD)
        kh = head(k[bb], h)
        n_h = n_scr[bb, h]
        num = s_v[bb, h] + w_inter[bb, h] * q_c[bb, h]
        qn = jnp.sum(head(q[bb], h) * n_h, axis=1, keepdims=True)
        den = jnp.sum(s_mat[bb, h], axis=1, keepdims=True) + w_inter[bb, h] * qn
        hh = num / jnp.maximum(jnp.abs(den), jnp.exp(-m_row[bb, h]))
        mu = jnp.sum(hh, axis=1, keepdims=True) * (1.0 / ML_HEAD_DIM)
        dv = jnp.where(lane_ok, hh - mu, 0.0)
        var = jnp.sum(dv * dv, axis=1, keepdims=True) * (1.0 / ML_HEAD_DIM)
        hn = dv * lax.rsqrt(var + LN_EPS) * lng_ref[:, hs]
        y_ref[bb, :, hs] = (_sigmoid(oz_ref[bb, :, hs]) * hn
                            * _silu(oz_ref[bb, :, ML_WIDTH_PAD + h * ML_HEAD_PAD:
                                           ML_WIDTH_PAD + (h + 1) * ML_HEAD_PAD]))
        keep_h = keep[bb][:, h:h + 1]
        c_scr[bb, h] = keep_h * c_scr[bb, h] + k_v[bb, h]
        n_scr[bb, h] = keep_h * n_h + jnp.sum(we[bb][:, h:h + 1] * kh, axis=0, keepdims=True)

    @pl.when(j == pl.num_programs(1) - 1)
    def _():
        c_out[...] = c_scr[:, :, 0:ML_HEAD_DIM, 0:ML_HEAD_DIM]
        n_out[...] = n_scr[:, :, :, 0:ML_HEAD_DIM]
        m_out[...] = m_scr[...]


def _mlstm(p3, conv0, c0_layers, n0, m0, lp, *, pad, bb_n):
    c0, c0_layer = c0_layers
    bsz, seq_rows, _ = p3.shape
    nblk = seq_rows // BLK
    blk = lambda w, off: pl.BlockSpec((bb_n, BLK, w), lambda b, j: (b, j, off // w))
    par = lambda r, w: pl.BlockSpec((r, w), lambda b, j: (0, 0))
    conv_spec = pl.BlockSpec((bb_n, ML_CONV - 1, W_QK), lambda b, j: (b, 0, 0))
    c_spec = pl.BlockSpec((bb_n, ML_HEADS, ML_HEAD_DIM, ML_HEAD_DIM), lambda b, j: (b, 0, 0, 0))
    n_spec = pl.BlockSpec((bb_n, ML_HEADS, 1, ML_HEAD_DIM), lambda b, j: (b, 0, 0, 0))
    m_spec = pl.BlockSpec((bb_n, 1, LANE), lambda b, j: (b, 0, 0))
    return pl.pallas_call(
        functools.partial(_mlstm_kernel, pad=pad, bb_n=bb_n),
        out_shape=(jax.ShapeDtypeStruct((bsz, seq_rows, ML_WIDTH_PAD), F32),
                   jax.ShapeDtypeStruct((bsz, ML_CONV - 1, W_QK), F32),
                   jax.ShapeDtypeStruct((bsz, ML_HEADS, ML_HEAD_DIM, ML_HEAD_DIM), F32),
                   jax.ShapeDtypeStruct((bsz, ML_HEADS, 1, ML_HEAD_DIM), F32),
                   jax.ShapeDtypeStruct((bsz, 1, LANE), F32)),
        grid=(bsz // bb_n, nblk),
        in_specs=[blk(W_QK, OFF_QK), blk(W_OZ, OFF_OZ), blk(W_MV, OFF_MV),
                  blk(W_MI, OFF_MI), blk(W_MF, OFF_MF),
                  conv_spec,
                  pl.BlockSpec((None, bb_n, ML_HEADS, ML_HEAD_DIM, ML_HEAD_DIM),
                               lambda b, j: (c0_layer, b, 0, 0, 0)),
                  n_spec, m_spec,
                  par(ML_CONV, W_QK), par(1, W_QK), par(1, LANE), par(1, LANE),
                  par(1, ML_WIDTH_PAD)],
        out_specs=(pl.BlockSpec((bb_n, BLK, ML_WIDTH_PAD), lambda b, j: (b, j, 0)),
                   conv_spec, c_spec, n_spec, m_spec),
        scratch_shapes=[pltpu.VMEM((bb_n, SUBLANES + BLK, W_QK), F32),
                        pltpu.VMEM((bb_n, ML_HEADS, ML_HEAD_PAD, ML_HEAD_PAD), F32),
                        pltpu.VMEM((bb_n, ML_HEADS, 1, ML_HEAD_PAD), F32),
                        pltpu.VMEM((bb_n, 1, LANE), F32)],
        compiler_params=_cparams(("parallel", "arbitrary")),
        name="mlstm",
    )(p3, p3, p3, p3, p3, conv0, c0, n0, m0, lp["ml_cw"], lp["ml_cb"], lp["ml_bi"], lp["ml_bf"],
      lp["ml_ln_g"])


def _merge_kernel(x_ref, mg_ref, yrw_ref, ys5_ref, yml_ref, bmg_ref, wrw_ref, ws5_ref, wml_ref,
                  wout_ref, g_ref, b_ref, o_ref, *, tm, seq_rows, pad):
    gates = _sigmoid(mg_ref[...] + bmg_ref[...])
    merged = (gates[:, 0:D_MODEL] * _bdot(yrw_ref[...], wrw_ref[...])
              + gates[:, D_MODEL:2 * D_MODEL] * _bdot(ys5_ref[...], ws5_ref[...])
              + gates[:, 2 * D_MODEL:3 * D_MODEL] * _bdot(yml_ref[...], wml_ref[...]))
    out = _bdot(merged, wout_ref[...])
    y = _layer_norm_rows(DN_ALPHA * x_ref[...] + out, g_ref[...], b_ref[...])
    if pad:
        y = jnp.where(_pad_row_mask(pl.program_id(0), tm, seq_rows, pad), 0.0, y)
    o_ref[...] = y


def _merge(x, p, y_rw, y_s5, y_ml, lp, *, tm, seq_rows, pad):
    n = x.shape[0]
    rows = lambda w: pl.BlockSpec((tm, w), lambda i: (i, 0))
    full = lambda r, w: pl.BlockSpec((r, w), lambda i: (0, 0))
    return pl.pallas_call(
        functools.partial(_merge_kernel, tm=tm, seq_rows=seq_rows, pad=pad),
        out_shape=jax.ShapeDtypeStruct((n, D_MODEL), F32),
        grid=(n // tm,),
        in_specs=[rows(D_MODEL), pl.BlockSpec((tm, W_MG), lambda i: (i, OFF_MG // W_MG)),
                  rows(RW_WIDTH), rows(S5_WIDTH), rows(ML_WIDTH_PAD),
                  full(1, W_MG), full(RW_WIDTH, D_MODEL), full(S5_WIDTH, D_MODEL),
                  full(ML_WIDTH_PAD, D_MODEL), full(D_MODEL, D_MODEL),
                  full(1, D_MODEL), full(1, D_MODEL)],
        out_specs=rows(D_MODEL),
        compiler_params=_cparams(("parallel",)),
        name="merge",
    )(x, p, y_rw, y_s5, y_ml, lp["b_merge"], lp["w_br_rw"], lp["w_br_s5"], lp["w_br_ml"],
      lp["w_out"], lp["ln_g"], lp["ln_b"])


def _pad_heads(w):
    lead = w.shape[:-1]
    w = w.reshape(lead + (ML_HEADS, ML_HEAD_DIM))
    w = jnp.pad(w, [(0, 0)] * len(lead) + [(0, 0), (0, ML_HEAD_PAD - ML_HEAD_DIM)])
    return w.reshape(lead + (ML_WIDTH_PAD,))


def _unpad_heads(w):
    lead = w.shape[:-1]
    return w.reshape(lead + (ML_HEADS, ML_HEAD_PAD))[..., :ML_HEAD_DIM].reshape(lead + (ML_WIDTH,))


def _pad_lanes(w, width=LANE):
    return jnp.pad(w, [(0, 0)] * (w.ndim - 1) + [(0, width - w.shape[-1])])


def _qk_pad(w):
    return jnp.concatenate([_pad_heads(w[..., :ML_WIDTH]), _pad_heads(w[..., ML_WIDTH:])], axis=-1)


def _layer_params(l, w_in, rw_mu, rw_w0, rw_w2, rw_a0, rw_a2, rw_kk, rw_ka, rw_rk, rw_ln_g, rw_ln_b,
                  s5_a_re, s5_a_im, s5_b_re, s5_b_im, s5_c_re, s5_c_im, s5_d, s5_log_dt, s5_w_glu,
                  s5_b_glu, ml_conv_w, ml_conv_b, ml_b_if, ml_ln_g, b_merge, w_br_rw, w_br_s5,
                  w_br_ml, w_out, ln_g, ln_b):
    w = w_in[l]
    o = 0
    cols = {}
    for name, size in (("rwc", RW_SHIFT), ("rwg", RW_WIDTH), ("s5", 2 * S5_WIDTH),
                       ("qk", 2 * ML_WIDTH), ("mv", ML_WIDTH), ("mi", ML_HEADS), ("mf", ML_HEADS),
                       ("mo", ML_WIDTH), ("mz", ML_WIDTH), ("mg", 3 * D_MODEL)):
        cols[name] = w[:, o:o + size]
        o += size
    w_all = jnp.concatenate([
        cols["mg"], cols["s5"], _qk_pad(cols["qk"]),
        _pad_heads(cols["mo"]), _pad_heads(cols["mz"]), _pad_heads(cols["mv"]),
        cols["rwc"][:, :RW_RKV], cols["rwg"], cols["rwc"][:, RW_RKV:],
        _pad_lanes(cols["mi"]), _pad_lanes(cols["mf"]),
        jnp.zeros((D_MODEL, P_COLS - P_USED), F32)], axis=1).astype(BF16)
    row = lambda a: a.reshape(1, -1)
    zeros_lora = jnp.zeros((RW_LORA, RW_WIDTH), F32)
    lane = jnp.arange(LANE)
    ones_blk = (lane[:, None] // RW_HEAD_DIM == lane[None, :] // RW_HEAD_DIM).astype(F32)

    bb_re, bb_im, lv_re, lv_im, pw_re, pw_im = _s5_prep(
        s5_a_re[l], s5_a_im[l], s5_log_dt[l].reshape(S5_GROUPS, 1),
        jnp.swapaxes(s5_b_re[l], 1, 2), jnp.swapaxes(s5_b_im[l], 1, 2))
    eye8 = jnp.eye(S5_BLK_GROUPS, dtype=F32)
    blocked = lambda a: a.reshape((S5_NBLK, S5_BLK_GROUPS) + a.shape[1:])
    bb = jnp.stack([blocked(bb_re), blocked(bb_im)])
    wb = jnp.einsum("cjghp,gk->jghckp", bb, eye8).reshape(S5_NBLK, LANE, 2 * S5_BLK_STATE)
    cc = jnp.stack([blocked(s5_c_re[l]), -blocked(s5_c_im[l])])
    wc = jnp.einsum("cjghp,gk->jcgpkh", cc, eye8).reshape(S5_NBLK, 2 * S5_BLK_STATE, LANE)
    state_rows = lambda re, im: jnp.stack(
        [re.reshape(re.shape[:-2] + (S5_NBLK, S5_BLK_STATE)),
         im.reshape(im.shape[:-2] + (S5_NBLK, S5_BLK_STATE))], axis=-2)
    lv = state_rows(lv_re, lv_im).reshape(SCAN_LEVELS, 2 * S5_NBLK, 1, S5_BLK_STATE)
    row_in_group = jnp.arange(SUBLANES)[None, None, :, None]
    lv = jnp.where(row_in_group >= (1 << jnp.arange(SCAN_LEVELS))[:, None, None, None], lv, 0.0)
    pw = jnp.moveaxis(state_rows(pw_re, pw_im).reshape(SUBLANES, 2 * S5_NBLK, S5_BLK_STATE), 0, 1)

    return dict(
        w_all=w_all,
        mu_rkv=row(rw_mu[l][:RW_RKV]), mu_wa=row(rw_mu[l][RW_RKV:]),
        rw_w0=row(rw_w0[l]), rw_a0=row(rw_a0[l]),
        rw_w2p=jnp.concatenate([rw_w2[l], zeros_lora], axis=0),
        rw_a2p=jnp.concatenate([zeros_lora, rw_a2[l]], axis=0),
        rw_kk=row(rw_kk[l]), rw_ka=row(rw_ka[l]), rw_rk=row(rw_rk[l]),
        rw_ln_g=row(rw_ln_g[l]), rw_ln_b=row(rw_ln_b[l]),
        ones2=jnp.concatenate([ones_blk, ones_blk], axis=0).astype(BF16),
        s5_wb=wb.astype(BF16), s5_wc=wc.astype(BF16), s5_lv=lv, s5_pw=pw,
        s5_d=row(s5_d[l]), s5_w_glu=s5_w_glu[l].astype(BF16), s5_b_glu=row(s5_b_glu[l]),
        ml_cw=_qk_pad(ml_conv_w[l]), ml_cb=row(_qk_pad(ml_conv_b[l])),
        ml_bi=row(_pad_lanes(ml_b_if[l][:ML_HEADS])), ml_bf=row(_pad_lanes(ml_b_if[l][ML_HEADS:])),
        ml_ln_g=row(_pad_heads(ml_ln_g[l])),
        b_merge=row(b_merge[l]), w_br_rw=w_br_rw[l].astype(BF16), w_br_s5=w_br_s5[l].astype(BF16),
        w_br_ml=jnp.pad(w_br_ml[l].reshape(ML_HEADS, ML_HEAD_DIM, D_MODEL),
                        ((0, 0), (0, ML_HEAD_PAD - ML_HEAD_DIM), (0, 0))
                        ).reshape(ML_WIDTH_PAD, D_MODEL).astype(BF16),
        w_out=w_out[l].astype(BF16), ln_g=row(ln_g[l]), ln_b=row(ln_b[l]))


def _pack_s5(re, im):
    b = re.shape[0]
    return jnp.stack([re.reshape(b, S5_NBLK, S5_BLK_STATE), im.reshape(b, S5_NBLK, S5_BLK_STATE)],
                     axis=2).reshape(b, 2 * S5_NBLK, S5_BLK_STATE)


def _unpack_s5(h):
    b = h.shape[0]
    h = h.reshape(b, S5_NBLK, 2, S5_BLK_STATE)
    return (h[:, :, 0].reshape(b, S5_GROUPS, S5_STATE), h[:, :, 1].reshape(b, S5_GROUPS, S5_STATE))


def _row_tile(n, seq_rows, target):
    best = 8
    for t in range(8, min(n, target) + 1, 8):
        if seq_rows % t == 0 or (t % seq_rows == 0 and n % t == 0):
            best = t
    return best


def _trunk_layer(x, st, lp, *, bsz, seq_rows, pad, bb_n, s5_tb, tm_proj, tm_merge):
    rw_shift0, rw_wkv0, s5_re0, s5_im0, ml_conv0, ml_c0, ml_n0, ml_m0 = st
    p = _proj(x, lp["w_all"], tm=tm_proj)
    p3 = p.reshape(bsz, seq_rows, P_COLS)
    n = bsz * seq_rows

    y_rw, wkv1, sh_rkv1, sh_wa1 = _rwkv(p3, rw_shift0[:, None, :RW_RKV], rw_shift0[:, None, RW_RKV:],
                                        rw_wkv0, lp, bb_n=4 if bsz % 4 == 0 else bb_n)
    rw_shift1 = jnp.concatenate([sh_rkv1[:, 0], sh_wa1[:, 0]], axis=-1)

    y_s5, h1 = _s5(p3, _pack_s5(s5_re0, s5_im0), lp, tb=s5_tb)
    s5_re1, s5_im1 = _unpack_s5(h1)

    y_ml, conv1, c1, n1, m1 = _mlstm(
        p3, _qk_pad(ml_conv0), ml_c0, ml_n0[:, :, None, :], _pad_lanes(ml_m0)[:, None, :], lp, pad=pad,
        bb_n=bb_n)
    ml_conv1 = jnp.concatenate([_unpad_heads(conv1[..., :ML_WIDTH_PAD]),
                                _unpad_heads(conv1[..., ML_WIDTH_PAD:])], axis=-1)

    x_new = _merge(x, p, y_rw.reshape(n, RW_WIDTH), y_s5.reshape(n, S5_WIDTH),
                   y_ml.reshape(n, ML_WIDTH_PAD), lp, tm=tm_merge, seq_rows=seq_rows, pad=pad)
    return x_new, (rw_shift1, wkv1, s5_re1, s5_im1, ml_conv1, c1, n1[:, :, 0, :],
                   m1[:, 0, :ML_HEADS])


def _run_group(x_rows, states, lps, *, bsz, seq_rows, pad, in_ln_g, in_ln_b):
    n = bsz * seq_rows
    bb_n = 2 if bsz % 2 == 0 else 1
    s5_tb = max(t for t in (BLK, 5 * BLK) if seq_rows % t == 0)
    tm_ln = _row_tile(n, seq_rows, 1024)
    tm_proj = _row_tile(n, seq_rows, 2080)
    tm_merge = _row_tile(n, seq_rows, 320)
    x = _ln_in(x_rows, in_ln_g, in_ln_b, tm=tm_ln, seq_rows=seq_rows, pad=pad)
    outs = []
    for l in range(DEPTH):
        x, st = _trunk_layer(x, states[l], lps[l], bsz=bsz, seq_rows=seq_rows, pad=pad, bb_n=bb_n,
                             s5_tb=s5_tb, tm_proj=tm_proj, tm_merge=tm_merge)
        outs.append(st)
    return x, outs


def kernel(x_prompt, x_sample, state_rwkv_shift, state_rwkv_wkv, state_s5_re, state_s5_im, state_mlstm_conv, state_mlstm_c, state_mlstm_n, state_mlstm_m, meta, in_ln_g, in_ln_b, w_in, rw_mu, rw_w0, rw_w2, rw_a0, rw_a2, rw_kk, rw_ka, rw_rk, rw_ln_g, rw_ln_b, s5_a_re, s5_a_im, s5_b_re, s5_b_im, s5_c_re, s5_c_im, s5_d, s5_log_dt, s5_w_glu, s5_b_glu, ml_conv_w, ml_conv_b, ml_b_if, ml_ln_g, b_merge, w_br_rw, w_br_s5, w_br_ml, w_out, ln_g, ln_b):
    lps = [_layer_params(l, w_in, rw_mu, rw_w0, rw_w2, rw_a0, rw_a2, rw_kk, rw_ka, rw_rk, rw_ln_g,
                         rw_ln_b, s5_a_re, s5_a_im, s5_b_re, s5_b_im, s5_c_re, s5_c_im, s5_d,
                         s5_log_dt, s5_w_glu, s5_b_glu, ml_conv_w, ml_conv_b, ml_b_if, ml_ln_g,
                         b_merge, w_br_rw, w_br_s5, w_br_ml, w_out, ln_g, ln_b)
           for l in range(DEPTH)]
    g_in, b_in = in_ln_g.reshape(1, D_MODEL), in_ln_b.reshape(1, D_MODEL)

    bp, sp = x_prompt.shape[0], x_prompt.shape[1]
    lp_rows = PAD + N_META + sp
    xp = jnp.concatenate([jnp.zeros((bp, PAD, D_MODEL), F32),
                          jnp.broadcast_to(meta[None], (bp, N_META, D_MODEL)), x_prompt], axis=1)
    z = lambda *shape: jnp.zeros((bp,) + shape, F32)
    zero_state = (z(RW_SHIFT), z(RW_HEADS, RW_HEAD_DIM, RW_HEAD_DIM), z(S5_GROUPS, S5_STATE),
                  z(S5_GROUPS, S5_STATE), z(ML_CONV - 1, 2 * ML_WIDTH),
                  (jnp.zeros((1, bp, ML_HEADS, ML_HEAD_DIM, ML_HEAD_DIM), F32), 0),
                  z(ML_HEADS, ML_HEAD_DIM), z(ML_HEADS))
    yp, p_states = _run_group(xp.reshape(bp * lp_rows, D_MODEL), [zero_state] * DEPTH, lps,
                              bsz=bp, seq_rows=lp_rows, pad=PAD, in_ln_g=g_in, in_ln_b=b_in)
    y_prompt = yp.reshape(bp, lp_rows, D_MODEL)[:, PAD + N_META:]

    bs, ds = x_sample.shape[0], x_sample.shape[1]
    s_in = [(state_rwkv_shift[l], state_rwkv_wkv[l], state_s5_re[l], state_s5_im[l],
             state_mlstm_conv[l], (state_mlstm_c, l), state_mlstm_n[l], state_mlstm_m[l])
            for l in range(DEPTH)]
    ys, s_states = _run_group(x_sample.reshape(bs * ds, D_MODEL), s_in, lps,
                              bsz=bs, seq_rows=ds, pad=0, in_ln_g=g_in, in_ln_b=b_in)
    y_sample = ys.reshape(bs, ds, D_MODEL)

    stack = lambda sts: tuple(jnp.stack(s, 0) for s in zip(*sts))
    return (y_prompt, y_sample) + stack(p_states) + stack(s_states)
```

```python
import functools
import math

import jax
import jax.numpy as jnp
from jax import lax
from jax.experimental import pallas as pl
from jax.experimental.pallas import tpu as pltpu

F32 = jnp.float32
BF16 = jnp.bfloat16

D_MODEL = 1024
DEPTH = 2
N_META = 16
RW_HEADS = 12
RW_HEAD_DIM = 64
RW_WIDTH = RW_HEADS * RW_HEAD_DIM
RW_PAIRS = RW_HEADS // 2
RW_LORA = 64
RW_RKV = 3 * RW_WIDTH
RW_SHIFT = RW_RKV + 2 * RW_LORA
S5_GROUPS = 32
S5_GROUP_CH = 16
S5_WIDTH = S5_GROUPS * S5_GROUP_CH
S5_STATE = 64
S5_NBLK = 4
S5_BLK_STATE = 512
ML_HEADS = 4
ML_HEAD_DIM = 192
ML_HEAD_PAD = 256
ML_WIDTH = ML_HEADS * ML_HEAD_DIM
ML_WIDTH_PAD = ML_HEADS * ML_HEAD_PAD
ML_CONV = 4
DN_ALPHA = (2 * DEPTH) ** 0.25
LN_EPS = 1e-5
RW_GN_EPS = 64e-5

LANE = 128
SUBLANES = 8
BLK = 64
PAD = BLK - N_META
SCAN_LEVELS = 3
SOLVE_BLK = 8
SOLVE_LEVELS = 3
S5_BLK_GROUPS = LANE // S5_GROUP_CH

W_MG, W_S5, W_QK, W_OZ, W_MV = 3 * D_MODEL, 2 * S5_WIDTH, 2 * ML_WIDTH_PAD, 2 * ML_WIDTH_PAD, ML_WIDTH_PAD
W_RKV, W_RWG, W_WA, W_MI, W_MF = RW_RKV, RW_WIDTH, LANE, LANE, LANE
OFF_MG = 0
OFF_S5 = OFF_MG + W_MG
OFF_QK = OFF_S5 + W_S5
OFF_OZ = OFF_QK + W_QK
OFF_MV = OFF_OZ + W_OZ
OFF_RKV = OFF_MV + W_MV
OFF_RWG = OFF_RKV + W_RKV
OFF_WA = OFF_RWG + W_RWG
OFF_MI = OFF_WA + W_WA
OFF_MF = OFF_MI + W_MI
P_USED = OFF_MF + W_MF
MXU_COLS = 256
P_TN = 5 * MXU_COLS
P_COLS = -(-P_USED // P_TN) * P_TN
assert all(off % w == 0 for off, w in (
    (OFF_MG, W_MG), (OFF_S5, W_S5), (OFF_QK, W_QK), (OFF_OZ, W_OZ), (OFF_MV, W_MV),
    (OFF_RKV, W_RKV), (OFF_RWG, W_RWG), (OFF_WA, W_WA), (OFF_MI, W_MI), (OFF_MF, W_MF)))

VMEM_LIMIT = 56 * 1024 * 1024


def _cparams(sem):
    return pltpu.CompilerParams(dimension_semantics=sem, vmem_limit_bytes=VMEM_LIMIT)


def _bdot(a, b):
    return jnp.dot(a.astype(BF16), b.astype(BF16), preferred_element_type=F32)


def _hdot(a, b):
    return jnp.dot(a, b, precision=lax.Precision.HIGHEST, preferred_element_type=F32)


def _dot01(a01, b):
    hi = b.astype(BF16)
    rest = b - hi.astype(F32)
    mid = rest.astype(BF16)
    lo = (rest - mid.astype(F32)).astype(BF16)
    a = a01.astype(BF16)
    return jnp.dot(jnp.concatenate([a, a, a], axis=1), jnp.concatenate([hi, mid, lo], axis=0),
                   preferred_element_type=F32)


def _sigmoid(x):
    return 1.0 / (1.0 + jnp.exp(-x))


def _silu(x):
    return x * _sigmoid(x)


def _softplus(x):
    return jnp.maximum(x, 0.0) + jnp.log1p(jnp.exp(-jnp.abs(x)))


def _pad_row_mask(tile_idx, tm, seq_rows, pad):
    pos0 = lax.rem(tile_idx * tm, seq_rows)
    row = lax.broadcasted_iota(jnp.int32, (tm, 1), 0) + pos0
    return row < pad


def _layer_norm_rows(x, g, b):
    mu = jnp.mean(x, axis=-1, keepdims=True)
    d = x - mu
    var = jnp.mean(d * d, axis=-1, keepdims=True)
    return d * lax.rsqrt(var + LN_EPS) * g + b


def _ln_in_kernel(x_ref, g_ref, b_ref, o_ref, *, tm, seq_rows, pad):
    y = _layer_norm_rows(x_ref[...], g_ref[...], b_ref[...])
    if pad:
        y = jnp.where(_pad_row_mask(pl.program_id(0), tm, seq_rows, pad), 0.0, y)
    o_ref[...] = y


def _ln_in(x, g, b, *, tm, seq_rows, pad):
    n = x.shape[0]
    return pl.pallas_call(
        functools.partial(_ln_in_kernel, tm=tm, seq_rows=seq_rows, pad=pad),
        out_shape=jax.ShapeDtypeStruct((n, D_MODEL), F32),
        grid=(n // tm,),
        in_specs=[pl.BlockSpec((tm, D_MODEL), lambda i: (i, 0)),
                  pl.BlockSpec((1, D_MODEL), lambda i: (0, 0)),
                  pl.BlockSpec((1, D_MODEL), lambda i: (0, 0))],
        out_specs=pl.BlockSpec((tm, D_MODEL), lambda i: (i, 0)),
        compiler_params=_cparams(("parallel",)),
        name="ln_in",
    )(x, g, b)


def _proj_kernel(x_ref, w_ref, o_ref):
    o_ref[...] = jnp.dot(x_ref[...].astype(BF16), w_ref[...], preferred_element_type=F32)


def _proj(x, w_all, *, tm):
    n = x.shape[0]
    return pl.pallas_call(
        _proj_kernel,
        out_shape=jax.ShapeDtypeStruct((n, P_COLS), F32),
        grid=(n // tm, P_COLS // P_TN),
        in_specs=[pl.BlockSpec((tm, D_MODEL), lambda i, j: (i, 0)),
                  pl.BlockSpec((D_MODEL, P_TN), lambda i, j: (0, j))],
        out_specs=pl.BlockSpec((tm, P_TN), lambda i, j: (i, j)),
        compiler_params=_cparams(("parallel", "arbitrary")),
        name="proj",
    )(x, w_all)


def _split_lhs(hi_f32, lo_f32):
    return jnp.concatenate([hi_f32.astype(BF16), lo_f32.astype(BF16)], axis=1)


def _seg_lhs(x):
    hi = x.astype(BF16).astype(F32)
    return _split_lhs(hi, x - hi)


def _head_sum(x, ones2):
    parts = [jnp.dot(_seg_lhs(x[:, i * LANE:(i + 1) * LANE]), ones2, preferred_element_type=F32)
             for i in range(RW_WIDTH // LANE)]
    return jnp.concatenate(parts, axis=-1)


def _shift_rows(u, carry):
    rolled = pltpu.roll(u, 1, 0)
    row = lax.broadcasted_iota(jnp.int32, u.shape, 0)
    return jnp.where(row == 0, carry, rolled)


def _hi_lo(a):
    hi = a.astype(BF16)
    return hi, (a - hi.astype(F32)).astype(BF16)


def _lhs3(hl):
    return jnp.concatenate([hl[0], hl[1], hl[0]], axis=1)


def _rhs3_rows(hl):
    return jnp.concatenate([hl[0], hl[0], hl[1]], axis=1)


def _rhs3(hl):
    return jnp.concatenate([hl[0], hl[0], hl[1]], axis=0)


def _dot_nt(lhs, rhs_rows):
    return lax.dot_general(lhs, rhs_rows, (((1,), (1,)), ((), ())), preferred_element_type=F32)


def _dot_nn(lhs, rhs):
    return jnp.dot(lhs, rhs, preferred_element_type=F32)


def _rw_prepare(u, uw, carry_rkv, carry_wa, mu_rkv, mu_wa, w0, w2p, a0, a2p, kk_gain, ka_gain,
                rk_gain, ones2):
    xs = u + (_shift_rows(u, carry_rkv) - u) * mu_rkv
    xwa = uw + (_shift_rows(uw, carry_wa) - uw) * mu_wa
    r = xs[:, 0:RW_WIDTH]
    k = xs[:, RW_WIDTH:2 * RW_WIDTH]
    v = xs[:, 2 * RW_WIDTH:3 * RW_WIDTH]
    log_decay = -math.exp(-0.5) * _sigmoid(w0 + _bdot(jnp.tanh(xwa), w2p))
    a = _sigmoid(a0 + _bdot(xwa, a2p))
    kk = k * kk_gain
    kk = kk * lax.rsqrt(_head_sum(kk * kk, ones2) + 1e-12)
    k = k * (1.0 + (a - 1.0) * ka_gain)
    kka = kk * a
    ti = lax.broadcasted_iota(jnp.int32, (BLK, BLK), 0)
    si = lax.broadcasted_iota(jnp.int32, (BLK, BLK), 1)
    cum = _dot01(jnp.where(si <= ti, 1.0, 0.0), log_decay)
    cum_end = cum[BLK - 1:BLK, :]
    inv_w = jnp.exp(-cum)
    tail_w = jnp.exp(cum_end - cum)
    return dict(at=-kk * jnp.exp(cum - log_decay), rt=r * jnp.exp(cum), bh=kka * inv_w, kh=k * inv_w,
                bc=kka * tail_w, kc=k * tail_w, v=v, wt=jnp.exp(cum_end),
                bonus=_head_sum(r * k * rk_gain, ones2) * v)


def _rw_kernel(rkv_ref, wa_ref, gate_ref, sh0_rkv_ref, sh0_wa_ref, s0_ref,
               mu_rkv_ref, mu_wa_ref, w0_ref, w2_ref, a0_ref, a2_ref, kk_ref, ka_ref, rk_ref,
               lng_ref, lnb_ref, ones_ref,
               y_ref, s_out, sh_rkv_out, sh_wa_out, s_scr, c_rkv, c_wa, *, bb_n):
    j = pl.program_id(1)

    @pl.when(j == 0)
    def _():
        for bb in range(bb_n):
            for p in range(RW_PAIRS):
                s_scr[bb, p] = jnp.concatenate([s0_ref[bb, 2 * p], s0_ref[bb, 2 * p + 1]], axis=1)
        c_rkv[...] = sh0_rkv_ref[...]
        c_wa[...] = sh0_wa_ref[...]

    ones2 = ones_ref[...]
    rows = []
    for bb in range(bb_n):
        u, uw = rkv_ref[bb], wa_ref[bb]
        rows.append(_rw_prepare(u, uw, c_rkv[bb], c_wa[bb], mu_rkv_ref[...], mu_wa_ref[...],
                                w0_ref[...], w2_ref[...], a0_ref[...], a2_ref[...], kk_ref[...],
                                ka_ref[...], rk_ref[...], ones2))
        for carry, out, src in ((c_rkv, sh_rkv_out, u), (c_wa, sh_wa_out, uw)):
            carry[bb] = src[BLK - 1:BLK, :]
            out[bb] = src[BLK - 1:BLK, :]

    lo_lane = lax.broadcasted_iota(jnp.int32, (1, LANE), 1) < RW_HEAD_DIM
    row_j = lax.broadcasted_iota(jnp.int32, (RW_HEAD_DIM, LANE), 0)
    col_t = lax.broadcasted_iota(jnp.int32, (RW_HEAD_DIM, LANE), 1) & (RW_HEAD_DIM - 1)
    strict = row_j < col_t
    incl = row_j <= col_t
    same_blk = (row_j // SOLVE_BLK) == (col_t // SOLVE_BLK)
    ident = jnp.where(row_j == col_t, 1.0, 0.0)

    def by_head(x):
        return jnp.concatenate([jnp.where(lo_lane, x, 0.0), jnp.where(lo_lane, 0.0, x)], axis=0)

    pick = jnp.where(row_j == col_t, 1.0, 0.0).astype(BF16)
    pick2 = jnp.concatenate([pick, pick], axis=1)

    bd = lambda hl: (by_head(hl[0]), by_head(hl[1]))
    stack_rows = lambda parts: tuple(jnp.concatenate(x, axis=0) for x in zip(*parts))
    stack_lanes = lambda parts: tuple(jnp.concatenate(x, axis=1) for x in zip(*parts))
    wide_rhs = lambda w: _rhs3(bd(_hi_lo(w)))
    wide_dot = lambda a, rhs3: _dot_nn(_lhs3(_hi_lo(a)), rhs3)

    def pair_transpose(x):
        return _dot_nt(pick2, jnp.concatenate(bd(_hi_lo(x)), axis=1))

    chains = [(bb, p) for bb in range(bb_n) for p in range(RW_PAIRS)]
    n_ch = len(chains)
    tile = lambda name, c: rows[chains[c][0]][name][:, chains[c][1] * LANE:(chains[c][1] + 1) * LANE]
    tile_hl = lambda name, c: _hi_lo(tile(name, c))

    ar_rows, gb_w, gd, gk_hk_kc3, bc3 = [], [], [], [], []
    for c in range(n_ch):
        ar_rows.append(_rhs3_rows(stack_rows([bd(tile_hl("at", c)), bd(tile_hl("rt", c))])))
        bk = stack_rows([tile_hl("bh", c), tile_hl("kh", c)])
        gram = _dot_nt(_lhs3(bk), ar_rows[c])
        gb = jnp.where(strict, gram[0:RW_HEAD_DIM, 0:LANE], 0.0)
        hb = jnp.where(incl, gram[0:RW_HEAD_DIM, LANE:2 * LANE], 0.0)
        gk = jnp.where(strict, gram[RW_HEAD_DIM:LANE, 0:LANE], 0.0)
        hk = jnp.where(incl, gram[RW_HEAD_DIM:LANE, LANE:2 * LANE], 0.0)
        gb_w.append(gb)
        gd.append(jnp.where(same_blk, gb, 0.0))
        gk_hk_kc3.append(_rhs3(stack_lanes([bd(_hi_lo(gk)), bd(_hi_lo(hk)), bd(tile_hl("kc", c))])))
        bc3.append(_rhs3(stack_lanes([bd(_hi_lo(hb)), bd(tile_hl("bc", c))])))
    vt = [pair_transpose(tile("v", c)) for c in range(n_ch)]
    v_terms = [_dot_nn(_lhs3(_hi_lo(vt[c])), gk_hk_kc3[c]) for c in range(n_ch)]
    t_inv = [ident + g for g in gd]
    pw = gd
    for _ in range(SOLVE_LEVELS - 1):
        pw = [wide_dot(pw[c], wide_rhs(pw[c])) for c in range(n_ch)]
        t_inv = [t_inv[c] + wide_dot(t_inv[c], wide_rhs(pw[c])) for c in range(n_ch)]
    size = SOLVE_BLK
    while size < BLK:
        off = ((row_j // (2 * size)) == (col_t // (2 * size))) & ((row_j // size) != (col_t // size))
        t_inv = [t_inv[c] + wide_dot(wide_dot(t_inv[c], wide_rhs(jnp.where(off, gb_w[c], 0.0))),
                                     wide_rhs(t_inv[c])) for c in range(n_ch)]
        size *= 2
    t3 = [wide_rhs(t) for t in t_inv]

    s_old = [s_scr[bb, p] for bb, p in chains]
    uy = [_dot_nt(_lhs3(_hi_lo(s_old[c])), ar_rows[c]) for c in range(n_ch)]
    x = [wide_dot(uy[c][:, 0:LANE] + v_terms[c][:, 0:LANE], t3[c]) for c in range(n_ch)]
    out2 = [wide_dot(x[c], bc3[c]) for c in range(n_ch)]
    y_tiles = []
    for c, (bb, p) in enumerate(chains):
        y_col = uy[c][:, LANE:2 * LANE] + out2[c][:, 0:LANE] + v_terms[c][:, LANE:2 * LANE]
        y_tiles.append(pair_transpose(y_col))
        s_scr[bb, p] = (s_old[c] * tile("wt", c) + out2[c][:, LANE:2 * LANE]
                        + v_terms[c][:, 2 * LANE:3 * LANE])

    inv = 1.0 / RW_HEAD_DIM
    for bb in range(bb_n):
        y = jnp.concatenate(y_tiles[bb * RW_PAIRS:(bb + 1) * RW_PAIRS], axis=1)
        mu = _head_sum(y, ones2) * inv
        d = y - mu
        var = _head_sum(d * d, ones2) * inv
        yn = d * lax.rsqrt(var + RW_GN_EPS) * lng_ref[...] + lnb_ref[...]
        y_ref[bb] = (yn + rows[bb]["bonus"]) * _silu(gate_ref[bb])

    @pl.when(j == pl.num_programs(1) - 1)
    def _():
        for bb in range(bb_n):
            for p in range(RW_PAIRS):
                s_out[bb, 2 * p] = s_scr[bb, p, :, 0:RW_HEAD_DIM]
                s_out[bb, 2 * p + 1] = s_scr[bb, p, :, RW_HEAD_DIM:LANE]


def _rwkv(p3, sh0_rkv, sh0_wa, s0, lp, *, bb_n):
    bsz, seq_rows, _ = p3.shape
    nblk = seq_rows // BLK
    blk = lambda w, off: pl.BlockSpec((bb_n, BLK, w), lambda b, j: (b, j, off // w))
    par = lambda w: pl.BlockSpec((1, w), lambda b, j: (0, 0))
    mat = pl.BlockSpec((LANE, RW_WIDTH), lambda b, j: (0, 0))
    st = lambda w: pl.BlockSpec((bb_n, 1, w), lambda b, j: (b, 0, 0))
    s_spec = pl.BlockSpec((bb_n, RW_HEADS, RW_HEAD_DIM, RW_HEAD_DIM), lambda b, j: (b, 0, 0, 0))
    return pl.pallas_call(
        functools.partial(_rw_kernel, bb_n=bb_n),
        out_shape=(jax.ShapeDtypeStruct((bsz, seq_rows, RW_WIDTH), F32),
                   jax.ShapeDtypeStruct((bsz, RW_HEADS, RW_HEAD_DIM, RW_HEAD_DIM), F32),
                   jax.ShapeDtypeStruct((bsz, 1, RW_RKV), F32),
                   jax.ShapeDtypeStruct((bsz, 1, LANE), F32)),
        grid=(bsz // bb_n, nblk),
        in_specs=[blk(W_RKV, OFF_RKV), blk(W_WA, OFF_WA), blk(W_RWG, OFF_RWG),
                  st(RW_RKV), st(LANE), s_spec,
                  par(RW_RKV), par(LANE), par(RW_WIDTH), mat, par(RW_WIDTH), mat,
                  par(RW_WIDTH), par(RW_WIDTH), par(RW_WIDTH), par(RW_WIDTH), par(RW_WIDTH),
                  pl.BlockSpec((2 * LANE, LANE), lambda b, j: (0, 0))],
        out_specs=(pl.BlockSpec((bb_n, BLK, RW_WIDTH), lambda b, j: (b, j, 0)), s_spec,
                   st(RW_RKV), st(LANE)),
        scratch_shapes=[pltpu.VMEM((bb_n, RW_PAIRS, RW_HEAD_DIM, LANE), F32),
                        pltpu.VMEM((bb_n, 1, RW_RKV), F32), pltpu.VMEM((bb_n, 1, LANE), F32)],
        compiler_params=_cparams(("parallel", "arbitrary")),
        name="rwkv",
    )(p3, p3, p3, sh0_rkv, sh0_wa, s0, lp["mu_rkv"], lp["mu_wa"], lp["rw_w0"], lp["rw_w2p"],
      lp["rw_a0"], lp["rw_a2p"], lp["rw_kk"], lp["rw_ka"], lp["rw_rk"], lp["rw_ln_g"],
      lp["rw_ln_b"], lp["ones2"])


def _cmul(ar, ai, br, bi):
    return ar * br - ai * bi, ar * bi + ai * br


def _s5_prep_kernel(are_ref, aim_ref, ldt_ref, bre_ref, bim_ref,
                    bbre_out, bbim_out, lvre_out, lvim_out, pwre_out, pwim_out):
    ar, ai = are_ref[...], aim_ref[...]
    dt = jnp.exp(ldt_ref[...])
    mag = jnp.exp(ar * dt)
    lr, li = mag * jnp.cos(ai * dt), mag * jnp.sin(ai * dt)
    nr, ni = lr - 1.0, li
    den = ar * ar + ai * ai
    qr, qi = (nr * ar + ni * ai) / den, (ni * ar - nr * ai) / den
    br, bi = bre_ref[...], bim_ref[...]
    bbr, bbi = _cmul(qr[:, None, :], qi[:, None, :], br, bi)
    bbre_out[...] = bbr
    bbim_out[...] = bbi
    sq_r, sq_i = lr, li
    pows = [(lr, li)]
    for lvl in range(SCAN_LEVELS):
        lvre_out[lvl] = sq_r
        lvim_out[lvl] = sq_i
        pows = pows + [_cmul(pr, pi, sq_r, sq_i) for pr, pi in pows]
        sq_r, sq_i = _cmul(sq_r, sq_i, sq_r, sq_i)
    for t in range(SUBLANES):
        pwre_out[t] = pows[t][0]
        pwim_out[t] = pows[t][1]


def _s5_prep(a_re, a_im, log_dt, b_re_t, b_im_t):
    ghp = jax.ShapeDtypeStruct((S5_GROUPS, S5_GROUP_CH, S5_STATE), F32)
    lv = jax.ShapeDtypeStruct((SCAN_LEVELS, S5_GROUPS, S5_STATE), F32)
    pw = jax.ShapeDtypeStruct((SUBLANES, S5_GROUPS, S5_STATE), F32)
    return pl.pallas_call(
        _s5_prep_kernel,
        out_shape=(ghp, ghp, lv, lv, pw, pw),
        name="s5_prep",
    )(a_re, a_im, log_dt, b_re_t, b_im_t)


def _gelu_tanh(x):
    return 0.5 * x * (1.0 + jnp.tanh(math.sqrt(2.0 / math.pi) * (x + 0.044715 * (x * x * x))))


def _s5_kernel(p_ref, h0_ref, wb_ref, wc_ref, lv_ref, pw_ref, d_ref, wglu_ref, bglu_ref,
               y_ref, h_out, h_scr, hs_scr, *, tb):
    j = pl.program_id(1)

    @pl.when(j == 0)
    def _():
        h_scr[...] = h0_ref[...]

    u = p_ref[:, 0:S5_WIDTH]
    gate = p_ref[:, S5_WIDTH:2 * S5_WIDTH]
    y_parts = []
    for jb in range(S5_NBLK):
        bu = _bdot(u[:, jb * LANE:(jb + 1) * LANE], wb_ref[jb])
        for s in range(tb // BLK):
            xr = bu[s * BLK:(s + 1) * BLK, 0:S5_BLK_STATE]
            xi = bu[s * BLK:(s + 1) * BLK, S5_BLK_STATE:2 * S5_BLK_STATE]
            xr = xr.reshape(BLK // SUBLANES, SUBLANES, S5_BLK_STATE)
            xi = xi.reshape(BLK // SUBLANES, SUBLANES, S5_BLK_STATE)
            for lvl in range(SCAN_LEVELS):
                sr = pltpu.roll(xr, 1 << lvl, 1)
                si = pltpu.roll(xi, 1 << lvl, 1)
                lr, li = lv_ref[lvl, 2 * jb], lv_ref[lvl, 2 * jb + 1]
                xr, xi = xr + (lr * sr - li * si), xi + (lr * si + li * sr)
            xr = xr.reshape(BLK, S5_BLK_STATE)
            xi = xi.reshape(BLK, S5_BLK_STATE)
            c_r = h_scr[2 * jb:2 * jb + 1, :]
            c_i = h_scr[2 * jb + 1:2 * jb + 2, :]
            pr, pi = pw_ref[2 * jb], pw_ref[2 * jb + 1]
            for grp in range(BLK // SUBLANES):
                rows = slice(grp * SUBLANES, (grp + 1) * SUBLANES)
                hr = xr[rows, :] + (pr * c_r - pi * c_i)
                hi = xi[rows, :] + (pr * c_i + pi * c_r)
                c_r, c_i = hr[SUBLANES - 1:SUBLANES, :], hi[SUBLANES - 1:SUBLANES, :]
                out_rows = slice(s * BLK + grp * SUBLANES, s * BLK + (grp + 1) * SUBLANES)
                hs_scr[out_rows, 0:S5_BLK_STATE] = hr
                hs_scr[out_rows, S5_BLK_STATE:2 * S5_BLK_STATE] = hi
            h_scr[2 * jb:2 * jb + 1, :] = c_r
            h_scr[2 * jb + 1:2 * jb + 2, :] = c_i
        y_parts.append(_bdot(hs_scr[...], wc_ref[jb]))
    y = jnp.concatenate(y_parts, axis=-1) + d_ref[...] * u
    y = _gelu_tanh(y)
    y = y * _sigmoid(_bdot(y, wglu_ref[...]) + bglu_ref[...])
    y_ref[...] = y * _silu(gate)

    @pl.when(j == pl.num_programs(1) - 1)
    def _():
        h_out[...] = h_scr[...]


def _s5(p3, h0, lp, *, tb):
    bsz, seq_rows, _ = p3.shape
    st_spec = pl.BlockSpec((None, 2 * S5_NBLK, S5_BLK_STATE), lambda b, j: (b, 0, 0))
    full = lambda shape: pl.BlockSpec(shape, lambda b, j: (0,) * len(shape))
    return pl.pallas_call(
        functools.partial(_s5_kernel, tb=tb),
        out_shape=(jax.ShapeDtypeStruct((bsz, seq_rows, S5_WIDTH), F32),
                   jax.ShapeDtypeStruct((bsz, 2 * S5_NBLK, S5_BLK_STATE), F32)),
        grid=(bsz, seq_rows // tb),
        in_specs=[pl.BlockSpec((None, tb, W_S5), lambda b, j: (b, j, OFF_S5 // W_S5)),
                  st_spec,
                  full((S5_NBLK, LANE, 2 * S5_BLK_STATE)),
                  full((S5_NBLK, 2 * S5_BLK_STATE, LANE)),
                  full((SCAN_LEVELS, 2 * S5_NBLK, SUBLANES, S5_BLK_STATE)),
                  full((2 * S5_NBLK, SUBLANES, S5_BLK_STATE)),
                  full((1, S5_WIDTH)), full((S5_WIDTH, S5_WIDTH)), full((1, S5_WIDTH))],
        out_specs=(pl.BlockSpec((None, tb, S5_WIDTH), lambda b, j: (b, j, 0)), st_spec),
        scratch_shapes=[pltpu.VMEM((2 * S5_NBLK, S5_BLK_STATE), F32),
                        pltpu.VMEM((tb, 2 * S5_BLK_STATE), F32)],
        compiler_params=_cparams(("parallel", "arbitrary")),
        name="s5",
    )(p3, h0, lp["s5_wb"], lp["s5_wc"], lp["s5_lv"], lp["s5_pw"], lp["s5_d"],
      lp["s5_w_glu"], lp["s5_b_glu"])


def _mlstm_kernel(qk_ref, oz_ref, v_ref, gi_ref, gf_ref, conv0_ref, c0_ref, n0_ref, m0_ref,
                  cw_ref, cb_ref, bi_ref, bf_ref, lng_ref,
                  y_ref, conv_out, c_out, n_out, m_out,
                  xp_scr, c_scr, n_scr, m_scr, *, pad, bb_n):
    j = pl.program_id(1)
    halo = SUBLANES

    @pl.when(j == 0)
    def _():
        xp_scr[:, 0:halo, :] = jnp.zeros((bb_n, halo, W_QK), F32)
        xp_scr[:, halo - (ML_CONV - 1):halo, :] = conv0_ref[...]
        c_scr[...] = jnp.zeros(c_scr.shape, F32)
        c_scr[:, :, 0:ML_HEAD_DIM, 0:ML_HEAD_DIM] = c0_ref[...]
        n_scr[...] = jnp.zeros(n_scr.shape, F32)
        n_scr[:, :, :, 0:ML_HEAD_DIM] = n0_ref[...]
        m_scr[...] = m0_ref[...]

    t0 = jnp.where(j == 0, pad, 0) if pad else 0
    row1 = lax.broadcasted_iota(jnp.int32, (BLK, 1), 0)
    row_ok = row1 >= t0
    ti = lax.broadcasted_iota(jnp.int32, (BLK, BLK), 0)
    si = lax.broadcasted_iota(jnp.int32, (BLK, BLK), 1)
    pair_ok = (si <= ti) & (si >= t0)
    tril = jnp.where(si <= ti, 1.0, 0.0)
    eye = jnp.where(si == ti, 1.0, 0.0)
    ones_sq = jnp.ones((BLK, BLK), F32)
    lane_ok = lax.broadcasted_iota(jnp.int32, (1, ML_HEAD_PAD), 1) < ML_HEAD_DIM
    head = lambda a, h: a[:, h * ML_HEAD_PAD:(h + 1) * ML_HEAD_PAD]
    chains = [(bb, h) for bb in range(bb_n) for h in range(ML_HEADS)]

    log_i, log_f, b = [], [], []
    for bb in range(bb_n):
        log_i.append(gi_ref[bb] + bi_ref[...])
        log_f.append(jnp.where(row_ok, -_softplus(-(gf_ref[bb] + bf_ref[...])), 0.0))
        b.append(_hdot(tril, log_f[bb]))
    d_row = {}
    for bb, h in chains:
        x_col = log_i[bb][:, h:h + 1] - b[bb][:, h:h + 1]
        d_row[bb, h] = _hdot(ones_sq, eye * x_col)

    q, k, v = [], [], []
    for bb in range(bb_n):
        xp_scr[bb, halo:halo + BLK, :] = qk_ref[bb]
        conv = cb_ref[...] + xp_scr[bb, halo - 3:halo - 3 + BLK, :] * cw_ref[0:1, :]
        for tap in range(1, ML_CONV):
            conv = conv + xp_scr[bb, halo - 3 + tap:halo - 3 + tap + BLK, :] * cw_ref[tap:tap + 1, :]
        tail = xp_scr[bb, halo + BLK - (ML_CONV - 1):halo + BLK, :]
        xp_scr[bb, halo - (ML_CONV - 1):halo, :] = tail
        conv_out[bb] = tail
        act = _silu(conv)
        q.append(act[:, 0:ML_WIDTH_PAD])
        k.append(act[:, ML_WIDTH_PAD:2 * ML_WIDTH_PAD] * (1.0 / math.sqrt(ML_HEAD_DIM)))
        v.append(v_ref[bb])

    qk, q_c = {}, {}
    for bb, h in chains:
        qh = head(q[bb], h).astype(BF16)
        qk[bb, h] = lax.dot_general(qh, head(k[bb], h).astype(BF16), (((1,), (1,)), ((), ())),
                                    preferred_element_type=F32)
        q_c[bb, h] = jnp.dot(qh, c_scr[bb, h].astype(BF16), preferred_element_type=F32)

    g, we, keep = [], [], []
    for bb in range(bb_n):
        m_prev = m_scr[bb]
        g.append(b[bb] + m_prev)
        b_end = b[bb][BLK - 1:BLK, :]
        e_log = jnp.where(row_ok, b_end - b[bb] + log_i[bb], -jnp.inf)
        m_new = jnp.maximum(b_end + m_prev, jnp.max(e_log, axis=0, keepdims=True))
        we.append(jnp.exp(e_log - m_new))
        keep.append(jnp.exp(b_end + m_prev - m_new))
        m_scr[bb] = m_new

    s_mat, m_row, w_inter = {}, {}, {}
    for bb, h in chains:
        d = b[bb][:, h:h + 1] + d_row[bb, h]
        d = jnp.where(pair_ok, d, -jnp.inf)
        g_col = g[bb][:, h:h + 1]
        m_row[bb, h] = jnp.maximum(g_col, jnp.max(d, axis=1, keepdims=True))
        s_mat[bb, h] = qk[bb, h] * jnp.exp(d - m_row[bb, h])
        w_inter[bb, h] = jnp.exp(g_col - m_row[bb, h])

    s_v, k_v = {}, {}
    for bb, h in chains:
        vh = head(v[bb], h)
        s_v[bb, h] = _bdot(s_mat[bb, h], vh)
        k_v[bb, h] = lax.dot_general(head(k[bb], h).astype(BF16),
                                     (we[bb][:, h:h + 1] * vh).astype(BF16),
                                     (((0,), (0,)), ((), ())), preferred_element_type=F32)

    for bb, h in chains:
        hs = slice(h * ML_HEAD_PAD, (h + 1) * ML_HEAD_PAD)
        kh = head(k[bb], h)
        n_h = n_scr[bb, h]
        num = s_v[bb, h] + w_inter[bb, h] * q_c[bb, h]
        qn = jnp.sum(head(q[bb], h) * n_h, axis=1, keepdims=True)
        den = jnp.sum(s_mat[bb, h], axis=1, keepdims=True) + w_inter[bb, h] * qn
        hh = num / jnp.maximum(jnp.abs(den), jnp.exp(-m_row[bb, h]))
        mu = jnp.sum(hh, axis=1, keepdims=True) * (1.0 / ML_HEAD_DIM)
        dv = jnp.where(lane_ok, hh - mu, 0.0)
        var = jnp.sum(dv * dv, axis=1, keepdims=True) * (1.0 / ML_HEAD_DIM)
        hn = dv * lax.rsqrt(var + LN_EPS) * lng_ref[:, hs]
        y_ref[bb, :, hs] = (_sigmoid(oz_ref[bb, :, hs]) * hn
                            * _silu(oz_ref[bb, :, ML_WIDTH_PAD + h * ML_HEAD_PAD:
                                           ML_WIDTH_PAD + (h + 1) * ML_HEAD_PAD]))
        keep_h = keep[bb][:, h:h + 1]
        c_scr[bb, h] = keep_h * c_scr[bb, h] + k_v[bb, h]
        n_scr[bb, h] = keep_h * n_h + jnp.sum(we[bb][:, h:h + 1] * kh, axis=0, keepdims=True)

    @pl.when(j == pl.num_programs(1) - 1)
    def _():
        c_out[...] = c_scr[:, :, 0:ML_HEAD_DIM, 0:ML_HEAD_DIM]
        n_out[...] = n_scr[:, :, :, 0:ML_HEAD_DIM]
        m_out[...] = m_scr[...]


def _mlstm(p3, conv0, c0_layers, n0, m0, lp, *, pad, bb_n):
    c0, c0_layer = c0_layers
    bsz, seq_rows, _ = p3.shape
    nblk = seq_rows // BLK
    blk = lambda w, off: pl.BlockSpec((bb_n, BLK, w), lambda b, j: (b, j, off // w))
    par = lambda r, w: pl.BlockSpec((r, w), lambda b, j: (0, 0))
    conv_spec = pl.BlockSpec((bb_n, ML_CONV - 1, W_QK), lambda b, j: (b, 0, 0))
    c_spec = pl.BlockSpec((bb_n, ML_HEADS, ML_HEAD_DIM, ML_HEAD_DIM), lambda b, j: (b, 0, 0, 0))
    n_spec = pl.BlockSpec((bb_n, ML_HEADS, 1, ML_HEAD_DIM), lambda b, j: (b, 0, 0, 0))
    m_spec = pl.BlockSpec((bb_n, 1, LANE), lambda b, j: (b, 0, 0))
    return pl.pallas_call(
        functools.partial(_mlstm_kernel, pad=pad, bb_n=bb_n),
        out_shape=(jax.ShapeDtypeStruct((bsz, seq_rows, ML_WIDTH_PAD), F32),
                   jax.ShapeDtypeStruct((bsz, ML_CONV - 1, W_QK), F32),
                   jax.ShapeDtypeStruct((bsz, ML_HEADS, ML_HEAD_DIM, ML_HEAD_DIM), F32),
                   jax.ShapeDtypeStruct((bsz, ML_HEADS, 1, ML_HEAD_DIM), F32),
                   jax.ShapeDtypeStruct((bsz, 1, LANE), F32)),
        grid=(bsz // bb_n, nblk),
        in_specs=[blk(W_QK, OFF_QK), blk(W_OZ, OFF_OZ), blk(W_MV, OFF_MV),
                  blk(W_MI, OFF_MI), blk(W_MF, OFF_MF),
                  conv_spec,
                  pl.BlockSpec((None, bb_n, ML_HEADS, ML_HEAD_DIM, ML_HEAD_DIM),
                               lambda b, j: (c0_layer, b, 0, 0, 0)),
                  n_spec, m_spec,
                  par(ML_CONV, W_QK), par(1, W_QK), par(1, LANE), par(1, LANE),
                  par(1, ML_WIDTH_PAD)],
        out_specs=(pl.BlockSpec((bb_n, BLK, ML_WIDTH_PAD), lambda b, j: (b, j, 0)),
                   conv_spec, c_spec, n_spec, m_spec),
        scratch_shapes=[pltpu.VMEM((bb_n, SUBLANES + BLK, W_QK), F32),
                        pltpu.VMEM((bb_n, ML_HEADS, ML_HEAD_PAD, ML_HEAD_PAD), F32),
                        pltpu.VMEM((bb_n, ML_HEADS, 1, ML_HEAD_PAD), F32),
                        pltpu.VMEM((bb_n, 1, LANE), F32)],
        compiler_params=_cparams(("parallel", "arbitrary")),
        name="mlstm",
    )(p3, p3, p3, p3, p3, conv0, c0, n0, m0, lp["ml_cw"], lp["ml_cb"], lp["ml_bi"], lp["ml_bf"],
      lp["ml_ln_g"])


def _merge_kernel(x_ref, mg_ref, yrw_ref, ys5_ref, yml_ref, bmg_ref, wrw_ref, ws5_ref, wml_ref,
                  wout_ref, g_ref, b_ref, o_ref, *, tm, seq_rows, pad):
    gates = _sigmoid(mg_ref[...] + bmg_ref[...])
    merged = (gates[:, 0:D_MODEL] * _bdot(yrw_ref[...], wrw_ref[...])
              + gates[:, D_MODEL:2 * D_MODEL] * _bdot(ys5_ref[...], ws5_ref[...])
              + gates[:, 2 * D_MODEL:3 * D_MODEL] * _bdot(yml_ref[...], wml_ref[...]))
    out = _bdot(merged, wout_ref[...])
    y = _layer_norm_rows(DN_ALPHA * x_ref[...] + out, g_ref[...], b_ref[...])
    if pad:
        y = jnp.where(_pad_row_mask(pl.program_id(0), tm, seq_rows, pad), 0.0, y)
    o_ref[...] = y


def _merge(x, p, y_rw, y_s5, y_ml, lp, *, tm, seq_rows, pad):
    n = x.shape[0]
    rows = lambda w: pl.BlockSpec((tm, w), lambda i: (i, 0))
    full = lambda r, w: pl.BlockSpec((r, w), lambda i: (0, 0))
    return pl.pallas_call(
        functools.partial(_merge_kernel, tm=tm, seq_rows=seq_rows, pad=pad),
        out_shape=jax.ShapeDtypeStruct((n, D_MODEL), F32),
        grid=(n // tm,),
        in_specs=[rows(D_MODEL), pl.BlockSpec((tm, W_MG), lambda i: (i, OFF_MG // W_MG)),
                  rows(RW_WIDTH), rows(S5_WIDTH), rows(ML_WIDTH_PAD),
                  full(1, W_MG), full(RW_WIDTH, D_MODEL), full(S5_WIDTH, D_MODEL),
                  full(ML_WIDTH_PAD, D_MODEL), full(D_MODEL, D_MODEL),
                  full(1, D_MODEL), full(1, D_MODEL)],
        out_specs=rows(D_MODEL),
        compiler_params=_cparams(("parallel",)),
        name="merge",
    )(x, p, y_rw, y_s5, y_ml, lp["b_merge"], lp["w_br_rw"], lp["w_br_s5"], lp["w_br_ml"],
      lp["w_out"], lp["ln_g"], lp["ln_b"])


def _pad_heads(w):
    lead = w.shape[:-1]
    w = w.reshape(lead + (ML_HEADS, ML_HEAD_DIM))
    w = jnp.pad(w, [(0, 0)] * len(lead) + [(0, 0), (0, ML_HEAD_PAD - ML_HEAD_DIM)])
    return w.reshape(lead + (ML_WIDTH_PAD,))


def _unpad_heads(w):
    lead = w.shape[:-1]
    return w.reshape(lead + (ML_HEADS, ML_HEAD_PAD))[..., :ML_HEAD_DIM].reshape(lead + (ML_WIDTH,))


def _pad_lanes(w, width=LANE):
    return jnp.pad(w, [(0, 0)] * (w.ndim - 1) + [(0, width - w.shape[-1])])


def _qk_pad(w):
    return jnp.concatenate([_pad_heads(w[..., :ML_WIDTH]), _pad_heads(w[..., ML_WIDTH:])], axis=-1)


def _layer_params(l, w_in, rw_mu, rw_w0, rw_w2, rw_a0, rw_a2, rw_kk, rw_ka, rw_rk, rw_ln_g, rw_ln_b,
                  s5_a_re, s5_a_im, s5_b_re, s5_b_im, s5_c_re, s5_c_im, s5_d, s5_log_dt, s5_w_glu,
                  s5_b_glu, ml_conv_w, ml_conv_b, ml_b_if, ml_ln_g, b_merge, w_br_rw, w_br_s5,
                  w_br_ml, w_out, ln_g, ln_b):
    w = w_in[l]
    o = 0
    cols = {}
    for name, size in (("rwc", RW_SHIFT), ("rwg", RW_WIDTH), ("s5", 2 * S5_WIDTH),
                       ("qk", 2 * ML_WIDTH), ("mv", ML_WIDTH), ("mi", ML_HEADS), ("mf", ML_HEADS),
                       ("mo", ML_WIDTH), ("mz", ML_WIDTH), ("mg", 3 * D_MODEL)):
        cols[name] = w[:, o:o + size]
        o += size
    w_all = jnp.concatenate([
        cols["mg"], cols["s5"], _qk_pad(cols["qk"]),
        _pad_heads(cols["mo"]), _pad_heads(cols["mz"]), _pad_heads(cols["mv"]),
        cols["rwc"][:, :RW_RKV], cols["rwg"], cols["rwc"][:, RW_RKV:],
        _pad_lanes(cols["mi"]), _pad_lanes(cols["mf"]),
        jnp.zeros((D_MODEL, P_COLS - P_USED), F32)], axis=1).astype(BF16)
    row = lambda a: a.reshape(1, -1)
    zeros_lora = jnp.zeros((RW_LORA, RW_WIDTH), F32)
    lane = jnp.arange(LANE)
    ones_blk = (lane[:, None] // RW_HEAD_DIM == lane[None, :] // RW_HEAD_DIM).astype(F32)

    bb_re, bb_im, lv_re, lv_im, pw_re, pw_im = _s5_prep(
        s5_a_re[l], s5_a_im[l], s5_log_dt[l].reshape(S5_GROUPS, 1),
        jnp.swapaxes(s5_b_re[l], 1, 2), jnp.swapaxes(s5_b_im[l], 1, 2))
    eye8 = jnp.eye(S5_BLK_GROUPS, dtype=F32)
    blocked = lambda a: a.reshape((S5_NBLK, S5_BLK_GROUPS) + a.shape[1:])
    bb = jnp.stack([blocked(bb_re), blocked(bb_im)])
    wb = jnp.einsum("cjghp,gk->jghckp", bb, eye8).reshape(S5_NBLK, LANE, 2 * S5_BLK_STATE)
    cc = jnp.stack([blocked(s5_c_re[l]), -blocked(s5_c_im[l])])
    wc = jnp.einsum("cjghp,gk->jcgpkh", cc, eye8).reshape(S5_NBLK, 2 * S5_BLK_STATE, LANE)
    state_rows = lambda re, im: jnp.stack(
        [re.reshape(re.shape[:-2] + (S5_NBLK, S5_BLK_STATE)),
         im.reshape(im.shape[:-2] + (S5_NBLK, S5_BLK_STATE))], axis=-2)
    lv = state_rows(lv_re, lv_im).reshape(SCAN_LEVELS, 2 * S5_NBLK, 1, S5_BLK_STATE)
    row_in_group = jnp.arange(SUBLANES)[None, None, :, None]
    lv = jnp.where(row_in_group >= (1 << jnp.arange(SCAN_LEVELS))[:, None, None, None], lv, 0.0)
    pw = jnp.moveaxis(state_rows(pw_re, pw_im).reshape(SUBLANES, 2 * S5_NBLK, S5_BLK_STATE), 0, 1)

    return dict(
        w_all=w_all,
        mu_rkv=row(rw_mu[l][:RW_RKV]), mu_wa=row(rw_mu[l][RW_RKV:]),
        rw_w0=row(rw_w0[l]), rw_a0=row(rw_a0[l]),
        rw_w2p=jnp.concatenate([rw_w2[l], zeros_lora], axis=0),
        rw_a2p=jnp.concatenate([zeros_lora, rw_a2[l]], axis=0),
        rw_kk=row(rw_kk[l]), rw_ka=row(rw_ka[l]), rw_rk=row(rw_rk[l]),
        rw_ln_g=row(rw_ln_g[l]), rw_ln_b=row(rw_ln_b[l]),
        ones2=jnp.concatenate([ones_blk, ones_blk], axis=0).astype(BF16),
        s5_wb=wb.astype(BF16), s5_wc=wc.astype(BF16), s5_lv=lv, s5_pw=pw,
        s5_d=row(s5_d[l]), s5_w_glu=s5_w_glu[l].astype(BF16), s5_b_glu=row(s5_b_glu[l]),
        ml_cw=_qk_pad(ml_conv_w[l]), ml_cb=row(_qk_pad(ml_conv_b[l])),
        ml_bi=row(_pad_lanes(ml_b_if[l][:ML_HEADS])), ml_bf=row(_pad_lanes(ml_b_if[l][ML_HEADS:])),
        ml_ln_g=row(_pad_heads(ml_ln_g[l])),
        b_merge=row(b_merge[l]), w_br_rw=w_br_rw[l].astype(BF16), w_br_s5=w_br_s5[l].astype(BF16),
        w_br_ml=jnp.pad(w_br_ml[l].reshape(ML_HEADS, ML_HEAD_DIM, D_MODEL),
                        ((0, 0), (0, ML_HEAD_PAD - ML_HEAD_DIM), (0, 0))
                        ).reshape(ML_WIDTH_PAD, D_MODEL).astype(BF16),
        w_out=w_out[l].astype(BF16), ln_g=row(ln_g[l]), ln_b=row(ln_b[l]))


def _pack_s5(re, im):
    b = re.shape[0]
    return jnp.stack([re.reshape(b, S5_NBLK, S5_BLK_STATE), im.reshape(b, S5_NBLK, S5_BLK_STATE)],
                     axis=2).reshape(b, 2 * S5_NBLK, S5_BLK_STATE)


def _unpack_s5(h):
    b = h.shape[0]
    h = h.reshape(b, S5_NBLK, 2, S5_BLK_STATE)
    return (h[:, :, 0].reshape(b, S5_GROUPS, S5_STATE), h[:, :, 1].reshape(b, S5_GROUPS, S5_STATE))


def _row_tile(n, seq_rows, target):
    best = 8
    for t in range(8, min(n, target) + 1, 8):
        if seq_rows % t == 0 or (t % seq_rows == 0 and n % t == 0):
            best = t
    return best


def _trunk_layer(x, st, lp, *, bsz, seq_rows, pad, bb_n, s5_tb, tm_proj, tm_merge):
    rw_shift0, rw_wkv0, s5_re0, s5_im0, ml_conv0, ml_c0, ml_n0, ml_m0 = st
    p = _proj(x, lp["w_all"], tm=tm_proj)
    p3 = p.reshape(bsz, seq_rows, P_COLS)
    n = bsz * seq_rows

    y_rw, wkv1, sh_rkv1, sh_wa1 = _rwkv(p3, rw_shift0[:, None, :RW_RKV], rw_shift0[:, None, RW_RKV:],
                                        rw_wkv0, lp, bb_n=4 if bsz % 4 == 0 else bb_n)
    rw_shift1 = jnp.concatenate([sh_rkv1[:, 0], sh_wa1[:, 0]], axis=-1)

    y_s5, h1 = _s5(p3, _pack_s5(s5_re0, s5_im0), lp, tb=s5_tb)
    s5_re1, s5_im1 = _unpack_s5(h1)

    y_ml, conv1, c1, n1, m1 = _mlstm(
        p3, _qk_pad(ml_conv0), ml_c0, ml_n0[:, :, None, :], _pad_lanes(ml_m0)[:, None, :], lp, pad=pad,
        bb_n=bb_n)
    ml_conv1 = jnp.concatenate([_unpad_heads(conv1[..., :ML_WIDTH_PAD]),
                                _unpad_heads(conv1[..., ML_WIDTH_PAD:])], axis=-1)

    x_new = _merge(x, p, y_rw.reshape(n, RW_WIDTH), y_s5.reshape(n, S5_WIDTH),
                   y_ml.reshape(n, ML_WIDTH_PAD), lp, tm=tm_merge, seq_rows=seq_rows, pad=pad)
    return x_new, (rw_shift1, wkv1, s5_re1, s5_im1, ml_conv1, c1, n1[:, :, 0, :],
                   m1[:, 0, :ML_HEADS])


def _run_group(x_rows, states, lps, *, bsz, seq_rows, pad, in_ln_g, in_ln_b):
    n = bsz * seq_rows
    bb_n = 2 if bsz % 2 == 0 else 1
    s5_tb = max(t for t in (BLK, 5 * BLK) if seq_rows % t == 0)
    tm_ln = _row_tile(n, seq_rows, 1024)
    tm_proj = _row_tile(n, seq_rows, 2080)
    tm_merge = _row_tile(n, seq_rows, 320)
    x = _ln_in(x_rows, in_ln_g, in_ln_b, tm=tm_ln, seq_rows=seq_rows, pad=pad)
    outs = []
    for l in range(DEPTH):
        x, st = _trunk_layer(x, states[l], lps[l], bsz=bsz, seq_rows=seq_rows, pad=pad, bb_n=bb_n,
                             s5_tb=s5_tb, tm_proj=tm_proj, tm_merge=tm_merge)
        outs.append(st)
    return x, outs


def kernel(x_prompt, x_sample, state_rwkv_shift, state_rwkv_wkv, state_s5_re, state_s5_im, state_mlstm_conv, state_mlstm_c, state_mlstm_n, state_mlstm_m, meta, in_ln_g, in_ln_b, w_in, rw_mu, rw_w0, rw_w2, rw_a0, rw_a2, rw_kk, rw_ka, rw_rk, rw_ln_g, rw_ln_b, s5_a_re, s5_a_im, s5_b_re, s5_b_im, s5_c_re, s5_c_im, s5_d, s5_log_dt, s5_w_glu, s5_b_glu, ml_conv_w, ml_conv_b, ml_b_if, ml_ln_g, b_merge, w_br_rw, w_br_s5, w_br_ml, w_out, ln_g, ln_b):
    lps = [_layer_params(l, w_in, rw_mu, rw_w0, rw_w2, rw_a0, rw_a2, rw_kk, rw_ka, rw_rk, rw_ln_g,
                         rw_ln_b, s5_a_re, s5_a_im, s5_b_re, s5_b_im, s5_c_re, s5_c_im, s5_d,
                         s5_log_dt, s5_w_glu, s5_b_glu, ml_conv_w, ml_conv_b, ml_b_if, ml_ln_g,
                         b_merge, w_br_rw, w_br_s5, w_br_ml, w_out, ln_g, ln_b)
           for l in range(DEPTH)]
    g_in, b_in = in_ln_g.reshape(1, D_MODEL), in_ln_b.reshape(1, D_MODEL)

    bp, sp = x_prompt.shape[0], x_prompt.shape[1]
    lp_rows = PAD + N_META + sp
    xp = jnp.concatenate([jnp.zeros((bp, PAD, D_MODEL), F32),
                          jnp.broadcast_to(meta[None], (bp, N_META, D_MODEL)), x_prompt], axis=1)
    z = lambda *shape: jnp.zeros((bp,) + shape, F32)
    zero_state = (z(RW_SHIFT), z(RW_HEADS, RW_HEAD_DIM, RW_HEAD_DIM), z(S5_GROUPS, S5_STATE),
                  z(S5_GROUPS, S5_STATE), z(ML_CONV - 1, 2 * ML_WIDTH),
                  (jnp.zeros((1, bp, ML_HEADS, ML_HEAD_DIM, ML_HEAD_DIM), F32), 0),
                  z(ML_HEADS, ML_HEAD_DIM), z(ML_HEADS))
    yp, p_states = _run_group(xp.reshape(bp * lp_rows, D_MODEL), [zero_state] * DEPTH, lps,
                              bsz=bp, seq_rows=lp_rows, pad=PAD, in_ln_g=g_in, in_ln_b=b_in)
    y_prompt = yp.reshape(bp, lp_rows, D_MODEL)[:, PAD + N_META:]

    bs, ds = x_sample.shape[0], x_sample.shape[1]
    s_in = [(state_rwkv_shift[l], state_rwkv_wkv[l], state_s5_re[l], state_s5_im[l],
             state_mlstm_conv[l], (state_mlstm_c, l), state_mlstm_n[l], state_mlstm_m[l])
            for l in range(DEPTH)]
    ys, s_states = _run_group(x_sample.reshape(bs * ds, D_MODEL), s_in, lps,
                              bsz=bs, seq_rows=ds, pad=0, in_ln_g=g_in, in_ln_b=b_in)
    y_sample = ys.reshape(bs, ds, D_MODEL)

    stack = lambda sts: tuple(jnp.stack(s, 0) for s in zip(*sts))
    return (y_prompt, y_sample) + stack(p_states) + stack(s_states)
```

```python
import functools
import math

import jax
import jax.numpy as jnp
from jax import lax
from jax.experimental import pallas as pl
from jax.experimental.pallas import tpu as pltpu

F32 = jnp.float32
BF16 = jnp.bfloat16

D_MODEL = 1024
DEPTH = 2
N_META = 16
RW_HEADS = 12
RW_HEAD_DIM = 64
RW_WIDTH = RW_HEADS * RW_HEAD_DIM
RW_PAIRS = RW_HEADS // 2
RW_LORA = 64
RW_RKV = 3 * RW_WIDTH
RW_SHIFT = RW_RKV + 2 * RW_LORA
S5_GROUPS = 32
S5_GROUP_CH = 16
S5_WIDTH = S5_GROUPS * S5_GROUP_CH
S5_STATE = 64
S5_NBLK = 4
S5_BLK_STATE = 512
ML_HEADS = 4
ML_HEAD_DIM = 192
ML_HEAD_PAD = 256
ML_WIDTH = ML_HEADS * ML_HEAD_DIM
ML_WIDTH_PAD = ML_HEADS * ML_HEAD_PAD
ML_CONV = 4
DN_ALPHA = (2 * DEPTH) ** 0.25
LN_EPS = 1e-5
RW_GN_EPS = 64e-5

LANE = 128
SUBLANES = 8
BLK = 64
PAD = BLK - N_META
SCAN_LEVELS = 3
SOLVE_BLK = 8
SOLVE_LEVELS = 3
S5_BLK_GROUPS = LANE // S5_GROUP_CH

W_MG, W_S5, W_QK, W_OZ, W_MV = 3 * D_MODEL, 2 * S5_WIDTH, 2 * ML_WIDTH_PAD, 2 * ML_WIDTH_PAD, ML_WIDTH_PAD
W_RKV, W_RWG, W_WA, W_MI, W_MF = RW_RKV, RW_WIDTH, LANE, LANE, LANE
OFF_MG = 0
OFF_S5 = OFF_MG + W_MG
OFF_QK = OFF_S5 + W_S5
OFF_OZ = OFF_QK + W_QK
OFF_MV = OFF_OZ + W_OZ
OFF_RKV = OFF_MV + W_MV
OFF_RWG = OFF_RKV + W_RKV
OFF_WA = OFF_RWG + W_RWG
OFF_MI = OFF_WA + W_WA
OFF_MF = OFF_MI + W_MI
P_USED = OFF_MF + W_MF
MXU_COLS = 256
P_TN = 5 * MXU_COLS
P_COLS = -(-P_USED // P_TN) * P_TN
assert all(off % w == 0 for off, w in (
    (OFF_MG, W_MG), (OFF_S5, W_S5), (OFF_QK, W_QK), (OFF_OZ, W_OZ), (OFF_MV, W_MV),
    (OFF_RKV, W_RKV), (OFF_RWG, W_RWG), (OFF_WA, W_WA), (OFF_MI, W_MI), (OFF_MF, W_MF)))

VMEM_LIMIT = 56 * 1024 * 1024


def _cparams(sem):
    return pltpu.CompilerParams(dimension_semantics=sem, vmem_limit_bytes=VMEM_LIMIT)


def _bdot(a, b):
    return jnp.dot(a.astype(BF16), b.astype(BF16), preferred_element_type=F32)


def _hdot(a, b):
    return jnp.dot(a, b, precision=lax.Precision.HIGHEST, preferred_element_type=F32)


def _dot01(a01, b):
    hi = b.astype(BF16)
    rest = b - hi.astype(F32)
    mid = rest.astype(BF16)
    lo = (rest - mid.astype(F32)).astype(BF16)
    a = a01.astype(BF16)
    return jnp.dot(jnp.concatenate([a, a, a], axis=1), jnp.concatenate([hi, mid, lo], axis=0),
                   preferred_element_type=F32)


def _sigmoid(x):
    return 1.0 / (1.0 + jnp.exp(-x))


def _silu(x):
    return x * _sigmoid(x)


def _softplus(x):
    return jnp.maximum(x, 0.0) + jnp.log1p(jnp.exp(-jnp.abs(x)))


def _pad_row_mask(tile_idx, tm, seq_rows, pad):
    pos0 = lax.rem(tile_idx * tm, seq_rows)
    row = lax.broadcasted_iota(jnp.int32, (tm, 1), 0) + pos0
    return row < pad


def _layer_norm_rows(x, g, b):
    mu = jnp.mean(x, axis=-1, keepdims=True)
    d = x - mu
    var = jnp.mean(d * d, axis=-1, keepdims=True)
    return d * lax.rsqrt(var + LN_EPS) * g + b


def _ln_in_kernel(x_ref, g_ref, b_ref, o_ref, *, tm, seq_rows, pad):
    y = _layer_norm_rows(x_ref[...], g_ref[...], b_ref[...])
    if pad:
        y = jnp.where(_pad_row_mask(pl.program_id(0), tm, seq_rows, pad), 0.0, y)
    o_ref[...] = y


def _ln_in(x, g, b, *, tm, seq_rows, pad):
    n = x.shape[0]
    return pl.pallas_call(
        functools.partial(_ln_in_kernel, tm=tm, seq_rows=seq_rows, pad=pad),
        out_shape=jax.ShapeDtypeStruct((n, D_MODEL), F32),
        grid=(n // tm,),
        in_specs=[pl.BlockSpec((tm, D_MODEL), lambda i: (i, 0)),
                  pl.BlockSpec((1, D_MODEL), lambda i: (0, 0)),
                  pl.BlockSpec((1, D_MODEL), lambda i: (0, 0))],
        out_specs=pl.BlockSpec((tm, D_MODEL), lambda i: (i, 0)),
        compiler_params=_cparams(("parallel",)),
        name="ln_in",
    )(x, g, b)


def _proj_kernel(x_ref, w_ref, o_ref):
    o_ref[...] = jnp.dot(x_ref[...].astype(BF16), w_ref[...], preferred_element_type=F32)


def _proj(x, w_all, *, tm):
    n = x.shape[0]
    return pl.pallas_call(
        _proj_kernel,
        out_shape=jax.ShapeDtypeStruct((n, P_COLS), F32),
        grid=(n // tm, P_COLS // P_TN),
        in_specs=[pl.BlockSpec((tm, D_MODEL), lambda i, j: (i, 0)),
                  pl.BlockSpec((D_MODEL, P_TN), lambda i, j: (0, j))],
        out_specs=pl.BlockSpec((tm, P_TN), lambda i, j: (i, j)),
        compiler_params=_cparams(("parallel", "arbitrary")),
        name="proj",
    )(x, w_all)


def _split_lhs(hi_f32, lo_f32):
    return jnp.concatenate([hi_f32.astype(BF16), lo_f32.astype(BF16)], axis=1)


def _seg_lhs(x):
    hi = x.astype(BF16).astype(F32)
    return _split_lhs(hi, x - hi)


def _head_sum(x, ones2):
    parts = [jnp.dot(_seg_lhs(x[:, i * LANE:(i + 1) * LANE]), ones2, preferred_element_type=F32)
             for i in range(RW_WIDTH // LANE)]
    return jnp.concatenate(parts, axis=-1)


def _shift_rows(u, carry):
    rolled = pltpu.roll(u, 1, 0)
    row = lax.broadcasted_iota(jnp.int32, u.shape, 0)
    return jnp.where(row == 0, carry, rolled)


def _hi_lo(a):
    hi = a.astype(BF16)
    return hi, (a - hi.astype(F32)).astype(BF16)


def _lhs3(hl):
    return jnp.concatenate([hl[0], hl[1], hl[0]], axis=1)


def _rhs3_rows(hl):
    return jnp.concatenate([hl[0], hl[0], hl[1]], axis=1)


def _rhs3(hl):
    return jnp.concatenate([hl[0], hl[0], hl[1]], axis=0)


def _dot_nt(lhs, rhs_rows):
    return lax.dot_general(lhs, rhs_rows, (((1,), (1,)), ((), ())), preferred_element_type=F32)


def _dot_nn(lhs, rhs):
    return jnp.dot(lhs, rhs, preferred_element_type=F32)


def _rw_prepare(u, uw, carry_rkv, carry_wa, mu_rkv, mu_wa, w0, w2p, a0, a2p, kk_gain, ka_gain,
                rk_gain, ones2):
    xs = u + (_shift_rows(u, carry_rkv) - u) * mu_rkv
    xwa = uw + (_shift_rows(uw, carry_wa) - uw) * mu_wa
    r = xs[:, 0:RW_WIDTH]
    k = xs[:, RW_WIDTH:2 * RW_WIDTH]
    v = xs[:, 2 * RW_WIDTH:3 * RW_WIDTH]
    log_decay = -math.exp(-0.5) * _sigmoid(w0 + _bdot(jnp.tanh(xwa), w2p))
    a = _sigmoid(a0 + _bdot(xwa, a2p))
    kk = k * kk_gain
    kk = kk * lax.rsqrt(_head_sum(kk * kk, ones2) + 1e-12)
    k = k * (1.0 + (a - 1.0) * ka_gain)
    kka = kk * a
    ti = lax.broadcasted_iota(jnp.int32, (BLK, BLK), 0)
    si = lax.broadcasted_iota(jnp.int32, (BLK, BLK), 1)
    cum = _dot01(jnp.where(si <= ti, 1.0, 0.0), log_decay)
    cum_end = cum[BLK - 1:BLK, :]
    inv_w = jnp.exp(-cum)
    tail_w = jnp.exp(cum_end - cum)
    return dict(at=-kk * jnp.exp(cum - log_decay), rt=r * jnp.exp(cum), bh=kka * inv_w, kh=k * inv_w,
                bc=kka * tail_w, kc=k * tail_w, v=v, wt=jnp.exp(cum_end),
                bonus=_head_sum(r * k * rk_gain, ones2) * v)


def _rw_kernel(rkv_ref, wa_ref, gate_ref, sh0_rkv_ref, sh0_wa_ref, s0_ref,
               mu_rkv_ref, mu_wa_ref, w0_ref, w2_ref, a0_ref, a2_ref, kk_ref, ka_ref, rk_ref,
               lng_ref, lnb_ref, ones_ref,
               y_ref, s_out, sh_rkv_out, sh_wa_out, s_scr, c_rkv, c_wa, *, bb_n):
    j = pl.program_id(1)

    @pl.when(j == 0)
    def _():
        for bb in range(bb_n):
            for p in range(RW_PAIRS):
                s_scr[bb, p] = jnp.concatenate([s0_ref[bb, 2 * p], s0_ref[bb, 2 * p + 1]], axis=1)
        c_rkv[...] = sh0_rkv_ref[...]
        c_wa[...] = sh0_wa_ref[...]

    ones2 = ones_ref[...]
    rows = []
    for bb in range(bb_n):
        u, uw = rkv_ref[bb], wa_ref[bb]
        rows.append(_rw_prepare(u, uw, c_rkv[bb], c_wa[bb], mu_rkv_ref[...], mu_wa_ref[...],
                                w0_ref[...], w2_ref[...], a0_ref[...], a2_ref[...], kk_ref[...],
                                ka_ref[...], rk_ref[...], ones2))
        for carry, out, src in ((c_rkv, sh_rkv_out, u), (c_wa, sh_wa_out, uw)):
            carry[bb] = src[BLK - 1:BLK, :]
            out[bb] = src[BLK - 1:BLK, :]

    lo_lane = lax.broadcasted_iota(jnp.int32, (1, LANE), 1) < RW_HEAD_DIM
    row_j = lax.broadcasted_iota(jnp.int32, (RW_HEAD_DIM, LANE), 0)
    col_t = lax.broadcasted_iota(jnp.int32, (RW_HEAD_DIM, LANE), 1) & (RW_HEAD_DIM - 1)
    strict = row_j < col_t
    incl = row_j <= col_t
    same_blk = (row_j // SOLVE_BLK) == (col_t // SOLVE_BLK)
    ident = jnp.where(row_j == col_t, 1.0, 0.0)
    col_blk = (lax.broadcasted_iota(jnp.int32, (1, LANE), 1) & (RW_HEAD_DIM - 1)) // SOLVE_BLK

    def by_head(x):
        return jnp.concatenate([jnp.where(lo_lane, x, 0.0), jnp.where(lo_lane, 0.0, x)], axis=0)

    pick = jnp.where(row_j == col_t, 1.0, 0.0).astype(BF16)
    pick2 = jnp.concatenate([pick, pick], axis=1)

    bd = lambda hl: (by_head(hl[0]), by_head(hl[1]))
    stack_rows = lambda parts: tuple(jnp.concatenate(x, axis=0) for x in zip(*parts))
    stack_lanes = lambda parts: tuple(jnp.concatenate(x, axis=1) for x in zip(*parts))
    wide_rhs = lambda w: _rhs3(bd(_hi_lo(w)))
    wide_dot = lambda a, rhs3: _dot_nn(_lhs3(_hi_lo(a)), rhs3)

    def pair_transpose(x):
        return _dot_nt(pick2, jnp.concatenate(bd(_hi_lo(x)), axis=1))

    chains = [(bb, p) for bb in range(bb_n) for p in range(RW_PAIRS)]
    n_ch = len(chains)
    tile = lambda name, c: rows[chains[c][0]][name][:, chains[c][1] * LANE:(chains[c][1] + 1) * LANE]
    tile_hl = lambda name, c: _hi_lo(tile(name, c))

    ar_rows, gb_w, gd, gk_hk_kc3, bc3 = [], [], [], [], []
    for c in range(n_ch):
        ar_rows.append(_rhs3_rows(stack_rows([bd(tile_hl("at", c)), bd(tile_hl("rt", c))])))
        bk = stack_rows([tile_hl("bh", c), tile_hl("kh", c)])
        gram = _dot_nt(_lhs3(bk), ar_rows[c])
        gb = jnp.where(strict, gram[0:RW_HEAD_DIM, 0:LANE], 0.0)
        hb = jnp.where(incl, gram[0:RW_HEAD_DIM, LANE:2 * LANE], 0.0)
        gk = jnp.where(strict, gram[RW_HEAD_DIM:LANE, 0:LANE], 0.0)
        hk = jnp.where(incl, gram[RW_HEAD_DIM:LANE, LANE:2 * LANE], 0.0)
        gb_w.append(gb)
        gd.append(jnp.where(same_blk, gb, 0.0))
        gk_hk_kc3.append(_rhs3(stack_lanes([bd(_hi_lo(gk)), bd(_hi_lo(hk)), bd(tile_hl("kc", c))])))
        bc3.append(_rhs3(stack_lanes([bd(_hi_lo(hb)), bd(tile_hl("bc", c))])))
    vt = [pair_transpose(tile("v", c)) for c in range(n_ch)]
    v_terms = [_dot_nn(_lhs3(_hi_lo(vt[c])), gk_hk_kc3[c]) for c in range(n_ch)]
    t_inv = [ident + g for g in gd]
    pw = gd
    for _ in range(SOLVE_LEVELS - 1):
        pw = [wide_dot(pw[c], wide_rhs(pw[c])) for c in range(n_ch)]
        t_inv = [t_inv[c] + wide_dot(t_inv[c], wide_rhs(pw[c])) for c in range(n_ch)]
    td3 = [wide_rhs(t) for t in t_inv]
    gt3 = [wide_rhs(wide_dot(gb_w[c], td3[c])) for c in range(n_ch)]

    s_old = [s_scr[bb, p] for bb, p in chains]
    uy = [_dot_nt(_lhs3(_hi_lo(s_old[c])), ar_rows[c]) for c in range(n_ch)]
    x0 = [wide_dot(uy[c][:, 0:LANE] + v_terms[c][:, 0:LANE], td3[c]) for c in range(n_ch)]
    x = [jnp.where(col_blk == 0, x0[c], 0.0) for c in range(n_ch)]
    for blk in range(1, BLK // SOLVE_BLK):
        x = [x[c] + jnp.where(col_blk == blk, x0[c] + wide_dot(x[c], gt3[c]), 0.0)
             for c in range(n_ch)]
    out2 = [wide_dot(x[c], bc3[c]) for c in range(n_ch)]
    y_tiles = []
    for c, (bb, p) in enumerate(chains):
        y_col = uy[c][:, LANE:2 * LANE] + out2[c][:, 0:LANE] + v_terms[c][:, LANE:2 * LANE]
        y_tiles.append(pair_transpose(y_col))
        s_scr[bb, p] = (s_old[c] * tile("wt", c) + out2[c][:, LANE:2 * LANE]
                        + v_terms[c][:, 2 * LANE:3 * LANE])

    inv = 1.0 / RW_HEAD_DIM
    for bb in range(bb_n):
        y = jnp.concatenate(y_tiles[bb * RW_PAIRS:(bb + 1) * RW_PAIRS], axis=1)
        mu = _head_sum(y, ones2) * inv
        d = y - mu
        var = _head_sum(d * d, ones2) * inv
        yn = d * lax.rsqrt(var + RW_GN_EPS) * lng_ref[...] + lnb_ref[...]
        y_ref[bb] = (yn + rows[bb]["bonus"]) * _silu(gate_ref[bb])

    @pl.when(j == pl.num_programs(1) - 1)
    def _():
        for bb in range(bb_n):
            for p in range(RW_PAIRS):
                s_out[bb, 2 * p] = s_scr[bb, p, :, 0:RW_HEAD_DIM]
                s_out[bb, 2 * p + 1] = s_scr[bb, p, :, RW_HEAD_DIM:LANE]


def _rwkv(p3, sh0_rkv, sh0_wa, s0, lp, *, bb_n):
    bsz, seq_rows, _ = p3.shape
    nblk = seq_rows // BLK
    blk = lambda w, off: pl.BlockSpec((bb_n, BLK, w), lambda b, j: (b, j, off // w))
    par = lambda w: pl.BlockSpec((1, w), lambda b, j: (0, 0))
    mat = pl.BlockSpec((LANE, RW_WIDTH), lambda b, j: (0, 0))
    st = lambda w: pl.BlockSpec((bb_n, 1, w), lambda b, j: (b, 0, 0))
    s_spec = pl.BlockSpec((bb_n, RW_HEADS, RW_HEAD_DIM, RW_HEAD_DIM), lambda b, j: (b, 0, 0, 0))
    return pl.pallas_call(
        functools.partial(_rw_kernel, bb_n=bb_n),
        out_shape=(jax.ShapeDtypeStruct((bsz, seq_rows, RW_WIDTH), F32),
                   jax.ShapeDtypeStruct((bsz, RW_HEADS, RW_HEAD_DIM, RW_HEAD_DIM), F32),
                   jax.ShapeDtypeStruct((bsz, 1, RW_RKV), F32),
                   jax.ShapeDtypeStruct((bsz, 1, LANE), F32)),
        grid=(bsz // bb_n, nblk),
        in_specs=[blk(W_RKV, OFF_RKV), blk(W_WA, OFF_WA), blk(W_RWG, OFF_RWG),
                  st(RW_RKV), st(LANE), s_spec,
                  par(RW_RKV), par(LANE), par(RW_WIDTH), mat, par(RW_WIDTH), mat,
                  par(RW_WIDTH), par(RW_WIDTH), par(RW_WIDTH), par(RW_WIDTH), par(RW_WIDTH),
                  pl.BlockSpec((2 * LANE, LANE), lambda b, j: (0, 0))],
        out_specs=(pl.BlockSpec((bb_n, BLK, RW_WIDTH), lambda b, j: (b, j, 0)), s_spec,
                   st(RW_RKV), st(LANE)),
        scratch_shapes=[pltpu.VMEM((bb_n, RW_PAIRS, RW_HEAD_DIM, LANE), F32),
                        pltpu.VMEM((bb_n, 1, RW_RKV), F32), pltpu.VMEM((bb_n, 1, LANE), F32)],
        compiler_params=_cparams(("parallel", "arbitrary")),
        name="rwkv",
    )(p3, p3, p3, sh0_rkv, sh0_wa, s0, lp["mu_rkv"], lp["mu_wa"], lp["rw_w0"], lp["rw_w2p"],
      lp["rw_a0"], lp["rw_a2p"], lp["rw_kk"], lp["rw_ka"], lp["rw_rk"], lp["rw_ln_g"],
      lp["rw_ln_b"], lp["ones2"])


def _cmul(ar, ai, br, bi):
    return ar * br - ai * bi, ar * bi + ai * br


def _s5_prep_kernel(are_ref, aim_ref, ldt_ref, bre_ref, bim_ref,
                    bbre_out, bbim_out, lvre_out, lvim_out, pwre_out, pwim_out):
    ar, ai = are_ref[...], aim_ref[...]
    dt = jnp.exp(ldt_ref[...])
    mag = jnp.exp(ar * dt)
    lr, li = mag * jnp.cos(ai * dt), mag * jnp.sin(ai * dt)
    nr, ni = lr - 1.0, li
    den = ar * ar + ai * ai
    qr, qi = (nr * ar + ni * ai) / den, (ni * ar - nr * ai) / den
    br, bi = bre_ref[...], bim_ref[...]
    bbr, bbi = _cmul(qr[:, None, :], qi[:, None, :], br, bi)
    bbre_out[...] = bbr
    bbim_out[...] = bbi
    sq_r, sq_i = lr, li
    pows = [(lr, li)]
    for lvl in range(SCAN_LEVELS):
        lvre_out[lvl] = sq_r
        lvim_out[lvl] = sq_i
        pows = pows + [_cmul(pr, pi, sq_r, sq_i) for pr, pi in pows]
        sq_r, sq_i = _cmul(sq_r, sq_i, sq_r, sq_i)
    for t in range(SUBLANES):
        pwre_out[t] = pows[t][0]
        pwim_out[t] = pows[t][1]


def _s5_prep(a_re, a_im, log_dt, b_re_t, b_im_t):
    ghp = jax.ShapeDtypeStruct((S5_GROUPS, S5_GROUP_CH, S5_STATE), F32)
    lv = jax.ShapeDtypeStruct((SCAN_LEVELS, S5_GROUPS, S5_STATE), F32)
    pw = jax.ShapeDtypeStruct((SUBLANES, S5_GROUPS, S5_STATE), F32)
    return pl.pallas_call(
        _s5_prep_kernel,
        out_shape=(ghp, ghp, lv, lv, pw, pw),
        name="s5_prep",
    )(a_re, a_im, log_dt, b_re_t, b_im_t)


def _gelu_tanh(x):
    return 0.5 * x * (1.0 + jnp.tanh(math.sqrt(2.0 / math.pi) * (x + 0.044715 * (x * x * x))))


def _s5_kernel(p_ref, h0_ref, wb_ref, wc_ref, lv_ref, pw_ref, d_ref, wglu_ref, bglu_ref,
               y_ref, h_out, h_scr, hs_scr, *, tb):
    j = pl.program_id(1)

    @pl.when(j == 0)
    def _():
        h_scr[...] = h0_ref[...]

    u = p_ref[:, 0:S5_WIDTH]
    gate = p_ref[:, S5_WIDTH:2 * S5_WIDTH]
    y_parts = []
    for jb in range(S5_NBLK):
        bu = _bdot(u[:, jb * LANE:(jb + 1) * LANE], wb_ref[jb])
        for s in range(tb // BLK):
            xr = bu[s * BLK:(s + 1) * BLK, 0:S5_BLK_STATE]
            xi = bu[s * BLK:(s + 1) * BLK, S5_BLK_STATE:2 * S5_BLK_STATE]
            xr = xr.reshape(BLK // SUBLANES, SUBLANES, S5_BLK_STATE)
            xi = xi.reshape(BLK // SUBLANES, SUBLANES, S5_BLK_STATE)
            for lvl in range(SCAN_LEVELS):
                sr = pltpu.roll(xr, 1 << lvl, 1)
                si = pltpu.roll(xi, 1 << lvl, 1)
                lr, li = lv_ref[lvl, 2 * jb], lv_ref[lvl, 2 * jb + 1]
                xr, xi = xr + (lr * sr - li * si), xi + (lr * si + li * sr)
            xr = xr.reshape(BLK, S5_BLK_STATE)
            xi = xi.reshape(BLK, S5_BLK_STATE)
            c_r = h_scr[2 * jb:2 * jb + 1, :]
            c_i = h_scr[2 * jb + 1:2 * jb + 2, :]
            pr, pi = pw_ref[2 * jb], pw_ref[2 * jb + 1]
            for grp in range(BLK // SUBLANES):
                rows = slice(grp * SUBLANES, (grp + 1) * SUBLANES)
                hr = xr[rows, :] + (pr * c_r - pi * c_i)
                hi = xi[rows, :] + (pr * c_i + pi * c_r)
                c_r, c_i = hr[SUBLANES - 1:SUBLANES, :], hi[SUBLANES - 1:SUBLANES, :]
                out_rows = slice(s * BLK + grp * SUBLANES, s * BLK + (grp + 1) * SUBLANES)
                hs_scr[out_rows, 0:S5_BLK_STATE] = hr
                hs_scr[out_rows, S5_BLK_STATE:2 * S5_BLK_STATE] = hi
            h_scr[2 * jb:2 * jb + 1, :] = c_r
            h_scr[2 * jb + 1:2 * jb + 2, :] = c_i
        y_parts.append(_bdot(hs_scr[...], wc_ref[jb]))
    y = jnp.concatenate(y_parts, axis=-1) + d_ref[...] * u
    y = _gelu_tanh(y)
    y = y * _sigmoid(_bdot(y, wglu_ref[...]) + bglu_ref[...])
    y_ref[...] = y * _silu(gate)

    @pl.when(j == pl.num_programs(1) - 1)
    def _():
        h_out[...] = h_scr[...]


def _s5(p3, h0, lp, *, tb):
    bsz, seq_rows, _ = p3.shape
    st_spec = pl.BlockSpec((None, 2 * S5_NBLK, S5_BLK_STATE), lambda b, j: (b, 0, 0))
    full = lambda shape: pl.BlockSpec(shape, lambda b, j: (0,) * len(shape))
    return pl.pallas_call(
        functools.partial(_s5_kernel, tb=tb),
        out_shape=(jax.ShapeDtypeStruct((bsz, seq_rows, S5_WIDTH), F32),
                   jax.ShapeDtypeStruct((bsz, 2 * S5_NBLK, S5_BLK_STATE), F32)),
        grid=(bsz, seq_rows // tb),
        in_specs=[pl.BlockSpec((None, tb, W_S5), lambda b, j: (b, j, OFF_S5 // W_S5)),
                  st_spec,
                  full((S5_NBLK, LANE, 2 * S5_BLK_STATE)),
                  full((S5_NBLK, 2 * S5_BLK_STATE, LANE)),
                  full((SCAN_LEVELS, 2 * S5_NBLK, SUBLANES, S5_BLK_STATE)),
                  full((2 * S5_NBLK, SUBLANES, S5_BLK_STATE)),
                  full((1, S5_WIDTH)), full((S5_WIDTH, S5_WIDTH)), full((1, S5_WIDTH))],
        out_specs=(pl.BlockSpec((None, tb, S5_WIDTH), lambda b, j: (b, j, 0)), st_spec),
        scratch_shapes=[pltpu.VMEM((2 * S5_NBLK, S5_BLK_STATE), F32),
                        pltpu.VMEM((tb, 2 * S5_BLK_STATE), F32)],
        compiler_params=_cparams(("parallel", "arbitrary")),
        name="s5",
    )(p3, h0, lp["s5_wb"], lp["s5_wc"], lp["s5_lv"], lp["s5_pw"], lp["s5_d"],
      lp["s5_w_glu"], lp["s5_b_glu"])


def _mlstm_kernel(qk_ref, oz_ref, v_ref, gi_ref, gf_ref, conv0_ref, c0_ref, n0_ref, m0_ref,
                  cw_ref, cb_ref, bi_ref, bf_ref, lng_ref,
                  y_ref, conv_out, c_out, n_out, m_out,
                  xp_scr, c_scr, n_scr, m_scr, *, pad, bb_n):
    j = pl.program_id(1)
    halo = SUBLANES

    @pl.when(j == 0)
    def _():
        xp_scr[:, 0:halo, :] = jnp.zeros((bb_n, halo, W_QK), F32)
        xp_scr[:, halo - (ML_CONV - 1):halo, :] = conv0_ref[...]
        c_scr[...] = jnp.zeros(c_scr.shape, F32)
        c_scr[:, :, 0:ML_HEAD_DIM, 0:ML_HEAD_DIM] = c0_ref[...]
        n_scr[...] = jnp.zeros(n_scr.shape, F32)
        n_scr[:, :, :, 0:ML_HEAD_DIM] = n0_ref[...]
        m_scr[...] = m0_ref[...]

    t0 = jnp.where(j == 0, pad, 0) if pad else 0
    row1 = lax.broadcasted_iota(jnp.int32, (BLK, 1), 0)
    row_ok = row1 >= t0
    ti = lax.broadcasted_iota(jnp.int32, (BLK, BLK), 0)
    si = lax.broadcasted_iota(jnp.int32, (BLK, BLK), 1)
    pair_ok = (si <= ti) & (si >= t0)
    tril = jnp.where(si <= ti, 1.0, 0.0)
    eye = jnp.where(si == ti, 1.0, 0.0)
    ones_sq = jnp.ones((BLK, BLK), F32)
    lane_ok = lax.broadcasted_iota(jnp.int32, (1, ML_HEAD_PAD), 1) < ML_HEAD_DIM
    head = lambda a, h: a[:, h * ML_HEAD_PAD:(h + 1) * ML_HEAD_PAD]
    chains = [(bb, h) for bb in range(bb_n) for h in range(ML_HEADS)]

    log_i, log_f, b = [], [], []
    for bb in range(bb_n):
        log_i.append(gi_ref[bb] + bi_ref[...])
        log_f.append(jnp.where(row_ok, -_softplus(-(gf_ref[bb] + bf_ref[...])), 0.0))
        b.append(_hdot(tril, log_f[bb]))
    d_row = {}
    for bb, h in chains:
        x_col = log_i[bb][:, h:h + 1] - b[bb][:, h:h + 1]
        d_row[bb, h] = _hdot(ones_sq, eye * x_col)

    q, k, v = [], [], []
    for bb in range(bb_n):
        xp_scr[bb, halo:halo + BLK, :] = qk_ref[bb]
        conv = cb_ref[...] + xp_scr[bb, halo - 3:halo - 3 + BLK, :] * cw_ref[0:1, :]
        for tap in range(1, ML_CONV):
            conv = conv + xp_scr[bb, halo - 3 + tap:halo - 3 + tap + BLK, :] * cw_ref[tap:tap + 1, :]
        tail = xp_scr[bb, halo + BLK - (ML_CONV - 1):halo + BLK, :]
        xp_scr[bb, halo - (ML_CONV - 1):halo, :] = tail
        conv_out[bb] = tail
        act = _silu(conv)
        q.append(act[:, 0:ML_WIDTH_PAD])
        k.append(act[:, ML_WIDTH_PAD:2 * ML_WIDTH_PAD] * (1.0 / math.sqrt(ML_HEAD_DIM)))
        v.append(v_ref[bb])

    qk, q_c = {}, {}
    for bb, h in chains:
        qh = head(q[bb], h).astype(BF16)
        qk[bb, h] = lax.dot_general(qh, head(k[bb], h).astype(BF16), (((1,), (1,)), ((), ())),
                                    preferred_element_type=F32)
        q_c[bb, h] = jnp.dot(qh, c_scr[bb, h].astype(BF16), preferred_element_type=F32)

    g, we, keep = [], [], []
    for bb in range(bb_n):
        m_prev = m_scr[bb]
        g.append(b[bb] + m_prev)
        b_end = b[bb][BLK - 1:BLK, :]
        e_log = jnp.where(row_ok, b_end - b[bb] + log_i[bb], -jnp.inf)
        m_new = jnp.maximum(b_end + m_prev, jnp.max(e_log, axis=0, keepdims=True))
        we.append(jnp.exp(e_log - m_new))
        keep.append(jnp.exp(b_end + m_prev - m_new))
        m_scr[bb] = m_new

    s_mat, m_row, w_inter = {}, {}, {}
    for bb, h in chains:
        d = b[bb][:, h:h + 1] + d_row[bb, h]
        d = jnp.where(pair_ok, d, -jnp.inf)
        g_col = g[bb][:, h:h + 1]
        m_row[bb, h] = jnp.maximum(g_col, jnp.max(d, axis=1, keepdims=True))
        s_mat[bb, h] = qk[bb, h] * jnp.exp(d - m_row[bb, h])
        w_inter[bb, h] = jnp.exp(g_col - m_row[bb, h])

    s_v, k_v = {}, {}
    for bb, h in chains:
        vh = head(v[bb], h)
        s_v[bb, h] = _bdot(s_mat[bb, h], vh)
        k_v[bb, h] = lax.dot_general(head(k[bb], h).astype(BF16),
                                     (we[bb][:, h:h + 1] * vh).astype(BF16),
                                     (((0,), (0,)), ((), ())), preferred_element_type=F32)

    for bb, h in chains:
        hs = slice(h * ML_HEAD_PAD, (h + 1) * ML_HEAD_PAD)
        kh = head(k[bb], h)
        n_h = n_scr[bb, h]
        num = s_v[bb, h] + w_inter[bb, h] * q_c[bb, h]
        qn = jnp.sum(head(q[bb], h) * n_h, axis=1, keepdims=True)
        den = jnp.sum(s_mat[bb, h], axis=1, keepdims=True) + w_inter[bb, h] * qn
        hh = num / jnp.maximum(jnp.abs(den), jnp.exp(-m_row[bb, h]))
        mu = jnp.sum(hh, axis=1, keepdims=True) * (1.0 / ML_HEAD_DIM)
        dv = jnp.where(lane_ok, hh - mu, 0.0)
        var = jnp.sum(dv * dv, axis=1, keepdims=True) * (1.0 / ML_HEAD_DIM)
        hn = dv * lax.rsqrt(var + LN_EPS) * lng_ref[:, hs]
        y_ref[bb, :, hs] = (_sigmoid(oz_ref[bb, :, hs]) * hn
                            * _silu(oz_ref[bb, :, ML_WIDTH_PAD + h * ML_HEAD_PAD:
                                           ML_WIDTH_PAD + (h + 1) * ML_HEAD_PAD]))
        keep_h = keep[bb][:, h:h + 1]
        c_scr[bb, h] = keep_h * c_scr[bb, h] + k_v[bb, h]
        n_scr[bb, h] = keep_h * n_h + jnp.sum(we[bb][:, h:h + 1] * kh, axis=0, keepdims=True)

    @pl.when(j == pl.num_programs(1) - 1)
    def _():
        c_out[...] = c_scr[:, :, 0:ML_HEAD_DIM, 0:ML_HEAD_DIM]
        n_out[...] = n_scr[:, :, :, 0:ML_HEAD_DIM]
        m_out[...] = m_scr[...]


def _mlstm(p3, conv0, c0_layers, n0, m0, lp, *, pad, bb_n):
    c0, c0_layer = c0_layers
    bsz, seq_rows, _ = p3.shape
    nblk = seq_rows // BLK
    blk = lambda w, off: pl.BlockSpec((bb_n, BLK, w), lambda b, j: (b, j, off // w))
    par = lambda r, w: pl.BlockSpec((r, w), lambda b, j: (0, 0))
    conv_spec = pl.BlockSpec((bb_n, ML_CONV - 1, W_QK), lambda b, j: (b, 0, 0))
    c_spec = pl.BlockSpec((bb_n, ML_HEADS, ML_HEAD_DIM, ML_HEAD_DIM), lambda b, j: (b, 0, 0, 0))
    n_spec = pl.BlockSpec((bb_n, ML_HEADS, 1, ML_HEAD_DIM), lambda b, j: (b, 0, 0, 0))
    m_spec = pl.BlockSpec((bb_n, 1, LANE), lambda b, j: (b, 0, 0))
    return pl.pallas_call(
        functools.partial(_mlstm_kernel, pad=pad, bb_n=bb_n),
        out_shape=(jax.ShapeDtypeStruct((bsz, seq_rows, ML_WIDTH_PAD), F32),
                   jax.ShapeDtypeStruct((bsz, ML_CONV - 1, W_QK), F32),
                   jax.ShapeDtypeStruct((bsz, ML_HEADS, ML_HEAD_DIM, ML_HEAD_DIM), F32),
                   jax.ShapeDtypeStruct((bsz, ML_HEADS, 1, ML_HEAD_DIM), F32),
                   jax.ShapeDtypeStruct((bsz, 1, LANE), F32)),
        grid=(bsz // bb_n, nblk),
        in_specs=[blk(W_QK, OFF_QK), blk(W_OZ, OFF_OZ), blk(W_MV, OFF_MV),
                  blk(W_MI, OFF_MI), blk(W_MF, OFF_MF),
                  conv_spec,
                  pl.BlockSpec((None, bb_n, ML_HEADS, ML_HEAD_DIM, ML_HEAD_DIM),
                               lambda b, j: (c0_layer, b, 0, 0, 0)),
                  n_spec, m_spec,
                  par(ML_CONV, W_QK), par(1, W_QK), par(1, LANE), par(1, LANE),
                  par(1, ML_WIDTH_PAD)],
        out_specs=(pl.BlockSpec((bb_n, BLK, ML_WIDTH_PAD), lambda b, j: (b, j, 0)),
                   conv_spec, c_spec, n_spec, m_spec),
        scratch_shapes=[pltpu.VMEM((bb_n, SUBLANES + BLK, W_QK), F32),
                        pltpu.VMEM((bb_n, ML_HEADS, ML_HEAD_PAD, ML_HEAD_PAD), F32),
                        pltpu.VMEM((bb_n, ML_HEADS, 1, ML_HEAD_PAD), F32),
                        pltpu.VMEM((bb_n, 1, LANE), F32)],
        compiler_params=_cparams(("parallel", "arbitrary")),
        name="mlstm",
    )(p3, p3, p3, p3, p3, conv0, c0, n0, m0, lp["ml_cw"], lp["ml_cb"], lp["ml_bi"], lp["ml_bf"],
      lp["ml_ln_g"])


def _merge_kernel(x_ref, mg_ref, yrw_ref, ys5_ref, yml_ref, bmg_ref, wrw_ref, ws5_ref, wml_ref,
                  wout_ref, g_ref, b_ref, o_ref, *, tm, seq_rows, pad):
    gates = _sigmoid(mg_ref[...] + bmg_ref[...])
    merged = (gates[:, 0:D_MODEL] * _bdot(yrw_ref[...], wrw_ref[...])
              + gates[:, D_MODEL:2 * D_MODEL] * _bdot(ys5_ref[...], ws5_ref[...])
              + gates[:, 2 * D_MODEL:3 * D_MODEL] * _bdot(yml_ref[...], wml_ref[...]))
    out = _bdot(merged, wout_ref[...])
    y = _layer_norm_rows(DN_ALPHA * x_ref[...] + out, g_ref[...], b_ref[...])
    if pad:
        y = jnp.where(_pad_row_mask(pl.program_id(0), tm, seq_rows, pad), 0.0, y)
    o_ref[...] = y


def _merge(x, p, y_rw, y_s5, y_ml, lp, *, tm, seq_rows, pad):
    n = x.shape[0]
    rows = lambda w: pl.BlockSpec((tm, w), lambda i: (i, 0))
    full = lambda r, w: pl.BlockSpec((r, w), lambda i: (0, 0))
    return pl.pallas_call(
        functools.partial(_merge_kernel, tm=tm, seq_rows=seq_rows, pad=pad),
        out_shape=jax.ShapeDtypeStruct((n, D_MODEL), F32),
        grid=(n // tm,),
        in_specs=[rows(D_MODEL), pl.BlockSpec((tm, W_MG), lambda i: (i, OFF_MG // W_MG)),
                  rows(RW_WIDTH), rows(S5_WIDTH), rows(ML_WIDTH_PAD),
                  full(1, W_MG), full(RW_WIDTH, D_MODEL), full(S5_WIDTH, D_MODEL),
                  full(ML_WIDTH_PAD, D_MODEL), full(D_MODEL, D_MODEL),
                  full(1, D_MODEL), full(1, D_MODEL)],
        out_specs=rows(D_MODEL),
        compiler_params=_cparams(("parallel",)),
        name="merge",
    )(x, p, y_rw, y_s5, y_ml, lp["b_merge"], lp["w_br_rw"], lp["w_br_s5"], lp["w_br_ml"],
      lp["w_out"], lp["ln_g"], lp["ln_b"])


def _pad_heads(w):
    lead = w.shape[:-1]
    w = w.reshape(lead + (ML_HEADS, ML_HEAD_DIM))
    w = jnp.pad(w, [(0, 0)] * len(lead) + [(0, 0), (0, ML_HEAD_PAD - ML_HEAD_DIM)])
    return w.reshape(lead + (ML_WIDTH_PAD,))


def _unpad_heads(w):
    lead = w.shape[:-1]
    return w.reshape(lead + (ML_HEADS, ML_HEAD_PAD))[..., :ML_HEAD_DIM].reshape(lead + (ML_WIDTH,))


def _pad_lanes(w, width=LANE):
    return jnp.pad(w, [(0, 0)] * (w.ndim - 1) + [(0, width - w.shape[-1])])


def _qk_pad(w):
    return jnp.concatenate([_pad_heads(w[..., :ML_WIDTH]), _pad_heads(w[..., ML_WIDTH:])], axis=-1)


def _layer_params(l, w_in, rw_mu, rw_w0, rw_w2, rw_a0, rw_a2, rw_kk, rw_ka, rw_rk, rw_ln_g, rw_ln_b,
                  s5_a_re, s5_a_im, s5_b_re, s5_b_im, s5_c_re, s5_c_im, s5_d, s5_log_dt, s5_w_glu,
                  s5_b_glu, ml_conv_w, ml_conv_b, ml_b_if, ml_ln_g, b_merge, w_br_rw, w_br_s5,
                  w_br_ml, w_out, ln_g, ln_b):
    w = w_in[l]
    o = 0
    cols = {}
    for name, size in (("rwc", RW_SHIFT), ("rwg", RW_WIDTH), ("s5", 2 * S5_WIDTH),
                       ("qk", 2 * ML_WIDTH), ("mv", ML_WIDTH), ("mi", ML_HEADS), ("mf", ML_HEADS),
                       ("mo", ML_WIDTH), ("mz", ML_WIDTH), ("mg", 3 * D_MODEL)):
        cols[name] = w[:, o:o + size]
        o += size
    w_all = jnp.concatenate([
        cols["mg"], cols["s5"], _qk_pad(cols["qk"]),
        _pad_heads(cols["mo"]), _pad_heads(cols["mz"]), _pad_heads(cols["mv"]),
        cols["rwc"][:, :RW_RKV], cols["rwg"], cols["rwc"][:, RW_RKV:],
        _pad_lanes(cols["mi"]), _pad_lanes(cols["mf"]),
        jnp.zeros((D_MODEL, P_COLS - P_USED), F32)], axis=1).astype(BF16)
    row = lambda a: a.reshape(1, -1)
    zeros_lora = jnp.zeros((RW_LORA, RW_WIDTH), F32)
    lane = jnp.arange(LANE)
    ones_blk = (lane[:, None] // RW_HEAD_DIM == lane[None, :] // RW_HEAD_DIM).astype(F32)

    bb_re, bb_im, lv_re, lv_im, pw_re, pw_im = _s5_prep(
        s5_a_re[l], s5_a_im[l], s5_log_dt[l].reshape(S5_GROUPS, 1),
        jnp.swapaxes(s5_b_re[l], 1, 2), jnp.swapaxes(s5_b_im[l], 1, 2))
    eye8 = jnp.eye(S5_BLK_GROUPS, dtype=F32)
    blocked = lambda a: a.reshape((S5_NBLK, S5_BLK_GROUPS) + a.shape[1:])
    bb = jnp.stack([blocked(bb_re), blocked(bb_im)])
    wb = jnp.einsum("cjghp,gk->jghckp", bb, eye8).reshape(S5_NBLK, LANE, 2 * S5_BLK_STATE)
    cc = jnp.stack([blocked(s5_c_re[l]), -blocked(s5_c_im[l])])
    wc = jnp.einsum("cjghp,gk->jcgpkh", cc, eye8).reshape(S5_NBLK, 2 * S5_BLK_STATE, LANE)
    state_rows = lambda re, im: jnp.stack(
        [re.reshape(re.shape[:-2] + (S5_NBLK, S5_BLK_STATE)),
         im.reshape(im.shape[:-2] + (S5_NBLK, S5_BLK_STATE))], axis=-2)
    lv = state_rows(lv_re, lv_im).reshape(SCAN_LEVELS, 2 * S5_NBLK, 1, S5_BLK_STATE)
    row_in_group = jnp.arange(SUBLANES)[None, None, :, None]
    lv = jnp.where(row_in_group >= (1 << jnp.arange(SCAN_LEVELS))[:, None, None, None], lv, 0.0)
    pw = jnp.moveaxis(state_rows(pw_re, pw_im).reshape(SUBLANES, 2 * S5_NBLK, S5_BLK_STATE), 0, 1)

    return dict(
        w_all=w_all,
        mu_rkv=row(rw_mu[l][:RW_RKV]), mu_wa=row(rw_mu[l][RW_RKV:]),
        rw_w0=row(rw_w0[l]), rw_a0=row(rw_a0[l]),
        rw_w2p=jnp.concatenate([rw_w2[l], zeros_lora], axis=0),
        rw_a2p=jnp.concatenate([zeros_lora, rw_a2[l]], axis=0),
        rw_kk=row(rw_kk[l]), rw_ka=row(rw_ka[l]), rw_rk=row(rw_rk[l]),
        rw_ln_g=row(rw_ln_g[l]), rw_ln_b=row(rw_ln_b[l]),
        ones2=jnp.concatenate([ones_blk, ones_blk], axis=0).astype(BF16),
        s5_wb=wb.astype(BF16), s5_wc=wc.astype(BF16), s5_lv=lv, s5_pw=pw,
        s5_d=row(s5_d[l]), s5_w_glu=s5_w_glu[l].astype(BF16), s5_b_glu=row(s5_b_glu[l]),
        ml_cw=_qk_pad(ml_conv_w[l]), ml_cb=row(_qk_pad(ml_conv_b[l])),
        ml_bi=row(_pad_lanes(ml_b_if[l][:ML_HEADS])), ml_bf=row(_pad_lanes(ml_b_if[l][ML_HEADS:])),
        ml_ln_g=row(_pad_heads(ml_ln_g[l])),
        b_merge=row(b_merge[l]), w_br_rw=w_br_rw[l].astype(BF16), w_br_s5=w_br_s5[l].astype(BF16),
        w_br_ml=jnp.pad(w_br_ml[l].reshape(ML_HEADS, ML_HEAD_DIM, D_MODEL),
                        ((0, 0), (0, ML_HEAD_PAD - ML_HEAD_DIM), (0, 0))
                        ).reshape(ML_WIDTH_PAD, D_MODEL).astype(BF16),
        w_out=w_out[l].astype(BF16), ln_g=row(ln_g[l]), ln_b=row(ln_b[l]))


def _pack_s5(re, im):
    b = re.shape[0]
    return jnp.stack([re.reshape(b, S5_NBLK, S5_BLK_STATE), im.reshape(b, S5_NBLK, S5_BLK_STATE)],
                     axis=2).reshape(b, 2 * S5_NBLK, S5_BLK_STATE)


def _unpack_s5(h):
    b = h.shape[0]
    h = h.reshape(b, S5_NBLK, 2, S5_BLK_STATE)
    return (h[:, :, 0].reshape(b, S5_GROUPS, S5_STATE), h[:, :, 1].reshape(b, S5_GROUPS, S5_STATE))


def _row_tile(n, seq_rows, target):
    best = 8
    for t in range(8, min(n, target) + 1, 8):
        if seq_rows % t == 0 or (t % seq_rows == 0 and n % t == 0):
            best = t
    return best


def _trunk_layer(x, st, lp, *, bsz, seq_rows, pad, bb_n, s5_tb, tm_proj, tm_merge):
    rw_shift0, rw_wkv0, s5_re0, s5_im0, ml_conv0, ml_c0, ml_n0, ml_m0 = st
    p = _proj(x, lp["w_all"], tm=tm_proj)
    p3 = p.reshape(bsz, seq_rows, P_COLS)
    n = bsz * seq_rows

    y_rw, wkv1, sh_rkv1, sh_wa1 = _rwkv(p3, rw_shift0[:, None, :RW_RKV], rw_shift0[:, None, RW_RKV:],
                                        rw_wkv0, lp, bb_n=4 if bsz % 4 == 0 else bb_n)
    rw_shift1 = jnp.concatenate([sh_rkv1[:, 0], sh_wa1[:, 0]], axis=-1)

    y_s5, h1 = _s5(p3, _pack_s5(s5_re0, s5_im0), lp, tb=s5_tb)
    s5_re1, s5_im1 = _unpack_s5(h1)

    y_ml, conv1, c1, n1, m1 = _mlstm(
        p3, _qk_pad(ml_conv0), ml_c0, ml_n0[:, :, None, :], _pad_lanes(ml_m0)[:, None, :], lp, pad=pad,
        bb_n=bb_n)
    ml_conv1 = jnp.concatenate([_unpad_heads(conv1[..., :ML_WIDTH_PAD]),
                                _unpad_heads(conv1[..., ML_WIDTH_PAD:])], axis=-1)

    x_new = _merge(x, p, y_rw.reshape(n, RW_WIDTH), y_s5.reshape(n, S5_WIDTH),
                   y_ml.reshape(n, ML_WIDTH_PAD), lp, tm=tm_merge, seq_rows=seq_rows, pad=pad)
    return x_new, (rw_shift1, wkv1, s5_re1, s5_im1, ml_conv1, c1, n1[:, :, 0, :],
                   m1[:, 0, :ML_HEADS])


def _run_group(x_rows, states, lps, *, bsz, seq_rows, pad, in_ln_g, in_ln_b):
    n = bsz * seq_rows
    bb_n = 2 if bsz % 2 == 0 else 1
    s5_tb = max(t for t in (BLK, 5 * BLK) if seq_rows % t == 0)
    tm_ln = _row_tile(n, seq_rows, 1024)
    tm_proj = _row_tile(n, seq_rows, 2080)
    tm_merge = _row_tile(n, seq_rows, 320)
    x = _ln_in(x_rows, in_ln_g, in_ln_b, tm=tm_ln, seq_rows=seq_rows, pad=pad)
    outs = []
    for l in range(DEPTH):
        x, st = _trunk_layer(x, states[l], lps[l], bsz=bsz, seq_rows=seq_rows, pad=pad, bb_n=bb_n,
                             s5_tb=s5_tb, tm_proj=tm_proj, tm_merge=tm_merge)
        outs.append(st)
    return x, outs


def kernel(x_prompt, x_sample, state_rwkv_shift, state_rwkv_wkv, state_s5_re, state_s5_im, state_mlstm_conv, state_mlstm_c, state_mlstm_n, state_mlstm_m, meta, in_ln_g, in_ln_b, w_in, rw_mu, rw_w0, rw_w2, rw_a0, rw_a2, rw_kk, rw_ka, rw_rk, rw_ln_g, rw_ln_b, s5_a_re, s5_a_im, s5_b_re, s5_b_im, s5_c_re, s5_c_im, s5_d, s5_log_dt, s5_w_glu, s5_b_glu, ml_conv_w, ml_conv_b, ml_b_if, ml_ln_g, b_merge, w_br_rw, w_br_s5, w_br_ml, w_out, ln_g, ln_b):
    lps = [_layer_params(l, w_in, rw_mu, rw_w0, rw_w2, rw_a0, rw_a2, rw_kk, rw_ka, rw_rk, rw_ln_g,
                         rw_ln_b, s5_a_re, s5_a_im, s5_b_re, s5_b_im, s5_c_re, s5_c_im, s5_d,
                         s5_log_dt, s5_w_glu, s5_b_glu, ml_conv_w, ml_conv_b, ml_b_if, ml_ln_g,
                         b_merge, w_br_rw, w_br_s5, w_br_ml, w_out, ln_g, ln_b)
           for l in range(DEPTH)]
    g_in, b_in = in_ln_g.reshape(1, D_MODEL), in_ln_b.reshape(1, D_MODEL)

    bp, sp = x_prompt.shape[0], x_prompt.shape[1]
    lp_rows = PAD + N_META + sp
    xp = jnp.concatenate([jnp.zeros((bp, PAD, D_MODEL), F32),
                          jnp.broadcast_to(meta[None], (bp, N_META, D_MODEL)), x_prompt], axis=1)
    z = lambda *shape: jnp.zeros((bp,) + shape, F32)
    zero_state = (z(RW_SHIFT), z(RW_HEADS, RW_HEAD_DIM, RW_HEAD_DIM), z(S5_GROUPS, S5_STATE),
                  z(S5_GROUPS, S5_STATE), z(ML_CONV - 1, 2 * ML_WIDTH),
                  (jnp.zeros((1, bp, ML_HEADS, ML_HEAD_DIM, ML_HEAD_DIM), F32), 0),
                  z(ML_HEADS, ML_HEAD_DIM), z(ML_HEADS))
    yp, p_states = _run_group(xp.reshape(bp * lp_rows, D_MODEL), [zero_state] * DEPTH, lps,
                              bsz=bp, seq_rows=lp_rows, pad=PAD, in_ln_g=g_in, in_ln_b=b_in)
    y_prompt = yp.reshape(bp, lp_rows, D_MODEL)[:, PAD + N_META:]

    bs, ds = x_sample.shape[0], x_sample.shape[1]
    s_in = [(state_rwkv_shift[l], state_rwkv_wkv[l], state_s5_re[l], state_s5_im[l],
             state_mlstm_conv[l], (state_mlstm_c, l), state_mlstm_n[l], state_mlstm_m[l])
            for l in range(DEPTH)]
    ys, s_states = _run_group(x_sample.reshape(bs * ds, D_MODEL), s_in, lps,
                              bsz=bs, seq_rows=ds, pad=0, in_ln_g=g_in, in_ln_b=b_in)
    y_sample = ys.reshape(bs, ds, D_MODEL)

    stack = lambda sts: tuple(jnp.stack(s, 0) for s in zip(*sts))
    return (y_prompt, y_sample) + stack(p_states) + stack(s_states)
```

```python
import functools
import math

import jax
import jax.numpy as jnp
from jax import lax
from jax.experimental import pallas as pl
from jax.experimental.pallas import tpu as pltpu

F32 = jnp.float32
BF16 = jnp.bfloat16

D_MODEL = 1024
DEPTH = 2
N_META = 16
RW_HEADS = 12
RW_HEAD_DIM = 64
RW_WIDTH = RW_HEADS * RW_HEAD_DIM
RW_PAIRS = RW_HEADS // 2
RW_LORA = 64
RW_RKV = 3 * RW_WIDTH
RW_SHIFT = RW_RKV + 2 * RW_LORA
S5_GROUPS = 32
S5_GROUP_CH = 16
S5_WIDTH = S5_GROUPS * S5_GROUP_CH
S5_STATE = 64
S5_NBLK = 4
S5_BLK_STATE = 512
ML_HEADS = 4
ML_HEAD_DIM = 192
ML_HEAD_PAD = 256
ML_WIDTH = ML_HEADS * ML_HEAD_DIM
ML_WIDTH_PAD = ML_HEADS * ML_HEAD_PAD
ML_CONV = 4
DN_ALPHA = (2 * DEPTH) ** 0.25
LN_EPS = 1e-5
RW_GN_EPS = 64e-5

LANE = 128
SUBLANES = 8
BLK = 64
PAD = BLK - N_META
SCAN_LEVELS = 3
SOLVE_BLK = 8
SOLVE_LEVELS = 3
S5_BLK_GROUPS = LANE // S5_GROUP_CH

W_MG, W_S5, W_QK, W_OZ, W_MV = 3 * D_MODEL, 2 * S5_WIDTH, 2 * ML_WIDTH_PAD, 2 * ML_WIDTH_PAD, ML_WIDTH_PAD
W_RKV, W_RWG, W_WA, W_MI, W_MF = RW_RKV, RW_WIDTH, LANE, LANE, LANE
OFF_MG = 0
OFF_S5 = OFF_MG + W_MG
OFF_QK = OFF_S5 + W_S5
OFF_OZ = OFF_QK + W_QK
OFF_MV = OFF_OZ + W_OZ
OFF_RKV = OFF_MV + W_MV
OFF_RWG = OFF_RKV + W_RKV
OFF_WA = OFF_RWG + W_RWG
OFF_MI = OFF_WA + W_WA
OFF_MF = OFF_MI + W_MI
P_USED = OFF_MF + W_MF
MXU_COLS = 256
P_TN = 5 * MXU_COLS
P_COLS = -(-P_USED // P_TN) * P_TN
assert all(off % w == 0 for off, w in (
    (OFF_MG, W_MG), (OFF_S5, W_S5), (OFF_QK, W_QK), (OFF_OZ, W_OZ), (OFF_MV, W_MV),
    (OFF_RKV, W_RKV), (OFF_RWG, W_RWG), (OFF_WA, W_WA), (OFF_MI, W_MI), (OFF_MF, W_MF)))

VMEM_LIMIT = 56 * 1024 * 1024


def _cparams(sem):
    return pltpu.CompilerParams(dimension_semantics=sem, vmem_limit_bytes=VMEM_LIMIT)


def _bdot(a, b):
    return jnp.dot(a.astype(BF16), b.astype(BF16), preferred_element_type=F32)


def _hdot(a, b):
    return jnp.dot(a, b, precision=lax.Precision.HIGHEST, preferred_element_type=F32)


def _dot01(a01, b):
    hi = b.astype(BF16)
    rest = b - hi.astype(F32)
    mid = rest.astype(BF16)
    lo = (rest - mid.astype(F32)).astype(BF16)
    a = a01.astype(BF16)
    return jnp.dot(jnp.concatenate([a, a, a], axis=1), jnp.concatenate([hi, mid, lo], axis=0),
                   preferred_element_type=F32)


def _sigmoid(x):
    return 1.0 / (1.0 + jnp.exp(-x))


def _silu(x):
    return x * _sigmoid(x)


def _softplus(x):
    return jnp.maximum(x, 0.0) + jnp.log1p(jnp.exp(-jnp.abs(x)))


def _pad_row_mask(tile_idx, tm, seq_rows, pad):
    pos0 = lax.rem(tile_idx * tm, seq_rows)
    row = lax.broadcasted_iota(jnp.int32, (tm, 1), 0) + pos0
    return row < pad


def _layer_norm_rows(x, g, b):
    mu = jnp.mean(x, axis=-1, keepdims=True)
    d = x - mu
    var = jnp.mean(d * d, axis=-1, keepdims=True)
    return d * lax.rsqrt(var + LN_EPS) * g + b


def _ln_in_kernel(x_ref, g_ref, b_ref, o_ref, *, tm, seq_rows, pad):
    y = _layer_norm_rows(x_ref[...], g_ref[...], b_ref[...])
    if pad:
        y = jnp.where(_pad_row_mask(pl.program_id(0), tm, seq_rows, pad), 0.0, y)
    o_ref[...] = y


def _ln_in(x, g, b, *, tm, seq_rows, pad):
    n = x.shape[0]
    return pl.pallas_call(
        functools.partial(_ln_in_kernel, tm=tm, seq_rows=seq_rows, pad=pad),
        out_shape=jax.ShapeDtypeStruct((n, D_MODEL), F32),
        grid=(n // tm,),
        in_specs=[pl.BlockSpec((tm, D_MODEL), lambda i: (i, 0)),
                  pl.BlockSpec((1, D_MODEL), lambda i: (0, 0)),
                  pl.BlockSpec((1, D_MODEL), lambda i: (0, 0))],
        out_specs=pl.BlockSpec((tm, D_MODEL), lambda i: (i, 0)),
        compiler_params=_cparams(("parallel",)),
        name="ln_in",
    )(x, g, b)


def _proj_kernel(x_ref, w_ref, o_ref):
    o_ref[...] = jnp.dot(x_ref[...].astype(BF16), w_ref[...], preferred_element_type=F32)


def _proj(x, w_all, *, tm):
    n = x.shape[0]
    return pl.pallas_call(
        _proj_kernel,
        out_shape=jax.ShapeDtypeStruct((n, P_COLS), F32),
        grid=(n // tm, P_COLS // P_TN),
        in_specs=[pl.BlockSpec((tm, D_MODEL), lambda i, j: (i, 0)),
                  pl.BlockSpec((D_MODEL, P_TN), lambda i, j: (0, j))],
        out_specs=pl.BlockSpec((tm, P_TN), lambda i, j: (i, j)),
        compiler_params=_cparams(("parallel", "arbitrary")),
        name="proj",
    )(x, w_all)


def _split_lhs(hi_f32, lo_f32):
    return jnp.concatenate([hi_f32.astype(BF16), lo_f32.astype(BF16)], axis=1)


def _seg_lhs(x):
    hi = x.astype(BF16).astype(F32)
    return _split_lhs(hi, x - hi)


def _head_sum(x, ones2):
    parts = [jnp.dot(_seg_lhs(x[:, i * LANE:(i + 1) * LANE]), ones2, preferred_element_type=F32)
             for i in range(RW_WIDTH // LANE)]
    return jnp.concatenate(parts, axis=-1)


def _shift_rows(u, carry):
    rolled = pltpu.roll(u, 1, 0)
    row = lax.broadcasted_iota(jnp.int32, u.shape, 0)
    return jnp.where(row == 0, carry, rolled)


def _hi_lo(a):
    hi = a.astype(BF16)
    return hi, (a - hi.astype(F32)).astype(BF16)


def _lhs3(hl):
    return jnp.concatenate([hl[0], hl[1], hl[0]], axis=1)


def _rhs3_rows(hl):
    return jnp.concatenate([hl[0], hl[0], hl[1]], axis=1)


def _rhs3(hl):
    return jnp.concatenate([hl[0], hl[0], hl[1]], axis=0)


def _dot_nt(lhs, rhs_rows):
    return lax.dot_general(lhs, rhs_rows, (((1,), (1,)), ((), ())), preferred_element_type=F32)


def _dot_nn(lhs, rhs):
    return jnp.dot(lhs, rhs, preferred_element_type=F32)


def _rw_prepare(u, uw, carry_rkv, carry_wa, mu_rkv, mu_wa, w0, w2p, a0, a2p, kk_gain, ka_gain,
                rk_gain, ones2):
    xs = u + (_shift_rows(u, carry_rkv) - u) * mu_rkv
    xwa = uw + (_shift_rows(uw, carry_wa) - uw) * mu_wa
    r = xs[:, 0:RW_WIDTH]
    k = xs[:, RW_WIDTH:2 * RW_WIDTH]
    v = xs[:, 2 * RW_WIDTH:3 * RW_WIDTH]
    log_decay = -math.exp(-0.5) * _sigmoid(w0 + _bdot(jnp.tanh(xwa), w2p))
    a = _sigmoid(a0 + _bdot(xwa, a2p))
    kk = k * kk_gain
    kk = kk * lax.rsqrt(_head_sum(kk * kk, ones2) + 1e-12)
    k = k * (1.0 + (a - 1.0) * ka_gain)
    kka = kk * a
    ti = lax.broadcasted_iota(jnp.int32, (BLK, BLK), 0)
    si = lax.broadcasted_iota(jnp.int32, (BLK, BLK), 1)
    cum = _dot01(jnp.where(si <= ti, 1.0, 0.0), log_decay)
    cum_end = cum[BLK - 1:BLK, :]
    inv_w = jnp.exp(-cum)
    tail_w = jnp.exp(cum_end - cum)
    return dict(at=-kk * jnp.exp(cum - log_decay), rt=r * jnp.exp(cum), bh=kka * inv_w, kh=k * inv_w,
                bc=kka * tail_w, kc=k * tail_w, v=v, wt=jnp.exp(cum_end),
                bonus=_head_sum(r * k * rk_gain, ones2) * v)


def _rw_kernel(rkv_ref, wa_ref, gate_ref, sh0_rkv_ref, sh0_wa_ref, s0_ref,
               mu_rkv_ref, mu_wa_ref, w0_ref, w2_ref, a0_ref, a2_ref, kk_ref, ka_ref, rk_ref,
               lng_ref, lnb_ref, ones_ref,
               y_ref, s_out, sh_rkv_out, sh_wa_out, s_scr, c_rkv, c_wa, *, bb_n):
    j = pl.program_id(1)

    @pl.when(j == 0)
    def _():
        for bb in range(bb_n):
            for p in range(RW_PAIRS):
                s_scr[bb, p] = jnp.concatenate([s0_ref[bb, 2 * p], s0_ref[bb, 2 * p + 1]], axis=1)
        c_rkv[...] = sh0_rkv_ref[...]
        c_wa[...] = sh0_wa_ref[...]

    ones2 = ones_ref[...]
    rows = []
    for bb in range(bb_n):
        u, uw = rkv_ref[bb], wa_ref[bb]
        rows.append(_rw_prepare(u, uw, c_rkv[bb], c_wa[bb], mu_rkv_ref[...], mu_wa_ref[...],
                                w0_ref[...], w2_ref[...], a0_ref[...], a2_ref[...], kk_ref[...],
                                ka_ref[...], rk_ref[...], ones2))
        for carry, out, src in ((c_rkv, sh_rkv_out, u), (c_wa, sh_wa_out, uw)):
            carry[bb] = src[BLK - 1:BLK, :]
            out[bb] = src[BLK - 1:BLK, :]

    lo_lane = lax.broadcasted_iota(jnp.int32, (1, LANE), 1) < RW_HEAD_DIM
    row_j = lax.broadcasted_iota(jnp.int32, (RW_HEAD_DIM, LANE), 0)
    col_t = lax.broadcasted_iota(jnp.int32, (RW_HEAD_DIM, LANE), 1) & (RW_HEAD_DIM - 1)
    strict = row_j < col_t
    incl = row_j <= col_t
    same_blk = (row_j // SOLVE_BLK) == (col_t // SOLVE_BLK)
    ident = jnp.where(row_j == col_t, 1.0, 0.0)
    col_blk = (lax.broadcasted_iota(jnp.int32, (1, LANE), 1) & (RW_HEAD_DIM - 1)) // SOLVE_BLK

    def by_head(x):
        return jnp.concatenate([jnp.where(lo_lane, x, 0.0), jnp.where(lo_lane, 0.0, x)], axis=0)

    pick = jnp.where(row_j == col_t, 1.0, 0.0).astype(BF16)
    pick2 = jnp.concatenate([pick, pick], axis=1)

    bd = lambda hl: (by_head(hl[0]), by_head(hl[1]))
    stack_rows = lambda parts: tuple(jnp.concatenate(x, axis=0) for x in zip(*parts))
    stack_lanes = lambda parts: tuple(jnp.concatenate(x, axis=1) for x in zip(*parts))
    wide_rhs = lambda w: _rhs3(bd(_hi_lo(w)))
    wide_dot = lambda a, rhs3: _dot_nn(_lhs3(_hi_lo(a)), rhs3)

    def pair_transpose(x):
        return _dot_nt(pick2, jnp.concatenate(bd(_hi_lo(x)), axis=1))

    chains = [(bb, p) for bb in range(bb_n) for p in range(RW_PAIRS)]
    n_ch = len(chains)
    tile = lambda name, c: rows[chains[c][0]][name][:, chains[c][1] * LANE:(chains[c][1] + 1) * LANE]
    tile_hl = lambda name, c: _hi_lo(tile(name, c))

    ar_rows, gb_w, gd, gk_hk_kc3, bc3 = [], [], [], [], []
    for c in range(n_ch):
        ar_rows.append(_rhs3_rows(stack_rows([bd(tile_hl("at", c)), bd(tile_hl("rt", c))])))
        bk = stack_rows([tile_hl("bh", c), tile_hl("kh", c)])
        gram = _dot_nt(_lhs3(bk), ar_rows[c])
        gb = jnp.where(strict, gram[0:RW_HEAD_DIM, 0:LANE], 0.0)
        hb = jnp.where(incl, gram[0:RW_HEAD_DIM, LANE:2 * LANE], 0.0)
        gk = jnp.where(strict, gram[RW_HEAD_DIM:LANE, 0:LANE], 0.0)
        hk = jnp.where(incl, gram[RW_HEAD_DIM:LANE, LANE:2 * LANE], 0.0)
        gb_w.append(gb)
        gd.append(jnp.where(same_blk, gb, 0.0))
        gk_hk_kc3.append(_rhs3(stack_lanes([bd(_hi_lo(gk)), bd(_hi_lo(hk)), bd(tile_hl("kc", c))])))
        bc3.append(_rhs3(stack_lanes([bd(_hi_lo(hb)), bd(tile_hl("bc", c))])))
    vt = [pair_transpose(tile("v", c)) for c in range(n_ch)]
    v_terms = [_dot_nn(_lhs3(_hi_lo(vt[c])), gk_hk_kc3[c]) for c in range(n_ch)]
    t_inv = [ident + g for g in gd]
    pw = gd
    for _ in range(SOLVE_LEVELS - 1):
        pw = [wide_dot(pw[c], wide_rhs(pw[c])) for c in range(n_ch)]
        t_inv = [t_inv[c] + wide_dot(t_inv[c], wide_rhs(pw[c])) for c in range(n_ch)]
    td3 = [wide_rhs(t) for t in t_inv]
    gt3 = [wide_rhs(wide_dot(gb_w[c], td3[c])) for c in range(n_ch)]

    s_old = [s_scr[bb, p] for bb, p in chains]
    uy = [_dot_nt(_lhs3(_hi_lo(s_old[c])), ar_rows[c]) for c in range(n_ch)]
    x0 = [wide_dot(uy[c][:, 0:LANE] + v_terms[c][:, 0:LANE], td3[c]) for c in range(n_ch)]
    x = [jnp.where(col_blk == 0, x0[c], 0.0) for c in range(n_ch)]
    for blk in range(1, BLK // SOLVE_BLK):
        x = [x[c] + jnp.where(col_blk == blk, x0[c] + wide_dot(x[c], gt3[c]), 0.0)
             for c in range(n_ch)]
    out2 = [wide_dot(x[c], bc3[c]) for c in range(n_ch)]
    y_tiles = []
    for c, (bb, p) in enumerate(chains):
        y_col = uy[c][:, LANE:2 * LANE] + out2[c][:, 0:LANE] + v_terms[c][:, LANE:2 * LANE]
        y_tiles.append(pair_transpose(y_col))
        s_scr[bb, p] = (s_old[c] * tile("wt", c) + out2[c][:, LANE:2 * LANE]
                        + v_terms[c][:, 2 * LANE:3 * LANE])

    inv = 1.0 / RW_HEAD_DIM
    for bb in range(bb_n):
        y = jnp.concatenate(y_tiles[bb * RW_PAIRS:(bb + 1) * RW_PAIRS], axis=1)
        mu = _head_sum(y, ones2) * inv
        d = y - mu
        var = _head_sum(d * d, ones2) * inv
        yn = d * lax.rsqrt(var + RW_GN_EPS) * lng_ref[...] + lnb_ref[...]
        y_ref[bb] = (yn + rows[bb]["bonus"]) * _silu(gate_ref[bb])

    @pl.when(j == pl.num_programs(1) - 1)
    def _():
        for bb in range(bb_n):
            for p in range(RW_PAIRS):
                s_out[bb, 2 * p] = s_scr[bb, p, :, 0:RW_HEAD_DIM]
                s_out[bb, 2 * p + 1] = s_scr[bb, p, :, RW_HEAD_DIM:LANE]


def _rwkv(p3, sh0_rkv, sh0_wa, s0, lp, *, bb_n):
    bsz, seq_rows, _ = p3.shape
    nblk = seq_rows // BLK
    blk = lambda w, off: pl.BlockSpec((bb_n, BLK, w), lambda b, j: (b, j, off // w))
    par = lambda w: pl.BlockSpec((1, w), lambda b, j: (0, 0))
    mat = pl.BlockSpec((LANE, RW_WIDTH), lambda b, j: (0, 0))
    st = lambda w: pl.BlockSpec((bb_n, 1, w), lambda b, j: (b, 0, 0))
    s_spec = pl.BlockSpec((bb_n, RW_HEADS, RW_HEAD_DIM, RW_HEAD_DIM), lambda b, j: (b, 0, 0, 0))
    return pl.pallas_call(
        functools.partial(_rw_kernel, bb_n=bb_n),
        out_shape=(jax.ShapeDtypeStruct((bsz, seq_rows, RW_WIDTH), F32),
                   jax.ShapeDtypeStruct((bsz, RW_HEADS, RW_HEAD_DIM, RW_HEAD_DIM), F32),
                   jax.ShapeDtypeStruct((bsz, 1, RW_RKV), F32),
                   jax.ShapeDtypeStruct((bsz, 1, LANE), F32)),
        grid=(bsz // bb_n, nblk),
        in_specs=[blk(W_RKV, OFF_RKV), blk(W_WA, OFF_WA), blk(W_RWG, OFF_RWG),
                  st(RW_RKV), st(LANE), s_spec,
                  par(RW_RKV), par(LANE), par(RW_WIDTH), mat, par(RW_WIDTH), mat,
                  par(RW_WIDTH), par(RW_WIDTH), par(RW_WIDTH), par(RW_WIDTH), par(RW_WIDTH),
                  pl.BlockSpec((2 * LANE, LANE), lambda b, j: (0, 0))],
        out_specs=(pl.BlockSpec((bb_n, BLK, RW_WIDTH), lambda b, j: (b, j, 0)), s_spec,
                   st(RW_RKV), st(LANE)),
        scratch_shapes=[pltpu.VMEM((bb_n, RW_PAIRS, RW_HEAD_DIM, LANE), F32),
                        pltpu.VMEM((bb_n, 1, RW_RKV), F32), pltpu.VMEM((bb_n, 1, LANE), F32)],
        compiler_params=_cparams(("parallel", "arbitrary")),
        name="rwkv",
    )(p3, p3, p3, sh0_rkv, sh0_wa, s0, lp["mu_rkv"], lp["mu_wa"], lp["rw_w0"], lp["rw_w2p"],
      lp["rw_a0"], lp["rw_a2p"], lp["rw_kk"], lp["rw_ka"], lp["rw_rk"], lp["rw_ln_g"],
      lp["rw_ln_b"], lp["ones2"])


def _cmul(ar, ai, br, bi):
    return ar * br - ai * bi, ar * bi + ai * br


def _s5_prep_kernel(are_ref, aim_ref, ldt_ref, bre_ref, bim_ref,
                    bbre_out, bbim_out, lvre_out, lvim_out, pwre_out, pwim_out):
    ar, ai = are_ref[...], aim_ref[...]
    dt = jnp.exp(ldt_ref[...])
    mag = jnp.exp(ar * dt)
    lr, li = mag * jnp.cos(ai * dt), mag * jnp.sin(ai * dt)
    nr, ni = lr - 1.0, li
    den = ar * ar + ai * ai
    qr, qi = (nr * ar + ni * ai) / den, (ni * ar - nr * ai) / den
    br, bi = bre_ref[...], bim_ref[...]
    bbr, bbi = _cmul(qr[:, None, :], qi[:, None, :], br, bi)
    bbre_out[...] = bbr
    bbim_out[...] = bbi
    sq_r, sq_i = lr, li
    pows = [(lr, li)]
    for lvl in range(SCAN_LEVELS):
        lvre_out[lvl] = sq_r
        lvim_out[lvl] = sq_i
        pows = pows + [_cmul(pr, pi, sq_r, sq_i) for pr, pi in pows]
        sq_r, sq_i = _cmul(sq_r, sq_i, sq_r, sq_i)
    for t in range(SUBLANES):
        pwre_out[t] = pows[t][0]
        pwim_out[t] = pows[t][1]


def _s5_prep(a_re, a_im, log_dt, b_re_t, b_im_t):
    ghp = jax.ShapeDtypeStruct((S5_GROUPS, S5_GROUP_CH, S5_STATE), F32)
    lv = jax.ShapeDtypeStruct((SCAN_LEVELS, S5_GROUPS, S5_STATE), F32)
    pw = jax.ShapeDtypeStruct((SUBLANES, S5_GROUPS, S5_STATE), F32)
    return pl.pallas_call(
        _s5_prep_kernel,
        out_shape=(ghp, ghp, lv, lv, pw, pw),
        name="s5_prep",
    )(a_re, a_im, log_dt, b_re_t, b_im_t)


def _gelu_tanh(x):
    return 0.5 * x * (1.0 + jnp.tanh(math.sqrt(2.0 / math.pi) * (x + 0.044715 * (x * x * x))))


def _s5_kernel(p_ref, h0_ref, wb_ref, wc_ref, lv_ref, pw_ref, d_ref, wglu_ref, bglu_ref,
               y_ref, h_out, h_scr, hs_scr, *, tb):
    j = pl.program_id(1)

    @pl.when(j == 0)
    def _():
        h_scr[...] = h0_ref[...]

    u = p_ref[:, 0:S5_WIDTH]
    gate = p_ref[:, S5_WIDTH:2 * S5_WIDTH]
    y_parts = []
    for jb in range(S5_NBLK):
        bu = _bdot(u[:, jb * LANE:(jb + 1) * LANE], wb_ref[jb])
        for s in range(tb // BLK):
            xr = bu[s * BLK:(s + 1) * BLK, 0:S5_BLK_STATE]
            xi = bu[s * BLK:(s + 1) * BLK, S5_BLK_STATE:2 * S5_BLK_STATE]
            xr = xr.reshape(BLK // SUBLANES, SUBLANES, S5_BLK_STATE)
            xi = xi.reshape(BLK // SUBLANES, SUBLANES, S5_BLK_STATE)
            for lvl in range(SCAN_LEVELS):
                sr = pltpu.roll(xr, 1 << lvl, 1)
                si = pltpu.roll(xi, 1 << lvl, 1)
                lr, li = lv_ref[lvl, 2 * jb], lv_ref[lvl, 2 * jb + 1]
                xr, xi = xr + (lr * sr - li * si), xi + (lr * si + li * sr)
            xr = xr.reshape(BLK, S5_BLK_STATE)
            xi = xi.reshape(BLK, S5_BLK_STATE)
            c_r = h_scr[2 * jb:2 * jb + 1, :]
            c_i = h_scr[2 * jb + 1:2 * jb + 2, :]
            pr, pi = pw_ref[2 * jb], pw_ref[2 * jb + 1]
            for grp in range(BLK // SUBLANES):
                rows = slice(grp * SUBLANES, (grp + 1) * SUBLANES)
                hr = xr[rows, :] + (pr * c_r - pi * c_i)
                hi = xi[rows, :] + (pr * c_i + pi * c_r)
                c_r, c_i = hr[SUBLANES - 1:SUBLANES, :], hi[SUBLANES - 1:SUBLANES, :]
                out_rows = slice(s * BLK + grp * SUBLANES, s * BLK + (grp + 1) * SUBLANES)
                hs_scr[out_rows, 0:S5_BLK_STATE] = hr
                hs_scr[out_rows, S5_BLK_STATE:2 * S5_BLK_STATE] = hi
            h_scr[2 * jb:2 * jb + 1, :] = c_r
            h_scr[2 * jb + 1:2 * jb + 2, :] = c_i
        y_parts.append(_bdot(hs_scr[...], wc_ref[jb]))
    y = jnp.concatenate(y_parts, axis=-1) + d_ref[...] * u
    y = _gelu_tanh(y)
    y = y * _sigmoid(_bdot(y, wglu_ref[...]) + bglu_ref[...])
    y_ref[...] = y * _silu(gate)

    @pl.when(j == pl.num_programs(1) - 1)
    def _():
        h_out[...] = h_scr[...]


def _s5(p3, h0, lp, *, tb):
    bsz, seq_rows, _ = p3.shape
    st_spec = pl.BlockSpec((None, 2 * S5_NBLK, S5_BLK_STATE), lambda b, j: (b, 0, 0))
    full = lambda shape: pl.BlockSpec(shape, lambda b, j: (0,) * len(shape))
    return pl.pallas_call(
        functools.partial(_s5_kernel, tb=tb),
        out_shape=(jax.ShapeDtypeStruct((bsz, seq_rows, S5_WIDTH), F32),
                   jax.ShapeDtypeStruct((bsz, 2 * S5_NBLK, S5_BLK_STATE), F32)),
        grid=(bsz, seq_rows // tb),
        in_specs=[pl.BlockSpec((None, tb, W_S5), lambda b, j: (b, j, OFF_S5 // W_S5)),
                  st_spec,
                  full((S5_NBLK, LANE, 2 * S5_BLK_STATE)),
                  full((S5_NBLK, 2 * S5_BLK_STATE, LANE)),
                  full((SCAN_LEVELS, 2 * S5_NBLK, SUBLANES, S5_BLK_STATE)),
                  full((2 * S5_NBLK, SUBLANES, S5_BLK_STATE)),
                  full((1, S5_WIDTH)), full((S5_WIDTH, S5_WIDTH)), full((1, S5_WIDTH))],
        out_specs=(pl.BlockSpec((None, tb, S5_WIDTH), lambda b, j: (b, j, 0)), st_spec),
        scratch_shapes=[pltpu.VMEM((2 * S5_NBLK, S5_BLK_STATE), F32),
                        pltpu.VMEM((tb, 2 * S5_BLK_STATE), F32)],
        compiler_params=_cparams(("parallel", "arbitrary")),
        name="s5",
    )(p3, h0, lp["s5_wb"], lp["s5_wc"], lp["s5_lv"], lp["s5_pw"], lp["s5_d"],
      lp["s5_w_glu"], lp["s5_b_glu"])


def _mlstm_kernel(qk_ref, oz_ref, v_ref, gi_ref, gf_ref, conv0_ref, c0_ref, n0_ref, m0_ref,
                  cw_ref, cb_ref, bi_ref, bf_ref, lng_ref,
                  y_ref, conv_out, c_out, n_out, m_out,
                  xp_scr, c_scr, n_scr, m_scr, *, pad, bb_n):
    j = pl.program_id(1)
    halo = SUBLANES

    @pl.when(j == 0)
    def _():
        xp_scr[:, 0:halo, :] = jnp.zeros((bb_n, halo, W_QK), F32)
        xp_scr[:, halo - (ML_CONV - 1):halo, :] = conv0_ref[...]
        c_scr[...] = jnp.zeros(c_scr.shape, F32)
        c_scr[:, :, 0:ML_HEAD_DIM, 0:ML_HEAD_DIM] = c0_ref[...]
        n_scr[...] = jnp.zeros(n_scr.shape, F32)
        n_scr[:, :, :, 0:ML_HEAD_DIM] = n0_ref[...]
        m_scr[...] = m0_ref[...]

    t0 = jnp.where(j == 0, pad, 0) if pad else 0
    row1 = lax.broadcasted_iota(jnp.int32, (BLK, 1), 0)
    row_ok = row1 >= t0
    ti = lax.broadcasted_iota(jnp.int32, (BLK, BLK), 0)
    si = lax.broadcasted_iota(jnp.int32, (BLK, BLK), 1)
    pair_ok = (si <= ti) & (si >= t0)
    tril = jnp.where(si <= ti, 1.0, 0.0)
    eye = jnp.where(si == ti, 1.0, 0.0)
    ones_sq = jnp.ones((BLK, BLK), F32)
    lane_ok = lax.broadcasted_iota(jnp.int32, (1, ML_HEAD_PAD), 1) < ML_HEAD_DIM
    head = lambda a, h: a[:, h * ML_HEAD_PAD:(h + 1) * ML_HEAD_PAD]
    chains = [(bb, h) for bb in range(bb_n) for h in range(ML_HEADS)]

    log_i, log_f, b = [], [], []
    for bb in range(bb_n):
        log_i.append(gi_ref[bb] + bi_ref[...])
        log_f.append(jnp.where(row_ok, -_softplus(-(gf_ref[bb] + bf_ref[...])), 0.0))
        b.append(_hdot(tril, log_f[bb]))
    d_row = {}
    for bb, h in chains:
        x_col = log_i[bb][:, h:h + 1] - b[bb][:, h:h + 1]
        d_row[bb, h] = _hdot(ones_sq, eye * x_col)

    q, k, v = [], [], []
    for bb in range(bb_n):
        xp_scr[bb, halo:halo + BLK, :] = qk_ref[bb]
        conv = cb_ref[...] + xp_scr[bb, halo - 3:halo - 3 + BLK, :] * cw_ref[0:1, :]
        for tap in range(1, ML_CONV):
            conv = conv + xp_scr[bb, halo - 3 + tap:halo - 3 + tap + BLK, :] * cw_ref[tap:tap + 1, :]
        tail = xp_scr[bb, halo + BLK - (ML_CONV - 1):halo + BLK, :]
        xp_scr[bb, halo - (ML_CONV - 1):halo, :] = tail
        conv_out[bb] = tail
        act = _silu(conv)
        q.append(act[:, 0:ML_WIDTH_PAD])
        k.append(act[:, ML_WIDTH_PAD:2 * ML_WIDTH_PAD] * (1.0 / math.sqrt(ML_HEAD_DIM)))
        v.append(v_ref[bb])

    qk, q_c = {}, {}
    for bb, h in chains:
        qh = head(q[bb], h).astype(BF16)
        qk[bb, h] = lax.dot_general(qh, head(k[bb], h).astype(BF16), (((1,), (1,)), ((), ())),
                                    preferred_element_type=F32)
        q_c[bb, h] = jnp.dot(qh, c_scr[bb, h].astype(BF16), preferred_element_type=F32)

    g, we, keep = [], [], []
    for bb in range(bb_n):
        m_prev = m_scr[bb]
        g.append(b[bb] + m_prev)
        b_end = b[bb][BLK - 1:BLK, :]
        e_log = jnp.where(row_ok, b_end - b[bb] + log_i[bb], -jnp.inf)
        m_new = jnp.maximum(b_end + m_prev, jnp.max(e_log, axis=0, keepdims=True))
        we.append(jnp.exp(e_log - m_new))
        keep.append(jnp.exp(b_end + m_prev - m_new))
        m_scr[bb] = m_new

    s_mat, m_row, w_inter = {}, {}, {}
    for bb, h in chains:
        d = b[bb][:, h:h + 1] + d_row[bb, h]
        d = jnp.where(pair_ok, d, -jnp.inf)
        g_col = g[bb][:, h:h + 1]
        m_row[bb, h] = jnp.maximum(g_col, jnp.max(d, axis=1, keepdims=True))
        s_mat[bb, h] = qk[bb, h] * jnp.exp(d - m_row[bb, h])
        w_inter[bb, h] = jnp.exp(g_col - m_row[bb, h])

    s_v, k_v = {}, {}
    for bb, h in chains:
        vh = head(v[bb], h)
        s_v[bb, h] = _bdot(s_mat[bb, h], vh)
        k_v[bb, h] = lax.dot_general(head(k[bb], h).astype(BF16),
                                     (we[bb][:, h:h + 1] * vh).astype(BF16),
                                     (((0,), (0,)), ((), ())), preferred_element_type=F32)

    for bb, h in chains:
        hs = slice(h * ML_HEAD_PAD, (h + 1) * ML_HEAD_PAD)
        kh = head(k[bb], h)
        n_h = n_scr[bb, h]
        num = s_v[bb, h] + w_inter[bb, h] * q_c[bb, h]
        qn = jnp.sum(head(q[bb], h) * n_h, axis=1, keepdims=True)
        den = jnp.sum(s_mat[bb, h], axis=1, keepdims=True) + w_inter[bb, h] * qn
        hh = num / jnp.maximum(jnp.abs(den), jnp.exp(-m_row[bb, h]))
        mu = jnp.sum(hh, axis=1, keepdims=True) * (1.0 / ML_HEAD_DIM)
        dv = jnp.where(lane_ok, hh - mu, 0.0)
        var = jnp.sum(dv * dv, axis=1, keepdims=True) * (1.0 / ML_HEAD_DIM)
        hn = dv * lax.rsqrt(var + LN_EPS) * lng_ref[:, hs]
        y_ref[bb, :, hs] = (_sigmoid(oz_ref[bb, :, hs]) * hn
                            * _silu(oz_ref[bb, :, ML_WIDTH_PAD + h * ML_HEAD_PAD:
                                           ML_WIDTH_PAD + (h + 1) * ML_HEAD_PAD]))
        keep_h = keep[bb][:, h:h + 1]
        c_scr[bb, h] = keep_h * c_scr[bb, h] + k_v[bb, h]
        n_scr[bb, h] = keep_h * n_h + jnp.sum(we[bb][:, h:h + 1] * kh, axis=0, keepdims=True)

    @pl.when(j == pl.num_programs(1) - 1)
    def _():
        c_out[...] = c_scr[:, :, 0:ML_HEAD_DIM, 0:ML_HEAD_DIM]
        n_out[...] = n_scr[:, :, :, 0:ML_HEAD_DIM]
        m_out[...] = m_scr[...]


def _mlstm(p3, conv0, c0_layers, n0, m0, lp, *, pad, bb_n):
    c0, c0_layer = c0_layers
    bsz, seq_rows, _ = p3.shape
    nblk = seq_rows // BLK
    blk = lambda w, off: pl.BlockSpec((bb_n, BLK, w), lambda b, j: (b, j, off // w))
    par = lambda r, w: pl.BlockSpec((r, w), lambda b, j: (0, 0))
    conv_spec = pl.BlockSpec((bb_n, ML_CONV - 1, W_QK), lambda b, j: (b, 0, 0))
    c_spec = pl.BlockSpec((bb_n, ML_HEADS, ML_HEAD_DIM, ML_HEAD_DIM), lambda b, j: (b, 0, 0, 0))
    n_spec = pl.BlockSpec((bb_n, ML_HEADS, 1, ML_HEAD_DIM), lambda b, j: (b, 0, 0, 0))
    m_spec = pl.BlockSpec((bb_n, 1, LANE), lambda b, j: (b, 0, 0))
    return pl.pallas_call(
        functools.partial(_mlstm_kernel, pad=pad, bb_n=bb_n),
        out_shape=(jax.ShapeDtypeStruct((bsz, seq_rows, ML_WIDTH_PAD), F32),
                   jax.ShapeDtypeStruct((bsz, ML_CONV - 1, W_QK), F32),
                   jax.ShapeDtypeStruct((bsz, ML_HEADS, ML_HEAD_DIM, ML_HEAD_DIM), F32),
                   jax.ShapeDtypeStruct((bsz, ML_HEADS, 1, ML_HEAD_DIM), F32),
                   jax.ShapeDtypeStruct((bsz, 1, LANE), F32)),
        grid=(bsz // bb_n, nblk),
        in_specs=[blk(W_QK, OFF_QK), blk(W_OZ, OFF_OZ), blk(W_MV, OFF_MV),
                  blk(W_MI, OFF_MI), blk(W_MF, OFF_MF),
                  conv_spec,
                  pl.BlockSpec((None, bb_n, ML_HEADS, ML_HEAD_DIM, ML_HEAD_DIM),
                               lambda b, j: (c0_layer, b, 0, 0, 0)),
                  n_spec, m_spec,
                  par(ML_CONV, W_QK), par(1, W_QK), par(1, LANE), par(1, LANE),
                  par(1, ML_WIDTH_PAD)],
        out_specs=(pl.BlockSpec((bb_n, BLK, ML_WIDTH_PAD), lambda b, j: (b, j, 0)),
                   conv_spec, c_spec, n_spec, m_spec),
        scratch_shapes=[pltpu.VMEM((bb_n, SUBLANES + BLK, W_QK), F32),
                        pltpu.VMEM((bb_n, ML_HEADS, ML_HEAD_PAD, ML_HEAD_PAD), F32),
                        pltpu.VMEM((bb_n, ML_HEADS, 1, ML_HEAD_PAD), F32),
                        pltpu.VMEM((bb_n, 1, LANE), F32)],
        compiler_params=_cparams(("parallel", "arbitrary")),
        name="mlstm",
    )(p3, p3, p3, p3, p3, conv0, c0, n0, m0, lp["ml_cw"], lp["ml_cb"], lp["ml_bi"], lp["ml_bf"],
      lp["ml_ln_g"])


def _merge_kernel(x_ref, mg_ref, yrw_ref, ys5_ref, yml_ref, bmg_ref, wrw_ref, ws5_ref, wml_ref,
                  wout_ref, g_ref, b_ref, o_ref, *, tm, seq_rows, pad):
    gates = _sigmoid(mg_ref[...] + bmg_ref[...])
    merged = (gates[:, 0:D_MODEL] * _bdot(yrw_ref[...], wrw_ref[...])
              + gates[:, D_MODEL:2 * D_MODEL] * _bdot(ys5_ref[...], ws5_ref[...])
              + gates[:, 2 * D_MODEL:3 * D_MODEL] * _bdot(yml_ref[...], wml_ref[...]))
    out = _bdot(merged, wout_ref[...])
    y = _layer_norm_rows(DN_ALPHA * x_ref[...] + out, g_ref[...], b_ref[...])
    if pad:
        y = jnp.where(_pad_row_mask(pl.program_id(0), tm, seq_rows, pad), 0.0, y)
    o_ref[...] = y


def _merge(x, p, y_rw, y_s5, y_ml, lp, *, tm, seq_rows, pad):
    n = x.shape[0]
    rows = lambda w: pl.BlockSpec((tm, w), lambda i: (i, 0))
    full = lambda r, w: pl.BlockSpec((r, w), lambda i: (0, 0), pipeline_mode=pl.Buffered(1))
    return pl.pallas_call(
        functools.partial(_merge_kernel, tm=tm, seq_rows=seq_rows, pad=pad),
        out_shape=jax.ShapeDtypeStruct((n, D_MODEL), F32),
        grid=(n // tm,),
        in_specs=[rows(D_MODEL), pl.BlockSpec((tm, W_MG), lambda i: (i, OFF_MG // W_MG)),
                  rows(RW_WIDTH), rows(S5_WIDTH), rows(ML_WIDTH_PAD),
                  full(1, W_MG), full(RW_WIDTH, D_MODEL), full(S5_WIDTH, D_MODEL),
                  full(ML_WIDTH_PAD, D_MODEL), full(D_MODEL, D_MODEL),
                  full(1, D_MODEL), full(1, D_MODEL)],
        out_specs=rows(D_MODEL),
        compiler_params=_cparams(("parallel",)),
        name="merge",
    )(x, p, y_rw, y_s5, y_ml, lp["b_merge"], lp["w_br_rw"], lp["w_br_s5"], lp["w_br_ml"],
      lp["w_out"], lp["ln_g"], lp["ln_b"])


def _pad_heads(w):
    lead = w.shape[:-1]
    w = w.reshape(lead + (ML_HEADS, ML_HEAD_DIM))
    w = jnp.pad(w, [(0, 0)] * len(lead) + [(0, 0), (0, ML_HEAD_PAD - ML_HEAD_DIM)])
    return w.reshape(lead + (ML_WIDTH_PAD,))


def _unpad_heads(w):
    lead = w.shape[:-1]
    return w.reshape(lead + (ML_HEADS, ML_HEAD_PAD))[..., :ML_HEAD_DIM].reshape(lead + (ML_WIDTH,))


def _pad_lanes(w, width=LANE):
    return jnp.pad(w, [(0, 0)] * (w.ndim - 1) + [(0, width - w.shape[-1])])


def _qk_pad(w):
    return jnp.concatenate([_pad_heads(w[..., :ML_WIDTH]), _pad_heads(w[..., ML_WIDTH:])], axis=-1)


def _layer_params(l, w_in, rw_mu, rw_w0, rw_w2, rw_a0, rw_a2, rw_kk, rw_ka, rw_rk, rw_ln_g, rw_ln_b,
                  s5_a_re, s5_a_im, s5_b_re, s5_b_im, s5_c_re, s5_c_im, s5_d, s5_log_dt, s5_w_glu,
                  s5_b_glu, ml_conv_w, ml_conv_b, ml_b_if, ml_ln_g, b_merge, w_br_rw, w_br_s5,
                  w_br_ml, w_out, ln_g, ln_b):
    w = w_in[l]
    o = 0
    cols = {}
    for name, size in (("rwc", RW_SHIFT), ("rwg", RW_WIDTH), ("s5", 2 * S5_WIDTH),
                       ("qk", 2 * ML_WIDTH), ("mv", ML_WIDTH), ("mi", ML_HEADS), ("mf", ML_HEADS),
                       ("mo", ML_WIDTH), ("mz", ML_WIDTH), ("mg", 3 * D_MODEL)):
        cols[name] = w[:, o:o + size]
        o += size
    w_all = jnp.concatenate([
        cols["mg"], cols["s5"], _qk_pad(cols["qk"]),
        _pad_heads(cols["mo"]), _pad_heads(cols["mz"]), _pad_heads(cols["mv"]),
        cols["rwc"][:, :RW_RKV], cols["rwg"], cols["rwc"][:, RW_RKV:],
        _pad_lanes(cols["mi"]), _pad_lanes(cols["mf"]),
        jnp.zeros((D_MODEL, P_COLS - P_USED), F32)], axis=1).astype(BF16)
    row = lambda a: a.reshape(1, -1)
    zeros_lora = jnp.zeros((RW_LORA, RW_WIDTH), F32)
    lane = jnp.arange(LANE)
    ones_blk = (lane[:, None] // RW_HEAD_DIM == lane[None, :] // RW_HEAD_DIM).astype(F32)

    bb_re, bb_im, lv_re, lv_im, pw_re, pw_im = _s5_prep(
        s5_a_re[l], s5_a_im[l], s5_log_dt[l].reshape(S5_GROUPS, 1),
        jnp.swapaxes(s5_b_re[l], 1, 2), jnp.swapaxes(s5_b_im[l], 1, 2))
    eye8 = jnp.eye(S5_BLK_GROUPS, dtype=F32)
    blocked = lambda a: a.reshape((S5_NBLK, S5_BLK_GROUPS) + a.shape[1:])
    bb = jnp.stack([blocked(bb_re), blocked(bb_im)])
    wb = jnp.einsum("cjghp,gk->jghckp", bb, eye8).reshape(S5_NBLK, LANE, 2 * S5_BLK_STATE)
    cc = jnp.stack([blocked(s5_c_re[l]), -blocked(s5_c_im[l])])
    wc = jnp.einsum("cjghp,gk->jcgpkh", cc, eye8).reshape(S5_NBLK, 2 * S5_BLK_STATE, LANE)
    state_rows = lambda re, im: jnp.stack(
        [re.reshape(re.shape[:-2] + (S5_NBLK, S5_BLK_STATE)),
         im.reshape(im.shape[:-2] + (S5_NBLK, S5_BLK_STATE))], axis=-2)
    lv = state_rows(lv_re, lv_im).reshape(SCAN_LEVELS, 2 * S5_NBLK, 1, S5_BLK_STATE)
    row_in_group = jnp.arange(SUBLANES)[None, None, :, None]
    lv = jnp.where(row_in_group >= (1 << jnp.arange(SCAN_LEVELS))[:, None, None, None], lv, 0.0)
    pw = jnp.moveaxis(state_rows(pw_re, pw_im).reshape(SUBLANES, 2 * S5_NBLK, S5_BLK_STATE), 0, 1)

    return dict(
        w_all=w_all,
        mu_rkv=row(rw_mu[l][:RW_RKV]), mu_wa=row(rw_mu[l][RW_RKV:]),
        rw_w0=row(rw_w0[l]), rw_a0=row(rw_a0[l]),
        rw_w2p=jnp.concatenate([rw_w2[l], zeros_lora], axis=0),
        rw_a2p=jnp.concatenate([zeros_lora, rw_a2[l]], axis=0),
        rw_kk=row(rw_kk[l]), rw_ka=row(rw_ka[l]), rw_rk=row(rw_rk[l]),
        rw_ln_g=row(rw_ln_g[l]), rw_ln_b=row(rw_ln_b[l]),
        ones2=jnp.concatenate([ones_blk, ones_blk], axis=0).astype(BF16),
        s5_wb=wb.astype(BF16), s5_wc=wc.astype(BF16), s5_lv=lv, s5_pw=pw,
        s5_d=row(s5_d[l]), s5_w_glu=s5_w_glu[l].astype(BF16), s5_b_glu=row(s5_b_glu[l]),
        ml_cw=_qk_pad(ml_conv_w[l]), ml_cb=row(_qk_pad(ml_conv_b[l])),
        ml_bi=row(_pad_lanes(ml_b_if[l][:ML_HEADS])), ml_bf=row(_pad_lanes(ml_b_if[l][ML_HEADS:])),
        ml_ln_g=row(_pad_heads(ml_ln_g[l])),
        b_merge=row(b_merge[l]), w_br_rw=w_br_rw[l].astype(BF16), w_br_s5=w_br_s5[l].astype(BF16),
        w_br_ml=jnp.pad(w_br_ml[l].reshape(ML_HEADS, ML_HEAD_DIM, D_MODEL),
                        ((0, 0), (0, ML_HEAD_PAD - ML_HEAD_DIM), (0, 0))
                        ).reshape(ML_WIDTH_PAD, D_MODEL).astype(BF16),
        w_out=w_out[l].astype(BF16), ln_g=row(ln_g[l]), ln_b=row(ln_b[l]))


def _pack_s5(re, im):
    b = re.shape[0]
    return jnp.stack([re.reshape(b, S5_NBLK, S5_BLK_STATE), im.reshape(b, S5_NBLK, S5_BLK_STATE)],
                     axis=2).reshape(b, 2 * S5_NBLK, S5_BLK_STATE)


def _unpack_s5(h):
    b = h.shape[0]
    h = h.reshape(b, S5_NBLK, 2, S5_BLK_STATE)
    return (h[:, :, 0].reshape(b, S5_GROUPS, S5_STATE), h[:, :, 1].reshape(b, S5_GROUPS, S5_STATE))


def _row_tile(n, seq_rows, target):
    best = 8
    for t in range(8, min(n, target) + 1, 8):
        if seq_rows % t == 0 or (t % seq_rows == 0 and n % t == 0):
            best = t
    return best


def _trunk_layer(x, st, lp, *, bsz, seq_rows, pad, bb_n, s5_tb, tm_proj, tm_merge):
    rw_shift0, rw_wkv0, s5_re0, s5_im0, ml_conv0, ml_c0, ml_n0, ml_m0 = st
    p = _proj(x, lp["w_all"], tm=tm_proj)
    p3 = p.reshape(bsz, seq_rows, P_COLS)
    n = bsz * seq_rows

    y_rw, wkv1, sh_rkv1, sh_wa1 = _rwkv(p3, rw_shift0[:, None, :RW_RKV], rw_shift0[:, None, RW_RKV:],
                                        rw_wkv0, lp, bb_n=4 if bsz % 4 == 0 else bb_n)
    rw_shift1 = jnp.concatenate([sh_rkv1[:, 0], sh_wa1[:, 0]], axis=-1)

    y_s5, h1 = _s5(p3, _pack_s5(s5_re0, s5_im0), lp, tb=s5_tb)
    s5_re1, s5_im1 = _unpack_s5(h1)

    y_ml, conv1, c1, n1, m1 = _mlstm(
        p3, _qk_pad(ml_conv0), ml_c0, ml_n0[:, :, None, :], _pad_lanes(ml_m0)[:, None, :], lp, pad=pad,
        bb_n=bb_n)
    ml_conv1 = jnp.concatenate([_unpad_heads(conv1[..., :ML_WIDTH_PAD]),
                                _unpad_heads(conv1[..., ML_WIDTH_PAD:])], axis=-1)

    x_new = _merge(x, p, y_rw.reshape(n, RW_WIDTH), y_s5.reshape(n, S5_WIDTH),
                   y_ml.reshape(n, ML_WIDTH_PAD), lp, tm=tm_merge, seq_rows=seq_rows, pad=pad)
    return x_new, (rw_shift1, wkv1, s5_re1, s5_im1, ml_conv1, c1, n1[:, :, 0, :],
                   m1[:, 0, :ML_HEADS])


def _run_group(x_rows, states, lps, *, bsz, seq_rows, pad, in_ln_g, in_ln_b):
    n = bsz * seq_rows
    bb_n = 2 if bsz % 2 == 0 else 1
    s5_tb = max(t for t in (BLK, 5 * BLK) if seq_rows % t == 0)
    tm_ln = _row_tile(n, seq_rows, 1024)
    tm_proj = _row_tile(n, seq_rows, 2080)
    tm_merge = _row_tile(n, seq_rows, 416)
    x = _ln_in(x_rows, in_ln_g, in_ln_b, tm=tm_ln, seq_rows=seq_rows, pad=pad)
    outs = []
    for l in range(DEPTH):
        x, st = _trunk_layer(x, states[l], lps[l], bsz=bsz, seq_rows=seq_rows, pad=pad, bb_n=bb_n,
                             s5_tb=s5_tb, tm_proj=tm_proj, tm_merge=tm_merge)
        outs.append(st)
    return x, outs


def kernel(x_prompt, x_sample, state_rwkv_shift, state_rwkv_wkv, state_s5_re, state_s5_im, state_mlstm_conv, state_mlstm_c, state_mlstm_n, state_mlstm_m, meta, in_ln_g, in_ln_b, w_in, rw_mu, rw_w0, rw_w2, rw_a0, rw_a2, rw_kk, rw_ka, rw_rk, rw_ln_g, rw_ln_b, s5_a_re, s5_a_im, s5_b_re, s5_b_im, s5_c_re, s5_c_im, s5_d, s5_log_dt, s5_w_glu, s5_b_glu, ml_conv_w, ml_conv_b, ml_b_if, ml_ln_g, b_merge, w_br_rw, w_br_s5, w_br_ml, w_out, ln_g, ln_b):
    lps = [_layer_params(l, w_in, rw_mu, rw_w0, rw_w2, rw_a0, rw_a2, rw_kk, rw_ka, rw_rk, rw_ln_g,
                         rw_ln_b, s5_a_re, s5_a_im, s5_b_re, s5_b_im, s5_c_re, s5_c_im, s5_d,
                         s5_log_dt, s5_w_glu, s5_b_glu, ml_conv_w, ml_conv_b, ml_b_if, ml_ln_g,
                         b_merge, w_br_rw, w_br_s5, w_br_ml, w_out, ln_g, ln_b)
           for l in range(DEPTH)]
    g_in, b_in = in_ln_g.reshape(1, D_MODEL), in_ln_b.reshape(1, D_MODEL)

    bp, sp = x_prompt.shape[0], x_prompt.shape[1]
    lp_rows = PAD + N_META + sp
    xp = jnp.concatenate([jnp.zeros((bp, PAD, D_MODEL), F32),
                          jnp.broadcast_to(meta[None], (bp, N_META, D_MODEL)), x_prompt], axis=1)
    z = lambda *shape: jnp.zeros((bp,) + shape, F32)
    zero_state = (z(RW_SHIFT), z(RW_HEADS, RW_HEAD_DIM, RW_HEAD_DIM), z(S5_GROUPS, S5_STATE),
                  z(S5_GROUPS, S5_STATE), z(ML_CONV - 1, 2 * ML_WIDTH),
                  (jnp.zeros((1, bp, ML_HEADS, ML_HEAD_DIM, ML_HEAD_DIM), F32), 0),
                  z(ML_HEADS, ML_HEAD_DIM), z(ML_HEADS))
    yp, p_states = _run_group(xp.reshape(bp * lp_rows, D_MODEL), [zero_state] * DEPTH, lps,
                              bsz=bp, seq_rows=lp_rows, pad=PAD, in_ln_g=g_in, in_ln_b=b_in)
    y_prompt = yp.reshape(bp, lp_rows, D_MODEL)[:, PAD + N_META:]

    bs, ds = x_sample.shape[0], x_sample.shape[1]
    s_in = [(state_rwkv_shift[l], state_rwkv_wkv[l], state_s5_re[l], state_s5_im[l],
             state_mlstm_conv[l], (state_mlstm_c, l), state_mlstm_n[l], state_mlstm_m[l])
            for l in range(DEPTH)]
    ys, s_states = _run_group(x_sample.reshape(bs * ds, D_MODEL), s_in, lps,
                              bsz=bs, seq_rows=ds, pad=0, in_ln_g=g_in, in_ln_b=b_in)
    y_sample = ys.reshape(bs, ds, D_MODEL)

    stack = lambda sts: tuple(jnp.stack(s, 0) for s in zip(*sts))
    return (y_prompt, y_sample) + stack(p_states) + stack(s_states)
```
